```python
import math
import jax, jax.numpy as jnp
from jax import lax
import numpy as np

D_MODEL = 1024
BATCH = 4
SEQ = 4096
DEPTH = 2

A_HEADS = 8
A_KV_HEADS = 2
A_HEAD_DIM = 64
WINDOW = 128
BLOCK = 128
N_BUCKETS = 32
MAX_DISTANCE = 128
B_HEADS = 4
B_KEY_DIM = 64
B_VAL_DIM = 128
B_RANK = 16
GATE_TAU = 16.0
CHUNK = 64
N_GROUPS = 4
EXPERTS_PER_GROUP = 8
TOP_K = 2
D_EXPERT = 256
EPS = 1e-6
NEG_INF = -1e30

IN_SPLITS = (
    A_HEADS * A_HEAD_DIM,
    A_KV_HEADS * A_HEAD_DIM,
    A_KV_HEADS * A_HEAD_DIM,
    B_HEADS * B_KEY_DIM,
    B_HEADS * B_KEY_DIM,
    B_HEADS * B_VAL_DIM,
    B_HEADS * B_VAL_DIM,
    B_RANK,
    D_MODEL,
    D_MODEL,
)
IN_WIDTH = sum(IN_SPLITS)

kernel_name = "hybrid_swa_gla_hier_moe_encoder"


def rms_norm(x, g):
    xf = x.astype(jnp.float32)
    y = xf * lax.rsqrt(jnp.mean(xf * xf, axis=-1, keepdims=True) + EPS)
    return (y * g.astype(jnp.float32)).astype(x.dtype)


def t5_bucket(rel):
    nb = N_BUCKETS // 2
    max_exact = nb // 2
    base = jnp.where(rel > 0, nb, 0)
    n = jnp.abs(rel)
    large = max_exact + (jnp.log(jnp.maximum(n, 1).astype(jnp.float32) / max_exact)
                         / math.log(MAX_DISTANCE / max_exact) * (nb - max_exact)).astype(jnp.int32)
    large = jnp.minimum(large, nb - 1)
    return base + jnp.where(n < max_exact, n, large)


def windowed_gqa(q, k, v, rel_bias, sink):
    b_, s, _, _ = q.shape
    nblk = s // BLOCK
    grp = A_HEADS // A_KV_HEADS
    qb = q.reshape(b_, nblk, BLOCK, A_KV_HEADS, grp, A_HEAD_DIM)
    pad = ((0, 0), (BLOCK, BLOCK), (0, 0), (0, 0))

    def band(t):
        t = jnp.pad(t, pad).reshape(b_, nblk + 2, BLOCK, A_KV_HEADS, A_HEAD_DIM)
        return jnp.concatenate([t[:, :-2], t[:, 1:-1], t[:, 2:]], axis=2)

    kb, vb = band(k), band(v)
    logits = jnp.einsum('bnqhgd,bnkhd->bhgnqk', qb, kb,
                        preferred_element_type=jnp.float32) * (A_HEAD_DIM ** -0.5)
    q_off = jnp.arange(BLOCK)
    k_off = jnp.arange(3 * BLOCK) - BLOCK
    rel = k_off[None, :] - q_off[:, None]
    bias = rel_bias.astype(jnp.float32)[t5_bucket(rel)]
    bias = bias.reshape(BLOCK, 3 * BLOCK, A_KV_HEADS, grp).transpose(2, 3, 0, 1)
    kpos = jnp.arange(nblk)[:, None] * BLOCK + k_off[None, :]
    mask = (jnp.abs(rel) <= WINDOW)[None] & ((kpos >= 0) & (kpos < s))[:, None, :]
    logits = jnp.where(mask[None, None, None], logits + bias[None, :, :, None], NEG_INF)
    sk = sink.astype(jnp.float32).reshape(1, A_KV_HEADS, grp, 1, 1, 1)
    m = jnp.maximum(jnp.max(logits, axis=-1, keepdims=True), sk)
    p = jnp.exp(logits - m)
    p = p / (jnp.sum(p, axis=-1, keepdims=True) + jnp.exp(sk - m))
    o = jnp.einsum('bhgnqk,bnkhd->bnqhgd', p.astype(v.dtype), vb)
    return o.reshape(b_, s, A_HEADS * A_HEAD_DIM)


def gla_scan(q, k, v, log_a, strict):
    b_, s, h, dk = q.shape
    dv = v.shape[-1]
    n = s // CHUNK

    def chunk(t):
        return t.reshape(b_, n, CHUNK, h, t.shape[-1])

    qc, kc, vc, ac = chunk(q), chunk(k), chunk(v), chunk(log_a)
    cum = jnp.cumsum(ac, axis=2)
    cum_last = cum[:, :, -1:]
    q_dec = qc * jnp.exp(cum)
    k_inv = kc * jnp.exp(-cum)
    k_end = kc * jnp.exp(cum_last - cum)
    scores = jnp.einsum('bnthd,bnshd->bnhts', q_dec, k_inv)
    tri = jnp.tril(jnp.ones((CHUNK, CHUNK), dtype=bool), k=-1 if strict else 0)
    o_intra = jnp.einsum('bnhts,bnshv->bnthv', jnp.where(tri, scores, 0.0), vc)
    upd = jnp.einsum('bnshd,bnshv->bnhdv', k_end, vc)
    decay = jnp.exp(cum_last[:, :, 0])

    def step(state, inp):
        d, u = inp
        return d[..., None] * state + u, state

    s0 = jnp.zeros((b_, h, dk, dv), jnp.float32)
    _, s_prev = lax.scan(step, s0, (jnp.swapaxes(decay, 0, 1), jnp.swapaxes(upd, 0, 1)))
    s_prev = jnp.swapaxes(s_prev, 0, 1)
    o_inter = jnp.einsum('bnthd,bnhdv->bnthv', q_dec, s_prev)
    return (o_intra + o_inter).reshape(b_, s, h, dv)


def bidir_gla(q, k, v, g, r, wr_f, br_f, wr_b, br_b, norm_g):
    b_, s, _ = q.shape
    f32 = jnp.float32
    qh = q.astype(f32).reshape(b_, s, B_HEADS, B_KEY_DIM) * (B_KEY_DIM ** -0.5)
    kh = k.astype(f32).reshape(b_, s, B_HEADS, B_KEY_DIM)
    vh = v.astype(f32).reshape(b_, s, B_HEADS, B_VAL_DIM)
    rf = r.astype(f32)
    la_f = (jax.nn.log_sigmoid(rf @ wr_f.astype(f32) + br_f.astype(f32)) / GATE_TAU
            ).reshape(b_, s, B_HEADS, B_KEY_DIM)
    la_b = (jax.nn.log_sigmoid(rf @ wr_b.astype(f32) + br_b.astype(f32)) / GATE_TAU
            ).reshape(b_, s, B_HEADS, B_KEY_DIM)
    o_fwd = gla_scan(qh, kh, vh, la_f, strict=False)
    flip = lambda t: jnp.flip(t, axis=1)
    o_bwd = flip(gla_scan(flip(qh), flip(kh), flip(vh), flip(la_b), strict=True))
    o = o_fwd + o_bwd
    o = o * lax.rsqrt(jnp.mean(o * o, axis=-1, keepdims=True) + EPS) * norm_g.astype(f32)
    o = o.reshape(b_, s, B_HEADS * B_VAL_DIM) * jax.nn.silu(g.astype(f32))
    return o.astype(q.dtype)


def token_mixer(xn, w_in, rel_bias, sink, wr_f, br_f, wr_b, br_b, gla_norm,
                w_branch_a, w_branch_b, w_out):
    b_, s, _ = xn.shape
    proj = xn @ w_in
    splits = [int(c) for c in np.cumsum(IN_SPLITS)[:-1]]
    aq, ak, av, bq, bk, bv, bg, br, gate_a, gate_b = jnp.split(proj, splits, axis=-1)
    ya = windowed_gqa(aq.reshape(b_, s, A_HEADS, A_HEAD_DIM),
                      ak.reshape(b_, s, A_KV_HEADS, A_HEAD_DIM),
                      av.reshape(b_, s, A_KV_HEADS, A_HEAD_DIM), rel_bias, sink)
    yb = bidir_gla(bq, bk, bv, bg, br, wr_f, br_f, wr_b, br_b, gla_norm)
    merged = jax.nn.sigmoid(gate_a) * (ya @ w_branch_a) + jax.nn.sigmoid(gate_b) * (yb @ w_branch_b)
    return merged @ w_out


def hier_moe(x, w_rg, b_rg, w_re, b_re, w_gate, w_up, w_down):
    b_, s, d = x.shape
    t = b_ * s
    xt = x.reshape(t, d)
    g_logits = (xt @ w_rg + b_rg).astype(jnp.float32)
    g_prob = jax.nn.softmax(g_logits, axis=-1)
    g_idx = jnp.argmax(g_logits, axis=-1)
    g_w = jnp.take_along_axis(g_prob, g_idx[:, None], axis=-1)
    e_logits = (xt @ w_re + b_re).astype(jnp.float32).reshape(t, N_GROUPS, EXPERTS_PER_GROUP)
    e_sel = jnp.take_along_axis(e_logits, g_idx[:, None, None], axis=1)[:, 0]
    top_v, top_i = lax.top_k(e_sel, TOP_K)
    top_w = jax.nn.softmax(top_v, axis=-1) * g_w
    within = jnp.sum(jax.nn.one_hot(top_i, EXPERTS_PER_GROUP, dtype=jnp.float32)
                     * top_w[..., None], axis=1)
    combine = jax.nn.one_hot(g_idx, N_GROUPS, dtype=jnp.float32)[:, :, None] * within[:, None, :]
    combine = combine.astype(x.dtype)
    out = jnp.zeros((t, d), x.dtype)
    for gi in range(N_GROUPS):
        h = jax.nn.silu(jnp.einsum('td,edf->tef', xt, w_gate[gi])) * jnp.einsum('td,edf->tef', xt, w_up[gi])
        out = out + jnp.einsum('tef,efd->td', h * combine[:, gi, :, None], w_down[gi])
    return out.reshape(b_, s, d)


def setup_inputs(seed: int = 0) -> dict:
    key = jax.random.key(seed)
    ks = jax.random.split(key, 22)
    f32 = jnp.float32

    def nrm(k, shape, scale):
        return jax.random.normal(k, shape, f32) * scale

    bk = B_HEADS * B_KEY_DIM
    return {
        "x": nrm(ks[0], (BATCH, SEQ, D_MODEL), 1.0),
        "w_in": nrm(ks[1], (DEPTH, D_MODEL, IN_WIDTH), D_MODEL ** -0.5),
        "rel_bias": nrm(ks[2], (N_BUCKETS, A_HEADS), 0.5),
        "attn_sink": nrm(ks[3], (DEPTH, A_HEADS), 0.5),
        "gla_wr_fwd": nrm(ks[4], (DEPTH, B_RANK, bk), B_RANK ** -0.5),
        "gla_br_fwd": nrm(ks[5], (DEPTH, bk), 0.1),
        "gla_wr_bwd": nrm(ks[6], (DEPTH, B_RANK, bk), B_RANK ** -0.5),
        "gla_br_bwd": nrm(ks[7], (DEPTH, bk), 0.1),
        "gla_norm": 1.0 + nrm(ks[8], (DEPTH, B_VAL_DIM), 0.01),
        "w_branch_a": nrm(ks[9], (DEPTH, A_HEADS * A_HEAD_DIM, D_MODEL), (A_HEADS * A_HEAD_DIM) ** -0.5),
        "w_branch_b": nrm(ks[10], (DEPTH, B_HEADS * B_VAL_DIM, D_MODEL), (B_HEADS * B_VAL_DIM) ** -0.5),
        "w_out": nrm(ks[11], (DEPTH, D_MODEL, D_MODEL), D_MODEL ** -0.5),
        "norm_mix": 1.0 + nrm(ks[12], (DEPTH, D_MODEL), 0.01),
        "norm_ffn": 1.0 + nrm(ks[13], (DEPTH, D_MODEL), 0.01),
        "router_group_w": nrm(ks[14], (DEPTH, D_MODEL, N_GROUPS), D_MODEL ** -0.5),
        "router_group_b": nrm(ks[15], (DEPTH, N_GROUPS), 0.01),
        "router_expert_w": nrm(ks[16], (DEPTH, D_MODEL, N_GROUPS * EXPERTS_PER_GROUP), D_MODEL ** -0.5),
        "router_expert_b": nrm(ks[17], (DEPTH, N_GROUPS * EXPERTS_PER_GROUP), 0.01),
        "expert_w_gate": nrm(ks[18], (DEPTH, N_GROUPS, EXPERTS_PER_GROUP, D_MODEL, D_EXPERT), D_MODEL ** -0.5),
        "expert_w_up": nrm(ks[19], (DEPTH, N_GROUPS, EXPERTS_PER_GROUP, D_MODEL, D_EXPERT), D_MODEL ** -0.5),
        "expert_w_down": nrm(ks[20], (DEPTH, N_GROUPS, EXPERTS_PER_GROUP, D_EXPERT, D_MODEL), D_EXPERT ** -0.5),
        "norm_final": 1.0 + nrm(ks[21], (D_MODEL,), 0.01),
    }


def reference(x, w_in, rel_bias, attn_sink, gla_wr_fwd, gla_br_fwd, gla_wr_bwd, gla_br_bwd,
              gla_norm, w_branch_a, w_branch_b, w_out, norm_mix, norm_ffn,
              router_group_w, router_group_b, router_expert_w, router_expert_b,
              expert_w_gate, expert_w_up, expert_w_down, norm_final):
    for l in range(DEPTH):
        xn = rms_norm(x, norm_mix[l])
        x = x + token_mixer(xn, w_in[l], rel_bias, attn_sink[l],
                            gla_wr_fwd[l], gla_br_fwd[l], gla_wr_bwd[l], gla_br_bwd[l],
                            gla_norm[l], w_branch_a[l], w_branch_b[l], w_out[l])
        hn = rms_norm(x, norm_ffn[l])
        x = x + hier_moe(hn, router_group_w[l], router_group_b[l], router_expert_w[l],
                         router_expert_b[l], expert_w_gate[l], expert_w_up[l], expert_w_down[l])
    return rms_norm(x, norm_final)
```

```python
import functools
import math

import numpy as np
import jax
import jax.numpy as jnp
from jax import lax
from jax.experimental import pallas as pl
from jax.experimental.pallas import tpu as pltpu

F32 = jnp.float32
BF16 = jnp.bfloat16

D_MODEL = 1024
A_HEADS = 8
A_KV_HEADS = 2
A_HEAD_DIM = 64
A_GROUP = A_HEADS // A_KV_HEADS
WINDOW = 128
BLOCK = 128
N_BUCKETS = 32
MAX_DISTANCE = 128
B_HEADS = 4
B_KEY_DIM = 64
B_VAL_DIM = 128
B_RANK = 16
GATE_TAU = 16.0
CHUNK = 64
N_GROUPS = 4
EXPERTS_PER_GROUP = 8
N_EXPERTS = N_GROUPS * EXPERTS_PER_GROUP
D_EXPERT = 256
EPS = 1e-6
NEG_INF = -1e30

LANE = 128
SUBLANE = 8
VMEM_LIMIT = 56 * 1024 * 1024

QA_W = A_HEADS * A_HEAD_DIM
KA_W = A_KV_HEADS * A_HEAD_DIM
QB_W = B_HEADS * B_KEY_DIM
VB_W = B_HEADS * B_VAL_DIM
COL_GA = 0
COL_GB = COL_GA + D_MODEL
COL_AQ = COL_GB + D_MODEL
COL_BV = COL_AQ + QA_W
COL_BG = COL_BV + VB_W
COL_BQ = COL_BG + VB_W
COL_BK = COL_BQ + QB_W
COL_AK = COL_BK + QB_W
COL_AV = COL_AK + KA_W
COL_R = COL_AV + KA_W
PROJ_W = COL_R + LANE

ROW_TILE = 512
PROJ_CHUNK = 640
GLA_TILE = 256
CPT = GLA_TILE // CHUNK
MOE_RT = 128
MOE_XS = MOE_RT + SUBLANE
MOE_U = 8
NCHUNK = D_MODEL // LANE


def _cparams(n_axes):
    return pltpu.CompilerParams(
        dimension_semantics=("arbitrary",) * n_axes, vmem_limit_bytes=VMEM_LIMIT)


def _inproj_kernel(x_ref, g_ref, w_ref, o_ref):
    x = x_ref[...]
    ms = jnp.mean(x * x, axis=-1, keepdims=True)
    xn = (x * lax.rsqrt(ms + EPS) * g_ref[...]).astype(BF16)
    for c in range(PROJ_W // PROJ_CHUNK):
        sl = slice(c * PROJ_CHUNK, (c + 1) * PROJ_CHUNK)
        o_ref[:, sl] = jnp.dot(xn, w_ref[:, sl], preferred_element_type=F32).astype(BF16)


def _inproj(x2, g, w_p):
    t = x2.shape[0]
    tm = min(ROW_TILE, t)
    return pl.pallas_call(
        _inproj_kernel,
        grid=(t // tm,),
        in_specs=[
            pl.BlockSpec((tm, D_MODEL), lambda i: (i, 0)),
            pl.BlockSpec((1, D_MODEL), lambda i: (0, 0)),
            pl.BlockSpec((D_MODEL, PROJ_W), lambda i: (0, 0)),
        ],
        out_specs=pl.BlockSpec((tm, PROJ_W), lambda i: (i, 0)),
        out_shape=jax.ShapeDtypeStruct((t, PROJ_W), BF16),
        compiler_params=_cparams(1),
        name="inproj",
    )(x2, g, w_p)


def _attn_kernel(bucket_ref, relb_ref, sink_ref, q_ref, kp_ref, kc_ref, kn_ref,
                 vp_ref, vc_ref, vn_ref, o_ref, bias_ref, *, nblk):
    b = pl.program_id(0)
    n = pl.program_id(1)
    kw = 3 * BLOCK

    @pl.when((b == 0) & (n == 0))
    def _build_bias():
        bucket = bucket_ref[...]
        row = lax.broadcasted_iota(jnp.int32, (BLOCK, kw), 0)
        col = lax.broadcasted_iota(jnp.int32, (BLOCK, kw), 1)
        win = jnp.abs(col - BLOCK - row) <= WINDOW
        for hq in range(A_HEADS):
            acc = jnp.zeros((BLOCK, kw), F32)
            for bb in range(N_BUCKETS):
                acc = jnp.where(bucket == bb, relb_ref[bb, hq], acc)
            g = hq % A_GROUP
            bias_ref[hq // A_GROUP, g * BLOCK:(g + 1) * BLOCK, :] = jnp.where(win, acc, NEG_INF)

    col1 = lax.broadcasted_iota(jnp.int32, (1, kw), 1)
    valid = ((col1 >= BLOCK) | (n > 0)) & ((col1 < 2 * BLOCK) | (n < nblk - 1))
    rowg = jnp.right_shift(
        lax.broadcasted_iota(jnp.int32, (A_GROUP * BLOCK, 1), 0), int(math.log2(BLOCK)))
    q = q_ref[...]
    outs = []
    for h in range(A_KV_HEADS):
        hs = slice(h * A_HEAD_DIM, (h + 1) * A_HEAD_DIM)
        k3 = jnp.concatenate([kp_ref[:, hs], kc_ref[:, hs], kn_ref[:, hs]], axis=0)
        v3 = jnp.concatenate([vp_ref[:, hs], vc_ref[:, hs], vn_ref[:, hs]], axis=0)
        q4 = jnp.concatenate(
            [q[:, (A_GROUP * h + g) * A_HEAD_DIM:(A_GROUP * h + g + 1) * A_HEAD_DIM]
             for g in range(A_GROUP)], axis=0)
        s = lax.dot_general(q4, k3, (((1,), (1,)), ((), ())), preferred_element_type=F32)
        s = s * (A_HEAD_DIM ** -0.5) + bias_ref[h]
        s = jnp.where(valid, s, NEG_INF)
        sk = jnp.zeros((A_GROUP * BLOCK, 1), F32)
        for g in range(A_GROUP):
            sk = jnp.where(rowg == g, sink_ref[A_GROUP * h + g], sk)
        m = jnp.maximum(jnp.max(s, axis=-1, keepdims=True), sk)
        p = jnp.exp(s - m)
        den = jnp.sum(p, axis=-1, keepdims=True) + jnp.exp(sk - m)
        o = jnp.dot(p.astype(BF16), v3, preferred_element_type=F32) / den
        for g in range(A_GROUP):
            outs.append(o[g * BLOCK:(g + 1) * BLOCK])
    o_ref[...] = jnp.concatenate(outs, axis=1).astype(BF16)


def _attention(proj, bucket, rel_bias, sink, bsz, seq):
    nblk = seq // BLOCK
    t = bsz * seq
    cq = COL_AQ // QA_W
    ck = COL_AK // KA_W
    cv = COL_AV // KA_W

    def prev(b, n):
        return b * nblk + jnp.maximum(n - 1, 0)

    def cur(b, n):
        return b * nblk + n

    def nxt(b, n):
        return b * nblk + jnp.minimum(n + 1, nblk - 1)

    smem = pl.BlockSpec(memory_space=pltpu.SMEM)
    kv_specs = [pl.BlockSpec((BLOCK, KA_W), functools.partial(lambda b, n, f, c: (f(b, n), c), f=f, c=c))
                for c in (ck, cv) for f in (prev, cur, nxt)]
    return pl.pallas_call(
        functools.partial(_attn_kernel, nblk=nblk),
        grid=(bsz, nblk),
        in_specs=[
            pl.BlockSpec((BLOCK, 3 * BLOCK), lambda b, n: (0, 0)),
            smem, smem,
            pl.BlockSpec((BLOCK, QA_W), lambda b, n: (b * nblk + n, cq)),
        ] + kv_specs,
        out_specs=pl.BlockSpec((BLOCK, QA_W), lambda b, n: (b * nblk + n, 0)),
        out_shape=jax.ShapeDtypeStruct((t, QA_W), BF16),
        scratch_shapes=[pltpu.VMEM((A_KV_HEADS, A_GROUP * BLOCK, 3 * BLOCK), F32)],
        compiler_params=_cparams(2),
        name="attn",
    )(bucket, rel_bias, sink, proj, proj, proj, proj, proj, proj, proj)


def _log_sigmoid(z):
    return jnp.minimum(z, 0.0) - jnp.log1p(jnp.exp(-jnp.abs(z)))


def _log_decay(r, wr, br):
    z = jnp.dot(r, wr, preferred_element_type=F32) + br
    return _log_sigmoid(z) / GATE_TAU


def _chunk_masks():
    row = lax.broadcasted_iota(jnp.int32, (GLA_TILE, GLA_TILE), 0)
    col = lax.broadcasted_iota(jnp.int32, (GLA_TILE, GLA_TILE), 1)
    sh = int(math.log2(CHUNK))
    same = jnp.right_shift(row, sh) == jnp.right_shift(col, sh)
    return same, same & (col <= row), same & (col >= row)


def _f32dot(a, b):
    return jnp.dot(a, b, preferred_element_type=F32, precision=lax.Precision.HIGHEST)


def _gla_state_dir(k_ref, v_ref, r_ref, wr_ref, br_ref, s_out_ref, state_ref, order, cum_mask, same):
    la = _log_decay(r_ref[...], wr_ref[...], br_ref[...])
    cum = _f32dot(cum_mask.astype(F32), la)
    tot = _f32dot(same.astype(F32), la)
    k_end = (k_ref[...].astype(F32) * jnp.exp(tot - cum)).astype(BF16)
    tot_t = tot.T
    v = v_ref[...]
    for c in order:
        rs = slice(c * CHUNK, (c + 1) * CHUNK)
        s_out_ref[0, c] = state_ref[...].astype(BF16)
        upd_all = lax.dot_general(k_end[rs], v[rs], (((0,), (0,)), ((), ())),
                                  preferred_element_type=F32)
        upd = jnp.concatenate(
            [upd_all[h * B_KEY_DIM:(h + 1) * B_KEY_DIM, h * B_VAL_DIM:(h + 1) * B_VAL_DIM]
             for h in range(B_HEADS)], axis=0)
        decay = jnp.exp(tot_t[:, c * CHUNK:c * CHUNK + 1])
        state_ref[...] = decay * state_ref[...] + upd


def _gla_state_kernel(kf_ref, vf_ref, rf_ref, kb_ref, vb_ref, rb_ref,
                      wrf_ref, brf_ref, wrb_ref, brb_ref,
                      sf_ref, sb_ref, stf_ref, stb_ref):
    @pl.when(pl.program_id(1) == 0)
    def _reset():
        stf_ref[...] = jnp.zeros_like(stf_ref)
        stb_ref[...] = jnp.zeros_like(stb_ref)

    same, lower, upper = _chunk_masks()
    _gla_state_dir(kf_ref, vf_ref, rf_ref, wrf_ref, brf_ref, sf_ref, stf_ref,
                   range(CPT), lower, same)
    _gla_state_dir(kb_ref, vb_ref, rb_ref, wrb_ref, brb_ref, sb_ref, stb_ref,
                   range(CPT - 1, -1, -1), upper, same)


def _gla_states(proj, wrf, brf, wrb, brb, bsz, seq):
    nt = seq // GLA_TILE
    nchunks = seq // CHUNK
    hk = B_HEADS * B_KEY_DIM
    ck, cv, cr = COL_BK // QB_W, COL_BV // VB_W, COL_R // LANE

    def fwd(b, i):
        return b * nt + i

    def bwd(b, i):
        return b * nt + (nt - 1 - i)

    def tile_specs(f):
        return [pl.BlockSpec((GLA_TILE, QB_W), lambda b, i: (f(b, i), ck)),
                pl.BlockSpec((GLA_TILE, VB_W), lambda b, i: (f(b, i), cv)),
                pl.BlockSpec((GLA_TILE, LANE), lambda b, i: (f(b, i), cr))]

    const = lambda b, i: (0, 0)
    out_sds = jax.ShapeDtypeStruct((bsz, nchunks, hk, B_VAL_DIM), BF16)
    return pl.pallas_call(
        _gla_state_kernel,
        grid=(bsz, nt),
        in_specs=tile_specs(fwd) + tile_specs(bwd) + [
            pl.BlockSpec((LANE, hk), const), pl.BlockSpec((1, hk), const),
            pl.BlockSpec((LANE, hk), const), pl.BlockSpec((1, hk), const)],
        out_specs=[pl.BlockSpec((1, CPT, hk, B_VAL_DIM), lambda b, i: (b, i, 0, 0)),
                   pl.BlockSpec((1, CPT, hk, B_VAL_DIM), lambda b, i: (b, nt - 1 - i, 0, 0))],
        out_shape=[out_sds, out_sds],
        scratch_shapes=[pltpu.VMEM((hk, B_VAL_DIM), F32), pltpu.VMEM((hk, B_VAL_DIM), F32)],
        compiler_params=_cparams(2),
        name="gla_state",
    )(proj, proj, proj, proj, proj, proj, wrf, brf, wrb, brb)


def _gla_out_kernel(q_ref, k_ref, v_ref, r_ref, g_ref, sf_ref, sb_ref,
                    wrf_ref, brf_ref, wrb_ref, brb_ref, ng_ref, o_ref):
    same, lower, upper = _chunk_masks()
    r = r_ref[...]
    cum_f = _f32dot(lower.astype(F32), _log_decay(r, wrf_ref[...], brf_ref[...]))
    cum_b = _f32dot(upper.astype(F32), _log_decay(r, wrb_ref[...], brb_ref[...]))
    q = q_ref[...].astype(F32) * (B_KEY_DIM ** -0.5)
    k = k_ref[...].astype(F32)
    qd_f = q * jnp.exp(cum_f)
    qd_b = q * jnp.exp(cum_b)
    ki_f = (k * jnp.exp(-cum_f)).astype(BF16)
    ki_b = (k * jnp.exp(-cum_b)).astype(BF16)
    lane_head = jnp.right_shift(
        lax.broadcasted_iota(jnp.int32, (1, B_HEADS * B_KEY_DIM), 1), int(math.log2(B_KEY_DIM)))

    def stack_heads(x):
        return jnp.concatenate(
            [jnp.where(lane_head == h, x, 0.0) for h in range(B_HEADS)], axis=0).astype(BF16)

    qs_f = stack_heads(qd_f)
    qs_b = stack_heads(qd_b)
    nt_dims = (((1,), (1,)), ((), ()))
    sc_f = lax.dot_general(qs_f, ki_f, nt_dims, preferred_element_type=F32)
    sc_b = lax.dot_general(qs_b, ki_b, nt_dims, preferred_element_type=F32)
    lower_s = jnp.concatenate([lower] * B_HEADS, axis=0)
    strict_upper_s = jnp.concatenate([upper & ~lower] * B_HEADS, axis=0)
    p = jnp.where(lower_s, sc_f, jnp.where(strict_upper_s, sc_b, 0.0)).astype(BF16)
    v = v_ref[...]
    inter = []
    for c in range(CPT):
        lhs_f = jnp.concatenate(
            [qs_f[h * GLA_TILE + c * CHUNK:h * GLA_TILE + (c + 1) * CHUNK] for h in range(B_HEADS)], axis=0)
        lhs_b = jnp.concatenate(
            [qs_b[h * GLA_TILE + c * CHUNK:h * GLA_TILE + (c + 1) * CHUNK] for h in range(B_HEADS)], axis=0)
        inter.append(jnp.dot(lhs_f, sf_ref[0, c], preferred_element_type=F32)
                     + jnp.dot(lhs_b, sb_ref[0, c], preferred_element_type=F32))
    ng = ng_ref[...]
    g = g_ref[...].astype(F32)
    for h in range(B_HEADS):
        vs = slice(h * B_VAL_DIM, (h + 1) * B_VAL_DIM)
        o = jnp.dot(p[h * GLA_TILE:(h + 1) * GLA_TILE], v[:, vs], preferred_element_type=F32)
        o = o + jnp.concatenate([inter[c][h * CHUNK:(h + 1) * CHUNK] for c in range(CPT)], axis=0)
        o = o * lax.rsqrt(jnp.mean(o * o, axis=-1, keepdims=True) + EPS) * ng
        gh = g[:, vs]
        o_ref[:, vs] = (o * (gh / (1.0 + jnp.exp(-gh)))).astype(BF16)


def _gla_out(proj, s_f, s_b, wrf, brf, wrb, brb, ng, bsz, seq):
    nt = seq // GLA_TILE
    t = bsz * seq
    hk = B_HEADS * B_KEY_DIM
    const = lambda b, i: (0, 0)
    row = lambda c: (lambda b, i: (b * nt + i, c))
    return pl.pallas_call(
        _gla_out_kernel,
        grid=(bsz, nt),
        in_specs=[
            pl.BlockSpec((GLA_TILE, QB_W), row(COL_BQ // QB_W)),
            pl.BlockSpec((GLA_TILE, QB_W), row(COL_BK // QB_W)),
            pl.BlockSpec((GLA_TILE, VB_W), row(COL_BV // VB_W)),
            pl.BlockSpec((GLA_TILE, LANE), row(COL_R // LANE)),
            pl.BlockSpec((GLA_TILE, VB_W), row(COL_BG // VB_W)),
            pl.BlockSpec((1, CPT, hk, B_VAL_DIM), lambda b, i: (b, i, 0, 0)),
            pl.BlockSpec((1, CPT, hk, B_VAL_DIM), lambda b, i: (b, i, 0, 0)),
            pl.BlockSpec((LANE, hk), const), pl.BlockSpec((1, hk), const),
            pl.BlockSpec((LANE, hk), const), pl.BlockSpec((1, hk), const),
            pl.BlockSpec((1, B_VAL_DIM), const),
        ],
        out_specs=pl.BlockSpec((GLA_TILE, VB_W), lambda b, i: (b * nt + i, 0)),
        out_shape=jax.ShapeDtypeStruct((t, VB_W), BF16),
        compiler_params=_cparams(2),
        name="gla_out",
    )(proj, proj, proj, proj, proj, s_f, s_b, wrf, brf, wrb, brb, ng)


def _sigmoid(x):
    return 1.0 / (1.0 + jnp.exp(-x))


def _merge_kernel(ya_ref, yb_ref, ga_ref, gb_ref, x_ref, wa_ref, wb_ref, wo_ref, o_ref):
    a = jnp.dot(ya_ref[...], wa_ref[...], preferred_element_type=F32)
    b = jnp.dot(yb_ref[...], wb_ref[...], preferred_element_type=F32)
    merged = _sigmoid(ga_ref[...].astype(F32)) * a + _sigmoid(gb_ref[...].astype(F32)) * b
    o_ref[...] = x_ref[...] + jnp.dot(merged.astype(BF16), wo_ref[...], preferred_element_type=F32)


def _merge(ya, yb, proj, x2, wa, wb, wo):
    t = x2.shape[0]
    tm = min(ROW_TILE, t)
    const = lambda i: (0, 0)
    return pl.pallas_call(
        _merge_kernel,
        grid=(t // tm,),
        in_specs=[
            pl.BlockSpec((tm, QA_W), lambda i: (i, 0)),
            pl.BlockSpec((tm, VB_W), lambda i: (i, 0)),
            pl.BlockSpec((tm, D_MODEL), lambda i: (i, COL_GA // D_MODEL)),
            pl.BlockSpec((tm, D_MODEL), lambda i: (i, COL_GB // D_MODEL)),
            pl.BlockSpec((tm, D_MODEL), lambda i: (i, 0)),
            pl.BlockSpec((QA_W, D_MODEL), const),
            pl.BlockSpec((VB_W, D_MODEL), const),
            pl.BlockSpec((D_MODEL, D_MODEL), const),
        ],
        out_specs=pl.BlockSpec((tm, D_MODEL), lambda i: (i, 0)),
        out_shape=jax.ShapeDtypeStruct((t, D_MODEL), F32),
        compiler_params=_cparams(1),
        name="merge",
    )(ya, yb, proj, proj, x2, wa, wb, wo)


def _router_kernel(x_ref, g_ref, w_ref, b_ref, route_ref, hn_ref):
    x = x_ref[...]
    tm = x.shape[0]
    ms = jnp.mean(x * x, axis=-1, keepdims=True)
    hn = x * lax.rsqrt(ms + EPS) * g_ref[...]
    for j in range(NCHUNK):
        hn_ref[pl.ds(j, tm, stride=NCHUNK), :] = hn[:, j * LANE:(j + 1) * LANE]
    logits = _f32dot(hn, w_ref[...]) + b_ref[...]
    lane = lax.broadcasted_iota(jnp.int32, logits.shape, 1).astype(F32)
    big = float(LANE)
    ninf = -jnp.inf
    gl = jnp.where(lane < N_GROUPS, logits, ninf)
    gmax = jnp.max(gl, axis=-1, keepdims=True)
    g_idx = jnp.min(jnp.where(gl == gmax, lane, big), axis=-1, keepdims=True)
    g_w = 1.0 / jnp.sum(jnp.exp(gl - gmax), axis=-1, keepdims=True)
    lo = float(N_GROUPS) + g_idx * float(EXPERTS_PER_GROUP)
    el = jnp.where((lane >= lo) & (lane < lo + EXPERTS_PER_GROUP), logits, ninf)
    v1 = jnp.max(el, axis=-1, keepdims=True)
    i1 = jnp.min(jnp.where(el == v1, lane, big), axis=-1, keepdims=True)
    el2 = jnp.where(lane == i1, ninf, el)
    v2 = jnp.max(el2, axis=-1, keepdims=True)
    i2 = jnp.min(jnp.where(el2 == v2, lane, big), axis=-1, keepdims=True)
    e21 = jnp.exp(v2 - v1)
    w1 = g_w / (1.0 + e21)
    w2 = g_w * e21 / (1.0 + e21)
    route = jnp.where(lane == 0.0, i1 - float(N_GROUPS),
                      jnp.where(lane == 1.0, i2 - float(N_GROUPS),
                                jnp.where(lane == 2.0, w1, jnp.where(lane == 3.0, w2, 0.0))))
    route_ref[...] = route


def _router(x2, g, w_r, b_r):
    t = x2.shape[0]
    tm = min(ROW_TILE, t)
    const = lambda i: (0, 0)
    return pl.pallas_call(
        _router_kernel,
        grid=(t // tm,),
        in_specs=[
            pl.BlockSpec((tm, D_MODEL), lambda i: (i, 0)),
            pl.BlockSpec((1, D_MODEL), const),
            pl.BlockSpec((D_MODEL, LANE), const),
            pl.BlockSpec((1, LANE), const),
        ],
        out_specs=[pl.BlockSpec((tm, LANE), lambda i: (i, 0)),
                   pl.BlockSpec((tm * NCHUNK, LANE), lambda i: (i, 0))],
        out_shape=[jax.ShapeDtypeStruct((t, LANE), F32),
                   jax.ShapeDtypeStruct((t * NCHUNK, LANE), F32)],
        compiler_params=_cparams(1),
        name="router",
    )(x2, g, w_r, b_r)


def _moe_tile_tokens(t):
    return min(2048, t)


def _moe_max_tiles(tt):
    return (2 * tt + N_EXPERTS * (MOE_RT - 1)) // MOE_RT


def _dispatch_plan(route, tt):
    t = route.shape[0]
    nb = t // tt
    nt = _moe_max_tiles(tt)
    p = 2 * tt
    e = route[:, 0:2].astype(jnp.int32).reshape(nb, p)
    w = route[:, 2:4].reshape(nb, p)
    tok = jnp.arange(p, dtype=jnp.int32) // 2
    onehot = (e[..., None] == jnp.arange(N_EXPERTS, dtype=jnp.int32)).astype(jnp.int32)
    csum = jnp.cumsum(onehot, axis=1)
    rank = jnp.sum((csum - onehot) * onehot, axis=-1)
    counts = csum[:, -1, :]
    tiles_e = (counts + MOE_RT - 1) // MOE_RT
    tiles_end = jnp.cumsum(tiles_e, axis=1)
    tile_off = tiles_end - tiles_e
    pos = jnp.take_along_axis(tile_off, e, axis=1) * MOE_RT + rank
    n_tiles = tiles_end[:, -1]
    bidx = jnp.arange(nb, dtype=jnp.int32)[:, None]
    slot = jnp.arange(nt * MOE_RT, dtype=jnp.int32)
    pad_dst = jnp.broadcast_to(tt + slot % MOE_RT, (nb, nt * MOE_RT))
    sidx = pad_dst.at[bidx, pos].set(jnp.broadcast_to(tok, (nb, p)))
    gidx = jnp.where(sidx >= tt, 0, sidx)
    slot_w = jnp.zeros((nb, nt * MOE_RT), F32).at[bidx, pos].set(w)
    tile_ids = jnp.arange(nt, dtype=jnp.int32)
    tile_e = jnp.sum((tile_ids[None, :, None] >= tiles_end[:, None, :]).astype(jnp.int32), axis=-1)
    last_e = jnp.take_along_axis(tile_e, jnp.maximum(n_tiles - 1, 0)[:, None], axis=1)
    tile_e = jnp.where(tile_ids[None, :] < n_tiles[:, None], tile_e, last_e)
    shape3 = (nb * nt, 1, MOE_RT)
    return (tile_e.reshape(-1), n_tiles.astype(jnp.int32),
            (gidx * NCHUNK).reshape(shape3), (sidx * NCHUNK).reshape(shape3), slot_w.reshape(shape3))


def _moe_kernel(te_ref, nt_ref, gidx_ref, sidx_ref, w_ref, hn_ref, x_ref, wg_ref, wu_ref, wd_ref,
                o_ref, acc_ref, xt_ref, yt_ref, *, tt):
    b = pl.program_id(0)
    j = pl.program_id(1)

    @pl.when(j == 0)
    def _zero():
        acc_ref[...] = jnp.zeros_like(acc_ref)

    @pl.when(j < nt_ref[b])
    def _tile():
        for mi in range(MOE_RT):
            src = pl.multiple_of(gidx_ref[0, 0, mi], NCHUNK)
            xt_ref[pl.ds(mi, NCHUNK, stride=MOE_XS), :] = hn_ref[pl.ds(src, NCHUNK), :]
        xs = jnp.concatenate(
            [xt_ref[c * MOE_XS:c * MOE_XS + MOE_RT, :] for c in range(NCHUNK)], axis=1).astype(BF16)
        gate = jnp.dot(xs, wg_ref[0], preferred_element_type=F32)
        up = jnp.dot(xs, wu_ref[0], preferred_element_type=F32)
        h = gate * _sigmoid(gate) * up
        eye = (lax.broadcasted_iota(jnp.int32, (MOE_RT, MOE_RT), 0)
               == lax.broadcasted_iota(jnp.int32, (MOE_RT, MOE_RT), 1))
        wcol = jnp.sum(jnp.where(eye, w_ref[0], 0.0), axis=1, keepdims=True)
        y = jnp.dot((h * wcol).astype(BF16), wd_ref[0], preferred_element_type=F32)
        for c in range(NCHUNK):
            yt_ref[c * MOE_XS:c * MOE_XS + MOE_RT, :] = y[:, c * LANE:(c + 1) * LANE]
        for m0 in range(0, MOE_RT, MOE_U):
            dst = [pl.multiple_of(sidx_ref[0, 0, m0 + u], NCHUNK) for u in range(MOE_U)]
            vals = [acc_ref[pl.ds(dst[u], NCHUNK), :]
                    + yt_ref[pl.ds(m0 + u, NCHUNK, stride=MOE_XS), :] for u in range(MOE_U)]
            for u in range(MOE_U):
                acc_ref[pl.ds(dst[u], NCHUNK), :] = vals[u]

    @pl.when(j == pl.num_programs(1) - 1)
    def _emit():
        for c in range(NCHUNK):
            o_ref[:, c * LANE:(c + 1) * LANE] = (
                x_ref[:, c * LANE:(c + 1) * LANE] + acc_ref[pl.ds(c, tt, stride=NCHUNK), :])


def _moe(x2, hn_g, plan, wg, wu, wd, tt):
    t = x2.shape[0]
    nb = t // tt
    nt = _moe_max_tiles(tt)
    tile_e, n_tiles, gidx, sidx, slot_w = plan
    smem_blk = pl.BlockSpec((1, 1, MOE_RT), lambda b, j, te, ntl: (b * nt + j, 0, 0),
                            memory_space=pltpu.SMEM)
    wsel = lambda b, j, te, ntl: (te[b * nt + j], 0, 0)
    tok = lambda b, j, te, ntl: (b, 0)
    grid_spec = pltpu.PrefetchScalarGridSpec(
        num_scalar_prefetch=2,
        grid=(nb, nt),
        in_specs=[
            smem_blk, smem_blk,
            pl.BlockSpec((1, 1, MOE_RT), lambda b, j, te, ntl: (b * nt + j, 0, 0)),
            pl.BlockSpec((tt * NCHUNK, LANE), tok, pipeline_mode=pl.Buffered(1)),
            pl.BlockSpec((tt, D_MODEL), tok, pipeline_mode=pl.Buffered(1)),
            pl.BlockSpec((1, D_MODEL, D_EXPERT), wsel),
            pl.BlockSpec((1, D_MODEL, D_EXPERT), wsel),
            pl.BlockSpec((1, D_EXPERT, D_MODEL), wsel),
        ],
        out_specs=pl.BlockSpec((tt, D_MODEL), tok),
        scratch_shapes=[
            pltpu.VMEM(((tt + MOE_RT) * NCHUNK, LANE), F32),
            pltpu.VMEM((NCHUNK * MOE_XS, LANE), F32),
            pltpu.VMEM((NCHUNK * MOE_XS, LANE), F32),
        ],
    )
    return pl.pallas_call(
        functools.partial(_moe_kernel, tt=tt),
        grid_spec=grid_spec,
        out_shape=jax.ShapeDtypeStruct((t, D_MODEL), F32),
        compiler_params=_cparams(2),
        name="moe",
    )(tile_e, n_tiles, gidx, sidx, slot_w, hn_g, x2, wg, wu, wd)


def _final_norm_kernel(x_ref, g_ref, o_ref):
    x = x_ref[...]
    ms = jnp.mean(x * x, axis=-1, keepdims=True)
    o_ref[...] = x * lax.rsqrt(ms + EPS) * g_ref[...]


def _final_norm(x2, g):
    t = x2.shape[0]
    tm = min(ROW_TILE, t)
    return pl.pallas_call(
        _final_norm_kernel,
        grid=(t // tm,),
        in_specs=[pl.BlockSpec((tm, D_MODEL), lambda i: (i, 0)),
                  pl.BlockSpec((1, D_MODEL), lambda i: (0, 0))],
        out_specs=pl.BlockSpec((tm, D_MODEL), lambda i: (i, 0)),
        out_shape=jax.ShapeDtypeStruct((t, D_MODEL), F32),
        compiler_params=_cparams(1),
        name="final_norm",
    )(x2, g)


def _t5_bucket(rel):
    nb = N_BUCKETS // 2
    max_exact = nb // 2
    base = jnp.where(rel > 0, nb, 0)
    n = jnp.abs(rel)
    large = max_exact + (jnp.log(jnp.maximum(n, 1).astype(jnp.float32) / max_exact)
                         / math.log(MAX_DISTANCE / max_exact) * (nb - max_exact)).astype(jnp.int32)
    large = jnp.minimum(large, nb - 1)
    return base + jnp.where(n < max_exact, n, large)


def _permute_w_in(w):
    o = np.cumsum([0, QA_W, KA_W, KA_W, QB_W, QB_W, VB_W, VB_W, B_RANK, D_MODEL, D_MODEL])
    aq, ak, av, bq, bk, bv, bg, br, ga, gb = [w[:, o[i]:o[i + 1]] for i in range(10)]
    pad = jnp.zeros((w.shape[0], LANE - B_RANK), w.dtype)
    return jnp.concatenate([ga, gb, aq, bv, bg, bq, bk, ak, av, br, pad], axis=1).astype(BF16)


def _pad_rank(wr):
    return jnp.pad(wr, ((0, LANE - B_RANK), (0, 0))).astype(BF16)


def kernel(x, w_in, rel_bias, attn_sink, gla_wr_fwd, gla_br_fwd, gla_wr_bwd, gla_br_bwd, gla_norm, w_branch_a, w_branch_b, w_out, norm_mix, norm_ffn, router_group_w, router_group_b, router_expert_w, router_expert_b, expert_w_gate, expert_w_up, expert_w_down, norm_final):
    bsz, seq, d = x.shape
    depth = w_in.shape[0]
    t = bsz * seq
    tt = _moe_tile_tokens(t)
    q_off = jnp.arange(BLOCK)
    k_off = jnp.arange(3 * BLOCK) - BLOCK
    bucket = _t5_bucket(k_off[None, :] - q_off[:, None]).astype(jnp.int32)
    x2 = x.reshape(t, d)
    for l in range(depth):
        proj = _inproj(x2, norm_mix[l][None, :], _permute_w_in(w_in[l]))
        ya = _attention(proj, bucket, rel_bias, attn_sink[l], bsz, seq)
        wrf, wrb = _pad_rank(gla_wr_fwd[l]), _pad_rank(gla_wr_bwd[l])
        brf, brb = gla_br_fwd[l][None, :], gla_br_bwd[l][None, :]
        s_f, s_b = _gla_states(proj, wrf, brf, wrb, brb, bsz, seq)
        yb = _gla_out(proj, s_f, s_b, wrf, brf, wrb, brb, gla_norm[l][None, :], bsz, seq)
        x2 = _merge(ya, yb, proj, x2, w_branch_a[l].astype(BF16), w_branch_b[l].astype(BF16),
                    w_out[l].astype(BF16))
        w_r = jnp.concatenate(
            [router_group_w[l], router_expert_w[l],
             jnp.zeros((d, LANE - N_GROUPS - N_EXPERTS), F32)], axis=1)
        b_r = jnp.concatenate(
            [router_group_b[l], router_expert_b[l],
             jnp.zeros((LANE - N_GROUPS - N_EXPERTS,), F32)])[None, :]
        route, hn_g = _router(x2, norm_ffn[l][None, :], w_r, b_r)
        plan = _dispatch_plan(route, tt)
        x2 = _moe(x2, hn_g, plan,
                  expert_w_gate[l].reshape(N_EXPERTS, d, D_EXPERT).astype(BF16),
                  expert_w_up[l].reshape(N_EXPERTS, d, D_EXPERT).astype(BF16),
                  expert_w_down[l].reshape(N_EXPERTS, D_EXPERT, d).astype(BF16), tt)
    return _final_norm(x2, norm_final[None, :]).reshape(bsz, seq, d)
```

```python
import functools
import math

import numpy as np
import jax
import jax.numpy as jnp
from jax import lax
from jax.experimental import pallas as pl
from jax.experimental.pallas import tpu as pltpu

F32 = jnp.float32
BF16 = jnp.bfloat16

D_MODEL = 1024
A_HEADS = 8
A_KV_HEADS = 2
A_HEAD_DIM = 64
A_GROUP = A_HEADS // A_KV_HEADS
WINDOW = 128
BLOCK = 128
N_BUCKETS = 32
MAX_DISTANCE = 128
B_HEADS = 4
B_KEY_DIM = 64
B_VAL_DIM = 128
B_RANK = 16
GATE_TAU = 16.0
CHUNK = 64
N_GROUPS = 4
EXPERTS_PER_GROUP = 8
N_EXPERTS = N_GROUPS * EXPERTS_PER_GROUP
D_EXPERT = 256
EPS = 1e-6
NEG_INF = -1e30

LANE = 128
SUBLANE = 8
VMEM_LIMIT = 56 * 1024 * 1024

QA_W = A_HEADS * A_HEAD_DIM
KA_W = A_KV_HEADS * A_HEAD_DIM
QB_W = B_HEADS * B_KEY_DIM
VB_W = B_HEADS * B_VAL_DIM
COL_GA = 0
COL_GB = COL_GA + D_MODEL
COL_AQ = COL_GB + D_MODEL
COL_BV = COL_AQ + QA_W
COL_BG = COL_BV + VB_W
COL_BQ = COL_BG + VB_W
COL_BK = COL_BQ + QB_W
COL_AK = COL_BK + QB_W
COL_AV = COL_AK + KA_W
COL_R = COL_AV + KA_W
PROJ_W = COL_R + LANE

ROW_TILE = 512
PROJ_CHUNK = 640
GLA_TILE = 256
CPT = GLA_TILE // CHUNK
MOE_RT = 128
RT_BITS = MOE_RT.bit_length() - 1
EXPERT_BITS = N_EXPERTS.bit_length() - 1
MOE_XS = MOE_RT + SUBLANE
MOE_U = 8
NCHUNK = D_MODEL // LANE


def _cparams(n_axes):
    return pltpu.CompilerParams(
        dimension_semantics=("arbitrary",) * n_axes, vmem_limit_bytes=VMEM_LIMIT)


def _unslab(acc_ref, rows):
    return jnp.concatenate(
        [acc_ref.at[0][pl.ds(c, rows, stride=NCHUNK), :] for c in range(NCHUNK)], axis=1)


def _slab_spec(tm, tt):
    spt = tt // tm
    return pl.BlockSpec((1, tm * NCHUNK, LANE), lambda i: (i // spt, i % spt, 0))


_W_IN_SRC = np.cumsum([0, QA_W, KA_W, KA_W, QB_W, QB_W, VB_W, VB_W, B_RANK, D_MODEL, D_MODEL])
_W_IN_SEGMENTS = tuple(zip(
    (COL_AQ, COL_AK, COL_AV, COL_BQ, COL_BK, COL_BV, COL_BG, COL_R, COL_GA, COL_GB),
    (int(s) for s in _W_IN_SRC[:-1]),
    (int(w) for w in np.diff(_W_IN_SRC))))
W_IN_COLS = int(_W_IN_SRC[-1])
W_ROWS_PER_COPY = 128


def _load_permuted_w_in(w_ref, wbf_ref):
    wbf_ref[:, COL_R:COL_R + LANE] = jnp.zeros((D_MODEL, LANE), BF16)
    for dst, src, width in _W_IN_SEGMENTS:
        lo = (src // LANE) * LANE
        hi = min(-(-(src + width) // LANE) * LANE, W_IN_COLS)
        for r0 in range(0, D_MODEL, W_ROWS_PER_COPY):
            rows = slice(r0, r0 + W_ROWS_PER_COPY)
            piece = w_ref[0, rows, lo:hi][:, src - lo:src - lo + width]
            wbf_ref[rows, dst:dst + width] = piece.astype(BF16)


def _inproj_kernel(*refs, has_moe):
    if has_moe:
        x_ref, acc_ref, g_ref, w_ref, o_ref, xnew_ref, wbf_ref = refs
    else:
        x_ref, g_ref, w_ref, o_ref, wbf_ref = refs

    @pl.when(pl.program_id(0) == 0)
    def _prep():
        _load_permuted_w_in(w_ref, wbf_ref)

    x = x_ref[...]
    if has_moe:
        x = x + _unslab(acc_ref, x.shape[0])
        xnew_ref[...] = x
    ms = jnp.mean(x * x, axis=-1, keepdims=True)
    xn = (x * lax.rsqrt(ms + EPS) * g_ref[0]).astype(BF16)
    for c in range(PROJ_W // PROJ_CHUNK):
        sl = slice(c * PROJ_CHUNK, (c + 1) * PROJ_CHUNK)
        o_ref[:, sl] = jnp.dot(xn, wbf_ref[:, sl], preferred_element_type=F32).astype(BF16)


def _inproj(x2, acc, g_all, w_all, layer, tt):
    t = x2.shape[0]
    tm = min(ROW_TILE, t)
    has_moe = acc is not None
    row = pl.BlockSpec((tm, D_MODEL), lambda i: (i, 0))
    in_specs = [row] + ([_slab_spec(tm, tt)] if has_moe else []) + [
        pl.BlockSpec((1, 1, D_MODEL), lambda i: (layer, 0, 0)),
        pl.BlockSpec((1, D_MODEL, W_IN_COLS), lambda i: (layer, 0, 0), pipeline_mode=pl.Buffered(1)),
    ]
    proj_spec = pl.BlockSpec((tm, PROJ_W), lambda i: (i, 0))
    proj_sds = jax.ShapeDtypeStruct((t, PROJ_W), BF16)
    out = pl.pallas_call(
        functools.partial(_inproj_kernel, has_moe=has_moe),
        grid=(t // tm,),
        in_specs=in_specs,
        out_specs=[proj_spec, row] if has_moe else proj_spec,
        out_shape=[proj_sds, jax.ShapeDtypeStruct((t, D_MODEL), F32)] if has_moe else proj_sds,
        scratch_shapes=[pltpu.VMEM((D_MODEL, PROJ_W), BF16)],
        compiler_params=_cparams(1),
        name="inproj",
    )(*([x2, acc] if has_moe else [x2]), g_all, w_all)
    return (out[0], out[1]) if has_moe else (out, x2)


def _attn_kernel(bucket_ref, relb_ref, sink_ref, q_ref, kp_ref, kc_ref, kn_ref,
                 vp_ref, vc_ref, vn_ref, o_ref, bias_ref, *, nblk):
    b = pl.program_id(0)
    n = pl.program_id(1)
    kw = 3 * BLOCK

    @pl.when((b == 0) & (n == 0))
    def _build_bias():
        bucket = bucket_ref[...]
        row = lax.broadcasted_iota(jnp.int32, (BLOCK, kw), 0)
        col = lax.broadcasted_iota(jnp.int32, (BLOCK, kw), 1)
        win = jnp.abs(col - BLOCK - row) <= WINDOW
        for hq in range(A_HEADS):
            acc = jnp.zeros((BLOCK, kw), F32)
            for bb in range(N_BUCKETS):
                acc = jnp.where(bucket == bb, relb_ref[bb, hq], acc)
            g = hq % A_GROUP
            bias_ref[hq // A_GROUP, g * BLOCK:(g + 1) * BLOCK, :] = jnp.where(win, acc, NEG_INF)

    col1 = lax.broadcasted_iota(jnp.int32, (1, kw), 1)
    valid = ((col1 >= BLOCK) | (n > 0)) & ((col1 < 2 * BLOCK) | (n < nblk - 1))
    rowg = jnp.right_shift(
        lax.broadcasted_iota(jnp.int32, (A_GROUP * BLOCK, 1), 0), int(math.log2(BLOCK)))
    q = q_ref[...]
    outs = []
    for h in range(A_KV_HEADS):
        hs = slice(h * A_HEAD_DIM, (h + 1) * A_HEAD_DIM)
        k3 = jnp.concatenate([kp_ref[:, hs], kc_ref[:, hs], kn_ref[:, hs]], axis=0)
        v3 = jnp.concatenate([vp_ref[:, hs], vc_ref[:, hs], vn_ref[:, hs]], axis=0)
        q4 = jnp.concatenate(
            [q[:, (A_GROUP * h + g) * A_HEAD_DIM:(A_GROUP * h + g + 1) * A_HEAD_DIM]
             for g in range(A_GROUP)], axis=0)
        s = lax.dot_general(q4, k3, (((1,), (1,)), ((), ())), preferred_element_type=F32)
        s = s * (A_HEAD_DIM ** -0.5) + bias_ref[h]
        s = jnp.where(valid, s, NEG_INF)
        sk = jnp.zeros((A_GROUP * BLOCK, 1), F32)
        for g in range(A_GROUP):
            sk = jnp.where(rowg == g, sink_ref[A_GROUP * h + g], sk)
        m = jnp.maximum(jnp.max(s, axis=-1, keepdims=True), sk)
        p = jnp.exp(s - m)
        den = jnp.sum(p, axis=-1, keepdims=True) + jnp.exp(sk - m)
        o = jnp.dot(p.astype(BF16), v3, preferred_element_type=F32) / den
        for g in range(A_GROUP):
            outs.append(o[g * BLOCK:(g + 1) * BLOCK])
    o_ref[...] = jnp.concatenate(outs, axis=1).astype(BF16)


def _attention(proj, bucket, rel_bias, sink, bsz, seq):
    nblk = seq // BLOCK
    t = bsz * seq
    cq = COL_AQ // QA_W
    ck = COL_AK // KA_W
    cv = COL_AV // KA_W

    def prev(b, n):
        return b * nblk + jnp.maximum(n - 1, 0)

    def cur(b, n):
        return b * nblk + n

    def nxt(b, n):
        return b * nblk + jnp.minimum(n + 1, nblk - 1)

    smem = pl.BlockSpec(memory_space=pltpu.SMEM)
    kv_specs = [pl.BlockSpec((BLOCK, KA_W), functools.partial(lambda b, n, f, c: (f(b, n), c), f=f, c=c))
                for c in (ck, cv) for f in (prev, cur, nxt)]
    return pl.pallas_call(
        functools.partial(_attn_kernel, nblk=nblk),
        grid=(bsz, nblk),
        in_specs=[
            pl.BlockSpec((BLOCK, 3 * BLOCK), lambda b, n: (0, 0)),
            smem, smem,
            pl.BlockSpec((BLOCK, QA_W), lambda b, n: (b * nblk + n, cq)),
        ] + kv_specs,
        out_specs=pl.BlockSpec((BLOCK, QA_W), lambda b, n: (b * nblk + n, 0)),
        out_shape=jax.ShapeDtypeStruct((t, QA_W), BF16),
        scratch_shapes=[pltpu.VMEM((A_KV_HEADS, A_GROUP * BLOCK, 3 * BLOCK), F32)],
        compiler_params=_cparams(2),
        name="attn",
    )(bucket, rel_bias, sink, proj, proj, proj, proj, proj, proj, proj)


def _log_sigmoid(z):
    return jnp.minimum(z, 0.0) - jnp.log1p(jnp.exp(-jnp.abs(z)))


def _log_decay(r, wr, br):
    z = jnp.dot(r, wr, preferred_element_type=F32) + br
    return _log_sigmoid(z) / GATE_TAU


def _chunk_masks():
    row = lax.broadcasted_iota(jnp.int32, (GLA_TILE, GLA_TILE), 0)
    col = lax.broadcasted_iota(jnp.int32, (GLA_TILE, GLA_TILE), 1)
    sh = int(math.log2(CHUNK))
    same = jnp.right_shift(row, sh) == jnp.right_shift(col, sh)
    return same, same & (col <= row), same & (col >= row)


def _f32dot(a, b):
    return jnp.dot(a, b, preferred_element_type=F32, precision=lax.Precision.HIGHEST)


def _gla_state_dir(k_ref, v_ref, r_ref, wr_ref, br_ref, s_out_ref, state_ref, order, cum_mask, same):
    la = _log_decay(r_ref[...], wr_ref[...], br_ref[...])
    cum = _f32dot(cum_mask.astype(F32), la)
    tot = _f32dot(same.astype(F32), la)
    k_end = (k_ref[...].astype(F32) * jnp.exp(tot - cum)).astype(BF16)
    tot_t = tot.T
    v = v_ref[...]
    for c in order:
        rs = slice(c * CHUNK, (c + 1) * CHUNK)
        s_out_ref[0, c] = state_ref[...].astype(BF16)
        upd_all = lax.dot_general(k_end[rs], v[rs], (((0,), (0,)), ((), ())),
                                  preferred_element_type=F32)
        upd = jnp.concatenate(
            [upd_all[h * B_KEY_DIM:(h + 1) * B_KEY_DIM, h * B_VAL_DIM:(h + 1) * B_VAL_DIM]
             for h in range(B_HEADS)], axis=0)
        decay = jnp.exp(tot_t[:, c * CHUNK:c * CHUNK + 1])
        state_ref[...] = decay * state_ref[...] + upd


def _gla_state_kernel(kf_ref, vf_ref, rf_ref, kb_ref, vb_ref, rb_ref,
                      wrf_ref, brf_ref, wrb_ref, brb_ref,
                      sf_ref, sb_ref, stf_ref, stb_ref):
    @pl.when(pl.program_id(1) == 0)
    def _reset():
        stf_ref[...] = jnp.zeros_like(stf_ref)
        stb_ref[...] = jnp.zeros_like(stb_ref)

    same, lower, upper = _chunk_masks()
    _gla_state_dir(kf_ref, vf_ref, rf_ref, wrf_ref, brf_ref, sf_ref, stf_ref,
                   range(CPT), lower, same)
    _gla_state_dir(kb_ref, vb_ref, rb_ref, wrb_ref, brb_ref, sb_ref, stb_ref,
                   range(CPT - 1, -1, -1), upper, same)


def _gla_states(proj, wrf, brf, wrb, brb, bsz, seq):
    nt = seq // GLA_TILE
    nchunks = seq // CHUNK
    hk = B_HEADS * B_KEY_DIM
    ck, cv, cr = COL_BK // QB_W, COL_BV // VB_W, COL_R // LANE

    def fwd(b, i):
        return b * nt + i

    def bwd(b, i):
        return b * nt + (nt - 1 - i)

    def tile_specs(f):
        return [pl.BlockSpec((GLA_TILE, QB_W), lambda b, i: (f(b, i), ck)),
                pl.BlockSpec((GLA_TILE, VB_W), lambda b, i: (f(b, i), cv)),
                pl.BlockSpec((GLA_TILE, LANE), lambda b, i: (f(b, i), cr))]

    const = lambda b, i: (0, 0)
    out_sds = jax.ShapeDtypeStruct((bsz, nchunks, hk, B_VAL_DIM), BF16)
    return pl.pallas_call(
        _gla_state_kernel,
        grid=(bsz, nt),
        in_specs=tile_specs(fwd) + tile_specs(bwd) + [
            pl.BlockSpec((LANE, hk), const), pl.BlockSpec((1, hk), const),
            pl.BlockSpec((LANE, hk), const), pl.BlockSpec((1, hk), const)],
        out_specs=[pl.BlockSpec((1, CPT, hk, B_VAL_DIM), lambda b, i: (b, i, 0, 0)),
                   pl.BlockSpec((1, CPT, hk, B_VAL_DIM), lambda b, i: (b, nt - 1 - i, 0, 0))],
        out_shape=[out_sds, out_sds],
        scratch_shapes=[pltpu.VMEM((hk, B_VAL_DIM), F32), pltpu.VMEM((hk, B_VAL_DIM), F32)],
        compiler_params=_cparams(2),
        name="gla_state",
    )(proj, proj, proj, proj, proj, proj, wrf, brf, wrb, brb)


def _gla_out_kernel(q_ref, k_ref, v_ref, r_ref, g_ref, sf_ref, sb_ref,
                    wrf_ref, brf_ref, wrb_ref, brb_ref, ng_ref, o_ref):
    same, lower, upper = _chunk_masks()
    r = r_ref[...]
    cum_f = _f32dot(lower.astype(F32), _log_decay(r, wrf_ref[...], brf_ref[...]))
    cum_b = _f32dot(upper.astype(F32), _log_decay(r, wrb_ref[...], brb_ref[...]))
    q = q_ref[...].astype(F32) * (B_KEY_DIM ** -0.5)
    k = k_ref[...].astype(F32)
    qd_f = q * jnp.exp(cum_f)
    qd_b = q * jnp.exp(cum_b)
    ki_f = (k * jnp.exp(-cum_f)).astype(BF16)
    ki_b = (k * jnp.exp(-cum_b)).astype(BF16)
    lane_head = jnp.right_shift(
        lax.broadcasted_iota(jnp.int32, (1, B_HEADS * B_KEY_DIM), 1), int(math.log2(B_KEY_DIM)))

    def stack_heads(x):
        return jnp.concatenate(
            [jnp.where(lane_head == h, x, 0.0) for h in range(B_HEADS)], axis=0).astype(BF16)

    qs_f = stack_heads(qd_f)
    qs_b = stack_heads(qd_b)
    nt_dims = (((1,), (1,)), ((), ()))
    sc_f = lax.dot_general(qs_f, ki_f, nt_dims, preferred_element_type=F32)
    sc_b = lax.dot_general(qs_b, ki_b, nt_dims, preferred_element_type=F32)
    lower_s = jnp.concatenate([lower] * B_HEADS, axis=0)
    strict_upper_s = jnp.concatenate([upper & ~lower] * B_HEADS, axis=0)
    p = jnp.where(lower_s, sc_f, jnp.where(strict_upper_s, sc_b, 0.0)).astype(BF16)
    v = v_ref[...]
    inter = []
    for c in range(CPT):
        lhs_f = jnp.concatenate(
            [qs_f[h * GLA_TILE + c * CHUNK:h * GLA_TILE + (c + 1) * CHUNK] for h in range(B_HEADS)], axis=0)
        lhs_b = jnp.concatenate(
            [qs_b[h * GLA_TILE + c * CHUNK:h * GLA_TILE + (c + 1) * CHUNK] for h in range(B_HEADS)], axis=0)
        inter.append(jnp.dot(lhs_f, sf_ref[0, c], preferred_element_type=F32)
                     + jnp.dot(lhs_b, sb_ref[0, c], preferred_element_type=F32))
    ng = ng_ref[...]
    g = g_ref[...].astype(F32)
    for h in range(B_HEADS):
        vs = slice(h * B_VAL_DIM, (h + 1) * B_VAL_DIM)
        o = jnp.dot(p[h * GLA_TILE:(h + 1) * GLA_TILE], v[:, vs], preferred_element_type=F32)
        o = o + jnp.concatenate([inter[c][h * CHUNK:(h + 1) * CHUNK] for c in range(CPT)], axis=0)
        o = o * lax.rsqrt(jnp.mean(o * o, axis=-1, keepdims=True) + EPS) * ng
        gh = g[:, vs]
        o_ref[:, vs] = (o * (gh / (1.0 + jnp.exp(-gh)))).astype(BF16)


def _gla_out(proj, s_f, s_b, wrf, brf, wrb, brb, ng, bsz, seq):
    nt = seq // GLA_TILE
    t = bsz * seq
    hk = B_HEADS * B_KEY_DIM
    const = lambda b, i: (0, 0)
    row = lambda c: (lambda b, i: (b * nt + i, c))
    return pl.pallas_call(
        _gla_out_kernel,
        grid=(bsz, nt),
        in_specs=[
            pl.BlockSpec((GLA_TILE, QB_W), row(COL_BQ // QB_W)),
            pl.BlockSpec((GLA_TILE, QB_W), row(COL_BK // QB_W)),
            pl.BlockSpec((GLA_TILE, VB_W), row(COL_BV // VB_W)),
            pl.BlockSpec((GLA_TILE, LANE), row(COL_R // LANE)),
            pl.BlockSpec((GLA_TILE, VB_W), row(COL_BG // VB_W)),
            pl.BlockSpec((1, CPT, hk, B_VAL_DIM), lambda b, i: (b, i, 0, 0)),
            pl.BlockSpec((1, CPT, hk, B_VAL_DIM), lambda b, i: (b, i, 0, 0)),
            pl.BlockSpec((LANE, hk), const), pl.BlockSpec((1, hk), const),
            pl.BlockSpec((LANE, hk), const), pl.BlockSpec((1, hk), const),
            pl.BlockSpec((1, B_VAL_DIM), const),
        ],
        out_specs=pl.BlockSpec((GLA_TILE, VB_W), lambda b, i: (b * nt + i, 0)),
        out_shape=jax.ShapeDtypeStruct((t, VB_W), BF16),
        compiler_params=_cparams(2),
        name="gla_out",
    )(proj, proj, proj, proj, proj, s_f, s_b, wrf, brf, wrb, brb, ng)


def _sigmoid(x):
    return 1.0 / (1.0 + jnp.exp(-x))


def _merge_kernel(ya_ref, yb_ref, ga_ref, gb_ref, x_ref, wa_ref, wb_ref, wo_ref, o_ref,
                  wa_bf, wb_bf, wo_bf):
    @pl.when(pl.program_id(0) == 0)
    def _prep():
        wa_bf[...] = wa_ref[0].astype(BF16)
        wb_bf[...] = wb_ref[0].astype(BF16)
        wo_bf[...] = wo_ref[0].astype(BF16)

    a = jnp.dot(ya_ref[...], wa_bf[...], preferred_element_type=F32)
    b = jnp.dot(yb_ref[...], wb_bf[...], preferred_element_type=F32)
    merged = _sigmoid(ga_ref[...].astype(F32)) * a + _sigmoid(gb_ref[...].astype(F32)) * b
    o_ref[...] = x_ref[...] + jnp.dot(merged.astype(BF16), wo_bf[...], preferred_element_type=F32)


def _merge(ya, yb, proj, x2, wa_all, wb_all, wo_all, layer):
    t = x2.shape[0]
    tm = min(ROW_TILE, t)
    wsel = lambda i: (layer, 0, 0)
    return pl.pallas_call(
        _merge_kernel,
        grid=(t // tm,),
        in_specs=[
            pl.BlockSpec((tm, QA_W), lambda i: (i, 0)),
            pl.BlockSpec((tm, VB_W), lambda i: (i, 0)),
            pl.BlockSpec((tm, D_MODEL), lambda i: (i, COL_GA // D_MODEL)),
            pl.BlockSpec((tm, D_MODEL), lambda i: (i, COL_GB // D_MODEL)),
            pl.BlockSpec((tm, D_MODEL), lambda i: (i, 0)),
            pl.BlockSpec((1, QA_W, D_MODEL), wsel, pipeline_mode=pl.Buffered(1)),
            pl.BlockSpec((1, VB_W, D_MODEL), wsel, pipeline_mode=pl.Buffered(1)),
            pl.BlockSpec((1, D_MODEL, D_MODEL), wsel, pipeline_mode=pl.Buffered(1)),
        ],
        out_specs=pl.BlockSpec((tm, D_MODEL), lambda i: (i, 0)),
        out_shape=jax.ShapeDtypeStruct((t, D_MODEL), F32),
        scratch_shapes=[pltpu.VMEM((QA_W, D_MODEL), BF16), pltpu.VMEM((VB_W, D_MODEL), BF16),
                        pltpu.VMEM((D_MODEL, D_MODEL), BF16)],
        compiler_params=_cparams(1),
        name="merge",
    )(ya, yb, proj, proj, x2, wa_all, wb_all, wo_all)


def _router_kernel(x_ref, g_ref, w_ref, b_ref, route_ref, hn_ref, cnt_ref, carry_ref, *, spt):
    @pl.when(pl.program_id(0) % spt == 0)
    def _reset():
        carry_ref[...] = jnp.zeros_like(carry_ref)

    x = x_ref[...]
    tm = x.shape[0]
    ms = jnp.mean(x * x, axis=-1, keepdims=True)
    hn = x * lax.rsqrt(ms + EPS) * g_ref[0]
    for j in range(NCHUNK):
        hn_ref[pl.ds(j, tm, stride=NCHUNK), :] = hn[:, j * LANE:(j + 1) * LANE]
    logits = _f32dot(hn, w_ref[...]) + b_ref[...]
    lane = lax.broadcasted_iota(jnp.int32, logits.shape, 1).astype(F32)
    big = float(LANE)
    ninf = -jnp.inf
    gl = jnp.where(lane < N_GROUPS, logits, ninf)
    gmax = jnp.max(gl, axis=-1, keepdims=True)
    g_idx = jnp.min(jnp.where(gl == gmax, lane, big), axis=-1, keepdims=True)
    g_w = 1.0 / jnp.sum(jnp.exp(gl - gmax), axis=-1, keepdims=True)
    lo = float(N_GROUPS) + g_idx * float(EXPERTS_PER_GROUP)
    el = jnp.where((lane >= lo) & (lane < lo + EXPERTS_PER_GROUP), logits, ninf)
    v1 = jnp.max(el, axis=-1, keepdims=True)
    i1 = jnp.min(jnp.where(el == v1, lane, big), axis=-1, keepdims=True)
    el2 = jnp.where(lane == i1, ninf, el)
    v2 = jnp.max(el2, axis=-1, keepdims=True)
    i2 = jnp.min(jnp.where(el2 == v2, lane, big), axis=-1, keepdims=True)
    e21 = jnp.exp(v2 - v1)
    w1 = g_w / (1.0 + e21)
    w2 = g_w * e21 / (1.0 + e21)
    e1 = i1 - float(N_GROUPS)
    e2 = i2 - float(N_GROUPS)
    onehot = ((lane == e1) | (lane == e2)).astype(F32)
    strict_lower = (lax.broadcasted_iota(jnp.int32, (tm, tm), 1)
                    < lax.broadcasted_iota(jnp.int32, (tm, tm), 0)).astype(BF16)
    before = jnp.dot(strict_lower, onehot.astype(BF16), preferred_element_type=F32) + carry_ref[0:1, :]
    rank1 = jnp.sum(jnp.where(lane == e1, before, 0.0), axis=-1, keepdims=True)
    rank2 = jnp.sum(jnp.where(lane == e2, before, 0.0), axis=-1, keepdims=True)
    total = carry_ref[0:1, :] + jnp.sum(onehot, axis=0, keepdims=True)
    carry_ref[...] = jnp.broadcast_to(total, carry_ref.shape)
    cnt_ref[0] = jnp.broadcast_to(total, carry_ref.shape)
    route = jnp.zeros_like(logits)
    for k, val in enumerate((e1, e2, w1, w2, rank1, rank2)):
        route = jnp.where(lane == float(k), val, route)
    route_ref[...] = route


def _router(x2, g_all, w_r, b_r, layer, tt):
    t = x2.shape[0]
    tm = min(ROW_TILE, t)
    spt = tt // tm
    const = lambda i: (0, 0)
    return pl.pallas_call(
        functools.partial(_router_kernel, spt=spt),
        grid=(t // tm,),
        in_specs=[
            pl.BlockSpec((tm, D_MODEL), lambda i: (i, 0)),
            pl.BlockSpec((1, 1, D_MODEL), lambda i: (layer, 0, 0)),
            pl.BlockSpec((D_MODEL, LANE), const),
            pl.BlockSpec((1, LANE), const),
        ],
        out_specs=[pl.BlockSpec((tm, LANE), lambda i: (i, 0)),
                   pl.BlockSpec((tm * NCHUNK, LANE), lambda i: (i, 0)),
                   pl.BlockSpec((1, SUBLANE, LANE), lambda i: (i // spt, 0, 0))],
        out_shape=[jax.ShapeDtypeStruct((t, LANE), F32),
                   jax.ShapeDtypeStruct((t * NCHUNK, LANE), F32),
                   jax.ShapeDtypeStruct((t // tt, SUBLANE, LANE), F32)],
        scratch_shapes=[pltpu.VMEM((SUBLANE, LANE), F32)],
        compiler_params=_cparams(1),
        name="router",
    )(x2, g_all, w_r, b_r)


def _moe_tile_tokens(t):
    return min(4096, t)


def _moe_max_slots(tt):
    return ((2 * tt + N_EXPERTS * (MOE_RT - 1)) // MOE_RT) * MOE_RT


def _pair_arrays(route, counts, tt):
    nb = route.shape[0] // tt
    p = 2 * tt
    e = route[:, 0:2].astype(jnp.int32)
    rank = route[:, 4:6].astype(jnp.int32)
    packed = (rank * N_EXPERTS + e).reshape(nb, 1, p)
    w = route[:, 2:4].reshape(nb, 1, p)
    cnt = counts[:, 0, :N_EXPERTS].astype(jnp.int32)
    padded = ((cnt + MOE_RT - 1) // MOE_RT) * MOE_RT
    off = jnp.cumsum(padded, axis=1) - padded
    return packed, w, off.reshape(-1), cnt.reshape(-1)


def _moe_kernel(off_ref, cnt_ref, pair_ref, pw_ref, hn_ref, wg_ref, wu_ref, wd_ref, out_hbm,
                acc_ref, xt_ref, yt_ref, wgu_bf, wd_bf, sidx_ref, sw_ref, sem, *, tt):
    b = pl.program_id(0)
    e = pl.program_id(1)
    n_pairs = 2 * tt
    dump0 = tt * NCHUNK

    @pl.when(e == 0)
    def _plan():
        acc_ref[...] = jnp.zeros_like(acc_ref)

        def place(i, carry):
            for u in range(MOE_U):
                p = i * MOE_U + u
                packed = pair_ref[0, 0, p]
                s = off_ref[b * N_EXPERTS + (packed & (N_EXPERTS - 1))] + (packed >> EXPERT_BITS)
                sidx_ref[s] = (p >> 1) * NCHUNK
                sw_ref[s] = pw_ref[0, 0, p]
            return carry
        lax.fori_loop(0, n_pairs // MOE_U, place, 0)

        def pad_expert(ex, carry):
            n = cnt_ref[b * N_EXPERTS + ex]
            base = off_ref[b * N_EXPERTS + ex]
            end = ((n + MOE_RT - 1) >> RT_BITS) << RT_BITS

            def pad_slot(s, c2):
                sidx_ref[base + s] = dump0 + (s & (MOE_RT - 1)) * NCHUNK
                sw_ref[base + s] = 0.0
                return c2
            lax.fori_loop(n, end, pad_slot, 0)
            return carry
        lax.fori_loop(0, N_EXPERTS, pad_expert, 0)

    n = cnt_ref[b * N_EXPERTS + e]
    base = off_ref[b * N_EXPERTS + e]
    wgu_bf[:, :D_EXPERT] = wg_ref[0].astype(BF16)
    wgu_bf[:, D_EXPERT:] = wu_ref[0].astype(BF16)
    wd_bf[...] = wd_ref[0].astype(BF16)

    def row_tile(r, carry):
        s0 = base + r * MOE_RT
        for mi in range(MOE_RT):
            s = sidx_ref[s0 + mi]
            src = pl.multiple_of(jnp.where(s >= dump0, 0, s), NCHUNK)
            xt_ref[pl.ds(mi, NCHUNK, stride=MOE_XS), :] = hn_ref[pl.ds(src, NCHUNK), :]
        xs = jnp.concatenate(
            [xt_ref[c * MOE_XS:c * MOE_XS + MOE_RT, :] for c in range(NCHUNK)], axis=1).astype(BF16)
        gu = jnp.dot(xs, wgu_bf[...], preferred_element_type=F32)
        gate = gu[:, :D_EXPERT]
        h = gate * _sigmoid(gate) * gu[:, D_EXPERT:]
        y = jnp.dot(h.astype(BF16), wd_bf[...], preferred_element_type=F32)
        for c in range(NCHUNK):
            yt_ref[c * MOE_XS:c * MOE_XS + MOE_RT, :] = y[:, c * LANE:(c + 1) * LANE]
        for m0 in range(0, MOE_RT, MOE_U):
            dst = [pl.multiple_of(sidx_ref[s0 + m0 + u], NCHUNK) for u in range(MOE_U)]
            vals = [acc_ref[pl.ds(dst[u], NCHUNK), :]
                    + sw_ref[s0 + m0 + u] * yt_ref[pl.ds(m0 + u, NCHUNK, stride=MOE_XS), :]
                    for u in range(MOE_U)]
            for u in range(MOE_U):
                acc_ref[pl.ds(dst[u], NCHUNK), :] = vals[u]
        return carry
    lax.fori_loop(0, (n + MOE_RT - 1) >> RT_BITS, row_tile, 0)

    @pl.when(e == N_EXPERTS - 1)
    def _emit():
        cp = pltpu.make_async_copy(acc_ref, out_hbm.at[b], sem)
        cp.start()
        cp.wait()


def _moe(hn_g, pairs, wg_all, wu_all, wd_all, layer, tt):
    packed, pw, off, cnt = pairs
    nb = packed.shape[0]
    n_pairs = 2 * tt
    acc_rows = (tt + MOE_RT) * NCHUNK
    n_slots = _moe_max_slots(tt)
    smem_blk = pl.BlockSpec((1, 1, n_pairs), lambda b, e, off_, cnt_: (b, 0, 0), memory_space=pltpu.SMEM)
    wsel = lambda b, e, off_, cnt_: (layer * N_EXPERTS + e, 0, 0)
    grid_spec = pltpu.PrefetchScalarGridSpec(
        num_scalar_prefetch=2,
        grid=(nb, N_EXPERTS),
        in_specs=[
            smem_blk, smem_blk,
            pl.BlockSpec((tt * NCHUNK, LANE), lambda b, e, off_, cnt_: (b, 0),
                         pipeline_mode=pl.Buffered(1)),
            pl.BlockSpec((1, D_MODEL, D_EXPERT), wsel),
            pl.BlockSpec((1, D_MODEL, D_EXPERT), wsel),
            pl.BlockSpec((1, D_EXPERT, D_MODEL), wsel),
        ],
        out_specs=pl.BlockSpec(memory_space=pl.ANY),
        scratch_shapes=[
            pltpu.VMEM((acc_rows, LANE), F32),
            pltpu.VMEM((NCHUNK * MOE_XS, LANE), F32),
            pltpu.VMEM((NCHUNK * MOE_XS, LANE), F32),
            pltpu.VMEM((D_MODEL, 2 * D_EXPERT), BF16),
            pltpu.VMEM((D_EXPERT, D_MODEL), BF16),
            pltpu.SMEM((n_slots,), jnp.int32),
            pltpu.SMEM((n_slots,), F32),
            pltpu.SemaphoreType.DMA,
        ],
    )
    return pl.pallas_call(
        functools.partial(_moe_kernel, tt=tt),
        grid_spec=grid_spec,
        out_shape=jax.ShapeDtypeStruct((nb, acc_rows, LANE), F32),
        compiler_params=_cparams(2),
        name="moe",
    )(off, cnt, packed, pw, hn_g, wg_all, wu_all, wd_all)


def _final_norm_kernel(x_ref, acc_ref, g_ref, o_ref):
    x = x_ref[...]
    x = x + _unslab(acc_ref, x.shape[0])
    ms = jnp.mean(x * x, axis=-1, keepdims=True)
    o_ref[...] = x * lax.rsqrt(ms + EPS) * g_ref[...]


def _final_norm(x2, acc, g, tt):
    t = x2.shape[0]
    tm = min(ROW_TILE, t)
    return pl.pallas_call(
        _final_norm_kernel,
        grid=(t // tm,),
        in_specs=[pl.BlockSpec((tm, D_MODEL), lambda i: (i, 0)),
                  _slab_spec(tm, tt),
                  pl.BlockSpec((1, D_MODEL), lambda i: (0, 0))],
        out_specs=pl.BlockSpec((tm, D_MODEL), lambda i: (i, 0)),
        out_shape=jax.ShapeDtypeStruct((t, D_MODEL), F32),
        compiler_params=_cparams(1),
        name="final_norm",
    )(x2, acc, g)


def _t5_bucket(rel):
    nb = N_BUCKETS // 2
    max_exact = nb // 2
    base = jnp.where(rel > 0, nb, 0)
    n = jnp.abs(rel)
    large = max_exact + (jnp.log(jnp.maximum(n, 1).astype(jnp.float32) / max_exact)
                         / math.log(MAX_DISTANCE / max_exact) * (nb - max_exact)).astype(jnp.int32)
    large = jnp.minimum(large, nb - 1)
    return base + jnp.where(n < max_exact, n, large)


def _pad_rank(wr):
    return jnp.pad(wr, ((0, LANE - B_RANK), (0, 0))).astype(BF16)


def kernel(x, w_in, rel_bias, attn_sink, gla_wr_fwd, gla_br_fwd, gla_wr_bwd, gla_br_bwd, gla_norm, w_branch_a, w_branch_b, w_out, norm_mix, norm_ffn, router_group_w, router_group_b, router_expert_w, router_expert_b, expert_w_gate, expert_w_up, expert_w_down, norm_final):
    bsz, seq, d = x.shape
    depth = w_in.shape[0]
    t = bsz * seq
    tt = _moe_tile_tokens(t)
    q_off = jnp.arange(BLOCK)
    k_off = jnp.arange(3 * BLOCK) - BLOCK
    bucket = _t5_bucket(k_off[None, :] - q_off[:, None]).astype(jnp.int32)
    x2 = x.reshape(t, d)
    acc = None
    norm_mix3 = norm_mix[:, None, :]
    norm_ffn3 = norm_ffn[:, None, :]
    wg_all = expert_w_gate.reshape(depth * N_EXPERTS, d, D_EXPERT)
    wu_all = expert_w_up.reshape(depth * N_EXPERTS, d, D_EXPERT)
    wd_all = expert_w_down.reshape(depth * N_EXPERTS, D_EXPERT, d)
    for l in range(depth):
        proj, x2 = _inproj(x2, acc, norm_mix3, w_in, l, tt)
        ya = _attention(proj, bucket, rel_bias, attn_sink[l], bsz, seq)
        wrf, wrb = _pad_rank(gla_wr_fwd[l]), _pad_rank(gla_wr_bwd[l])
        brf, brb = gla_br_fwd[l][None, :], gla_br_bwd[l][None, :]
        s_f, s_b = _gla_states(proj, wrf, brf, wrb, brb, bsz, seq)
        yb = _gla_out(proj, s_f, s_b, wrf, brf, wrb, brb, gla_norm[l][None, :], bsz, seq)
        x2 = _merge(ya, yb, proj, x2, w_branch_a, w_branch_b, w_out, l)
        w_r = jnp.concatenate(
            [router_group_w[l], router_expert_w[l],
             jnp.zeros((d, LANE - N_GROUPS - N_EXPERTS), F32)], axis=1)
        b_r = jnp.concatenate(
            [router_group_b[l], router_expert_b[l],
             jnp.zeros((LANE - N_GROUPS - N_EXPERTS,), F32)])[None, :]
        route, hn_g, counts = _router(x2, norm_ffn3, w_r, b_r, l, tt)
        acc = _moe(hn_g, _pair_arrays(route, counts, tt), wg_all, wu_all, wd_all, l, tt)
    return _final_norm(x2, acc, norm_final[None, :], tt).reshape(bsz, seq, d)
```

```python
import functools
import math

import numpy as np
import jax
import jax.numpy as jnp
from jax import lax
from jax.experimental import pallas as pl
from jax.experimental.pallas import tpu as pltpu

F32 = jnp.float32
BF16 = jnp.bfloat16

D_MODEL = 1024
A_HEADS = 8
A_KV_HEADS = 2
A_HEAD_DIM = 64
A_GROUP = A_HEADS // A_KV_HEADS
WINDOW = 128
BLOCK = 128
N_BUCKETS = 32
MAX_DISTANCE = 128
B_HEADS = 4
B_KEY_DIM = 64
B_VAL_DIM = 128
B_RANK = 16
GATE_TAU = 16.0
CHUNK = 64
N_GROUPS = 4
EXPERTS_PER_GROUP = 8
N_EXPERTS = N_GROUPS * EXPERTS_PER_GROUP
D_EXPERT = 256
EPS = 1e-6
NEG_INF = -1e30

LANE = 128
SUBLANE = 8
VMEM_LIMIT = 56 * 1024 * 1024

QA_W = A_HEADS * A_HEAD_DIM
KA_W = A_KV_HEADS * A_HEAD_DIM
QB_W = B_HEADS * B_KEY_DIM
VB_W = B_HEADS * B_VAL_DIM
COL_GA = 0
COL_GB = COL_GA + D_MODEL
COL_AQ = COL_GB + D_MODEL
COL_BV = COL_AQ + QA_W
COL_BG = COL_BV + VB_W
COL_BQ = COL_BG + VB_W
COL_BK = COL_BQ + QB_W
COL_AK = COL_BK + QB_W
COL_AV = COL_AK + KA_W
COL_R = COL_AV + KA_W
PROJ_W = COL_R + LANE

ROW_TILE = 512
PROJ_CHUNK = 640
GLA_TILE = 256
CPT = GLA_TILE // CHUNK
MOE_RT = 128
RT_BITS = MOE_RT.bit_length() - 1
EXPERT_BITS = N_EXPERTS.bit_length() - 1
MOE_XS = MOE_RT + 1
MOE_U = 8
NCHUNK = D_MODEL // LANE


def _cparams(n_axes):
    return pltpu.CompilerParams(
        dimension_semantics=("arbitrary",) * n_axes, vmem_limit_bytes=VMEM_LIMIT)


def _unslab(acc_ref, rows):
    return jnp.concatenate(
        [acc_ref.at[0][pl.ds(c, rows, stride=NCHUNK), :] for c in range(NCHUNK)], axis=1)


def _slab_spec(tm, tt):
    spt = tt // tm
    return pl.BlockSpec((1, tm * NCHUNK, LANE), lambda i: (i // spt, i % spt, 0))


_W_IN_SRC = np.cumsum([0, QA_W, KA_W, KA_W, QB_W, QB_W, VB_W, VB_W, B_RANK, D_MODEL, D_MODEL])
_W_IN_SEGMENTS = tuple(zip(
    (COL_AQ, COL_AK, COL_AV, COL_BQ, COL_BK, COL_BV, COL_BG, COL_R, COL_GA, COL_GB),
    (int(s) for s in _W_IN_SRC[:-1]),
    (int(w) for w in np.diff(_W_IN_SRC))))
W_IN_COLS = int(_W_IN_SRC[-1])
W_ROWS_PER_COPY = 128


def _load_permuted_w_in(w_ref, wbf_ref):
    wbf_ref[:, COL_R:COL_R + LANE] = jnp.zeros((D_MODEL, LANE), BF16)
    for dst, src, width in _W_IN_SEGMENTS:
        lo = (src // LANE) * LANE
        hi = min(-(-(src + width) // LANE) * LANE, W_IN_COLS)
        for r0 in range(0, D_MODEL, W_ROWS_PER_COPY):
            rows = slice(r0, r0 + W_ROWS_PER_COPY)
            piece = w_ref[0, rows, lo:hi][:, src - lo:src - lo + width]
            wbf_ref[rows, dst:dst + width] = piece.astype(BF16)


def _inproj_kernel(*refs, has_moe):
    if has_moe:
        x_ref, acc_ref, g_ref, w_ref, o_ref, xnew_ref, wbf_ref = refs
    else:
        x_ref, g_ref, w_ref, o_ref, wbf_ref = refs

    @pl.when(pl.program_id(0) == 0)
    def _prep():
        _load_permuted_w_in(w_ref, wbf_ref)

    x = x_ref[...]
    if has_moe:
        x = x + _unslab(acc_ref, x.shape[0])
        xnew_ref[...] = x
    ms = jnp.mean(x * x, axis=-1, keepdims=True)
    xn = (x * lax.rsqrt(ms + EPS) * g_ref[0]).astype(BF16)
    for c in range(PROJ_W // PROJ_CHUNK):
        sl = slice(c * PROJ_CHUNK, (c + 1) * PROJ_CHUNK)
        o_ref[:, sl] = jnp.dot(xn, wbf_ref[:, sl], preferred_element_type=F32).astype(BF16)


def _inproj(x2, acc, g_all, w_all, layer, tt):
    t = x2.shape[0]
    tm = min(ROW_TILE, t)
    has_moe = acc is not None
    row = pl.BlockSpec((tm, D_MODEL), lambda i: (i, 0))
    in_specs = [row] + ([_slab_spec(tm, tt)] if has_moe else []) + [
        pl.BlockSpec((1, 1, D_MODEL), lambda i: (layer, 0, 0)),
        pl.BlockSpec((1, D_MODEL, W_IN_COLS), lambda i: (layer, 0, 0), pipeline_mode=pl.Buffered(1)),
    ]
    proj_spec = pl.BlockSpec((tm, PROJ_W), lambda i: (i, 0))
    proj_sds = jax.ShapeDtypeStruct((t, PROJ_W), BF16)
    out = pl.pallas_call(
        functools.partial(_inproj_kernel, has_moe=has_moe),
        grid=(t // tm,),
        in_specs=in_specs,
        out_specs=[proj_spec, row] if has_moe else proj_spec,
        out_shape=[proj_sds, jax.ShapeDtypeStruct((t, D_MODEL), F32)] if has_moe else proj_sds,
        scratch_shapes=[pltpu.VMEM((D_MODEL, PROJ_W), BF16)],
        compiler_params=_cparams(1),
        name="inproj",
    )(*([x2, acc] if has_moe else [x2]), g_all, w_all)
    return (out[0], out[1]) if has_moe else (out, x2)


def _attn_kernel(bucket_ref, relb_ref, sink_ref, q_ref, kp_ref, kc_ref, kn_ref,
                 vp_ref, vc_ref, vn_ref, o_ref, bias_ref, *, nblk):
    b = pl.program_id(0)
    n = pl.program_id(1)
    kw = 3 * BLOCK

    @pl.when((b == 0) & (n == 0))
    def _build_bias():
        bucket = bucket_ref[...]
        row = lax.broadcasted_iota(jnp.int32, (BLOCK, kw), 0)
        col = lax.broadcasted_iota(jnp.int32, (BLOCK, kw), 1)
        win = jnp.abs(col - BLOCK - row) <= WINDOW
        for hq in range(A_HEADS):
            acc = jnp.zeros((BLOCK, kw), F32)
            for bb in range(N_BUCKETS):
                acc = jnp.where(bucket == bb, relb_ref[bb, hq], acc)
            g = hq % A_GROUP
            bias_ref[hq // A_GROUP, g * BLOCK:(g + 1) * BLOCK, :] = jnp.where(win, acc, NEG_INF)

    col1 = lax.broadcasted_iota(jnp.int32, (1, kw), 1)
    valid = ((col1 >= BLOCK) | (n > 0)) & ((col1 < 2 * BLOCK) | (n < nblk - 1))
    rowg = jnp.right_shift(
        lax.broadcasted_iota(jnp.int32, (A_GROUP * BLOCK, 1), 0), int(math.log2(BLOCK)))
    q = q_ref[...]
    outs = []
    for h in range(A_KV_HEADS):
        hs = slice(h * A_HEAD_DIM, (h + 1) * A_HEAD_DIM)
        k3 = jnp.concatenate([kp_ref[:, hs], kc_ref[:, hs], kn_ref[:, hs]], axis=0)
        v3 = jnp.concatenate([vp_ref[:, hs], vc_ref[:, hs], vn_ref[:, hs]], axis=0)
        q4 = jnp.concatenate(
            [q[:, (A_GROUP * h + g) * A_HEAD_DIM:(A_GROUP * h + g + 1) * A_HEAD_DIM]
             for g in range(A_GROUP)], axis=0)
        s = lax.dot_general(q4, k3, (((1,), (1,)), ((), ())), preferred_element_type=F32)
        s = s * (A_HEAD_DIM ** -0.5) + bias_ref[h]
        s = jnp.where(valid, s, NEG_INF)
        sk = jnp.zeros((A_GROUP * BLOCK, 1), F32)
        for g in range(A_GROUP):
            sk = jnp.where(rowg == g, sink_ref[A_GROUP * h + g], sk)
        m = jnp.maximum(jnp.max(s, axis=-1, keepdims=True), sk)
        p = jnp.exp(s - m)
        den = jnp.sum(p, axis=-1, keepdims=True) + jnp.exp(sk - m)
        o = jnp.dot(p.astype(BF16), v3, preferred_element_type=F32) / den
        for g in range(A_GROUP):
            outs.append(o[g * BLOCK:(g + 1) * BLOCK])
    o_ref[...] = jnp.concatenate(outs, axis=1).astype(BF16)


def _attention(proj, bucket, rel_bias, sink, bsz, seq):
    nblk = seq // BLOCK
    t = bsz * seq
    cq = COL_AQ // QA_W
    ck = COL_AK // KA_W
    cv = COL_AV // KA_W

    def prev(b, n):
        return b * nblk + jnp.maximum(n - 1, 0)

    def cur(b, n):
        return b * nblk + n

    def nxt(b, n):
        return b * nblk + jnp.minimum(n + 1, nblk - 1)

    smem = pl.BlockSpec(memory_space=pltpu.SMEM)
    kv_specs = [pl.BlockSpec((BLOCK, KA_W), functools.partial(lambda b, n, f, c: (f(b, n), c), f=f, c=c))
                for c in (ck, cv) for f in (prev, cur, nxt)]
    return pl.pallas_call(
        functools.partial(_attn_kernel, nblk=nblk),
        grid=(bsz, nblk),
        in_specs=[
            pl.BlockSpec((BLOCK, 3 * BLOCK), lambda b, n: (0, 0)),
            smem, smem,
            pl.BlockSpec((BLOCK, QA_W), lambda b, n: (b * nblk + n, cq)),
        ] + kv_specs,
        out_specs=pl.BlockSpec((BLOCK, QA_W), lambda b, n: (b * nblk + n, 0)),
        out_shape=jax.ShapeDtypeStruct((t, QA_W), BF16),
        scratch_shapes=[pltpu.VMEM((A_KV_HEADS, A_GROUP * BLOCK, 3 * BLOCK), F32)],
        compiler_params=_cparams(2),
        name="attn",
    )(bucket, rel_bias, sink, proj, proj, proj, proj, proj, proj, proj)


def _log_sigmoid(z):
    return jnp.minimum(z, 0.0) - jnp.log1p(jnp.exp(-jnp.abs(z)))


def _log_decay(r, wr, br):
    z = jnp.dot(r, wr, preferred_element_type=F32) + br
    return _log_sigmoid(z) / GATE_TAU


def _chunk_masks():
    row = lax.broadcasted_iota(jnp.int32, (GLA_TILE, GLA_TILE), 0)
    col = lax.broadcasted_iota(jnp.int32, (GLA_TILE, GLA_TILE), 1)
    sh = int(math.log2(CHUNK))
    same = jnp.right_shift(row, sh) == jnp.right_shift(col, sh)
    return same, same & (col <= row), same & (col >= row)


def _f32dot(a, b):
    return jnp.dot(a, b, preferred_element_type=F32, precision=lax.Precision.HIGHEST)


def _split3(x):
    hi = x.astype(BF16)
    r1 = x - hi.astype(F32)
    mid = r1.astype(BF16)
    lo = (r1 - mid.astype(F32)).astype(BF16)
    return hi, mid, lo


def _mask_dot(mask, x):
    m = mask.astype(BF16)
    hi, mid, lo = _split3(x)
    return (jnp.dot(m, hi, preferred_element_type=F32) + jnp.dot(m, mid, preferred_element_type=F32)
            + jnp.dot(m, lo, preferred_element_type=F32))


def _gla_state_dir(k_ref, v_ref, r_ref, wr_ref, br_ref, s_out_ref, state_ref, order, cum_mask, same):
    la = _log_decay(r_ref[...], wr_ref[...], br_ref[...])
    both = _mask_dot(jnp.concatenate([cum_mask, same], axis=0), la)
    cum = both[:GLA_TILE]
    tot = both[GLA_TILE:]
    k_end = (k_ref[...].astype(F32) * jnp.exp(tot - cum)).astype(BF16)
    tot_t = tot.T
    v = v_ref[...]
    for c in order:
        rs = slice(c * CHUNK, (c + 1) * CHUNK)
        s_out_ref[0, c] = state_ref[...].astype(BF16)
        upd_all = lax.dot_general(k_end[rs], v[rs], (((0,), (0,)), ((), ())),
                                  preferred_element_type=F32)
        upd = jnp.concatenate(
            [upd_all[h * B_KEY_DIM:(h + 1) * B_KEY_DIM, h * B_VAL_DIM:(h + 1) * B_VAL_DIM]
             for h in range(B_HEADS)], axis=0)
        decay = jnp.exp(tot_t[:, c * CHUNK:c * CHUNK + 1])
        state_ref[...] = decay * state_ref[...] + upd


def _gla_state_kernel(kf_ref, vf_ref, rf_ref, kb_ref, vb_ref, rb_ref,
                      wrf_ref, brf_ref, wrb_ref, brb_ref,
                      sf_ref, sb_ref, stf_ref, stb_ref):
    @pl.when(pl.program_id(1) == 0)
    def _reset():
        stf_ref[...] = jnp.zeros_like(stf_ref)
        stb_ref[...] = jnp.zeros_like(stb_ref)

    same, lower, upper = _chunk_masks()
    _gla_state_dir(kf_ref, vf_ref, rf_ref, wrf_ref, brf_ref, sf_ref, stf_ref,
                   range(CPT), lower, same)
    _gla_state_dir(kb_ref, vb_ref, rb_ref, wrb_ref, brb_ref, sb_ref, stb_ref,
                   range(CPT - 1, -1, -1), upper, same)


def _gla_states(proj, wrf, brf, wrb, brb, bsz, seq):
    nt = seq // GLA_TILE
    nchunks = seq // CHUNK
    hk = B_HEADS * B_KEY_DIM
    ck, cv, cr = COL_BK // QB_W, COL_BV // VB_W, COL_R // LANE

    def fwd(b, i):
        return b * nt + i

    def bwd(b, i):
        return b * nt + (nt - 1 - i)

    def tile_specs(f):
        return [pl.BlockSpec((GLA_TILE, QB_W), lambda b, i: (f(b, i), ck)),
                pl.BlockSpec((GLA_TILE, VB_W), lambda b, i: (f(b, i), cv)),
                pl.BlockSpec((GLA_TILE, LANE), lambda b, i: (f(b, i), cr))]

    const = lambda b, i: (0, 0)
    out_sds = jax.ShapeDtypeStruct((bsz, nchunks, hk, B_VAL_DIM), BF16)
    return pl.pallas_call(
        _gla_state_kernel,
        grid=(bsz, nt),
        in_specs=tile_specs(fwd) + tile_specs(bwd) + [
            pl.BlockSpec((LANE, hk), const), pl.BlockSpec((1, hk), const),
            pl.BlockSpec((LANE, hk), const), pl.BlockSpec((1, hk), const)],
        out_specs=[pl.BlockSpec((1, CPT, hk, B_VAL_DIM), lambda b, i: (b, i, 0, 0)),
                   pl.BlockSpec((1, CPT, hk, B_VAL_DIM), lambda b, i: (b, nt - 1 - i, 0, 0))],
        out_shape=[out_sds, out_sds],
        scratch_shapes=[pltpu.VMEM((hk, B_VAL_DIM), F32), pltpu.VMEM((hk, B_VAL_DIM), F32)],
        compiler_params=_cparams(2),
        name="gla_state",
    )(proj, proj, proj, proj, proj, proj, wrf, brf, wrb, brb)


def _gla_out_kernel(q_ref, k_ref, v_ref, r_ref, g_ref, sf_ref, sb_ref,
                    wrf_ref, brf_ref, wrb_ref, brb_ref, ng_ref, o_ref):
    same, lower, upper = _chunk_masks()
    r = r_ref[...]
    cum_f = _mask_dot(lower, _log_decay(r, wrf_ref[...], brf_ref[...]))
    cum_b = _mask_dot(upper, _log_decay(r, wrb_ref[...], brb_ref[...]))
    q = q_ref[...].astype(F32) * (B_KEY_DIM ** -0.5)
    k = k_ref[...].astype(F32)
    qd_f = q * jnp.exp(cum_f)
    qd_b = q * jnp.exp(cum_b)
    ki_f = (k * jnp.exp(-cum_f)).astype(BF16)
    ki_b = (k * jnp.exp(-cum_b)).astype(BF16)
    lane_head = jnp.right_shift(
        lax.broadcasted_iota(jnp.int32, (1, B_HEADS * B_KEY_DIM), 1), int(math.log2(B_KEY_DIM)))

    def stack_heads(x):
        return jnp.concatenate(
            [jnp.where(lane_head == h, x, 0.0) for h in range(B_HEADS)], axis=0).astype(BF16)

    qs_f = stack_heads(qd_f)
    qs_b = stack_heads(qd_b)
    nt_dims = (((1,), (1,)), ((), ()))
    sc_f = lax.dot_general(qs_f, ki_f, nt_dims, preferred_element_type=F32)
    sc_b = lax.dot_general(qs_b, ki_b, nt_dims, preferred_element_type=F32)
    lower_s = jnp.concatenate([lower] * B_HEADS, axis=0)
    strict_upper_s = jnp.concatenate([upper & ~lower] * B_HEADS, axis=0)
    p = jnp.where(lower_s, sc_f, jnp.where(strict_upper_s, sc_b, 0.0)).astype(BF16)
    v = v_ref[...]
    inter = []
    for c in range(CPT):
        lhs_f = jnp.concatenate(
            [qs_f[h * GLA_TILE + c * CHUNK:h * GLA_TILE + (c + 1) * CHUNK] for h in range(B_HEADS)], axis=0)
        lhs_b = jnp.concatenate(
            [qs_b[h * GLA_TILE + c * CHUNK:h * GLA_TILE + (c + 1) * CHUNK] for h in range(B_HEADS)], axis=0)
        inter.append(jnp.dot(lhs_f, sf_ref[0, c], preferred_element_type=F32)
                     + jnp.dot(lhs_b, sb_ref[0, c], preferred_element_type=F32))
    ng = ng_ref[...]
    g = g_ref[...].astype(F32)
    for h in range(B_HEADS):
        vs = slice(h * B_VAL_DIM, (h + 1) * B_VAL_DIM)
        o = jnp.dot(p[h * GLA_TILE:(h + 1) * GLA_TILE], v[:, vs], preferred_element_type=F32)
        o = o + jnp.concatenate([inter[c][h * CHUNK:(h + 1) * CHUNK] for c in range(CPT)], axis=0)
        o = o * lax.rsqrt(jnp.mean(o * o, axis=-1, keepdims=True) + EPS) * ng
        gh = g[:, vs]
        o_ref[:, vs] = (o * (gh / (1.0 + jnp.exp(-gh)))).astype(BF16)


def _gla_out(proj, s_f, s_b, wrf, brf, wrb, brb, ng, bsz, seq):
    nt = seq // GLA_TILE
    t = bsz * seq
    hk = B_HEADS * B_KEY_DIM
    const = lambda b, i: (0, 0)
    row = lambda c: (lambda b, i: (b * nt + i, c))
    return pl.pallas_call(
        _gla_out_kernel,
        grid=(bsz, nt),
        in_specs=[
            pl.BlockSpec((GLA_TILE, QB_W), row(COL_BQ // QB_W)),
            pl.BlockSpec((GLA_TILE, QB_W), row(COL_BK // QB_W)),
            pl.BlockSpec((GLA_TILE, VB_W), row(COL_BV // VB_W)),
            pl.BlockSpec((GLA_TILE, LANE), row(COL_R // LANE)),
            pl.BlockSpec((GLA_TILE, VB_W), row(COL_BG // VB_W)),
            pl.BlockSpec((1, CPT, hk, B_VAL_DIM), lambda b, i: (b, i, 0, 0)),
            pl.BlockSpec((1, CPT, hk, B_VAL_DIM), lambda b, i: (b, i, 0, 0)),
            pl.BlockSpec((LANE, hk), const), pl.BlockSpec((1, hk), const),
            pl.BlockSpec((LANE, hk), const), pl.BlockSpec((1, hk), const),
            pl.BlockSpec((1, B_VAL_DIM), const),
        ],
        out_specs=pl.BlockSpec((GLA_TILE, VB_W), lambda b, i: (b * nt + i, 0)),
        out_shape=jax.ShapeDtypeStruct((t, VB_W), BF16),
        compiler_params=_cparams(2),
        name="gla_out",
    )(proj, proj, proj, proj, proj, s_f, s_b, wrf, brf, wrb, brb, ng)


def _sigmoid(x):
    return 1.0 / (1.0 + jnp.exp(-x))


def _merge_kernel(ya_ref, yb_ref, ga_ref, gb_ref, x_ref, wa_ref, wb_ref, wo_ref, o_ref,
                  wa_bf, wb_bf, wo_bf):
    @pl.when(pl.program_id(0) == 0)
    def _prep():
        wa_bf[...] = wa_ref[0].astype(BF16)
        wb_bf[...] = wb_ref[0].astype(BF16)
        wo_bf[...] = wo_ref[0].astype(BF16)

    a = jnp.dot(ya_ref[...], wa_bf[...], preferred_element_type=F32)
    b = jnp.dot(yb_ref[...], wb_bf[...], preferred_element_type=F32)
    merged = _sigmoid(ga_ref[...].astype(F32)) * a + _sigmoid(gb_ref[...].astype(F32)) * b
    o_ref[...] = x_ref[...] + jnp.dot(merged.astype(BF16), wo_bf[...], preferred_element_type=F32)


def _merge(ya, yb, proj, x2, wa_all, wb_all, wo_all, layer):
    t = x2.shape[0]
    tm = min(ROW_TILE, t)
    wsel = lambda i: (layer, 0, 0)
    return pl.pallas_call(
        _merge_kernel,
        grid=(t // tm,),
        in_specs=[
            pl.BlockSpec((tm, QA_W), lambda i: (i, 0)),
            pl.BlockSpec((tm, VB_W), lambda i: (i, 0)),
            pl.BlockSpec((tm, D_MODEL), lambda i: (i, COL_GA // D_MODEL)),
            pl.BlockSpec((tm, D_MODEL), lambda i: (i, COL_GB // D_MODEL)),
            pl.BlockSpec((tm, D_MODEL), lambda i: (i, 0)),
            pl.BlockSpec((1, QA_W, D_MODEL), wsel, pipeline_mode=pl.Buffered(1)),
            pl.BlockSpec((1, VB_W, D_MODEL), wsel, pipeline_mode=pl.Buffered(1)),
            pl.BlockSpec((1, D_MODEL, D_MODEL), wsel, pipeline_mode=pl.Buffered(1)),
        ],
        out_specs=pl.BlockSpec((tm, D_MODEL), lambda i: (i, 0)),
        out_shape=jax.ShapeDtypeStruct((t, D_MODEL), F32),
        scratch_shapes=[pltpu.VMEM((QA_W, D_MODEL), BF16), pltpu.VMEM((VB_W, D_MODEL), BF16),
                        pltpu.VMEM((D_MODEL, D_MODEL), BF16)],
        compiler_params=_cparams(1),
        name="merge",
    )(ya, yb, proj, proj, x2, wa_all, wb_all, wo_all)


def _router_kernel(x_ref, g_ref, w_ref, b_ref, route_ref, hn_ref, cnt_ref, carry_ref, *, spt):
    @pl.when(pl.program_id(0) % spt == 0)
    def _reset():
        carry_ref[...] = jnp.zeros_like(carry_ref)

    x = x_ref[...]
    tm = x.shape[0]
    ms = jnp.mean(x * x, axis=-1, keepdims=True)
    hn = x * lax.rsqrt(ms + EPS) * g_ref[0]
    for j in range(NCHUNK):
        hn_ref[pl.ds(j, tm, stride=NCHUNK), :] = hn[:, j * LANE:(j + 1) * LANE]
    logits = jnp.dot(hn.astype(BF16), w_ref[...].astype(BF16),
                     preferred_element_type=F32) + b_ref[...]
    lane = lax.broadcasted_iota(jnp.int32, logits.shape, 1).astype(F32)
    big = float(LANE)
    ninf = -jnp.inf
    gl = jnp.where(lane < N_GROUPS, logits, ninf)
    gmax = jnp.max(gl, axis=-1, keepdims=True)
    g_idx = jnp.min(jnp.where(gl == gmax, lane, big), axis=-1, keepdims=True)
    g_w = 1.0 / jnp.sum(jnp.exp(gl - gmax), axis=-1, keepdims=True)
    lo = float(N_GROUPS) + g_idx * float(EXPERTS_PER_GROUP)
    el = jnp.where((lane >= lo) & (lane < lo + EXPERTS_PER_GROUP), logits, ninf)
    v1 = jnp.max(el, axis=-1, keepdims=True)
    i1 = jnp.min(jnp.where(el == v1, lane, big), axis=-1, keepdims=True)
    el2 = jnp.where(lane == i1, ninf, el)
    v2 = jnp.max(el2, axis=-1, keepdims=True)
    i2 = jnp.min(jnp.where(el2 == v2, lane, big), axis=-1, keepdims=True)
    e21 = jnp.exp(v2 - v1)
    w1 = g_w / (1.0 + e21)
    w2 = g_w * e21 / (1.0 + e21)
    e1 = i1 - float(N_GROUPS)
    e2 = i2 - float(N_GROUPS)
    onehot = ((lane == e1) | (lane == e2)).astype(F32)
    strict_lower = (lax.broadcasted_iota(jnp.int32, (tm, tm), 1)
                    < lax.broadcasted_iota(jnp.int32, (tm, tm), 0)).astype(BF16)
    before = jnp.dot(strict_lower, onehot.astype(BF16), preferred_element_type=F32) + carry_ref[0:1, :]
    rank1 = jnp.sum(jnp.where(lane == e1, before, 0.0), axis=-1, keepdims=True)
    rank2 = jnp.sum(jnp.where(lane == e2, before, 0.0), axis=-1, keepdims=True)
    total = carry_ref[0:1, :] + jnp.sum(onehot, axis=0, keepdims=True)
    carry_ref[...] = jnp.broadcast_to(total, carry_ref.shape)
    cnt_ref[0] = jnp.broadcast_to(total, carry_ref.shape)
    route = jnp.zeros_like(logits)
    for k, val in enumerate((e1, e2, w1, w2, rank1, rank2)):
        route = jnp.where(lane == float(k), val, route)
    route_ref[...] = route


def _router(x2, g_all, w_r, b_r, layer, tt):
    t = x2.shape[0]
    tm = min(ROW_TILE, t)
    spt = tt // tm
    const = lambda i: (0, 0)
    return pl.pallas_call(
        functools.partial(_router_kernel, spt=spt),
        grid=(t // tm,),
        in_specs=[
            pl.BlockSpec((tm, D_MODEL), lambda i: (i, 0)),
            pl.BlockSpec((1, 1, D_MODEL), lambda i: (layer, 0, 0)),
            pl.BlockSpec((D_MODEL, LANE), const),
            pl.BlockSpec((1, LANE), const),
        ],
        out_specs=[pl.BlockSpec((tm, LANE), lambda i: (i, 0)),
                   pl.BlockSpec((tm * NCHUNK, LANE), lambda i: (i, 0)),
                   pl.BlockSpec((1, SUBLANE, LANE), lambda i: (i // spt, 0, 0))],
        out_shape=[jax.ShapeDtypeStruct((t, LANE), F32),
                   jax.ShapeDtypeStruct((t * NCHUNK, LANE), F32),
                   jax.ShapeDtypeStruct((t // tt, SUBLANE, LANE), F32)],
        scratch_shapes=[pltpu.VMEM((SUBLANE, LANE), F32)],
        compiler_params=_cparams(1),
        name="router",
    )(x2, g_all, w_r, b_r)


def _moe_tile_tokens(t):
    return min(4096, t)


def _moe_max_slots(tt):
    return ((2 * tt + N_EXPERTS * (MOE_RT - 1)) // MOE_RT) * MOE_RT


def _pair_arrays(route, counts, tt):
    nb = route.shape[0] // tt
    p = 2 * tt
    e = route[:, 0:2].astype(jnp.int32)
    rank = route[:, 4:6].astype(jnp.int32)
    packed = (rank * N_EXPERTS + e).reshape(nb, p // LANE, LANE)
    w = route[:, 2:4].reshape(nb, 1, p)
    cnt = counts[:, 0, :N_EXPERTS].astype(jnp.int32)
    padded = ((cnt + MOE_RT - 1) // MOE_RT) * MOE_RT
    off = jnp.cumsum(padded, axis=1) - padded
    return packed, w, off.reshape(-1), cnt.reshape(-1)


def _moe_kernel(off_ref, cnt_ref, pair_ref, pw_ref, hn_ref, wg_ref, wu_ref, wd_ref, out_hbm,
                acc_ref, xt_ref, yt_ref, wgu_bf, wd_bf, slot_v, slot_s, sidx_ref, sw_ref,
                sem, sem_slot, *, tt):
    b = pl.program_id(0)
    e = pl.program_id(1)
    n_pairs = 2 * tt
    dump0 = tt * NCHUNK

    @pl.when(e == 0)
    def _plan():
        acc_ref[...] = jnp.zeros_like(acc_ref)
        packed = pair_ref[0]
        expert = packed & (N_EXPERTS - 1)
        off_v = jnp.zeros_like(packed)
        for ex in range(N_EXPERTS):
            off_v = jnp.where(expert == ex, off_ref[b * N_EXPERTS + ex], off_v)
        slot_v[...] = off_v + (packed >> EXPERT_BITS)
        cp = pltpu.make_async_copy(slot_v, slot_s, sem_slot)
        cp.start()
        cp.wait()

        def place(row, carry):
            for c in range(LANE):
                s = slot_s[row, c]
                sidx_ref[s] = (row * (LANE // 2) + c // 2) * NCHUNK
                sw_ref[s] = pw_ref[0, 0, row * LANE + c]
            return carry
        lax.fori_loop(0, n_pairs // LANE, place, 0)

        def pad_expert(ex, carry):
            n = cnt_ref[b * N_EXPERTS + ex]
            base = off_ref[b * N_EXPERTS + ex]
            end = ((n + MOE_RT - 1) >> RT_BITS) << RT_BITS

            def pad_slot(s, c2):
                sidx_ref[base + s] = dump0 + (s & (MOE_RT - 1)) * NCHUNK
                sw_ref[base + s] = 0.0
                return c2
            lax.fori_loop(n, end, pad_slot, 0)
            return carry
        lax.fori_loop(0, N_EXPERTS, pad_expert, 0)

    n = cnt_ref[b * N_EXPERTS + e]
    base = off_ref[b * N_EXPERTS + e]
    wgu_bf[:, :D_EXPERT] = wg_ref[0].astype(BF16)
    wgu_bf[:, D_EXPERT:] = wu_ref[0].astype(BF16)
    wd_bf[...] = wd_ref[0].astype(BF16)

    def row_tile(r, carry):
        s0 = base + r * MOE_RT
        for mi in range(MOE_RT):
            src = pl.multiple_of(jnp.minimum(sidx_ref[s0 + mi], dump0 - NCHUNK), NCHUNK)
            xt_ref[pl.ds(mi, NCHUNK, stride=MOE_XS), :] = hn_ref[pl.ds(src, NCHUNK), :]
        xs = jnp.concatenate(
            [xt_ref[c * MOE_XS:c * MOE_XS + MOE_RT, :] for c in range(NCHUNK)], axis=1).astype(BF16)
        gu = jnp.dot(xs, wgu_bf[...], preferred_element_type=F32)
        gate = gu[:, :D_EXPERT]
        h = gate * _sigmoid(gate) * gu[:, D_EXPERT:]
        y = jnp.dot(h.astype(BF16), wd_bf[...], preferred_element_type=F32)
        for c in range(NCHUNK):
            yt_ref[c * MOE_XS:c * MOE_XS + MOE_RT, :] = y[:, c * LANE:(c + 1) * LANE]
        for m0 in range(0, MOE_RT, MOE_U):
            dst = [pl.multiple_of(sidx_ref[s0 + m0 + u], NCHUNK) for u in range(MOE_U)]
            vals = [acc_ref[pl.ds(dst[u], NCHUNK), :]
                    + sw_ref[s0 + m0 + u] * yt_ref[pl.ds(m0 + u, NCHUNK, stride=MOE_XS), :]
                    for u in range(MOE_U)]
            for u in range(MOE_U):
                acc_ref[pl.ds(dst[u], NCHUNK), :] = vals[u]
        return carry
    lax.fori_loop(0, (n + MOE_RT - 1) >> RT_BITS, row_tile, 0)

    @pl.when(e == N_EXPERTS - 1)
    def _emit():
        cp = pltpu.make_async_copy(acc_ref, out_hbm.at[b], sem)
        cp.start()
        cp.wait()


def _moe(hn_g, pairs, wg_all, wu_all, wd_all, layer, tt):
    packed, pw, off, cnt = pairs
    nb = packed.shape[0]
    n_pairs = 2 * tt
    acc_rows = (tt + MOE_RT) * NCHUNK
    n_slots = _moe_max_slots(tt)
    wsel = lambda b, e, off_, cnt_: (layer * N_EXPERTS + e, 0, 0)
    grid_spec = pltpu.PrefetchScalarGridSpec(
        num_scalar_prefetch=2,
        grid=(nb, N_EXPERTS),
        in_specs=[
            pl.BlockSpec((1, n_pairs // LANE, LANE), lambda b, e, off_, cnt_: (b, 0, 0)),
            pl.BlockSpec((1, 1, n_pairs), lambda b, e, off_, cnt_: (b, 0, 0), memory_space=pltpu.SMEM),
            pl.BlockSpec((tt * NCHUNK, LANE), lambda b, e, off_, cnt_: (b, 0),
                         pipeline_mode=pl.Buffered(1)),
            pl.BlockSpec((1, D_MODEL, D_EXPERT), wsel),
            pl.BlockSpec((1, D_MODEL, D_EXPERT), wsel),
            pl.BlockSpec((1, D_EXPERT, D_MODEL), wsel),
        ],
        out_specs=pl.BlockSpec(memory_space=pl.ANY),
        scratch_shapes=[
            pltpu.VMEM((acc_rows, LANE), F32),
            pltpu.VMEM((NCHUNK * MOE_XS, LANE), F32),
            pltpu.VMEM((NCHUNK * MOE_XS, LANE), F32),
            pltpu.VMEM((D_MODEL, 2 * D_EXPERT), BF16),
            pltpu.VMEM((D_EXPERT, D_MODEL), BF16),
            pltpu.VMEM((n_pairs // LANE, LANE), jnp.int32),
            pltpu.SMEM((n_pairs // LANE, LANE), jnp.int32),
            pltpu.SMEM((n_slots,), jnp.int32),
            pltpu.SMEM((n_slots,), F32),
            pltpu.SemaphoreType.DMA,
            pltpu.SemaphoreType.DMA,
        ],
    )
    return pl.pallas_call(
        functools.partial(_moe_kernel, tt=tt),
        grid_spec=grid_spec,
        out_shape=jax.ShapeDtypeStruct((nb, acc_rows, LANE), F32),
        compiler_params=_cparams(2),
        name="moe",
    )(off, cnt, packed, pw, hn_g, wg_all, wu_all, wd_all)


def _final_norm_kernel(x_ref, acc_ref, g_ref, o_ref):
    x = x_ref[...]
    x = x + _unslab(acc_ref, x.shape[0])
    ms = jnp.mean(x * x, axis=-1, keepdims=True)
    o_ref[...] = x * lax.rsqrt(ms + EPS) * g_ref[...]


def _final_norm(x2, acc, g, tt):
    t = x2.shape[0]
    tm = min(ROW_TILE, t)
    return pl.pallas_call(
        _final_norm_kernel,
        grid=(t // tm,),
        in_specs=[pl.BlockSpec((tm, D_MODEL), lambda i: (i, 0)),
                  _slab_spec(tm, tt),
                  pl.BlockSpec((1, D_MODEL), lambda i: (0, 0))],
        out_specs=pl.BlockSpec((tm, D_MODEL), lambda i: (i, 0)),
        out_shape=jax.ShapeDtypeStruct((t, D_MODEL), F32),
        compiler_params=_cparams(1),
        name="final_norm",
    )(x2, acc, g)


def _t5_bucket(rel):
    nb = N_BUCKETS // 2
    max_exact = nb // 2
    base = jnp.where(rel > 0, nb, 0)
    n = jnp.abs(rel)
    large = max_exact + (jnp.log(jnp.maximum(n, 1).astype(jnp.float32) / max_exact)
                         / math.log(MAX_DISTANCE / max_exact) * (nb - max_exact)).astype(jnp.int32)
    large = jnp.minimum(large, nb - 1)
    return base + jnp.where(n < max_exact, n, large)


def _pad_rank(wr):
    return jnp.pad(wr, ((0, LANE - B_RANK), (0, 0))).astype(BF16)


def kernel(x, w_in, rel_bias, attn_sink, gla_wr_fwd, gla_br_fwd, gla_wr_bwd, gla_br_bwd, gla_norm, w_branch_a, w_branch_b, w_out, norm_mix, norm_ffn, router_group_w, router_group_b, router_expert_w, router_expert_b, expert_w_gate, expert_w_up, expert_w_down, norm_final):
    bsz, seq, d = x.shape
    depth = w_in.shape[0]
    t = bsz * seq
    tt = _moe_tile_tokens(t)
    q_off = jnp.arange(BLOCK)
    k_off = jnp.arange(3 * BLOCK) - BLOCK
    bucket = _t5_bucket(k_off[None, :] - q_off[:, None]).astype(jnp.int32)
    x2 = x.reshape(t, d)
    acc = None
    norm_mix3 = norm_mix[:, None, :]
    norm_ffn3 = norm_ffn[:, None, :]
    wg_all = expert_w_gate.reshape(depth * N_EXPERTS, d, D_EXPERT)
    wu_all = expert_w_up.reshape(depth * N_EXPERTS, d, D_EXPERT)
    wd_all = expert_w_down.reshape(depth * N_EXPERTS, D_EXPERT, d)
    for l in range(depth):
        proj, x2 = _inproj(x2, acc, norm_mix3, w_in, l, tt)
        ya = _attention(proj, bucket, rel_bias, attn_sink[l], bsz, seq)
        wrf, wrb = _pad_rank(gla_wr_fwd[l]), _pad_rank(gla_wr_bwd[l])
        brf, brb = gla_br_fwd[l][None, :], gla_br_bwd[l][None, :]
        s_f, s_b = _gla_states(proj, wrf, brf, wrb, brb, bsz, seq)
        yb = _gla_out(proj, s_f, s_b, wrf, brf, wrb, brb, gla_norm[l][None, :], bsz, seq)
        x2 = _merge(ya, yb, proj, x2, w_branch_a, w_branch_b, w_out, l)
        w_r = jnp.concatenate(
            [router_group_w[l], router_expert_w[l],
             jnp.zeros((d, LANE - N_GROUPS - N_EXPERTS), F32)], axis=1)
        b_r = jnp.concatenate(
            [router_group_b[l], router_expert_b[l],
             jnp.zeros((LANE - N_GROUPS - N_EXPERTS,), F32)])[None, :]
        route, hn_g, counts = _router(x2, norm_ffn3, w_r, b_r, l, tt)
        acc = _moe(hn_g, _pair_arrays(route, counts, tt), wg_all, wu_all, wd_all, l, tt)
    return _final_norm(x2, acc, norm_final[None, :], tt).reshape(bsz, seq, d)
```

```python
import functools
import math

import numpy as np
import jax
import jax.numpy as jnp
from jax import lax
from jax.experimental import pallas as pl
from jax.experimental.pallas import tpu as pltpu

F32 = jnp.float32
BF16 = jnp.bfloat16

D_MODEL = 1024
A_HEADS = 8
A_KV_HEADS = 2
A_HEAD_DIM = 64
A_GROUP = A_HEADS // A_KV_HEADS
WINDOW = 128
BLOCK = 128
N_BUCKETS = 32
MAX_DISTANCE = 128
B_HEADS = 4
B_KEY_DIM = 64
B_VAL_DIM = 128
B_RANK = 16
GATE_TAU = 16.0
CHUNK = 64
N_GROUPS = 4
EXPERTS_PER_GROUP = 8
N_EXPERTS = N_GROUPS * EXPERTS_PER_GROUP
D_EXPERT = 256
EPS = 1e-6
NEG_INF = -1e30

LANE = 128
SUBLANE = 8
VMEM_LIMIT = 56 * 1024 * 1024

QA_W = A_HEADS * A_HEAD_DIM
KA_W = A_KV_HEADS * A_HEAD_DIM
QB_W = B_HEADS * B_KEY_DIM
VB_W = B_HEADS * B_VAL_DIM
COL_GA = 0
COL_GB = COL_GA + D_MODEL
COL_AQ = COL_GB + D_MODEL
COL_BV = COL_AQ + QA_W
COL_BG = COL_BV + VB_W
COL_BQ = COL_BG + VB_W
COL_BK = COL_BQ + QB_W
COL_AK = COL_BK + QB_W
COL_AV = COL_AK + KA_W
COL_R = COL_AV + KA_W
PROJ_W = COL_R + LANE

ROW_TILE = 512
PROJ_CHUNK = 640
GLA_TILE = 256
CPT = GLA_TILE // CHUNK


def _cparams(n_axes):
    return pltpu.CompilerParams(
        dimension_semantics=("arbitrary",) * n_axes, vmem_limit_bytes=VMEM_LIMIT)


_W_IN_SRC = np.cumsum([0, QA_W, KA_W, KA_W, QB_W, QB_W, VB_W, VB_W, B_RANK, D_MODEL, D_MODEL])
_W_IN_SEGMENTS = tuple(zip(
    (COL_AQ, COL_AK, COL_AV, COL_BQ, COL_BK, COL_BV, COL_BG, COL_R, COL_GA, COL_GB),
    (int(s) for s in _W_IN_SRC[:-1]),
    (int(w) for w in np.diff(_W_IN_SRC))))
W_IN_COLS = int(_W_IN_SRC[-1])
W_ROWS_PER_COPY = 128


def _load_permuted_w_in(w_ref, wbf_ref):
    wbf_ref[:, COL_R:COL_R + LANE] = jnp.zeros((D_MODEL, LANE), BF16)
    for dst, src, width in _W_IN_SEGMENTS:
        lo = (src // LANE) * LANE
        hi = min(-(-(src + width) // LANE) * LANE, W_IN_COLS)
        for r0 in range(0, D_MODEL, W_ROWS_PER_COPY):
            rows = slice(r0, r0 + W_ROWS_PER_COPY)
            piece = w_ref[0, rows, lo:hi][:, src - lo:src - lo + width]
            wbf_ref[rows, dst:dst + width] = piece.astype(BF16)


def _inproj_kernel(x_ref, g_ref, w_ref, o_ref, wbf_ref):
    @pl.when(pl.program_id(0) == 0)
    def _prep():
        _load_permuted_w_in(w_ref, wbf_ref)

    x = x_ref[...]
    ms = jnp.mean(x * x, axis=-1, keepdims=True)
    xn = (x * lax.rsqrt(ms + EPS) * g_ref[0]).astype(BF16)
    for c in range(PROJ_W // PROJ_CHUNK):
        sl = slice(c * PROJ_CHUNK, (c + 1) * PROJ_CHUNK)
        o_ref[:, sl] = jnp.dot(xn, wbf_ref[:, sl], preferred_element_type=F32).astype(BF16)


def _inproj(x2, g_all, w_all, layer):
    t = x2.shape[0]
    tm = min(ROW_TILE, t)
    return pl.pallas_call(
        _inproj_kernel,
        grid=(t // tm,),
        in_specs=[
            pl.BlockSpec((tm, D_MODEL), lambda i: (i, 0)),
            pl.BlockSpec((1, 1, D_MODEL), lambda i: (layer, 0, 0)),
            pl.BlockSpec((1, D_MODEL, W_IN_COLS), lambda i: (layer, 0, 0), pipeline_mode=pl.Buffered(1)),
        ],
        out_specs=pl.BlockSpec((tm, PROJ_W), lambda i: (i, 0)),
        out_shape=jax.ShapeDtypeStruct((t, PROJ_W), BF16),
        scratch_shapes=[pltpu.VMEM((D_MODEL, PROJ_W), BF16)],
        compiler_params=_cparams(1),
        name="inproj",
    )(x2, g_all, w_all)


def _attn_kernel(bucket_ref, relb_ref, sink_ref, q_ref, kp_ref, kc_ref, kn_ref,
                 vp_ref, vc_ref, vn_ref, o_ref, bias_ref, *, nblk):
    b = pl.program_id(0)
    n = pl.program_id(1)
    kw = 3 * BLOCK

    @pl.when((b == 0) & (n == 0))
    def _build_bias():
        bucket = bucket_ref[...]
        row = lax.broadcasted_iota(jnp.int32, (BLOCK, kw), 0)
        col = lax.broadcasted_iota(jnp.int32, (BLOCK, kw), 1)
        win = jnp.abs(col - BLOCK - row) <= WINDOW
        for hq in range(A_HEADS):
            acc = jnp.zeros((BLOCK, kw), F32)
            for bb in range(N_BUCKETS):
                acc = jnp.where(bucket == bb, relb_ref[bb, hq], acc)
            g = hq % A_GROUP
            bias_ref[hq // A_GROUP, g * BLOCK:(g + 1) * BLOCK, :] = jnp.where(win, acc, NEG_INF)

    col1 = lax.broadcasted_iota(jnp.int32, (1, kw), 1)
    valid = ((col1 >= BLOCK) | (n > 0)) & ((col1 < 2 * BLOCK) | (n < nblk - 1))
    rowg = jnp.right_shift(
        lax.broadcasted_iota(jnp.int32, (A_GROUP * BLOCK, 1), 0), int(math.log2(BLOCK)))
    q = q_ref[...]
    outs = []
    for h in range(A_KV_HEADS):
        hs = slice(h * A_HEAD_DIM, (h + 1) * A_HEAD_DIM)
        k3 = jnp.concatenate([kp_ref[:, hs], kc_ref[:, hs], kn_ref[:, hs]], axis=0)
        v3 = jnp.concatenate([vp_ref[:, hs], vc_ref[:, hs], vn_ref[:, hs]], axis=0)
        q4 = jnp.concatenate(
            [q[:, (A_GROUP * h + g) * A_HEAD_DIM:(A_GROUP * h + g + 1) * A_HEAD_DIM]
             for g in range(A_GROUP)], axis=0)
        s = lax.dot_general(q4, k3, (((1,), (1,)), ((), ())), preferred_element_type=F32)
        s = s * (A_HEAD_DIM ** -0.5) + bias_ref[h]
        s = jnp.where(valid, s, NEG_INF)
        sk = jnp.zeros((A_GROUP * BLOCK, 1), F32)
        for g in range(A_GROUP):
            sk = jnp.where(rowg == g, sink_ref[A_GROUP * h + g], sk)
        m = jnp.maximum(jnp.max(s, axis=-1, keepdims=True), sk)
        p = jnp.exp(s - m)
        den = jnp.sum(p, axis=-1, keepdims=True) + jnp.exp(sk - m)
        o = jnp.dot(p.astype(BF16), v3, preferred_element_type=F32) / den
        for g in range(A_GROUP):
            outs.append(o[g * BLOCK:(g + 1) * BLOCK])
    o_ref[...] = jnp.concatenate(outs, axis=1).astype(BF16)


def _attention(proj, bucket, rel_bias, sink, bsz, seq):
    nblk = seq // BLOCK
    t = bsz * seq
    cq = COL_AQ // QA_W
    ck = COL_AK // KA_W
    cv = COL_AV // KA_W

    def prev(b, n):
        return b * nblk + jnp.maximum(n - 1, 0)

    def cur(b, n):
        return b * nblk + n

    def nxt(b, n):
        return b * nblk + jnp.minimum(n + 1, nblk - 1)

    smem = pl.BlockSpec(memory_space=pltpu.SMEM)
    kv_specs = [pl.BlockSpec((BLOCK, KA_W), functools.partial(lambda b, n, f, c: (f(b, n), c), f=f, c=c))
                for c in (ck, cv) for f in (prev, cur, nxt)]
    return pl.pallas_call(
        functools.partial(_attn_kernel, nblk=nblk),
        grid=(bsz, nblk),
        in_specs=[
            pl.BlockSpec((BLOCK, 3 * BLOCK), lambda b, n: (0, 0)),
            smem, smem,
            pl.BlockSpec((BLOCK, QA_W), lambda b, n: (b * nblk + n, cq)),
        ] + kv_specs,
        out_specs=pl.BlockSpec((BLOCK, QA_W), lambda b, n: (b * nblk + n, 0)),
        out_shape=jax.ShapeDtypeStruct((t, QA_W), BF16),
        scratch_shapes=[pltpu.VMEM((A_KV_HEADS, A_GROUP * BLOCK, 3 * BLOCK), F32)],
        compiler_params=_cparams(2),
        name="attn",
    )(bucket, rel_bias, sink, proj, proj, proj, proj, proj, proj, proj)


def _log_sigmoid(z):
    return jnp.minimum(z, 0.0) - jnp.log1p(jnp.exp(-jnp.abs(z)))


def _log_decay(r, wr, br):
    z = jnp.dot(r, wr, preferred_element_type=F32) + br
    return _log_sigmoid(z) / GATE_TAU


def _chunk_masks():
    row = lax.broadcasted_iota(jnp.int32, (GLA_TILE, GLA_TILE), 0)
    col = lax.broadcasted_iota(jnp.int32, (GLA_TILE, GLA_TILE), 1)
    sh = int(math.log2(CHUNK))
    same = jnp.right_shift(row, sh) == jnp.right_shift(col, sh)
    return same, same & (col <= row), same & (col >= row)


def _f32dot(a, b):
    return jnp.dot(a, b, preferred_element_type=F32, precision=lax.Precision.HIGHEST)


def _split3(x):
    hi = x.astype(BF16)
    r1 = x - hi.astype(F32)
    mid = r1.astype(BF16)
    lo = (r1 - mid.astype(F32)).astype(BF16)
    return hi, mid, lo


def _mask_dot(mask, x):
    m = mask.astype(BF16)
    hi, mid, lo = _split3(x)
    return (jnp.dot(m, hi, preferred_element_type=F32) + jnp.dot(m, mid, preferred_element_type=F32)
            + jnp.dot(m, lo, preferred_element_type=F32))


def _gla_state_dir(k_ref, v_ref, r_ref, wr_ref, br_ref, s_out_ref, state_ref, order, cum_mask, same):
    la = _log_decay(r_ref[...], wr_ref[...], br_ref[...])
    both = _mask_dot(jnp.concatenate([cum_mask, same], axis=0), la)
    cum = both[:GLA_TILE]
    tot = both[GLA_TILE:]
    k_end = (k_ref[...].astype(F32) * jnp.exp(tot - cum)).astype(BF16)
    tot_t = tot.T
    v = v_ref[...]
    for c in order:
        rs = slice(c * CHUNK, (c + 1) * CHUNK)
        s_out_ref[0, c] = state_ref[...].astype(BF16)
        upd_all = lax.dot_general(k_end[rs], v[rs], (((0,), (0,)), ((), ())),
                                  preferred_element_type=F32)
        upd = jnp.concatenate(
            [upd_all[h * B_KEY_DIM:(h + 1) * B_KEY_DIM, h * B_VAL_DIM:(h + 1) * B_VAL_DIM]
             for h in range(B_HEADS)], axis=0)
        decay = jnp.exp(tot_t[:, c * CHUNK:c * CHUNK + 1])
        state_ref[...] = decay * state_ref[...] + upd


def _gla_state_kernel(kf_ref, vf_ref, rf_ref, kb_ref, vb_ref, rb_ref,
                      wrf_ref, brf_ref, wrb_ref, brb_ref,
                      sf_ref, sb_ref, stf_ref, stb_ref):
    @pl.when(pl.program_id(1) == 0)
    def _reset():
        stf_ref[...] = jnp.zeros_like(stf_ref)
        stb_ref[...] = jnp.zeros_like(stb_ref)

    same, lower, upper = _chunk_masks()
    _gla_state_dir(kf_ref, vf_ref, rf_ref, wrf_ref, brf_ref, sf_ref, stf_ref,
                   range(CPT), lower, same)
    _gla_state_dir(kb_ref, vb_ref, rb_ref, wrb_ref, brb_ref, sb_ref, stb_ref,
                   range(CPT - 1, -1, -1), upper, same)


def _gla_states(proj, wrf, brf, wrb, brb, bsz, seq):
    nt = seq // GLA_TILE
    nchunks = seq // CHUNK
    hk = B_HEADS * B_KEY_DIM
    ck, cv, cr = COL_BK // QB_W, COL_BV // VB_W, COL_R // LANE

    def fwd(b, i):
        return b * nt + i

    def bwd(b, i):
        return b * nt + (nt - 1 - i)

    def tile_specs(f):
        return [pl.BlockSpec((GLA_TILE, QB_W), lambda b, i: (f(b, i), ck)),
                pl.BlockSpec((GLA_TILE, VB_W), lambda b, i: (f(b, i), cv)),
                pl.BlockSpec((GLA_TILE, LANE), lambda b, i: (f(b, i), cr))]

    const = lambda b, i: (0, 0)
    out_sds = jax.ShapeDtypeStruct((bsz, nchunks, hk, B_VAL_DIM), BF16)
    return pl.pallas_call(
        _gla_state_kernel,
        grid=(bsz, nt),
        in_specs=tile_specs(fwd) + tile_specs(bwd) + [
            pl.BlockSpec((LANE, hk), const), pl.BlockSpec((1, hk), const),
            pl.BlockSpec((LANE, hk), const), pl.BlockSpec((1, hk), const)],
        out_specs=[pl.BlockSpec((1, CPT, hk, B_VAL_DIM), lambda b, i: (b, i, 0, 0)),
                   pl.BlockSpec((1, CPT, hk, B_VAL_DIM), lambda b, i: (b, nt - 1 - i, 0, 0))],
        out_shape=[out_sds, out_sds],
        scratch_shapes=[pltpu.VMEM((hk, B_VAL_DIM), F32), pltpu.VMEM((hk, B_VAL_DIM), F32)],
        compiler_params=_cparams(2),
        name="gla_state",
    )(proj, proj, proj, proj, proj, proj, wrf, brf, wrb, brb)


def _gla_out_kernel(q_ref, k_ref, v_ref, r_ref, g_ref, sf_ref, sb_ref,
                    wrf_ref, brf_ref, wrb_ref, brb_ref, ng_ref, o_ref):
    same, lower, upper = _chunk_masks()
    r = r_ref[...]
    cum_f = _mask_dot(lower, _log_decay(r, wrf_ref[...], brf_ref[...]))
    cum_b = _mask_dot(upper, _log_decay(r, wrb_ref[...], brb_ref[...]))
    q = q_ref[...].astype(F32) * (B_KEY_DIM ** -0.5)
    k = k_ref[...].astype(F32)
    qd_f = q * jnp.exp(cum_f)
    qd_b = q * jnp.exp(cum_b)
    ki_f = (k * jnp.exp(-cum_f)).astype(BF16)
    ki_b = (k * jnp.exp(-cum_b)).astype(BF16)
    lane_head = jnp.right_shift(
        lax.broadcasted_iota(jnp.int32, (1, B_HEADS * B_KEY_DIM), 1), int(math.log2(B_KEY_DIM)))

    def stack_heads(x):
        return jnp.concatenate(
            [jnp.where(lane_head == h, x, 0.0) for h in range(B_HEADS)], axis=0).astype(BF16)

    qs_f = stack_heads(qd_f)
    qs_b = stack_heads(qd_b)
    nt_dims = (((1,), (1,)), ((), ()))
    sc_f = lax.dot_general(qs_f, ki_f, nt_dims, preferred_element_type=F32)
    sc_b = lax.dot_general(qs_b, ki_b, nt_dims, preferred_element_type=F32)
    lower_s = jnp.concatenate([lower] * B_HEADS, axis=0)
    strict_upper_s = jnp.concatenate([upper & ~lower] * B_HEADS, axis=0)
    p = jnp.where(lower_s, sc_f, jnp.where(strict_upper_s, sc_b, 0.0)).astype(BF16)
    v = v_ref[...]
    inter = []
    for c in range(CPT):
        lhs_f = jnp.concatenate(
            [qs_f[h * GLA_TILE + c * CHUNK:h * GLA_TILE + (c + 1) * CHUNK] for h in range(B_HEADS)], axis=0)
        lhs_b = jnp.concatenate(
            [qs_b[h * GLA_TILE + c * CHUNK:h * GLA_TILE + (c + 1) * CHUNK] for h in range(B_HEADS)], axis=0)
        inter.append(jnp.dot(lhs_f, sf_ref[0, c], preferred_element_type=F32)
                     + jnp.dot(lhs_b, sb_ref[0, c], preferred_element_type=F32))
    ng = ng_ref[...]
    g = g_ref[...].astype(F32)
    for h in range(B_HEADS):
        vs = slice(h * B_VAL_DIM, (h + 1) * B_VAL_DIM)
        o = jnp.dot(p[h * GLA_TILE:(h + 1) * GLA_TILE], v[:, vs], preferred_element_type=F32)
        o = o + jnp.concatenate([inter[c][h * CHUNK:(h + 1) * CHUNK] for c in range(CPT)], axis=0)
        o = o * lax.rsqrt(jnp.mean(o * o, axis=-1, keepdims=True) + EPS) * ng
        gh = g[:, vs]
        o_ref[:, vs] = (o * (gh / (1.0 + jnp.exp(-gh)))).astype(BF16)


def _gla_out(proj, s_f, s_b, wrf, brf, wrb, brb, ng, bsz, seq):
    nt = seq // GLA_TILE
    t = bsz * seq
    hk = B_HEADS * B_KEY_DIM
    const = lambda b, i: (0, 0)
    row = lambda c: (lambda b, i: (b * nt + i, c))
    return pl.pallas_call(
        _gla_out_kernel,
        grid=(bsz, nt),
        in_specs=[
            pl.BlockSpec((GLA_TILE, QB_W), row(COL_BQ // QB_W)),
            pl.BlockSpec((GLA_TILE, QB_W), row(COL_BK // QB_W)),
            pl.BlockSpec((GLA_TILE, VB_W), row(COL_BV // VB_W)),
            pl.BlockSpec((GLA_TILE, LANE), row(COL_R // LANE)),
            pl.BlockSpec((GLA_TILE, VB_W), row(COL_BG // VB_W)),
            pl.BlockSpec((1, CPT, hk, B_VAL_DIM), lambda b, i: (b, i, 0, 0)),
            pl.BlockSpec((1, CPT, hk, B_VAL_DIM), lambda b, i: (b, i, 0, 0)),
            pl.BlockSpec((LANE, hk), const), pl.BlockSpec((1, hk), const),
            pl.BlockSpec((LANE, hk), const), pl.BlockSpec((1, hk), const),
            pl.BlockSpec((1, B_VAL_DIM), const),
        ],
        out_specs=pl.BlockSpec((GLA_TILE, VB_W), lambda b, i: (b * nt + i, 0)),
        out_shape=jax.ShapeDtypeStruct((t, VB_W), BF16),
        compiler_params=_cparams(2),
        name="gla_out",
    )(proj, proj, proj, proj, proj, s_f, s_b, wrf, brf, wrb, brb, ng)


def _sigmoid(x):
    return 1.0 / (1.0 + jnp.exp(-x))


def _merge_kernel(ya_ref, yb_ref, ga_ref, gb_ref, x_ref, wa_ref, wb_ref, wo_ref, o_ref,
                  wa_bf, wb_bf, wo_bf):
    @pl.when(pl.program_id(0) == 0)
    def _prep():
        wa_bf[...] = wa_ref[0].astype(BF16)
        wb_bf[...] = wb_ref[0].astype(BF16)
        wo_bf[...] = wo_ref[0].astype(BF16)

    a = jnp.dot(ya_ref[...], wa_bf[...], preferred_element_type=F32)
    b = jnp.dot(yb_ref[...], wb_bf[...], preferred_element_type=F32)
    merged = _sigmoid(ga_ref[...].astype(F32)) * a + _sigmoid(gb_ref[...].astype(F32)) * b
    o_ref[...] = x_ref[...] + jnp.dot(merged.astype(BF16), wo_bf[...], preferred_element_type=F32)


def _merge(ya, yb, proj, x2, wa_all, wb_all, wo_all, layer):
    t = x2.shape[0]
    tm = min(ROW_TILE, t)
    wsel = lambda i: (layer, 0, 0)
    return pl.pallas_call(
        _merge_kernel,
        grid=(t // tm,),
        in_specs=[
            pl.BlockSpec((tm, QA_W), lambda i: (i, 0)),
            pl.BlockSpec((tm, VB_W), lambda i: (i, 0)),
            pl.BlockSpec((tm, D_MODEL), lambda i: (i, COL_GA // D_MODEL)),
            pl.BlockSpec((tm, D_MODEL), lambda i: (i, COL_GB // D_MODEL)),
            pl.BlockSpec((tm, D_MODEL), lambda i: (i, 0)),
            pl.BlockSpec((1, QA_W, D_MODEL), wsel, pipeline_mode=pl.Buffered(1)),
            pl.BlockSpec((1, VB_W, D_MODEL), wsel, pipeline_mode=pl.Buffered(1)),
            pl.BlockSpec((1, D_MODEL, D_MODEL), wsel, pipeline_mode=pl.Buffered(1)),
        ],
        out_specs=pl.BlockSpec((tm, D_MODEL), lambda i: (i, 0)),
        out_shape=jax.ShapeDtypeStruct((t, D_MODEL), F32),
        scratch_shapes=[pltpu.VMEM((QA_W, D_MODEL), BF16), pltpu.VMEM((VB_W, D_MODEL), BF16),
                        pltpu.VMEM((D_MODEL, D_MODEL), BF16)],
        compiler_params=_cparams(1),
        name="merge",
    )(ya, yb, proj, proj, x2, wa_all, wb_all, wo_all)


def _moe_tile_tokens(t):
    return min(4096, t)


MOE_BLK = 512
MOE_UNIT = 16
MOE_CAP = 1536
XS_W = D_MODEL + LANE
MOE_CH = 1024
MOE_CHU = MOE_CH // MOE_UNIT
MOE_MT = 256
MOE_NC = 256
assert MOE_CAP >= 2 * MOE_BLK + N_EXPERTS * (MOE_UNIT - 1) and MOE_CAP % MOE_UNIT == 0


def _route(hn_bf, wr_ref, br_ref):
    logits = jnp.dot(hn_bf, wr_ref[...].astype(BF16), preferred_element_type=F32) + br_ref[...]
    lane = lax.broadcasted_iota(jnp.int32, logits.shape, 1).astype(F32)
    big = float(LANE)
    ninf = -jnp.inf
    gl = jnp.where(lane < N_GROUPS, logits, ninf)
    gmax = jnp.max(gl, axis=-1, keepdims=True)
    g_idx = jnp.min(jnp.where(gl == gmax, lane, big), axis=-1, keepdims=True)
    g_w = 1.0 / jnp.sum(jnp.exp(gl - gmax), axis=-1, keepdims=True)
    lo = float(N_GROUPS) + g_idx * float(EXPERTS_PER_GROUP)
    el = jnp.where((lane >= lo) & (lane < lo + EXPERTS_PER_GROUP), logits, ninf)
    v1 = jnp.max(el, axis=-1, keepdims=True)
    i1 = jnp.min(jnp.where(el == v1, lane, big), axis=-1, keepdims=True)
    el2 = jnp.where(lane == i1, ninf, el)
    v2 = jnp.max(el2, axis=-1, keepdims=True)
    i2 = jnp.min(jnp.where(el2 == v2, lane, big), axis=-1, keepdims=True)
    e21 = jnp.exp(v2 - v1)
    w1 = g_w / (1.0 + e21)
    w2 = g_w * e21 / (1.0 + e21)
    return i1 - float(N_GROUPS), i2 - float(N_GROUPS), w1, w2, lane


def _one_hot_slots(pos1, pos2):
    slot = lax.broadcasted_iota(jnp.int32, (pos1.shape[0], MOE_CAP), 1)
    return slot == pos1, slot == pos2


def _weight_lanes(w, lane):
    hi = w.astype(BF16).astype(F32)
    r1 = w - hi
    mid = r1.astype(BF16).astype(F32)
    lo = r1 - mid
    return jnp.where(lane == 0.0, hi, jnp.where(lane == 1.0, mid,
                                                jnp.where(lane == 2.0, lo, 0.0))).astype(BF16)


def _moe_kernel(x_ref, g_ref, wr_ref, br_ref, wg_ref, wu_ref, wd_ref, o_ref,
                xs_ref, pos_ref, tab_v, tab_s, xt_ref, addr_ref, wgu_bf, wd_bf, sem, *, nblk):
    s = pl.program_id(1)
    dump_row = nblk * MOE_CAP
    tn_dims = (((0,), (0,)), ((), ()))

    @pl.when(s < nblk)
    def _dispatch():
        @pl.when(s == 0)
        def _init():
            xs_ref[dump_row:dump_row + MOE_UNIT, :] = jnp.zeros((MOE_UNIT, XS_W), BF16)
            xt_ref[...] = jnp.zeros_like(xt_ref)

        x = x_ref[...]
        ms = jnp.mean(x * x, axis=-1, keepdims=True)
        hn = (x * lax.rsqrt(ms + EPS) * g_ref[0]).astype(BF16)
        e1, e2, w1, w2, lane = _route(hn, wr_ref, br_ref)
        onehot = ((lane == e1) | (lane == e2)).astype(F32)
        strict_lower = (lax.broadcasted_iota(jnp.int32, (MOE_BLK, MOE_BLK), 1)
                        < lax.broadcasted_iota(jnp.int32, (MOE_BLK, MOE_BLK), 0)).astype(BF16)
        before = jnp.dot(strict_lower, onehot.astype(BF16), preferred_element_type=F32)
        counts = jnp.sum(onehot, axis=0, keepdims=True)
        units = jnp.floor((counts + (MOE_UNIT - 1.0)) * (1.0 / MOE_UNIT))
        strict_upper = (lax.broadcasted_iota(jnp.int32, (LANE, LANE), 0)
                        < lax.broadcasted_iota(jnp.int32, (LANE, LANE), 1)).astype(BF16)
        padded8 = jnp.broadcast_to(units * MOE_UNIT, (SUBLANE, LANE))
        run_off = jnp.dot(padded8.astype(BF16), strict_upper, preferred_element_type=F32)
        start = run_off[0:1, :] + before
        pos1 = jnp.sum(jnp.where(lane == e1, start, 0.0), axis=-1, keepdims=True)
        pos2 = jnp.sum(jnp.where(lane == e2, start, 0.0), axis=-1, keepdims=True)
        pt1, pt2 = _one_hot_slots(pos1.astype(jnp.int32), pos2.astype(jnp.int32))
        pt = (pt1 | pt2).astype(BF16)
        rows = pl.ds(pl.multiple_of(s * MOE_CAP, MOE_CAP), MOE_CAP)
        for c in range(D_MODEL // MOE_NC):
            cs = slice(c * MOE_NC, (c + 1) * MOE_NC)
            xs_ref[rows, cs] = lax.dot_general(pt, hn[:, cs], tn_dims,
                                               preferred_element_type=F32).astype(BF16)
        wl = (lax.dot_general(pt1.astype(BF16), _weight_lanes(w1, lane), tn_dims,
                              preferred_element_type=F32)
              + lax.dot_general(pt2.astype(BF16), _weight_lanes(w2, lane), tn_dims,
                                preferred_element_type=F32))
        xs_ref[rows, D_MODEL:] = wl.astype(BF16)
        pos_ref[s] = jnp.where(lane == 0.0, pos1, jnp.where(lane == 1.0, pos2, 0.0))
        sub = lax.broadcasted_iota(jnp.int32, (SUBLANE, LANE), 0)
        tab_v[s] = jnp.where(sub == 0, run_off, jnp.where(sub == 1, padded8 * (1.0 / MOE_UNIT), 0.0)
                             ).astype(jnp.int32)

        @pl.when(s == nblk - 1)
        def _publish():
            cp = pltpu.make_async_copy(tab_v, tab_s, sem)
            cp.start()
            cp.wait()

    @pl.when((s >= nblk) & (s < nblk + N_EXPERTS))
    def _experts():
        e = s - nblk
        wgu_bf[:, :D_EXPERT] = wg_ref[0].astype(BF16)
        wgu_bf[:, D_EXPERT:] = wu_ref[0].astype(BF16)
        wd_bf[...] = wd_ref[0].astype(BF16)
        total = tab_s[0, 1, e]
        for j in range(1, nblk):
            total = total + tab_s[j, 1, e]

        def chunk(c, carry):
            u0 = c * MOE_CHU
            for k in range(MOE_CHU):
                addr_ref[k] = dump_row
            cum = 0
            for j in range(nblk):
                nj = tab_s[j, 1, e]
                run0 = j * MOE_CAP + tab_s[j, 0, e]
                k_off = cum - u0

                def copy_unit(u, c2, run0=run0, k_off=k_off):
                    src = pl.multiple_of(run0 + u * MOE_UNIT, MOE_UNIT)
                    dst = pl.multiple_of((k_off + u) * MOE_UNIT, MOE_UNIT)
                    xt_ref[pl.ds(dst, MOE_UNIT), :] = xs_ref[pl.ds(src, MOE_UNIT), :]
                    addr_ref[k_off + u] = src
                    return c2
                lax.fori_loop(jnp.clip(u0 - cum, 0, nj), jnp.clip(u0 + MOE_CHU - cum, 0, nj),
                              copy_unit, 0)
                cum = cum + nj
            rows_here = jnp.minimum(MOE_CHU, total - u0) * MOE_UNIT

            def tile(t, c3):
                xt = xt_ref[pl.ds(pl.multiple_of(t * MOE_MT, MOE_MT), MOE_MT), :]
                gu = jnp.dot(xt[:, :D_MODEL], wgu_bf[...], preferred_element_type=F32)
                wl = xt[:, D_MODEL:].astype(F32)
                wcol = wl[:, 0:1] + wl[:, 1:2] + wl[:, 2:3]
                gate = gu[:, :D_EXPERT]
                h = gate * _sigmoid(gate) * gu[:, D_EXPERT:] * wcol
                y = jnp.dot(h.astype(BF16), wd_bf[...], preferred_element_type=F32).astype(BF16)
                for uu in range(MOE_MT // MOE_UNIT):
                    dst = pl.multiple_of(addr_ref[t * (MOE_MT // MOE_UNIT) + uu], MOE_UNIT)
                    xs_ref[pl.ds(dst, MOE_UNIT), :D_MODEL] = y[uu * MOE_UNIT:(uu + 1) * MOE_UNIT]
                return c3
            lax.fori_loop(0, (rows_here + MOE_MT - 1) // MOE_MT, tile, 0)
            return carry
        lax.fori_loop(0, (total + MOE_CHU - 1) // MOE_CHU, chunk, 0)

    @pl.when(s >= nblk + N_EXPERTS)
    def _combine():
        j = s - nblk - N_EXPERTS
        p = pos_ref[j]
        pt1, pt2 = _one_hot_slots(p[:, 0:1].astype(jnp.int32), p[:, 1:2].astype(jnp.int32))
        pt = (pt1 | pt2).astype(BF16)
        rows = pl.ds(pl.multiple_of(j * MOE_CAP, MOE_CAP), MOE_CAP)
        for c in range(D_MODEL // MOE_NC):
            cs = slice(c * MOE_NC, (c + 1) * MOE_NC)
            o_ref[:, cs] = x_ref[:, cs] + jnp.dot(pt, xs_ref[rows, cs], preferred_element_type=F32)


def _moe(x2, g_all, w_r, b_r, wg_all, wu_all, wd_all, layer, tt):
    t = x2.shape[0]
    nblk = tt // MOE_BLK
    steps = 2 * nblk + N_EXPERTS

    def tok(b, s):
        j = jnp.where(s < nblk, s, jnp.where(s < nblk + N_EXPERTS, nblk - 1, s - nblk - N_EXPERTS))
        return (b * nblk + j, 0)

    def out_tok(b, s):
        return (b * nblk + jnp.maximum(s - nblk - N_EXPERTS, 0), 0)

    wsel = lambda b, s: (layer * N_EXPERTS + jnp.clip(s - nblk, 0, N_EXPERTS - 1), 0, 0)
    const = lambda b, s: (0, 0)
    return pl.pallas_call(
        functools.partial(_moe_kernel, nblk=nblk),
        grid=(t // tt, steps),
        in_specs=[
            pl.BlockSpec((MOE_BLK, D_MODEL), tok),
            pl.BlockSpec((1, 1, D_MODEL), lambda b, s: (layer, 0, 0)),
            pl.BlockSpec((D_MODEL, LANE), const),
            pl.BlockSpec((1, LANE), const),
            pl.BlockSpec((1, D_MODEL, D_EXPERT), wsel),
            pl.BlockSpec((1, D_MODEL, D_EXPERT), wsel),
            pl.BlockSpec((1, D_EXPERT, D_MODEL), wsel),
        ],
        out_specs=pl.BlockSpec((MOE_BLK, D_MODEL), out_tok),
        out_shape=jax.ShapeDtypeStruct((t, D_MODEL), F32),
        scratch_shapes=[
            pltpu.VMEM((nblk * MOE_CAP + MOE_UNIT, XS_W), BF16),
            pltpu.VMEM((nblk, MOE_BLK, LANE), F32),
            pltpu.VMEM((nblk, SUBLANE, LANE), jnp.int32),
            pltpu.SMEM((nblk, SUBLANE, LANE), jnp.int32),
            pltpu.VMEM((MOE_CH, XS_W), BF16),
            pltpu.SMEM((MOE_CHU,), jnp.int32),
            pltpu.VMEM((D_MODEL, 2 * D_EXPERT), BF16),
            pltpu.VMEM((D_EXPERT, D_MODEL), BF16),
            pltpu.SemaphoreType.DMA,
        ],
        compiler_params=_cparams(2),
        name="moe",
    )(x2, g_all, w_r, b_r, wg_all, wu_all, wd_all)


def _final_norm_kernel(x_ref, g_ref, o_ref):
    x = x_ref[...]
    ms = jnp.mean(x * x, axis=-1, keepdims=True)
    o_ref[...] = x * lax.rsqrt(ms + EPS) * g_ref[...]


def _final_norm(x2, g):
    t = x2.shape[0]
    tm = min(ROW_TILE, t)
    return pl.pallas_call(
        _final_norm_kernel,
        grid=(t // tm,),
        in_specs=[pl.BlockSpec((tm, D_MODEL), lambda i: (i, 0)),
                  pl.BlockSpec((1, D_MODEL), lambda i: (0, 0))],
        out_specs=pl.BlockSpec((tm, D_MODEL), lambda i: (i, 0)),
        out_shape=jax.ShapeDtypeStruct((t, D_MODEL), F32),
        compiler_params=_cparams(1),
        name="final_norm",
    )(x2, g)


def _t5_bucket(rel):
    nb = N_BUCKETS // 2
    max_exact = nb // 2
    base = jnp.where(rel > 0, nb, 0)
    n = jnp.abs(rel)
    large = max_exact + (jnp.log(jnp.maximum(n, 1).astype(jnp.float32) / max_exact)
                         / math.log(MAX_DISTANCE / max_exact) * (nb - max_exact)).astype(jnp.int32)
    large = jnp.minimum(large, nb - 1)
    return base + jnp.where(n < max_exact, n, large)


def _pad_rank(wr):
    return jnp.pad(wr, ((0, LANE - B_RANK), (0, 0))).astype(BF16)


def kernel(x, w_in, rel_bias, attn_sink, gla_wr_fwd, gla_br_fwd, gla_wr_bwd, gla_br_bwd, gla_norm, w_branch_a, w_branch_b, w_out, norm_mix, norm_ffn, router_group_w, router_group_b, router_expert_w, router_expert_b, expert_w_gate, expert_w_up, expert_w_down, norm_final):
    bsz, seq, d = x.shape
    depth = w_in.shape[0]
    t = bsz * seq
    tt = _moe_tile_tokens(t)
    q_off = jnp.arange(BLOCK)
    k_off = jnp.arange(3 * BLOCK) - BLOCK
    bucket = _t5_bucket(k_off[None, :] - q_off[:, None]).astype(jnp.int32)
    x2 = x.reshape(t, d)
    norm_mix3 = norm_mix[:, None, :]
    norm_ffn3 = norm_ffn[:, None, :]
    wg_all = expert_w_gate.reshape(depth * N_EXPERTS, d, D_EXPERT)
    wu_all = expert_w_up.reshape(depth * N_EXPERTS, d, D_EXPERT)
    wd_all = expert_w_down.reshape(depth * N_EXPERTS, D_EXPERT, d)
    for l in range(depth):
        proj = _inproj(x2, norm_mix3, w_in, l)
        ya = _attention(proj, bucket, rel_bias, attn_sink[l], bsz, seq)
        wrf, wrb = _pad_rank(gla_wr_fwd[l]), _pad_rank(gla_wr_bwd[l])
        brf, brb = gla_br_fwd[l][None, :], gla_br_bwd[l][None, :]
        s_f, s_b = _gla_states(proj, wrf, brf, wrb, brb, bsz, seq)
        yb = _gla_out(proj, s_f, s_b, wrf, brf, wrb, brb, gla_norm[l][None, :], bsz, seq)
        x2 = _merge(ya, yb, proj, x2, w_branch_a, w_branch_b, w_out, l)
        w_r = jnp.concatenate(
            [router_group_w[l], router_expert_w[l],
             jnp.zeros((d, LANE - N_GROUPS - N_EXPERTS), F32)], axis=1)
        b_r = jnp.concatenate(
            [router_group_b[l], router_expert_b[l],
             jnp.zeros((LANE - N_GROUPS - N_EXPERTS,), F32)])[None, :]
        x2 = _moe(x2, norm_ffn3, w_r, b_r, wg_all, wu_all, wd_all, l, tt)
    return _final_norm(x2, norm_final[None, :]).reshape(bsz, seq, d)
```

```python
import functools
import math

import numpy as np
import jax
import jax.numpy as jnp
from jax import lax
from jax.experimental import pallas as pl
from jax.experimental.pallas import tpu as pltpu

F32 = jnp.float32
BF16 = jnp.bfloat16

D_MODEL = 1024
A_HEADS = 8
A_KV_HEADS = 2
A_HEAD_DIM = 64
A_GROUP = A_HEADS // A_KV_HEADS
WINDOW = 128
BLOCK = 128
N_BUCKETS = 32
MAX_DISTANCE = 128
B_HEADS = 4
B_KEY_DIM = 64
B_VAL_DIM = 128
B_RANK = 16
GATE_TAU = 16.0
CHUNK = 64
N_GROUPS = 4
EXPERTS_PER_GROUP = 8
N_EXPERTS = N_GROUPS * EXPERTS_PER_GROUP
D_EXPERT = 256
EPS = 1e-6
NEG_INF = -1e30

LANE = 128
SUBLANE = 8
VMEM_LIMIT = 56 * 1024 * 1024

QA_W = A_HEADS * A_HEAD_DIM
KA_W = A_KV_HEADS * A_HEAD_DIM
QB_W = B_HEADS * B_KEY_DIM
VB_W = B_HEADS * B_VAL_DIM
COL_GA = 0
COL_GB = COL_GA + D_MODEL
COL_AQ = COL_GB + D_MODEL
COL_BV = COL_AQ + QA_W
COL_BG = COL_BV + VB_W
COL_BQ = COL_BG + VB_W
COL_BK = COL_BQ + QB_W
COL_AK = COL_BK + QB_W
COL_AV = COL_AK + KA_W
COL_R = COL_AV + KA_W
PROJ_W = COL_R + LANE

ROW_TILE = 512
PROJ_CHUNK = 640
GLA_TILE = 256
CPT = GLA_TILE // CHUNK


def _cparams(n_axes):
    return pltpu.CompilerParams(
        dimension_semantics=("arbitrary",) * n_axes, vmem_limit_bytes=VMEM_LIMIT)


_W_IN_SRC = np.cumsum([0, QA_W, KA_W, KA_W, QB_W, QB_W, VB_W, VB_W, B_RANK, D_MODEL, D_MODEL])
_W_IN_SEGMENTS = tuple(zip(
    (COL_AQ, COL_AK, COL_AV, COL_BQ, COL_BK, COL_BV, COL_BG, COL_R, COL_GA, COL_GB),
    (int(s) for s in _W_IN_SRC[:-1]),
    (int(w) for w in np.diff(_W_IN_SRC))))
W_IN_COLS = int(_W_IN_SRC[-1])
W_ROWS_PER_COPY = 128


def _load_permuted_w_in(w_ref, wbf_ref):
    wbf_ref[:, COL_R:COL_R + LANE] = jnp.zeros((D_MODEL, LANE), BF16)
    for dst, src, width in _W_IN_SEGMENTS:
        lo = (src // LANE) * LANE
        hi = min(-(-(src + width) // LANE) * LANE, W_IN_COLS)
        for r0 in range(0, D_MODEL, W_ROWS_PER_COPY):
            rows = slice(r0, r0 + W_ROWS_PER_COPY)
            piece = w_ref[0, rows, lo:hi][:, src - lo:src - lo + width]
            wbf_ref[rows, dst:dst + width] = piece.astype(BF16)


def _inproj_kernel(x_ref, g_ref, w_ref, o_ref, wbf_ref):
    @pl.when(pl.program_id(0) == 0)
    def _prep():
        _load_permuted_w_in(w_ref, wbf_ref)

    x = x_ref[...]
    ms = jnp.mean(x * x, axis=-1, keepdims=True)
    xn = (x * lax.rsqrt(ms + EPS) * g_ref[0]).astype(BF16)
    for c in range(PROJ_W // PROJ_CHUNK):
        sl = slice(c * PROJ_CHUNK, (c + 1) * PROJ_CHUNK)
        o_ref[:, sl] = jnp.dot(xn, wbf_ref[:, sl], preferred_element_type=F32).astype(BF16)


def _inproj(x2, g_all, w_all, layer):
    t = x2.shape[0]
    tm = min(ROW_TILE, t)
    return pl.pallas_call(
        _inproj_kernel,
        grid=(t // tm,),
        in_specs=[
            pl.BlockSpec((tm, D_MODEL), lambda i: (i, 0)),
            pl.BlockSpec((1, 1, D_MODEL), lambda i: (layer, 0, 0)),
            pl.BlockSpec((1, D_MODEL, W_IN_COLS), lambda i: (layer, 0, 0), pipeline_mode=pl.Buffered(1)),
        ],
        out_specs=pl.BlockSpec((tm, PROJ_W), lambda i: (i, 0)),
        out_shape=jax.ShapeDtypeStruct((t, PROJ_W), BF16),
        scratch_shapes=[pltpu.VMEM((D_MODEL, PROJ_W), BF16)],
        compiler_params=_cparams(1),
        name="inproj",
    )(x2, g_all, w_all)


def _attn_kernel(bucket_ref, relb_ref, sink_ref, q_ref, kp_ref, kc_ref, kn_ref,
                 vp_ref, vc_ref, vn_ref, o_ref, bias_ref, *, nblk):
    b = pl.program_id(0)
    n = pl.program_id(1)
    kw = 3 * BLOCK

    @pl.when((b == 0) & (n == 0))
    def _build_bias():
        bucket = bucket_ref[...]
        row = lax.broadcasted_iota(jnp.int32, (BLOCK, kw), 0)
        col = lax.broadcasted_iota(jnp.int32, (BLOCK, kw), 1)
        win = jnp.abs(col - BLOCK - row) <= WINDOW
        for hq in range(A_HEADS):
            acc = jnp.zeros((BLOCK, kw), F32)
            for bb in range(N_BUCKETS):
                acc = jnp.where(bucket == bb, relb_ref[bb, hq], acc)
            g = hq % A_GROUP
            bias_ref[hq // A_GROUP, g * BLOCK:(g + 1) * BLOCK, :] = jnp.where(win, acc, NEG_INF)

    col1 = lax.broadcasted_iota(jnp.int32, (1, kw), 1)
    valid = ((col1 >= BLOCK) | (n > 0)) & ((col1 < 2 * BLOCK) | (n < nblk - 1))
    rowg = jnp.right_shift(
        lax.broadcasted_iota(jnp.int32, (A_GROUP * BLOCK, 1), 0), int(math.log2(BLOCK)))
    q = q_ref[...]
    outs = []
    for h in range(A_KV_HEADS):
        hs = slice(h * A_HEAD_DIM, (h + 1) * A_HEAD_DIM)
        k3 = jnp.concatenate([kp_ref[:, hs], kc_ref[:, hs], kn_ref[:, hs]], axis=0)
        v3 = jnp.concatenate([vp_ref[:, hs], vc_ref[:, hs], vn_ref[:, hs]], axis=0)
        q4 = jnp.concatenate(
            [q[:, (A_GROUP * h + g) * A_HEAD_DIM:(A_GROUP * h + g + 1) * A_HEAD_DIM]
             for g in range(A_GROUP)], axis=0)
        s = lax.dot_general(q4, k3, (((1,), (1,)), ((), ())), preferred_element_type=F32)
        s = s * (A_HEAD_DIM ** -0.5) + bias_ref[h]
        s = jnp.where(valid, s, NEG_INF)
        sk = jnp.zeros((A_GROUP * BLOCK, 1), F32)
        for g in range(A_GROUP):
            sk = jnp.where(rowg == g, sink_ref[A_GROUP * h + g], sk)
        m = jnp.maximum(jnp.max(s, axis=-1, keepdims=True), sk)
        p = jnp.exp(s - m)
        den = jnp.sum(p, axis=-1, keepdims=True) + jnp.exp(sk - m)
        o = jnp.dot(p.astype(BF16), v3, preferred_element_type=F32) / den
        for g in range(A_GROUP):
            outs.append(o[g * BLOCK:(g + 1) * BLOCK])
    o_ref[...] = jnp.concatenate(outs, axis=1).astype(BF16)


def _attention(proj, bucket, rel_bias, sink, bsz, seq):
    nblk = seq // BLOCK
    t = bsz * seq
    cq = COL_AQ // QA_W
    ck = COL_AK // KA_W
    cv = COL_AV // KA_W

    def prev(b, n):
        return b * nblk + jnp.maximum(n - 1, 0)

    def cur(b, n):
        return b * nblk + n

    def nxt(b, n):
        return b * nblk + jnp.minimum(n + 1, nblk - 1)

    smem = pl.BlockSpec(memory_space=pltpu.SMEM)
    kv_specs = [pl.BlockSpec((BLOCK, KA_W), functools.partial(lambda b, n, f, c: (f(b, n), c), f=f, c=c))
                for c in (ck, cv) for f in (prev, cur, nxt)]
    return pl.pallas_call(
        functools.partial(_attn_kernel, nblk=nblk),
        grid=(bsz, nblk),
        in_specs=[
            pl.BlockSpec((BLOCK, 3 * BLOCK), lambda b, n: (0, 0)),
            smem, smem,
            pl.BlockSpec((BLOCK, QA_W), lambda b, n: (b * nblk + n, cq)),
        ] + kv_specs,
        out_specs=pl.BlockSpec((BLOCK, QA_W), lambda b, n: (b * nblk + n, 0)),
        out_shape=jax.ShapeDtypeStruct((t, QA_W), BF16),
        scratch_shapes=[pltpu.VMEM((A_KV_HEADS, A_GROUP * BLOCK, 3 * BLOCK), F32)],
        compiler_params=_cparams(2),
        name="attn",
    )(bucket, rel_bias, sink, proj, proj, proj, proj, proj, proj, proj)


def _log_sigmoid(z):
    return jnp.minimum(z, 0.0) - jnp.log1p(jnp.exp(-jnp.abs(z)))


def _log_decay(r, wr, br):
    z = jnp.dot(r, wr, preferred_element_type=F32) + br
    return _log_sigmoid(z) / GATE_TAU


def _chunk_masks():
    row = lax.broadcasted_iota(jnp.int32, (GLA_TILE, GLA_TILE), 0)
    col = lax.broadcasted_iota(jnp.int32, (GLA_TILE, GLA_TILE), 1)
    sh = int(math.log2(CHUNK))
    same = jnp.right_shift(row, sh) == jnp.right_shift(col, sh)
    return same, same & (col <= row), same & (col >= row)


def _f32dot(a, b):
    return jnp.dot(a, b, preferred_element_type=F32, precision=lax.Precision.HIGHEST)


def _split3(x):
    hi = x.astype(BF16)
    r1 = x - hi.astype(F32)
    mid = r1.astype(BF16)
    lo = (r1 - mid.astype(F32)).astype(BF16)
    return hi, mid, lo


def _mask_dot(mask, x):
    m = mask.astype(BF16)
    hi, mid, lo = _split3(x)
    return (jnp.dot(m, hi, preferred_element_type=F32) + jnp.dot(m, mid, preferred_element_type=F32)
            + jnp.dot(m, lo, preferred_element_type=F32))


def _gla_state_dir(k_ref, v_ref, r_ref, wr_ref, br_ref, s_out_ref, state_ref, order, cum_mask, same):
    la = _log_decay(r_ref[...], wr_ref[...], br_ref[...])
    both = _mask_dot(jnp.concatenate([cum_mask, same], axis=0), la)
    cum = both[:GLA_TILE]
    tot = both[GLA_TILE:]
    k_end = (k_ref[...].astype(F32) * jnp.exp(tot - cum)).astype(BF16)
    tot_t = tot.T
    v = v_ref[...]
    for c in order:
        rs = slice(c * CHUNK, (c + 1) * CHUNK)
        s_out_ref[0, c] = state_ref[...].astype(BF16)
        upd_all = lax.dot_general(k_end[rs], v[rs], (((0,), (0,)), ((), ())),
                                  preferred_element_type=F32)
        upd = jnp.concatenate(
            [upd_all[h * B_KEY_DIM:(h + 1) * B_KEY_DIM, h * B_VAL_DIM:(h + 1) * B_VAL_DIM]
             for h in range(B_HEADS)], axis=0)
        decay = jnp.exp(tot_t[:, c * CHUNK:c * CHUNK + 1])
        state_ref[...] = decay * state_ref[...] + upd


def _gla_state_kernel(kf_ref, vf_ref, rf_ref, kb_ref, vb_ref, rb_ref,
                      wrf_ref, brf_ref, wrb_ref, brb_ref,
                      sf_ref, sb_ref, stf_ref, stb_ref):
    @pl.when(pl.program_id(1) == 0)
    def _reset():
        stf_ref[...] = jnp.zeros_like(stf_ref)
        stb_ref[...] = jnp.zeros_like(stb_ref)

    same, lower, upper = _chunk_masks()
    _gla_state_dir(kf_ref, vf_ref, rf_ref, wrf_ref, brf_ref, sf_ref, stf_ref,
                   range(CPT), lower, same)
    _gla_state_dir(kb_ref, vb_ref, rb_ref, wrb_ref, brb_ref, sb_ref, stb_ref,
                   range(CPT - 1, -1, -1), upper, same)


def _gla_states(proj, wrf, brf, wrb, brb, bsz, seq):
    nt = seq // GLA_TILE
    nchunks = seq // CHUNK
    hk = B_HEADS * B_KEY_DIM
    ck, cv, cr = COL_BK // QB_W, COL_BV // VB_W, COL_R // LANE

    def fwd(b, i):
        return b * nt + i

    def bwd(b, i):
        return b * nt + (nt - 1 - i)

    def tile_specs(f):
        return [pl.BlockSpec((GLA_TILE, QB_W), lambda b, i: (f(b, i), ck)),
                pl.BlockSpec((GLA_TILE, VB_W), lambda b, i: (f(b, i), cv)),
                pl.BlockSpec((GLA_TILE, LANE), lambda b, i: (f(b, i), cr))]

    const = lambda b, i: (0, 0)
    out_sds = jax.ShapeDtypeStruct((bsz, nchunks, hk, B_VAL_DIM), BF16)
    return pl.pallas_call(
        _gla_state_kernel,
        grid=(bsz, nt),
        in_specs=tile_specs(fwd) + tile_specs(bwd) + [
            pl.BlockSpec((LANE, hk), const), pl.BlockSpec((1, hk), const),
            pl.BlockSpec((LANE, hk), const), pl.BlockSpec((1, hk), const)],
        out_specs=[pl.BlockSpec((1, CPT, hk, B_VAL_DIM), lambda b, i: (b, i, 0, 0)),
                   pl.BlockSpec((1, CPT, hk, B_VAL_DIM), lambda b, i: (b, nt - 1 - i, 0, 0))],
        out_shape=[out_sds, out_sds],
        scratch_shapes=[pltpu.VMEM((hk, B_VAL_DIM), F32), pltpu.VMEM((hk, B_VAL_DIM), F32)],
        compiler_params=_cparams(2),
        name="gla_state",
    )(proj, proj, proj, proj, proj, proj, wrf, brf, wrb, brb)


def _gla_out_kernel(q_ref, k_ref, v_ref, r_ref, g_ref, sf_ref, sb_ref,
                    wrf_ref, brf_ref, wrb_ref, brb_ref, ng_ref, o_ref):
    same, lower, upper = _chunk_masks()
    r = r_ref[...]
    cum_f = _mask_dot(lower, _log_decay(r, wrf_ref[...], brf_ref[...]))
    cum_b = _mask_dot(upper, _log_decay(r, wrb_ref[...], brb_ref[...]))
    q = q_ref[...].astype(F32) * (B_KEY_DIM ** -0.5)
    k = k_ref[...].astype(F32)
    qd_f = q * jnp.exp(cum_f)
    qd_b = q * jnp.exp(cum_b)
    ki_f = (k * jnp.exp(-cum_f)).astype(BF16)
    ki_b = (k * jnp.exp(-cum_b)).astype(BF16)
    lane_head = jnp.right_shift(
        lax.broadcasted_iota(jnp.int32, (1, B_HEADS * B_KEY_DIM), 1), int(math.log2(B_KEY_DIM)))

    def stack_heads(x):
        return jnp.concatenate(
            [jnp.where(lane_head == h, x, 0.0) for h in range(B_HEADS)], axis=0).astype(BF16)

    qs_f = stack_heads(qd_f)
    qs_b = stack_heads(qd_b)
    nt_dims = (((1,), (1,)), ((), ()))
    sc_f = lax.dot_general(qs_f, ki_f, nt_dims, preferred_element_type=F32)
    sc_b = lax.dot_general(qs_b, ki_b, nt_dims, preferred_element_type=F32)
    lower_s = jnp.concatenate([lower] * B_HEADS, axis=0)
    strict_upper_s = jnp.concatenate([upper & ~lower] * B_HEADS, axis=0)
    p = jnp.where(lower_s, sc_f, jnp.where(strict_upper_s, sc_b, 0.0)).astype(BF16)
    v = v_ref[...]
    inter = []
    for c in range(CPT):
        lhs_f = jnp.concatenate(
            [qs_f[h * GLA_TILE + c * CHUNK:h * GLA_TILE + (c + 1) * CHUNK] for h in range(B_HEADS)], axis=0)
        lhs_b = jnp.concatenate(
            [qs_b[h * GLA_TILE + c * CHUNK:h * GLA_TILE + (c + 1) * CHUNK] for h in range(B_HEADS)], axis=0)
        inter.append(jnp.dot(lhs_f, sf_ref[0, c], preferred_element_type=F32)
                     + jnp.dot(lhs_b, sb_ref[0, c], preferred_element_type=F32))
    ng = ng_ref[...]
    g = g_ref[...].astype(F32)
    for h in range(B_HEADS):
        vs = slice(h * B_VAL_DIM, (h + 1) * B_VAL_DIM)
        o = jnp.dot(p[h * GLA_TILE:(h + 1) * GLA_TILE], v[:, vs], preferred_element_type=F32)
        o = o + jnp.concatenate([inter[c][h * CHUNK:(h + 1) * CHUNK] for c in range(CPT)], axis=0)
        o = o * lax.rsqrt(jnp.mean(o * o, axis=-1, keepdims=True) + EPS) * ng
        gh = g[:, vs]
        o_ref[:, vs] = (o * (gh / (1.0 + jnp.exp(-gh)))).astype(BF16)


def _gla_out(proj, s_f, s_b, wrf, brf, wrb, brb, ng, bsz, seq):
    nt = seq // GLA_TILE
    t = bsz * seq
    hk = B_HEADS * B_KEY_DIM
    const = lambda b, i: (0, 0)
    row = lambda c: (lambda b, i: (b * nt + i, c))
    return pl.pallas_call(
        _gla_out_kernel,
        grid=(bsz, nt),
        in_specs=[
            pl.BlockSpec((GLA_TILE, QB_W), row(COL_BQ // QB_W)),
            pl.BlockSpec((GLA_TILE, QB_W), row(COL_BK // QB_W)),
            pl.BlockSpec((GLA_TILE, VB_W), row(COL_BV // VB_W)),
            pl.BlockSpec((GLA_TILE, LANE), row(COL_R // LANE)),
            pl.BlockSpec((GLA_TILE, VB_W), row(COL_BG // VB_W)),
            pl.BlockSpec((1, CPT, hk, B_VAL_DIM), lambda b, i: (b, i, 0, 0)),
            pl.BlockSpec((1, CPT, hk, B_VAL_DIM), lambda b, i: (b, i, 0, 0)),
            pl.BlockSpec((LANE, hk), const), pl.BlockSpec((1, hk), const),
            pl.BlockSpec((LANE, hk), const), pl.BlockSpec((1, hk), const),
            pl.BlockSpec((1, B_VAL_DIM), const),
        ],
        out_specs=pl.BlockSpec((GLA_TILE, VB_W), lambda b, i: (b * nt + i, 0)),
        out_shape=jax.ShapeDtypeStruct((t, VB_W), BF16),
        compiler_params=_cparams(2),
        name="gla_out",
    )(proj, proj, proj, proj, proj, s_f, s_b, wrf, brf, wrb, brb, ng)


def _sigmoid(x):
    return 1.0 / (1.0 + jnp.exp(-x))


def _merge_kernel(ya_ref, yb_ref, ga_ref, gb_ref, x_ref, wa_ref, wb_ref, wo_ref, o_ref,
                  wa_bf, wb_bf, wo_bf):
    @pl.when(pl.program_id(0) == 0)
    def _prep():
        wa_bf[...] = wa_ref[0].astype(BF16)
        wb_bf[...] = wb_ref[0].astype(BF16)
        wo_bf[...] = wo_ref[0].astype(BF16)

    a = jnp.dot(ya_ref[...], wa_bf[...], preferred_element_type=F32)
    b = jnp.dot(yb_ref[...], wb_bf[...], preferred_element_type=F32)
    merged = _sigmoid(ga_ref[...].astype(F32)) * a + _sigmoid(gb_ref[...].astype(F32)) * b
    o_ref[...] = x_ref[...] + jnp.dot(merged.astype(BF16), wo_bf[...], preferred_element_type=F32)


def _merge(ya, yb, proj, x2, wa_all, wb_all, wo_all, layer):
    t = x2.shape[0]
    tm = min(ROW_TILE, t)
    wsel = lambda i: (layer, 0, 0)
    return pl.pallas_call(
        _merge_kernel,
        grid=(t // tm,),
        in_specs=[
            pl.BlockSpec((tm, QA_W), lambda i: (i, 0)),
            pl.BlockSpec((tm, VB_W), lambda i: (i, 0)),
            pl.BlockSpec((tm, D_MODEL), lambda i: (i, COL_GA // D_MODEL)),
            pl.BlockSpec((tm, D_MODEL), lambda i: (i, COL_GB // D_MODEL)),
            pl.BlockSpec((tm, D_MODEL), lambda i: (i, 0)),
            pl.BlockSpec((1, QA_W, D_MODEL), wsel, pipeline_mode=pl.Buffered(1)),
            pl.BlockSpec((1, VB_W, D_MODEL), wsel, pipeline_mode=pl.Buffered(1)),
            pl.BlockSpec((1, D_MODEL, D_MODEL), wsel, pipeline_mode=pl.Buffered(1)),
        ],
        out_specs=pl.BlockSpec((tm, D_MODEL), lambda i: (i, 0)),
        out_shape=jax.ShapeDtypeStruct((t, D_MODEL), F32),
        scratch_shapes=[pltpu.VMEM((QA_W, D_MODEL), BF16), pltpu.VMEM((VB_W, D_MODEL), BF16),
                        pltpu.VMEM((D_MODEL, D_MODEL), BF16)],
        compiler_params=_cparams(1),
        name="merge",
    )(ya, yb, proj, proj, x2, wa_all, wb_all, wo_all)


def _moe_tile_tokens(t):
    return min(4096, t)


MOE_BLK = 512
MOE_UNIT = 16
MOE_CAP = 1536
XS_W = D_MODEL + LANE
MOE_MT = 384
MOE_CH = 2 * MOE_MT
MOE_CHU = MOE_CH // MOE_UNIT
MOE_NC = 256
assert MOE_CAP >= 2 * MOE_BLK + N_EXPERTS * (MOE_UNIT - 1) and MOE_CAP % MOE_UNIT == 0


def _route(hn_bf, wr_ref, br_ref):
    logits = jnp.dot(hn_bf, wr_ref[...].astype(BF16), preferred_element_type=F32) + br_ref[...]
    lane = lax.broadcasted_iota(jnp.int32, logits.shape, 1).astype(F32)
    big = float(LANE)
    ninf = -jnp.inf
    gl = jnp.where(lane < N_GROUPS, logits, ninf)
    gmax = jnp.max(gl, axis=-1, keepdims=True)
    g_idx = jnp.min(jnp.where(gl == gmax, lane, big), axis=-1, keepdims=True)
    g_w = 1.0 / jnp.sum(jnp.exp(gl - gmax), axis=-1, keepdims=True)
    lo = float(N_GROUPS) + g_idx * float(EXPERTS_PER_GROUP)
    el = jnp.where((lane >= lo) & (lane < lo + EXPERTS_PER_GROUP), logits, ninf)
    v1 = jnp.max(el, axis=-1, keepdims=True)
    i1 = jnp.min(jnp.where(el == v1, lane, big), axis=-1, keepdims=True)
    el2 = jnp.where(lane == i1, ninf, el)
    v2 = jnp.max(el2, axis=-1, keepdims=True)
    i2 = jnp.min(jnp.where(el2 == v2, lane, big), axis=-1, keepdims=True)
    e21 = jnp.exp(v2 - v1)
    w1 = g_w / (1.0 + e21)
    w2 = g_w * e21 / (1.0 + e21)
    return i1 - float(N_GROUPS), i2 - float(N_GROUPS), w1, w2, lane


def _slot_matrix(pos1, pos2, second):
    slot = lax.broadcasted_iota(jnp.int32, (pos1.shape[0], MOE_CAP), 1)
    return jnp.where(slot == pos1.astype(jnp.int32), 1.0,
                     jnp.where(slot == pos2.astype(jnp.int32), second, 0.0)).astype(BF16)


def _split3_f32(w):
    hi = w.astype(BF16).astype(F32)
    r1 = w - hi
    mid = r1.astype(BF16).astype(F32)
    return hi, mid, r1 - mid


def _weight_lanes(w1, w2, lane):
    vals = _split3_f32(w1) + _split3_f32(w2) + (jnp.ones_like(w1),)
    out = jnp.zeros(lane.shape, F32)
    for k, v in enumerate(vals):
        out = jnp.where(lane == float(k), v, out)
    return out.astype(BF16)


def _moe_kernel(x_ref, g_ref, wr_ref, br_ref, wg_ref, wu_ref, wd_ref, o_ref,
                xs_ref, pos_ref, tab_v, tab_s, xt_ref, addr_ref, wgu_bf, wd_bf, sem, *, nblk):
    s = pl.program_id(1)
    dump_row = nblk * MOE_CAP
    tn_dims = (((0,), (0,)), ((), ()))

    @pl.when(s < nblk)
    def _dispatch():
        @pl.when(s == 0)
        def _init():
            xs_ref[dump_row:dump_row + MOE_UNIT, :] = jnp.zeros((MOE_UNIT, XS_W), BF16)
            xt_ref[...] = jnp.zeros_like(xt_ref)

        x = x_ref[...]
        ms = jnp.mean(x * x, axis=-1, keepdims=True)
        hn = (x * lax.rsqrt(ms + EPS) * g_ref[0]).astype(BF16)
        e1, e2, w1, w2, lane = _route(hn, wr_ref, br_ref)
        onehot = ((lane == e1) | (lane == e2)).astype(F32)
        strict_lower = (lax.broadcasted_iota(jnp.int32, (MOE_BLK, MOE_BLK), 1)
                        < lax.broadcasted_iota(jnp.int32, (MOE_BLK, MOE_BLK), 0)).astype(BF16)
        before = jnp.dot(strict_lower, onehot.astype(BF16), preferred_element_type=F32)
        counts = jnp.sum(onehot, axis=0, keepdims=True)
        units = jnp.floor((counts + (MOE_UNIT - 1.0)) * (1.0 / MOE_UNIT))
        strict_upper = (lax.broadcasted_iota(jnp.int32, (LANE, LANE), 0)
                        < lax.broadcasted_iota(jnp.int32, (LANE, LANE), 1)).astype(BF16)
        padded8 = jnp.broadcast_to(units * MOE_UNIT, (SUBLANE, LANE))
        run_off = jnp.dot(padded8.astype(BF16), strict_upper, preferred_element_type=F32)
        start = run_off[0:1, :] + before
        pos1 = jnp.sum(jnp.where(lane == e1, start, 0.0), axis=-1, keepdims=True)
        pos2 = jnp.sum(jnp.where(lane == e2, start, 0.0), axis=-1, keepdims=True)
        pt = _slot_matrix(pos1, pos2, 2.0)
        rows = pl.ds(pl.multiple_of(s * MOE_CAP, MOE_CAP), MOE_CAP)
        for c in range(D_MODEL // MOE_NC):
            cs = slice(c * MOE_NC, (c + 1) * MOE_NC)
            xs_ref[rows, cs] = lax.dot_general(pt, hn[:, cs], tn_dims,
                                               preferred_element_type=F32).astype(BF16)
        xs_ref[rows, D_MODEL:] = lax.dot_general(pt, _weight_lanes(w1, w2, lane), tn_dims,
                                                 preferred_element_type=F32).astype(BF16)
        pos_ref[s] = jnp.where(lane == 0.0, pos1, jnp.where(lane == 1.0, pos2, 0.0))
        sub = lax.broadcasted_iota(jnp.int32, (SUBLANE, LANE), 0)
        tab_v[s] = jnp.where(sub == 0, run_off, jnp.where(sub == 1, padded8 * (1.0 / MOE_UNIT), 0.0)
                             ).astype(jnp.int32)

        @pl.when(s == nblk - 1)
        def _publish():
            cp = pltpu.make_async_copy(tab_v, tab_s, sem)
            cp.start()
            cp.wait()

    @pl.when((s >= nblk) & (s < nblk + N_EXPERTS))
    def _experts():
        e = s - nblk
        wgu_bf[:, :D_EXPERT] = wg_ref[0].astype(BF16)
        wgu_bf[:, D_EXPERT:] = wu_ref[0].astype(BF16)
        wd_bf[...] = wd_ref[0].astype(BF16)
        total = tab_s[0, 1, e]
        for j in range(1, nblk):
            total = total + tab_s[j, 1, e]

        def chunk(c, carry):
            u0 = c * MOE_CHU
            for k in range(MOE_CHU):
                addr_ref[k] = dump_row
            cum = 0
            for j in range(nblk):
                nj = tab_s[j, 1, e]
                run0 = j * MOE_CAP + tab_s[j, 0, e]
                k_off = cum - u0

                def copy_unit(u, c2, run0=run0, k_off=k_off):
                    src = pl.multiple_of(run0 + u * MOE_UNIT, MOE_UNIT)
                    dst = pl.multiple_of((k_off + u) * MOE_UNIT, MOE_UNIT)
                    xt_ref[pl.ds(dst, MOE_UNIT), :] = xs_ref[pl.ds(src, MOE_UNIT), :]
                    addr_ref[k_off + u] = src
                    return c2
                lax.fori_loop(jnp.clip(u0 - cum, 0, nj), jnp.clip(u0 + MOE_CHU - cum, 0, nj),
                              copy_unit, 0)
                cum = cum + nj
            rows_here = jnp.minimum(MOE_CHU, total - u0) * MOE_UNIT

            def tile(t, c3):
                xt = xt_ref[pl.ds(pl.multiple_of(t * MOE_MT, MOE_MT), MOE_MT), :]
                wl = xt[:, D_MODEL:].astype(F32)
                second = wl[:, 6:7] > 1.5
                wcol = jnp.where(second, 0.5 * (wl[:, 3:4] + wl[:, 4:5] + wl[:, 5:6]),
                                 wl[:, 0:1] + wl[:, 1:2] + wl[:, 2:3])
                gu = (jnp.dot(xt[:, :D_MODEL], wgu_bf[...], preferred_element_type=F32)
                      * jnp.where(second, 0.5, 1.0))
                gate = gu[:, :D_EXPERT]
                h = gate * _sigmoid(gate) * gu[:, D_EXPERT:] * wcol
                y = jnp.dot(h.astype(BF16), wd_bf[...], preferred_element_type=F32).astype(BF16)
                for uu in range(MOE_MT // MOE_UNIT):
                    dst = pl.multiple_of(addr_ref[t * (MOE_MT // MOE_UNIT) + uu], MOE_UNIT)
                    xs_ref[pl.ds(dst, MOE_UNIT), :D_MODEL] = y[uu * MOE_UNIT:(uu + 1) * MOE_UNIT]
                return c3
            lax.fori_loop(0, (rows_here + MOE_MT - 1) // MOE_MT, tile, 0)
            return carry
        lax.fori_loop(0, (total + MOE_CHU - 1) // MOE_CHU, chunk, 0)

    @pl.when(s >= nblk + N_EXPERTS)
    def _combine():
        j = s - nblk - N_EXPERTS
        p = pos_ref[j]
        pt = _slot_matrix(p[:, 0:1], p[:, 1:2], 1.0)
        rows = pl.ds(pl.multiple_of(j * MOE_CAP, MOE_CAP), MOE_CAP)
        for c in range(D_MODEL // MOE_NC):
            cs = slice(c * MOE_NC, (c + 1) * MOE_NC)
            o_ref[:, cs] = x_ref[:, cs] + jnp.dot(pt, xs_ref[rows, cs], preferred_element_type=F32)


def _moe(x2, g_all, w_r, b_r, wg_all, wu_all, wd_all, layer, tt):
    t = x2.shape[0]
    nblk = tt // MOE_BLK
    steps = 2 * nblk + N_EXPERTS

    def tok(b, s):
        j = jnp.where(s < nblk, s, jnp.where(s < nblk + N_EXPERTS, nblk - 1, s - nblk - N_EXPERTS))
        return (b * nblk + j, 0)

    def out_tok(b, s):
        return (b * nblk + jnp.maximum(s - nblk - N_EXPERTS, 0), 0)

    wsel = lambda b, s: (layer * N_EXPERTS + jnp.clip(s - nblk, 0, N_EXPERTS - 1), 0, 0)
    const = lambda b, s: (0, 0)
    return pl.pallas_call(
        functools.partial(_moe_kernel, nblk=nblk),
        grid=(t // tt, steps),
        in_specs=[
            pl.BlockSpec((MOE_BLK, D_MODEL), tok),
            pl.BlockSpec((1, 1, D_MODEL), lambda b, s: (layer, 0, 0)),
            pl.BlockSpec((D_MODEL, LANE), const),
            pl.BlockSpec((1, LANE), const),
            pl.BlockSpec((1, D_MODEL, D_EXPERT), wsel),
            pl.BlockSpec((1, D_MODEL, D_EXPERT), wsel),
            pl.BlockSpec((1, D_EXPERT, D_MODEL), wsel),
        ],
        out_specs=pl.BlockSpec((MOE_BLK, D_MODEL), out_tok),
        out_shape=jax.ShapeDtypeStruct((t, D_MODEL), F32),
        scratch_shapes=[
            pltpu.VMEM((nblk * MOE_CAP + MOE_UNIT, XS_W), BF16),
            pltpu.VMEM((nblk, MOE_BLK, LANE), F32),
            pltpu.VMEM((nblk, SUBLANE, LANE), jnp.int32),
            pltpu.SMEM((nblk, SUBLANE, LANE), jnp.int32),
            pltpu.VMEM((MOE_CH, XS_W), BF16),
            pltpu.SMEM((MOE_CHU,), jnp.int32),
            pltpu.VMEM((D_MODEL, 2 * D_EXPERT), BF16),
            pltpu.VMEM((D_EXPERT, D_MODEL), BF16),
            pltpu.SemaphoreType.DMA,
        ],
        compiler_params=_cparams(2),
        name="moe",
    )(x2, g_all, w_r, b_r, wg_all, wu_all, wd_all)


def _final_norm_kernel(x_ref, g_ref, o_ref):
    x = x_ref[...]
    ms = jnp.mean(x * x, axis=-1, keepdims=True)
    o_ref[...] = x * lax.rsqrt(ms + EPS) * g_ref[...]


def _final_norm(x2, g):
    t = x2.shape[0]
    tm = min(ROW_TILE, t)
    return pl.pallas_call(
        _final_norm_kernel,
        grid=(t // tm,),
        in_specs=[pl.BlockSpec((tm, D_MODEL), lambda i: (i, 0)),
                  pl.BlockSpec((1, D_MODEL), lambda i: (0, 0))],
        out_specs=pl.BlockSpec((tm, D_MODEL), lambda i: (i, 0)),
        out_shape=jax.ShapeDtypeStruct((t, D_MODEL), F32),
        compiler_params=_cparams(1),
        name="final_norm",
    )(x2, g)


def _t5_bucket(rel):
    nb = N_BUCKETS // 2
    max_exact = nb // 2
    base = jnp.where(rel > 0, nb, 0)
    n = jnp.abs(rel)
    large = max_exact + (jnp.log(jnp.maximum(n, 1).astype(jnp.float32) / max_exact)
                         / math.log(MAX_DISTANCE / max_exact) * (nb - max_exact)).astype(jnp.int32)
    large = jnp.minimum(large, nb - 1)
    return base + jnp.where(n < max_exact, n, large)


def _pad_rank(wr):
    return jnp.pad(wr, ((0, LANE - B_RANK), (0, 0))).astype(BF16)


def kernel(x, w_in, rel_bias, attn_sink, gla_wr_fwd, gla_br_fwd, gla_wr_bwd, gla_br_bwd, gla_norm, w_branch_a, w_branch_b, w_out, norm_mix, norm_ffn, router_group_w, router_group_b, router_expert_w, router_expert_b, expert_w_gate, expert_w_up, expert_w_down, norm_final):
    bsz, seq, d = x.shape
    depth = w_in.shape[0]
    t = bsz * seq
    tt = _moe_tile_tokens(t)
    q_off = jnp.arange(BLOCK)
    k_off = jnp.arange(3 * BLOCK) - BLOCK
    bucket = _t5_bucket(k_off[None, :] - q_off[:, None]).astype(jnp.int32)
    x2 = x.reshape(t, d)
    norm_mix3 = norm_mix[:, None, :]
    norm_ffn3 = norm_ffn[:, None, :]
    wg_all = expert_w_gate.reshape(depth * N_EXPERTS, d, D_EXPERT)
    wu_all = expert_w_up.reshape(depth * N_EXPERTS, d, D_EXPERT)
    wd_all = expert_w_down.reshape(depth * N_EXPERTS, D_EXPERT, d)
    for l in range(depth):
        proj = _inproj(x2, norm_mix3, w_in, l)
        ya = _attention(proj, bucket, rel_bias, attn_sink[l], bsz, seq)
        wrf, wrb = _pad_rank(gla_wr_fwd[l]), _pad_rank(gla_wr_bwd[l])
        brf, brb = gla_br_fwd[l][None, :], gla_br_bwd[l][None, :]
        s_f, s_b = _gla_states(proj, wrf, brf, wrb, brb, bsz, seq)
        yb = _gla_out(proj, s_f, s_b, wrf, brf, wrb, brb, gla_norm[l][None, :], bsz, seq)
        x2 = _merge(ya, yb, proj, x2, w_branch_a, w_branch_b, w_out, l)
        w_r = jnp.concatenate(
            [router_group_w[l], router_expert_w[l],
             jnp.zeros((d, LANE - N_GROUPS - N_EXPERTS), F32)], axis=1)
        b_r = jnp.concatenate(
            [router_group_b[l], router_expert_b[l],
             jnp.zeros((LANE - N_GROUPS - N_EXPERTS,), F32)])[None, :]
        x2 = _moe(x2, norm_ffn3, w_r, b_r, wg_all, wu_all, wd_all, l, tt)
    return _final_norm(x2, norm_final[None, :]).reshape(bsz, seq, d)
```

```python
import functools
import math

import numpy as np
import jax
import jax.numpy as jnp
from jax import lax
from jax.experimental import pallas as pl
from jax.experimental.pallas import tpu as pltpu

F32 = jnp.float32
BF16 = jnp.bfloat16

D_MODEL = 1024
A_HEADS = 8
A_KV_HEADS = 2
A_HEAD_DIM = 64
A_GROUP = A_HEADS // A_KV_HEADS
WINDOW = 128
BLOCK = 128
N_BUCKETS = 32
MAX_DISTANCE = 128
B_HEADS = 4
B_KEY_DIM = 64
B_VAL_DIM = 128
B_RANK = 16
GATE_TAU = 16.0
CHUNK = 64
N_GROUPS = 4
EXPERTS_PER_GROUP = 8
N_EXPERTS = N_GROUPS * EXPERTS_PER_GROUP
D_EXPERT = 256
EPS = 1e-6
NEG_INF = -1e30

LANE = 128
SUBLANE = 8
VMEM_LIMIT = 56 * 1024 * 1024

QA_W = A_HEADS * A_HEAD_DIM
KA_W = A_KV_HEADS * A_HEAD_DIM
QB_W = B_HEADS * B_KEY_DIM
VB_W = B_HEADS * B_VAL_DIM
COL_GA = 0
COL_GB = COL_GA + D_MODEL
COL_AQ = COL_GB + D_MODEL
COL_BV = COL_AQ + QA_W
COL_BG = COL_BV + VB_W
COL_BQ = COL_BG + VB_W
COL_BK = COL_BQ + QB_W
COL_AK = COL_BK + QB_W
COL_AV = COL_AK + KA_W
COL_R = COL_AV + KA_W
PROJ_W = COL_R + LANE

ROW_TILE = 512
PROJ_CHUNK = 640
GLA_TILE = 256
CPT = GLA_TILE // CHUNK


def _cparams(n_axes):
    return pltpu.CompilerParams(
        dimension_semantics=("arbitrary",) * n_axes, vmem_limit_bytes=VMEM_LIMIT)


_W_IN_SRC = np.cumsum([0, QA_W, KA_W, KA_W, QB_W, QB_W, VB_W, VB_W, B_RANK, D_MODEL, D_MODEL])
_W_IN_SEGMENTS = tuple(zip(
    (COL_AQ, COL_AK, COL_AV, COL_BQ, COL_BK, COL_BV, COL_BG, COL_R, COL_GA, COL_GB),
    (int(s) for s in _W_IN_SRC[:-1]),
    (int(w) for w in np.diff(_W_IN_SRC))))
W_IN_COLS = int(_W_IN_SRC[-1])
W_ROWS_PER_COPY = 256


def _load_permuted_w_in(wt_ref, wbf_ref):
    wbf_ref[COL_R:COL_R + LANE, :] = jnp.zeros((LANE, D_MODEL), BF16)
    for dst, src, width in _W_IN_SEGMENTS:
        for r0 in range(0, width, W_ROWS_PER_COPY):
            n = min(W_ROWS_PER_COPY, width - r0)
            wbf_ref[dst + r0:dst + r0 + n, :] = wt_ref[0, src + r0:src + r0 + n, :].astype(BF16)


def _inproj_kernel(x_ref, g_ref, wt_ref, o_ref, wbf_ref):
    @pl.when(pl.program_id(0) == 0)
    def _prep():
        _load_permuted_w_in(wt_ref, wbf_ref)

    x = x_ref[...]
    ms = jnp.mean(x * x, axis=-1, keepdims=True)
    xn = (x * lax.rsqrt(ms + EPS) * g_ref[0]).astype(BF16)
    for c in range(PROJ_W // PROJ_CHUNK):
        sl = slice(c * PROJ_CHUNK, (c + 1) * PROJ_CHUNK)
        o_ref[:, sl] = lax.dot_general(xn, wbf_ref[sl, :], (((1,), (1,)), ((), ())),
                                       preferred_element_type=F32).astype(BF16)


def _inproj(x2, g_all, wt_all, layer):
    t = x2.shape[0]
    tm = min(ROW_TILE, t)
    return pl.pallas_call(
        _inproj_kernel,
        grid=(t // tm,),
        in_specs=[
            pl.BlockSpec((tm, D_MODEL), lambda i: (i, 0)),
            pl.BlockSpec((1, 1, D_MODEL), lambda i: (layer, 0, 0)),
            pl.BlockSpec((1, W_IN_COLS, D_MODEL), lambda i: (layer, 0, 0), pipeline_mode=pl.Buffered(1)),
        ],
        out_specs=pl.BlockSpec((tm, PROJ_W), lambda i: (i, 0)),
        out_shape=jax.ShapeDtypeStruct((t, PROJ_W), BF16),
        scratch_shapes=[pltpu.VMEM((PROJ_W, D_MODEL), BF16)],
        compiler_params=_cparams(1),
        name="inproj",
    )(x2, g_all, wt_all)


def _attn_kernel(bucket_ref, relb_ref, sink_ref, q_ref, kp_ref, kc_ref, kn_ref,
                 vp_ref, vc_ref, vn_ref, o_ref, bias_ref, *, nblk):
    b = pl.program_id(0)
    n = pl.program_id(1)
    kw = 3 * BLOCK

    @pl.when((b == 0) & (n == 0))
    def _build_bias():
        bucket = bucket_ref[...]
        row = lax.broadcasted_iota(jnp.int32, (BLOCK, kw), 0)
        col = lax.broadcasted_iota(jnp.int32, (BLOCK, kw), 1)
        win = jnp.abs(col - BLOCK - row) <= WINDOW
        for hq in range(A_HEADS):
            acc = jnp.zeros((BLOCK, kw), F32)
            for bb in range(N_BUCKETS):
                acc = jnp.where(bucket == bb, relb_ref[bb, hq], acc)
            g = hq % A_GROUP
            bias_ref[hq // A_GROUP, g * BLOCK:(g + 1) * BLOCK, :] = jnp.where(win, acc, NEG_INF)

    col1 = lax.broadcasted_iota(jnp.int32, (1, kw), 1)
    valid = ((col1 >= BLOCK) | (n > 0)) & ((col1 < 2 * BLOCK) | (n < nblk - 1))
    rowg = jnp.right_shift(
        lax.broadcasted_iota(jnp.int32, (A_GROUP * BLOCK, 1), 0), int(math.log2(BLOCK)))
    q = q_ref[...]
    outs = []
    for h in range(A_KV_HEADS):
        hs = slice(h * A_HEAD_DIM, (h + 1) * A_HEAD_DIM)
        k3 = jnp.concatenate([kp_ref[:, hs], kc_ref[:, hs], kn_ref[:, hs]], axis=0)
        v3 = jnp.concatenate([vp_ref[:, hs], vc_ref[:, hs], vn_ref[:, hs]], axis=0)
        q4 = jnp.concatenate(
            [q[:, (A_GROUP * h + g) * A_HEAD_DIM:(A_GROUP * h + g + 1) * A_HEAD_DIM]
             for g in range(A_GROUP)], axis=0)
        s = lax.dot_general(q4, k3, (((1,), (1,)), ((), ())), preferred_element_type=F32)
        s = s * (A_HEAD_DIM ** -0.5) + bias_ref[h]
        s = jnp.where(valid, s, NEG_INF)
        sk = jnp.zeros((A_GROUP * BLOCK, 1), F32)
        for g in range(A_GROUP):
            sk = jnp.where(rowg == g, sink_ref[A_GROUP * h + g], sk)
        m = jnp.maximum(jnp.max(s, axis=-1, keepdims=True), sk)
        p = jnp.exp(s - m)
        den = jnp.sum(p, axis=-1, keepdims=True) + jnp.exp(sk - m)
        o = jnp.dot(p.astype(BF16), v3, preferred_element_type=F32) / den
        for g in range(A_GROUP):
            outs.append(o[g * BLOCK:(g + 1) * BLOCK])
    o_ref[...] = jnp.concatenate(outs, axis=1).astype(BF16)


def _attention(proj, bucket, rel_bias, sink, bsz, seq):
    nblk = seq // BLOCK
    t = bsz * seq
    cq = COL_AQ // QA_W
    ck = COL_AK // KA_W
    cv = COL_AV // KA_W

    def prev(b, n):
        return b * nblk + jnp.maximum(n - 1, 0)

    def cur(b, n):
        return b * nblk + n

    def nxt(b, n):
        return b * nblk + jnp.minimum(n + 1, nblk - 1)

    smem = pl.BlockSpec(memory_space=pltpu.SMEM)
    kv_specs = [pl.BlockSpec((BLOCK, KA_W), functools.partial(lambda b, n, f, c: (f(b, n), c), f=f, c=c))
                for c in (ck, cv) for f in (prev, cur, nxt)]
    return pl.pallas_call(
        functools.partial(_attn_kernel, nblk=nblk),
        grid=(bsz, nblk),
        in_specs=[
            pl.BlockSpec((BLOCK, 3 * BLOCK), lambda b, n: (0, 0)),
            smem, smem,
            pl.BlockSpec((BLOCK, QA_W), lambda b, n: (b * nblk + n, cq)),
        ] + kv_specs,
        out_specs=pl.BlockSpec((BLOCK, QA_W), lambda b, n: (b * nblk + n, 0)),
        out_shape=jax.ShapeDtypeStruct((t, QA_W), BF16),
        scratch_shapes=[pltpu.VMEM((A_KV_HEADS, A_GROUP * BLOCK, 3 * BLOCK), F32)],
        compiler_params=_cparams(2),
        name="attn",
    )(bucket, rel_bias, sink, proj, proj, proj, proj, proj, proj, proj)


def _log_sigmoid(z):
    return jnp.minimum(z, 0.0) - jnp.log1p(jnp.exp(-jnp.abs(z)))


def _log_decay(r, wr, br):
    z = jnp.dot(r, wr, preferred_element_type=F32) + br
    return _log_sigmoid(z) / GATE_TAU


def _chunk_masks():
    row = lax.broadcasted_iota(jnp.int32, (GLA_TILE, GLA_TILE), 0)
    col = lax.broadcasted_iota(jnp.int32, (GLA_TILE, GLA_TILE), 1)
    sh = int(math.log2(CHUNK))
    same = jnp.right_shift(row, sh) == jnp.right_shift(col, sh)
    return same, same & (col <= row), same & (col >= row)


def _f32dot(a, b):
    return jnp.dot(a, b, preferred_element_type=F32, precision=lax.Precision.HIGHEST)


def _split3(x):
    hi = x.astype(BF16)
    r1 = x - hi.astype(F32)
    mid = r1.astype(BF16)
    lo = (r1 - mid.astype(F32)).astype(BF16)
    return hi, mid, lo


def _mask_dot(mask, x):
    m = mask.astype(BF16)
    hi, mid, lo = _split3(x)
    return (jnp.dot(m, hi, preferred_element_type=F32) + jnp.dot(m, mid, preferred_element_type=F32)
            + jnp.dot(m, lo, preferred_element_type=F32))


def _gla_state_dir(k_ref, v_ref, r_ref, wr_ref, br_ref, s_out_ref, state_ref, order, cum_mask, same):
    la = _log_decay(r_ref[...], wr_ref[...], br_ref[...])
    both = _mask_dot(jnp.concatenate([cum_mask, same], axis=0), la)
    cum = both[:GLA_TILE]
    tot = both[GLA_TILE:]
    k_end = (k_ref[...].astype(F32) * jnp.exp(tot - cum)).astype(BF16)
    tot_t = tot.T
    v = v_ref[...]
    for c in order:
        rs = slice(c * CHUNK, (c + 1) * CHUNK)
        s_out_ref[0, c] = state_ref[...].astype(BF16)
        upd_all = lax.dot_general(k_end[rs], v[rs], (((0,), (0,)), ((), ())),
                                  preferred_element_type=F32)
        upd = jnp.concatenate(
            [upd_all[h * B_KEY_DIM:(h + 1) * B_KEY_DIM, h * B_VAL_DIM:(h + 1) * B_VAL_DIM]
             for h in range(B_HEADS)], axis=0)
        decay = jnp.exp(tot_t[:, c * CHUNK:c * CHUNK + 1])
        state_ref[...] = decay * state_ref[...] + upd


def _gla_state_kernel(kf_ref, vf_ref, rf_ref, kb_ref, vb_ref, rb_ref,
                      wrf_ref, brf_ref, wrb_ref, brb_ref,
                      sf_ref, sb_ref, stf_ref, stb_ref):
    @pl.when(pl.program_id(1) == 0)
    def _reset():
        stf_ref[...] = jnp.zeros_like(stf_ref)
        stb_ref[...] = jnp.zeros_like(stb_ref)

    same, lower, upper = _chunk_masks()
    _gla_state_dir(kf_ref, vf_ref, rf_ref, wrf_ref, brf_ref, sf_ref, stf_ref,
                   range(CPT), lower, same)
    _gla_state_dir(kb_ref, vb_ref, rb_ref, wrb_ref, brb_ref, sb_ref, stb_ref,
                   range(CPT - 1, -1, -1), upper, same)


def _gla_states(proj, wrf, brf, wrb, brb, bsz, seq):
    nt = seq // GLA_TILE
    nchunks = seq // CHUNK
    hk = B_HEADS * B_KEY_DIM
    ck, cv, cr = COL_BK // QB_W, COL_BV // VB_W, COL_R // LANE

    def fwd(b, i):
        return b * nt + i

    def bwd(b, i):
        return b * nt + (nt - 1 - i)

    def tile_specs(f):
        return [pl.BlockSpec((GLA_TILE, QB_W), lambda b, i: (f(b, i), ck)),
                pl.BlockSpec((GLA_TILE, VB_W), lambda b, i: (f(b, i), cv)),
                pl.BlockSpec((GLA_TILE, LANE), lambda b, i: (f(b, i), cr))]

    const = lambda b, i: (0, 0)
    out_sds = jax.ShapeDtypeStruct((bsz, nchunks, hk, B_VAL_DIM), BF16)
    return pl.pallas_call(
        _gla_state_kernel,
        grid=(bsz, nt),
        in_specs=tile_specs(fwd) + tile_specs(bwd) + [
            pl.BlockSpec((LANE, hk), const), pl.BlockSpec((1, hk), const),
            pl.BlockSpec((LANE, hk), const), pl.BlockSpec((1, hk), const)],
        out_specs=[pl.BlockSpec((1, CPT, hk, B_VAL_DIM), lambda b, i: (b, i, 0, 0)),
                   pl.BlockSpec((1, CPT, hk, B_VAL_DIM), lambda b, i: (b, nt - 1 - i, 0, 0))],
        out_shape=[out_sds, out_sds],
        scratch_shapes=[pltpu.VMEM((hk, B_VAL_DIM), F32), pltpu.VMEM((hk, B_VAL_DIM), F32)],
        compiler_params=_cparams(2),
        name="gla_state",
    )(proj, proj, proj, proj, proj, proj, wrf, brf, wrb, brb)


def _gla_out_kernel(q_ref, k_ref, v_ref, r_ref, g_ref, sf_ref, sb_ref,
                    wrf_ref, brf_ref, wrb_ref, brb_ref, ng_ref, o_ref):
    same, lower, upper = _chunk_masks()
    r = r_ref[...]
    cum_f = _mask_dot(lower, _log_decay(r, wrf_ref[...], brf_ref[...]))
    cum_b = _mask_dot(upper, _log_decay(r, wrb_ref[...], brb_ref[...]))
    q = q_ref[...].astype(F32) * (B_KEY_DIM ** -0.5)
    k = k_ref[...].astype(F32)
    qd_f = q * jnp.exp(cum_f)
    qd_b = q * jnp.exp(cum_b)
    ki_f = (k * jnp.exp(-cum_f)).astype(BF16)
    ki_b = (k * jnp.exp(-cum_b)).astype(BF16)
    lane_head = jnp.right_shift(
        lax.broadcasted_iota(jnp.int32, (1, B_HEADS * B_KEY_DIM), 1), int(math.log2(B_KEY_DIM)))

    def stack_heads(x):
        return jnp.concatenate(
            [jnp.where(lane_head == h, x, 0.0) for h in range(B_HEADS)], axis=0).astype(BF16)

    qs_f = stack_heads(qd_f)
    qs_b = stack_heads(qd_b)
    nt_dims = (((1,), (1,)), ((), ()))
    sc_f = lax.dot_general(qs_f, ki_f, nt_dims, preferred_element_type=F32)
    sc_b = lax.dot_general(qs_b, ki_b, nt_dims, preferred_element_type=F32)
    lower_s = jnp.concatenate([lower] * B_HEADS, axis=0)
    strict_upper_s = jnp.concatenate([upper & ~lower] * B_HEADS, axis=0)
    p = jnp.where(lower_s, sc_f, jnp.where(strict_upper_s, sc_b, 0.0)).astype(BF16)
    v = v_ref[...]
    inter = []
    for c in range(CPT):
        lhs_f = jnp.concatenate(
            [qs_f[h * GLA_TILE + c * CHUNK:h * GLA_TILE + (c + 1) * CHUNK] for h in range(B_HEADS)], axis=0)
        lhs_b = jnp.concatenate(
            [qs_b[h * GLA_TILE + c * CHUNK:h * GLA_TILE + (c + 1) * CHUNK] for h in range(B_HEADS)], axis=0)
        inter.append(jnp.dot(lhs_f, sf_ref[0, c], preferred_element_type=F32)
                     + jnp.dot(lhs_b, sb_ref[0, c], preferred_element_type=F32))
    ng = ng_ref[...]
    g = g_ref[...].astype(F32)
    for h in range(B_HEADS):
        vs = slice(h * B_VAL_DIM, (h + 1) * B_VAL_DIM)
        o = jnp.dot(p[h * GLA_TILE:(h + 1) * GLA_TILE], v[:, vs], preferred_element_type=F32)
        o = o + jnp.concatenate([inter[c][h * CHUNK:(h + 1) * CHUNK] for c in range(CPT)], axis=0)
        o = o * lax.rsqrt(jnp.mean(o * o, axis=-1, keepdims=True) + EPS) * ng
        gh = g[:, vs]
        o_ref[:, vs] = (o * (gh / (1.0 + jnp.exp(-gh)))).astype(BF16)


def _gla_out(proj, s_f, s_b, wrf, brf, wrb, brb, ng, bsz, seq):
    nt = seq // GLA_TILE
    t = bsz * seq
    hk = B_HEADS * B_KEY_DIM
    const = lambda b, i: (0, 0)
    row = lambda c: (lambda b, i: (b * nt + i, c))
    return pl.pallas_call(
        _gla_out_kernel,
        grid=(bsz, nt),
        in_specs=[
            pl.BlockSpec((GLA_TILE, QB_W), row(COL_BQ // QB_W)),
            pl.BlockSpec((GLA_TILE, QB_W), row(COL_BK // QB_W)),
            pl.BlockSpec((GLA_TILE, VB_W), row(COL_BV // VB_W)),
            pl.BlockSpec((GLA_TILE, LANE), row(COL_R // LANE)),
            pl.BlockSpec((GLA_TILE, VB_W), row(COL_BG // VB_W)),
            pl.BlockSpec((1, CPT, hk, B_VAL_DIM), lambda b, i: (b, i, 0, 0)),
            pl.BlockSpec((1, CPT, hk, B_VAL_DIM), lambda b, i: (b, i, 0, 0)),
            pl.BlockSpec((LANE, hk), const), pl.BlockSpec((1, hk), const),
            pl.BlockSpec((LANE, hk), const), pl.BlockSpec((1, hk), const),
            pl.BlockSpec((1, B_VAL_DIM), const),
        ],
        out_specs=pl.BlockSpec((GLA_TILE, VB_W), lambda b, i: (b * nt + i, 0)),
        out_shape=jax.ShapeDtypeStruct((t, VB_W), BF16),
        compiler_params=_cparams(2),
        name="gla_out",
    )(proj, proj, proj, proj, proj, s_f, s_b, wrf, brf, wrb, brb, ng)


def _sigmoid(x):
    return 1.0 / (1.0 + jnp.exp(-x))


def _merge_kernel(ya_ref, yb_ref, ga_ref, gb_ref, x_ref, wa_ref, wb_ref, wo_ref, o_ref,
                  wa_bf, wb_bf, wo_bf):
    @pl.when(pl.program_id(0) == 0)
    def _prep():
        wa_bf[...] = wa_ref[0].astype(BF16)
        wb_bf[...] = wb_ref[0].astype(BF16)
        wo_bf[...] = wo_ref[0].astype(BF16)

    a = jnp.dot(ya_ref[...], wa_bf[...], preferred_element_type=F32)
    b = jnp.dot(yb_ref[...], wb_bf[...], preferred_element_type=F32)
    merged = _sigmoid(ga_ref[...].astype(F32)) * a + _sigmoid(gb_ref[...].astype(F32)) * b
    o_ref[...] = x_ref[...] + jnp.dot(merged.astype(BF16), wo_bf[...], preferred_element_type=F32)


def _merge(ya, yb, proj, x2, wa_all, wb_all, wo_all, layer):
    t = x2.shape[0]
    tm = min(ROW_TILE, t)
    wsel = lambda i: (layer, 0, 0)
    return pl.pallas_call(
        _merge_kernel,
        grid=(t // tm,),
        in_specs=[
            pl.BlockSpec((tm, QA_W), lambda i: (i, 0)),
            pl.BlockSpec((tm, VB_W), lambda i: (i, 0)),
            pl.BlockSpec((tm, D_MODEL), lambda i: (i, COL_GA // D_MODEL)),
            pl.BlockSpec((tm, D_MODEL), lambda i: (i, COL_GB // D_MODEL)),
            pl.BlockSpec((tm, D_MODEL), lambda i: (i, 0)),
            pl.BlockSpec((1, QA_W, D_MODEL), wsel, pipeline_mode=pl.Buffered(1)),
            pl.BlockSpec((1, VB_W, D_MODEL), wsel, pipeline_mode=pl.Buffered(1)),
            pl.BlockSpec((1, D_MODEL, D_MODEL), wsel, pipeline_mode=pl.Buffered(1)),
        ],
        out_specs=pl.BlockSpec((tm, D_MODEL), lambda i: (i, 0)),
        out_shape=jax.ShapeDtypeStruct((t, D_MODEL), F32),
        scratch_shapes=[pltpu.VMEM((QA_W, D_MODEL), BF16), pltpu.VMEM((VB_W, D_MODEL), BF16),
                        pltpu.VMEM((D_MODEL, D_MODEL), BF16)],
        compiler_params=_cparams(1),
        name="merge",
    )(ya, yb, proj, proj, x2, wa_all, wb_all, wo_all)


def _moe_tile_tokens(t):
    return min(4096, t)


MOE_BLK = 512
MOE_UNIT = 16
MOE_CAP = 1536
XS_W = D_MODEL + LANE
MOE_MT = 384
MOE_CH = 2 * MOE_MT
MOE_CHU = MOE_CH // MOE_UNIT
MOE_NC = 256
assert MOE_CAP >= 2 * MOE_BLK + N_EXPERTS * (MOE_UNIT - 1) and MOE_CAP % MOE_UNIT == 0


NT_DIMS = (((1,), (1,)), ((), ()))


def _route(hn_bf, wrt_ref, brc_ref):
    logits = lax.dot_general(wrt_ref[...].astype(BF16), hn_bf, NT_DIMS,
                             preferred_element_type=F32) + brc_ref[:, 0:1]
    row = lax.broadcasted_iota(jnp.int32, logits.shape, 0).astype(F32)
    big = float(LANE)
    ninf = -jnp.inf
    gl = jnp.where(row < N_GROUPS, logits, ninf)
    gmax = jnp.max(gl, axis=0, keepdims=True)
    g_idx = jnp.min(jnp.where(gl == gmax, row, big), axis=0, keepdims=True)
    g_w = 1.0 / jnp.sum(jnp.exp(gl - gmax), axis=0, keepdims=True)
    lo = float(N_GROUPS) + g_idx * float(EXPERTS_PER_GROUP)
    el = jnp.where((row >= lo) & (row < lo + EXPERTS_PER_GROUP), logits, ninf)
    v1 = jnp.max(el, axis=0, keepdims=True)
    i1 = jnp.min(jnp.where(el == v1, row, big), axis=0, keepdims=True)
    el2 = jnp.where(row == i1, ninf, el)
    v2 = jnp.max(el2, axis=0, keepdims=True)
    i2 = jnp.min(jnp.where(el2 == v2, row, big), axis=0, keepdims=True)
    e21 = jnp.exp(v2 - v1)
    w1 = g_w / (1.0 + e21)
    w2 = g_w * e21 / (1.0 + e21)
    return i1 - float(N_GROUPS), i2 - float(N_GROUPS), w1, w2, row


def _slot_matrix(pos1, pos2, second):
    slot = lax.broadcasted_iota(jnp.int32, (pos1.shape[0], MOE_CAP), 1)
    return jnp.where(slot == pos1.astype(jnp.int32), 1.0,
                     jnp.where(slot == pos2.astype(jnp.int32), second, 0.0)).astype(BF16)


def _slot_matrix_t(pos1, pos2, second):
    slot = lax.broadcasted_iota(jnp.int32, (MOE_CAP, pos1.shape[1]), 0)
    return jnp.where(slot == pos1.astype(jnp.int32), 1.0,
                     jnp.where(slot == pos2.astype(jnp.int32), second, 0.0)).astype(BF16)


def _split3_f32(w):
    hi = w.astype(BF16).astype(F32)
    r1 = w - hi
    mid = r1.astype(BF16).astype(F32)
    return hi, mid, r1 - mid


def _rows_to_sublanes(vals, row):
    out = jnp.zeros(row.shape, F32)
    for k, v in enumerate(vals):
        out = jnp.where(row == float(k), v, out)
    return out


def _moe_kernel(x_ref, g_ref, wrt_ref, brc_ref, wg_ref, wu_ref, wd_ref, o_ref,
                xs_ref, pos_ref, tab_v, tab_s, xt_ref, addr_ref, wgu_bf, wd_bf, sem, *, nblk):
    s = pl.program_id(1)
    dump_row = nblk * MOE_CAP

    @pl.when(s < nblk)
    def _dispatch():
        @pl.when(s == 0)
        def _init():
            xs_ref[dump_row:dump_row + MOE_UNIT, :] = jnp.zeros((MOE_UNIT, XS_W), BF16)
            xt_ref[...] = jnp.zeros_like(xt_ref)

        x = x_ref[...]
        ms = jnp.mean(x * x, axis=-1, keepdims=True)
        hn = (x * lax.rsqrt(ms + EPS) * g_ref[0]).astype(BF16)
        e1, e2, w1, w2, row = _route(hn, wrt_ref, brc_ref)
        onehot = ((row == e1) | (row == e2)).astype(F32)
        earlier = (lax.broadcasted_iota(jnp.int32, (MOE_BLK, MOE_BLK), 0)
                   < lax.broadcasted_iota(jnp.int32, (MOE_BLK, MOE_BLK), 1)).astype(BF16)
        before = jnp.dot(onehot.astype(BF16), earlier, preferred_element_type=F32)
        counts = jnp.sum(onehot, axis=1, keepdims=True)
        units = jnp.floor((counts + (MOE_UNIT - 1.0)) * (1.0 / MOE_UNIT))
        lower = (lax.broadcasted_iota(jnp.int32, (LANE, LANE), 1)
                 < lax.broadcasted_iota(jnp.int32, (LANE, LANE), 0)).astype(BF16)
        padded = jnp.broadcast_to(units * MOE_UNIT, (LANE, LANE))
        run_off = jnp.dot(lower, padded.astype(BF16), preferred_element_type=F32)[:, 0:1]
        start = run_off + before
        pos1 = jnp.sum(jnp.where(row == e1, start, 0.0), axis=0, keepdims=True)
        pos2 = jnp.sum(jnp.where(row == e2, start, 0.0), axis=0, keepdims=True)
        pt_t = _slot_matrix_t(pos1, pos2, 2.0)
        rows = pl.ds(pl.multiple_of(s * MOE_CAP, MOE_CAP), MOE_CAP)
        for c in range(D_MODEL // MOE_NC):
            cs = slice(c * MOE_NC, (c + 1) * MOE_NC)
            xs_ref[rows, cs] = jnp.dot(pt_t, hn[:, cs], preferred_element_type=F32).astype(BF16)
        w_rows = _rows_to_sublanes(
            _split3_f32(w1) + _split3_f32(w2) + (jnp.ones_like(w1),), row).astype(BF16)
        xs_ref[rows, D_MODEL:] = lax.dot_general(pt_t, w_rows, NT_DIMS,
                                                 preferred_element_type=F32).astype(BF16)
        pos_ref[s] = _rows_to_sublanes((pos1, pos2), row).T
        lane_e = lax.broadcasted_iota(jnp.int32, (LANE, LANE), 1)
        per_expert = jnp.where(lane_e == 0, run_off, jnp.where(lane_e == 1, units, 0.0))
        tab_v[s] = per_expert.T[0:SUBLANE, :].astype(jnp.int32)

        @pl.when(s == nblk - 1)
        def _publish():
            cp = pltpu.make_async_copy(tab_v, tab_s, sem)
            cp.start()
            cp.wait()

    @pl.when((s >= nblk) & (s < nblk + N_EXPERTS))
    def _experts():
        e = s - nblk
        wgu_bf[:, :D_EXPERT] = wg_ref[0].astype(BF16)
        wgu_bf[:, D_EXPERT:] = wu_ref[0].astype(BF16)
        wd_bf[...] = wd_ref[0].astype(BF16)
        total = tab_s[0, 1, e]
        for j in range(1, nblk):
            total = total + tab_s[j, 1, e]

        def chunk(c, carry):
            u0 = c * MOE_CHU
            for k in range(MOE_CHU):
                addr_ref[k] = dump_row
            cum = 0
            for j in range(nblk):
                nj = tab_s[j, 1, e]
                run0 = j * MOE_CAP + tab_s[j, 0, e]
                k_off = cum - u0

                def copy_unit(u, c2, run0=run0, k_off=k_off):
                    src = pl.multiple_of(run0 + u * MOE_UNIT, MOE_UNIT)
                    dst = pl.multiple_of((k_off + u) * MOE_UNIT, MOE_UNIT)
                    xt_ref[pl.ds(dst, MOE_UNIT), :] = xs_ref[pl.ds(src, MOE_UNIT), :]
                    addr_ref[k_off + u] = src
                    return c2
                lax.fori_loop(jnp.clip(u0 - cum, 0, nj), jnp.clip(u0 + MOE_CHU - cum, 0, nj),
                              copy_unit, 0)
                cum = cum + nj
            rows_here = jnp.minimum(MOE_CHU, total - u0) * MOE_UNIT

            def tile(t, c3):
                xt = xt_ref[pl.ds(pl.multiple_of(t * MOE_MT, MOE_MT), MOE_MT), :]
                wl = xt[:, D_MODEL:].astype(F32)
                second = wl[:, 6:7] > 1.5
                wcol = jnp.where(second, 0.5 * (wl[:, 3:4] + wl[:, 4:5] + wl[:, 5:6]),
                                 wl[:, 0:1] + wl[:, 1:2] + wl[:, 2:3])
                gu = (jnp.dot(xt[:, :D_MODEL], wgu_bf[...], preferred_element_type=F32)
                      * jnp.where(second, 0.5, 1.0))
                gate = gu[:, :D_EXPERT]
                h = gate * _sigmoid(gate) * gu[:, D_EXPERT:] * wcol
                y = jnp.dot(h.astype(BF16), wd_bf[...], preferred_element_type=F32).astype(BF16)
                for uu in range(MOE_MT // MOE_UNIT):
                    dst = pl.multiple_of(addr_ref[t * (MOE_MT // MOE_UNIT) + uu], MOE_UNIT)
                    xs_ref[pl.ds(dst, MOE_UNIT), :D_MODEL] = y[uu * MOE_UNIT:(uu + 1) * MOE_UNIT]
                return c3
            lax.fori_loop(0, (rows_here + MOE_MT - 1) // MOE_MT, tile, 0)
            return carry
        lax.fori_loop(0, (total + MOE_CHU - 1) // MOE_CHU, chunk, 0)

    @pl.when(s >= nblk + N_EXPERTS)
    def _combine():
        j = s - nblk - N_EXPERTS
        p = pos_ref[j]
        pt = _slot_matrix(p[:, 0:1], p[:, 1:2], 1.0)
        rows = pl.ds(pl.multiple_of(j * MOE_CAP, MOE_CAP), MOE_CAP)
        for c in range(D_MODEL // MOE_NC):
            cs = slice(c * MOE_NC, (c + 1) * MOE_NC)
            o_ref[:, cs] = x_ref[:, cs] + jnp.dot(pt, xs_ref[rows, cs], preferred_element_type=F32)


def _moe(x2, g_all, w_r, b_r, wg_all, wu_all, wd_all, layer, tt):
    t = x2.shape[0]
    nblk = tt // MOE_BLK
    steps = 2 * nblk + N_EXPERTS

    def tok(b, s):
        j = jnp.where(s < nblk, s, jnp.where(s < nblk + N_EXPERTS, nblk - 1, s - nblk - N_EXPERTS))
        return (b * nblk + j, 0)

    def out_tok(b, s):
        return (b * nblk + jnp.maximum(s - nblk - N_EXPERTS, 0), 0)

    wsel = lambda b, s: (layer * N_EXPERTS + jnp.clip(s - nblk, 0, N_EXPERTS - 1), 0, 0)
    const = lambda b, s: (0, 0)
    return pl.pallas_call(
        functools.partial(_moe_kernel, nblk=nblk),
        grid=(t // tt, steps),
        in_specs=[
            pl.BlockSpec((MOE_BLK, D_MODEL), tok),
            pl.BlockSpec((1, 1, D_MODEL), lambda b, s: (layer, 0, 0)),
            pl.BlockSpec((LANE, D_MODEL), const),
            pl.BlockSpec((LANE, 1), const),
            pl.BlockSpec((1, D_MODEL, D_EXPERT), wsel),
            pl.BlockSpec((1, D_MODEL, D_EXPERT), wsel),
            pl.BlockSpec((1, D_EXPERT, D_MODEL), wsel),
        ],
        out_specs=pl.BlockSpec((MOE_BLK, D_MODEL), out_tok),
        out_shape=jax.ShapeDtypeStruct((t, D_MODEL), F32),
        scratch_shapes=[
            pltpu.VMEM((nblk * MOE_CAP + MOE_UNIT, XS_W), BF16),
            pltpu.VMEM((nblk, MOE_BLK, LANE), F32),
            pltpu.VMEM((nblk, SUBLANE, LANE), jnp.int32),
            pltpu.SMEM((nblk, SUBLANE, LANE), jnp.int32),
            pltpu.VMEM((MOE_CH, XS_W), BF16),
            pltpu.SMEM((MOE_CHU,), jnp.int32),
            pltpu.VMEM((D_MODEL, 2 * D_EXPERT), BF16),
            pltpu.VMEM((D_EXPERT, D_MODEL), BF16),
            pltpu.SemaphoreType.DMA,
        ],
        compiler_params=_cparams(2),
        name="moe",
    )(x2, g_all, w_r, b_r, wg_all, wu_all, wd_all)


def _final_norm_kernel(x_ref, g_ref, o_ref):
    x = x_ref[...]
    ms = jnp.mean(x * x, axis=-1, keepdims=True)
    o_ref[...] = x * lax.rsqrt(ms + EPS) * g_ref[...]


def _final_norm(x2, g):
    t = x2.shape[0]
    tm = min(ROW_TILE, t)
    return pl.pallas_call(
        _final_norm_kernel,
        grid=(t // tm,),
        in_specs=[pl.BlockSpec((tm, D_MODEL), lambda i: (i, 0)),
                  pl.BlockSpec((1, D_MODEL), lambda i: (0, 0))],
        out_specs=pl.BlockSpec((tm, D_MODEL), lambda i: (i, 0)),
        out_shape=jax.ShapeDtypeStruct((t, D_MODEL), F32),
        compiler_params=_cparams(1),
        name="final_norm",
    )(x2, g)


def _t5_bucket(rel):
    nb = N_BUCKETS // 2
    max_exact = nb // 2
    base = jnp.where(rel > 0, nb, 0)
    n = jnp.abs(rel)
    large = max_exact + (jnp.log(jnp.maximum(n, 1).astype(jnp.float32) / max_exact)
                         / math.log(MAX_DISTANCE / max_exact) * (nb - max_exact)).astype(jnp.int32)
    large = jnp.minimum(large, nb - 1)
    return base + jnp.where(n < max_exact, n, large)


def _pad_rank(wr):
    return jnp.pad(wr, ((0, LANE - B_RANK), (0, 0))).astype(BF16)


def kernel(x, w_in, rel_bias, attn_sink, gla_wr_fwd, gla_br_fwd, gla_wr_bwd, gla_br_bwd, gla_norm, w_branch_a, w_branch_b, w_out, norm_mix, norm_ffn, router_group_w, router_group_b, router_expert_w, router_expert_b, expert_w_gate, expert_w_up, expert_w_down, norm_final):
    bsz, seq, d = x.shape
    depth = w_in.shape[0]
    t = bsz * seq
    tt = _moe_tile_tokens(t)
    q_off = jnp.arange(BLOCK)
    k_off = jnp.arange(3 * BLOCK) - BLOCK
    bucket = _t5_bucket(k_off[None, :] - q_off[:, None]).astype(jnp.int32)
    x2 = x.reshape(t, d)
    norm_mix3 = norm_mix[:, None, :]
    w_in_t = jnp.swapaxes(w_in, 1, 2)
    norm_ffn3 = norm_ffn[:, None, :]
    wg_all = expert_w_gate.reshape(depth * N_EXPERTS, d, D_EXPERT)
    wu_all = expert_w_up.reshape(depth * N_EXPERTS, d, D_EXPERT)
    wd_all = expert_w_down.reshape(depth * N_EXPERTS, D_EXPERT, d)
    for l in range(depth):
        proj = _inproj(x2, norm_mix3, w_in_t, l)
        ya = _attention(proj, bucket, rel_bias, attn_sink[l], bsz, seq)
        wrf, wrb = _pad_rank(gla_wr_fwd[l]), _pad_rank(gla_wr_bwd[l])
        brf, brb = gla_br_fwd[l][None, :], gla_br_bwd[l][None, :]
        s_f, s_b = _gla_states(proj, wrf, brf, wrb, brb, bsz, seq)
        yb = _gla_out(proj, s_f, s_b, wrf, brf, wrb, brb, gla_norm[l][None, :], bsz, seq)
        x2 = _merge(ya, yb, proj, x2, w_branch_a, w_branch_b, w_out, l)
        w_r = jnp.concatenate(
            [router_group_w[l].T, router_expert_w[l].T,
             jnp.zeros((LANE - N_GROUPS - N_EXPERTS, d), F32)], axis=0)
        b_r = jnp.concatenate(
            [router_group_b[l], router_expert_b[l],
             jnp.zeros((LANE - N_GROUPS - N_EXPERTS,), F32)])[:, None]
        x2 = _moe(x2, norm_ffn3, w_r, b_r, wg_all, wu_all, wd_all, l, tt)
    return _final_norm(x2, norm_final[None, :]).reshape(bsz, seq, d)
```

```python
import functools
import math

import numpy as np
import jax
import jax.numpy as jnp
from jax import lax
from jax.experimental import pallas as pl
from jax.experimental.pallas import tpu as pltpu

F32 = jnp.float32
BF16 = jnp.bfloat16

D_MODEL = 1024
A_HEADS = 8
A_KV_HEADS = 2
A_HEAD_DIM = 64
A_GROUP = A_HEADS // A_KV_HEADS
WINDOW = 128
BLOCK = 128
N_BUCKETS = 32
MAX_DISTANCE = 128
B_HEADS = 4
B_KEY_DIM = 64
B_VAL_DIM = 128
B_RANK = 16
GATE_TAU = 16.0
CHUNK = 64
N_GROUPS = 4
EXPERTS_PER_GROUP = 8
N_EXPERTS = N_GROUPS * EXPERTS_PER_GROUP
D_EXPERT = 256
EPS = 1e-6
NEG_INF = -1e30

LANE = 128
SUBLANE = 8
VMEM_LIMIT = 56 * 1024 * 1024

QA_W = A_HEADS * A_HEAD_DIM
KA_W = A_KV_HEADS * A_HEAD_DIM
QB_W = B_HEADS * B_KEY_DIM
VB_W = B_HEADS * B_VAL_DIM
COL_GA = 0
COL_GB = COL_GA + D_MODEL
COL_AQ = COL_GB + D_MODEL
COL_BV = COL_AQ + QA_W
COL_BG = COL_BV + VB_W
COL_BQ = COL_BG + VB_W
COL_BK = COL_BQ + QB_W
COL_AK = COL_BK + QB_W
COL_AV = COL_AK + KA_W
COL_R = COL_AV + KA_W
PROJ_W = COL_R + LANE

ROW_TILE = 512
PROJ_CHUNK = 640
GLA_TILE = 256
CPT = GLA_TILE // CHUNK


def _cparams(n_axes):
    return pltpu.CompilerParams(
        dimension_semantics=("arbitrary",) * n_axes, vmem_limit_bytes=VMEM_LIMIT)


_W_IN_SRC = np.cumsum([0, QA_W, KA_W, KA_W, QB_W, QB_W, VB_W, VB_W, B_RANK, D_MODEL, D_MODEL])
_W_IN_SEGMENTS = tuple(zip(
    (COL_AQ, COL_AK, COL_AV, COL_BQ, COL_BK, COL_BV, COL_BG, COL_R, COL_GA, COL_GB),
    (int(s) for s in _W_IN_SRC[:-1]),
    (int(w) for w in np.diff(_W_IN_SRC))))
W_IN_COLS = int(_W_IN_SRC[-1])
W_ROWS_PER_COPY = 256


def _load_permuted_w_in(wt_ref, wbf_ref):
    wbf_ref[COL_R:COL_R + LANE, :] = jnp.zeros((LANE, D_MODEL), BF16)
    for dst, src, width in _W_IN_SEGMENTS:
        for r0 in range(0, width, W_ROWS_PER_COPY):
            n = min(W_ROWS_PER_COPY, width - r0)
            wbf_ref[dst + r0:dst + r0 + n, :] = wt_ref[0, src + r0:src + r0 + n, :].astype(BF16)


def _inproj_kernel(x_ref, g_ref, wt_ref, o_ref, wbf_ref):
    @pl.when(pl.program_id(0) == 0)
    def _prep():
        _load_permuted_w_in(wt_ref, wbf_ref)

    x = x_ref[...]
    ms = jnp.mean(x * x, axis=-1, keepdims=True)
    xn = (x * lax.rsqrt(ms + EPS) * g_ref[0]).astype(BF16)
    for c in range(PROJ_W // PROJ_CHUNK):
        sl = slice(c * PROJ_CHUNK, (c + 1) * PROJ_CHUNK)
        o_ref[:, sl] = lax.dot_general(xn, wbf_ref[sl, :], (((1,), (1,)), ((), ())),
                                       preferred_element_type=F32).astype(BF16)


def _inproj(x2, g_all, wt_all, layer):
    t = x2.shape[0]
    tm = min(ROW_TILE, t)
    return pl.pallas_call(
        _inproj_kernel,
        grid=(t // tm,),
        in_specs=[
            pl.BlockSpec((tm, D_MODEL), lambda i: (i, 0)),
            pl.BlockSpec((1, 1, D_MODEL), lambda i: (layer, 0, 0)),
            pl.BlockSpec((1, W_IN_COLS, D_MODEL), lambda i: (layer, 0, 0), pipeline_mode=pl.Buffered(1)),
        ],
        out_specs=pl.BlockSpec((tm, PROJ_W), lambda i: (i, 0)),
        out_shape=jax.ShapeDtypeStruct((t, PROJ_W), BF16),
        scratch_shapes=[pltpu.VMEM((PROJ_W, D_MODEL), BF16)],
        compiler_params=_cparams(1),
        name="inproj",
    )(x2, g_all, wt_all)


def _attn_kernel(bucket_ref, relb_ref, sink_ref, q_ref, kp_ref, kc_ref, kn_ref,
                 vp_ref, vc_ref, vn_ref, o_ref, bias_ref, *, nblk):
    b = pl.program_id(0)
    n = pl.program_id(1)
    kw = 3 * BLOCK

    @pl.when((b == 0) & (n == 0))
    def _build_bias():
        bucket = bucket_ref[...]
        row = lax.broadcasted_iota(jnp.int32, (BLOCK, kw), 0)
        col = lax.broadcasted_iota(jnp.int32, (BLOCK, kw), 1)
        win = jnp.abs(col - BLOCK - row) <= WINDOW
        keep = (win & (col >= BLOCK), win, win & (col < 2 * BLOCK))
        for hq in range(A_HEADS):
            acc = jnp.zeros((BLOCK, kw), F32)
            for bb in range(N_BUCKETS):
                acc = jnp.where(bucket == bb, relb_ref[bb, hq], acc)
            for v in range(3):
                bias_ref[v, hq] = jnp.where(keep[v], acc, NEG_INF)

    variant = jnp.where(n == 0, 0, jnp.where(n == nblk - 1, 2, 1))
    q = (q_ref[...].astype(F32) * (A_HEAD_DIM ** -0.5)).astype(BF16)
    k3 = [jnp.concatenate([kp_ref[:, hs], kc_ref[:, hs], kn_ref[:, hs]], axis=0)
          for hs in (slice(h * A_HEAD_DIM, (h + 1) * A_HEAD_DIM) for h in range(A_KV_HEADS))]
    v3 = [jnp.concatenate([vp_ref[:, hs], vc_ref[:, hs], vn_ref[:, hs]], axis=0)
          for hs in (slice(h * A_HEAD_DIM, (h + 1) * A_HEAD_DIM) for h in range(A_KV_HEADS))]
    logits = [lax.dot_general(q[:, hq * A_HEAD_DIM:(hq + 1) * A_HEAD_DIM], k3[hq // A_GROUP],
                              (((1,), (1,)), ((), ())), preferred_element_type=F32)
              for hq in range(A_HEADS)]
    probs, dens = [], []
    for hq in range(A_HEADS):
        s = logits[hq] + bias_ref[variant, hq]
        sk = sink_ref[hq]
        m = jnp.maximum(jnp.max(s, axis=-1, keepdims=True), sk)
        p = jnp.exp(s - m)
        dens.append(jnp.sum(p, axis=-1, keepdims=True) + jnp.exp(sk - m))
        probs.append(p.astype(BF16))
    outs = [jnp.dot(probs[hq], v3[hq // A_GROUP], preferred_element_type=F32) / dens[hq]
            for hq in range(A_HEADS)]
    o_ref[...] = jnp.concatenate(outs, axis=1).astype(BF16)


def _attention(proj, bucket, rel_bias, sink, bsz, seq):
    nblk = seq // BLOCK
    t = bsz * seq
    cq = COL_AQ // QA_W
    ck = COL_AK // KA_W
    cv = COL_AV // KA_W

    def prev(b, n):
        return b * nblk + jnp.maximum(n - 1, 0)

    def cur(b, n):
        return b * nblk + n

    def nxt(b, n):
        return b * nblk + jnp.minimum(n + 1, nblk - 1)

    smem = pl.BlockSpec(memory_space=pltpu.SMEM)
    kv_specs = [pl.BlockSpec((BLOCK, KA_W), functools.partial(lambda b, n, f, c: (f(b, n), c), f=f, c=c))
                for c in (ck, cv) for f in (prev, cur, nxt)]
    return pl.pallas_call(
        functools.partial(_attn_kernel, nblk=nblk),
        grid=(bsz, nblk),
        in_specs=[
            pl.BlockSpec((BLOCK, 3 * BLOCK), lambda b, n: (0, 0)),
            smem, smem,
            pl.BlockSpec((BLOCK, QA_W), lambda b, n: (b * nblk + n, cq)),
        ] + kv_specs,
        out_specs=pl.BlockSpec((BLOCK, QA_W), lambda b, n: (b * nblk + n, 0)),
        out_shape=jax.ShapeDtypeStruct((t, QA_W), BF16),
        scratch_shapes=[pltpu.VMEM((3, A_HEADS, BLOCK, 3 * BLOCK), F32)],
        compiler_params=_cparams(2),
        name="attn",
    )(bucket, rel_bias, sink, proj, proj, proj, proj, proj, proj, proj)


def _log_sigmoid(z):
    return jnp.minimum(z, 0.0) - jnp.log1p(jnp.exp(-jnp.abs(z)))


def _log_decay(r, wr, br):
    z = jnp.dot(r, wr, preferred_element_type=F32) + br
    return _log_sigmoid(z) / GATE_TAU


def _chunk_masks():
    row = lax.broadcasted_iota(jnp.int32, (GLA_TILE, GLA_TILE), 0)
    col = lax.broadcasted_iota(jnp.int32, (GLA_TILE, GLA_TILE), 1)
    sh = int(math.log2(CHUNK))
    same = jnp.right_shift(row, sh) == jnp.right_shift(col, sh)
    return same, same & (col <= row), same & (col >= row)


def _f32dot(a, b):
    return jnp.dot(a, b, preferred_element_type=F32, precision=lax.Precision.HIGHEST)


def _split3(x):
    hi = x.astype(BF16)
    r1 = x - hi.astype(F32)
    mid = r1.astype(BF16)
    lo = (r1 - mid.astype(F32)).astype(BF16)
    return hi, mid, lo


def _mask_dot(mask, x):
    m = mask.astype(BF16)
    hi, mid, lo = _split3(x)
    return (jnp.dot(m, hi, preferred_element_type=F32) + jnp.dot(m, mid, preferred_element_type=F32)
            + jnp.dot(m, lo, preferred_element_type=F32))


def _gla_state_dir(k_ref, v_ref, r_ref, wrt_ref, brc_ref, s_out_ref, state_ref, order, cum_mask, same):
    z = lax.dot_general(wrt_ref[...], r_ref[...], (((1,), (1,)), ((), ())),
                        preferred_element_type=F32) + brc_ref[:, 0:1]
    hi, mid, lo = _split3(_log_sigmoid(z) / GATE_TAU)
    m = jnp.concatenate([cum_mask, same], axis=1).astype(BF16)
    both = (jnp.dot(hi, m, preferred_element_type=F32) + jnp.dot(mid, m, preferred_element_type=F32)
            + jnp.dot(lo, m, preferred_element_type=F32))
    cum = both[:, :GLA_TILE]
    tot = both[:, GLA_TILE:]
    k_end = (k_ref[...].astype(F32).T * jnp.exp(tot - cum)).astype(BF16)
    lane_chunk = jnp.right_shift(lax.broadcasted_iota(jnp.int32, (1, GLA_TILE), 1),
                                 int(math.log2(CHUNK)))
    upd = []
    for h in range(B_HEADS):
        kh = k_end[h * B_KEY_DIM:(h + 1) * B_KEY_DIM]
        lhs = jnp.concatenate([jnp.where(lane_chunk == c, kh, 0.0).astype(BF16) for c in range(CPT)],
                              axis=0)
        upd.append(jnp.dot(lhs, v_ref[:, h * B_VAL_DIM:(h + 1) * B_VAL_DIM],
                           preferred_element_type=F32))
    for c in order:
        s_out_ref[0, c] = state_ref[...].astype(BF16)
        upd_c = jnp.concatenate([u[c * B_KEY_DIM:(c + 1) * B_KEY_DIM] for u in upd], axis=0)
        decay = jnp.exp(tot[:, c * CHUNK:c * CHUNK + 1])
        state_ref[...] = decay * state_ref[...] + upd_c


def _gla_state_kernel(kf_ref, vf_ref, rf_ref, kb_ref, vb_ref, rb_ref,
                      wrf_ref, brf_ref, wrb_ref, brb_ref,
                      sf_ref, sb_ref, stf_ref, stb_ref):
    @pl.when(pl.program_id(1) == 0)
    def _reset():
        stf_ref[...] = jnp.zeros_like(stf_ref)
        stb_ref[...] = jnp.zeros_like(stb_ref)

    same, lower, upper = _chunk_masks()
    _gla_state_dir(kf_ref, vf_ref, rf_ref, wrf_ref, brf_ref, sf_ref, stf_ref,
                   range(CPT), upper, same)
    _gla_state_dir(kb_ref, vb_ref, rb_ref, wrb_ref, brb_ref, sb_ref, stb_ref,
                   range(CPT - 1, -1, -1), lower, same)


def _gla_states(proj, wrf_t, brf_c, wrb_t, brb_c, bsz, seq):
    nt = seq // GLA_TILE
    nchunks = seq // CHUNK
    hk = B_HEADS * B_KEY_DIM
    ck, cv, cr = COL_BK // QB_W, COL_BV // VB_W, COL_R // LANE

    def fwd(b, i):
        return b * nt + i

    def bwd(b, i):
        return b * nt + (nt - 1 - i)

    def tile_specs(f):
        return [pl.BlockSpec((GLA_TILE, QB_W), lambda b, i: (f(b, i), ck)),
                pl.BlockSpec((GLA_TILE, VB_W), lambda b, i: (f(b, i), cv)),
                pl.BlockSpec((GLA_TILE, LANE), lambda b, i: (f(b, i), cr))]

    const = lambda b, i: (0, 0)
    out_sds = jax.ShapeDtypeStruct((bsz, nchunks, hk, B_VAL_DIM), BF16)
    return pl.pallas_call(
        _gla_state_kernel,
        grid=(bsz, nt),
        in_specs=tile_specs(fwd) + tile_specs(bwd) + [
            pl.BlockSpec((hk, LANE), const), pl.BlockSpec((hk, 1), const),
            pl.BlockSpec((hk, LANE), const), pl.BlockSpec((hk, 1), const)],
        out_specs=[pl.BlockSpec((1, CPT, hk, B_VAL_DIM), lambda b, i: (b, i, 0, 0)),
                   pl.BlockSpec((1, CPT, hk, B_VAL_DIM), lambda b, i: (b, nt - 1 - i, 0, 0))],
        out_shape=[out_sds, out_sds],
        scratch_shapes=[pltpu.VMEM((hk, B_VAL_DIM), F32), pltpu.VMEM((hk, B_VAL_DIM), F32)],
        compiler_params=_cparams(2),
        name="gla_state",
    )(proj, proj, proj, proj, proj, proj, wrf_t, brf_c, wrb_t, brb_c)


def _gla_out_kernel(q_ref, k_ref, v_ref, r_ref, g_ref, sf_ref, sb_ref,
                    wrf_ref, brf_ref, wrb_ref, brb_ref, ng_ref, o_ref):
    same, lower, upper = _chunk_masks()
    r = r_ref[...]
    cum_f = _mask_dot(lower, _log_decay(r, wrf_ref[...], brf_ref[...]))
    cum_b = _mask_dot(upper, _log_decay(r, wrb_ref[...], brb_ref[...]))
    q = q_ref[...].astype(F32) * (B_KEY_DIM ** -0.5)
    k = k_ref[...].astype(F32)
    qd_f = q * jnp.exp(cum_f)
    qd_b = q * jnp.exp(cum_b)
    ki_f = (k * jnp.exp(-cum_f)).astype(BF16)
    ki_b = (k * jnp.exp(-cum_b)).astype(BF16)
    lane_head = jnp.right_shift(
        lax.broadcasted_iota(jnp.int32, (1, B_HEADS * B_KEY_DIM), 1), int(math.log2(B_KEY_DIM)))

    def stack_heads(x):
        return jnp.concatenate(
            [jnp.where(lane_head == h, x, 0.0) for h in range(B_HEADS)], axis=0).astype(BF16)

    qs_f = stack_heads(qd_f)
    qs_b = stack_heads(qd_b)
    nt_dims = (((1,), (1,)), ((), ()))
    sc_f = lax.dot_general(qs_f, ki_f, nt_dims, preferred_element_type=F32)
    sc_b = lax.dot_general(qs_b, ki_b, nt_dims, preferred_element_type=F32)
    lower_s = jnp.concatenate([lower] * B_HEADS, axis=0)
    strict_upper_s = jnp.concatenate([upper & ~lower] * B_HEADS, axis=0)
    p = jnp.where(lower_s, sc_f, jnp.where(strict_upper_s, sc_b, 0.0)).astype(BF16)
    v = v_ref[...]
    inter = []
    for c in range(CPT):
        lhs_f = jnp.concatenate(
            [qs_f[h * GLA_TILE + c * CHUNK:h * GLA_TILE + (c + 1) * CHUNK] for h in range(B_HEADS)], axis=0)
        lhs_b = jnp.concatenate(
            [qs_b[h * GLA_TILE + c * CHUNK:h * GLA_TILE + (c + 1) * CHUNK] for h in range(B_HEADS)], axis=0)
        inter.append(jnp.dot(lhs_f, sf_ref[0, c], preferred_element_type=F32)
                     + jnp.dot(lhs_b, sb_ref[0, c], preferred_element_type=F32))
    ng = ng_ref[...]
    g = g_ref[...].astype(F32)
    for h in range(B_HEADS):
        vs = slice(h * B_VAL_DIM, (h + 1) * B_VAL_DIM)
        o = jnp.dot(p[h * GLA_TILE:(h + 1) * GLA_TILE], v[:, vs], preferred_element_type=F32)
        o = o + jnp.concatenate([inter[c][h * CHUNK:(h + 1) * CHUNK] for c in range(CPT)], axis=0)
        o = o * lax.rsqrt(jnp.mean(o * o, axis=-1, keepdims=True) + EPS) * ng
        gh = g[:, vs]
        o_ref[:, vs] = (o * (gh / (1.0 + jnp.exp(-gh)))).astype(BF16)


def _gla_out(proj, s_f, s_b, wrf, brf, wrb, brb, ng, bsz, seq):
    nt = seq // GLA_TILE
    t = bsz * seq
    hk = B_HEADS * B_KEY_DIM
    const = lambda b, i: (0, 0)
    row = lambda c: (lambda b, i: (b * nt + i, c))
    return pl.pallas_call(
        _gla_out_kernel,
        grid=(bsz, nt),
        in_specs=[
            pl.BlockSpec((GLA_TILE, QB_W), row(COL_BQ // QB_W)),
            pl.BlockSpec((GLA_TILE, QB_W), row(COL_BK // QB_W)),
            pl.BlockSpec((GLA_TILE, VB_W), row(COL_BV // VB_W)),
            pl.BlockSpec((GLA_TILE, LANE), row(COL_R // LANE)),
            pl.BlockSpec((GLA_TILE, VB_W), row(COL_BG // VB_W)),
            pl.BlockSpec((1, CPT, hk, B_VAL_DIM), lambda b, i: (b, i, 0, 0)),
            pl.BlockSpec((1, CPT, hk, B_VAL_DIM), lambda b, i: (b, i, 0, 0)),
            pl.BlockSpec((LANE, hk), const), pl.BlockSpec((1, hk), const),
            pl.BlockSpec((LANE, hk), const), pl.BlockSpec((1, hk), const),
            pl.BlockSpec((1, B_VAL_DIM), const),
        ],
        out_specs=pl.BlockSpec((GLA_TILE, VB_W), lambda b, i: (b * nt + i, 0)),
        out_shape=jax.ShapeDtypeStruct((t, VB_W), BF16),
        compiler_params=_cparams(2),
        name="gla_out",
    )(proj, proj, proj, proj, proj, s_f, s_b, wrf, brf, wrb, brb, ng)


def _sigmoid(x):
    return 1.0 / (1.0 + jnp.exp(-x))


def _merge_kernel(ya_ref, yb_ref, ga_ref, gb_ref, x_ref, wa_ref, wb_ref, wo_ref, o_ref,
                  wa_bf, wb_bf, wo_bf):
    @pl.when(pl.program_id(0) == 0)
    def _prep():
        wa_bf[...] = wa_ref[0].astype(BF16)
        wb_bf[...] = wb_ref[0].astype(BF16)
        wo_bf[...] = wo_ref[0].astype(BF16)

    a = jnp.dot(ya_ref[...], wa_bf[...], preferred_element_type=F32)
    b = jnp.dot(yb_ref[...], wb_bf[...], preferred_element_type=F32)
    merged = _sigmoid(ga_ref[...].astype(F32)) * a + _sigmoid(gb_ref[...].astype(F32)) * b
    o_ref[...] = x_ref[...] + jnp.dot(merged.astype(BF16), wo_bf[...], preferred_element_type=F32)


def _merge(ya, yb, proj, x2, wa_all, wb_all, wo_all, layer):
    t = x2.shape[0]
    tm = min(ROW_TILE, t)
    wsel = lambda i: (layer, 0, 0)
    return pl.pallas_call(
        _merge_kernel,
        grid=(t // tm,),
        in_specs=[
            pl.BlockSpec((tm, QA_W), lambda i: (i, 0)),
            pl.BlockSpec((tm, VB_W), lambda i: (i, 0)),
            pl.BlockSpec((tm, D_MODEL), lambda i: (i, COL_GA // D_MODEL)),
            pl.BlockSpec((tm, D_MODEL), lambda i: (i, COL_GB // D_MODEL)),
            pl.BlockSpec((tm, D_MODEL), lambda i: (i, 0)),
            pl.BlockSpec((1, QA_W, D_MODEL), wsel, pipeline_mode=pl.Buffered(1)),
            pl.BlockSpec((1, VB_W, D_MODEL), wsel, pipeline_mode=pl.Buffered(1)),
            pl.BlockSpec((1, D_MODEL, D_MODEL), wsel, pipeline_mode=pl.Buffered(1)),
        ],
        out_specs=pl.BlockSpec((tm, D_MODEL), lambda i: (i, 0)),
        out_shape=jax.ShapeDtypeStruct((t, D_MODEL), F32),
        scratch_shapes=[pltpu.VMEM((QA_W, D_MODEL), BF16), pltpu.VMEM((VB_W, D_MODEL), BF16),
                        pltpu.VMEM((D_MODEL, D_MODEL), BF16)],
        compiler_params=_cparams(1),
        name="merge",
    )(ya, yb, proj, proj, x2, wa_all, wb_all, wo_all)


def _moe_tile_tokens(t):
    return min(4096, t)


MOE_BLK = 512
MOE_UNIT = 16
MOE_CAP = 1536
XS_W = D_MODEL + LANE
MOE_MT = 384
MOE_CH = 2 * MOE_MT
MOE_CHU = MOE_CH // MOE_UNIT
MOE_NC = 256
assert MOE_CAP >= 2 * MOE_BLK + N_EXPERTS * (MOE_UNIT - 1) and MOE_CAP % MOE_UNIT == 0


NT_DIMS = (((1,), (1,)), ((), ()))


def _route(hn_bf, wrt_ref, brc_ref):
    logits = lax.dot_general(wrt_ref[...].astype(BF16), hn_bf, NT_DIMS,
                             preferred_element_type=F32) + brc_ref[:, 0:1]
    row = lax.broadcasted_iota(jnp.int32, logits.shape, 0).astype(F32)
    big = float(LANE)
    ninf = -jnp.inf
    gl = jnp.where(row < N_GROUPS, logits, ninf)
    gmax = jnp.max(gl, axis=0, keepdims=True)
    g_idx = jnp.min(jnp.where(gl == gmax, row, big), axis=0, keepdims=True)
    g_w = 1.0 / jnp.sum(jnp.exp(gl - gmax), axis=0, keepdims=True)
    lo = float(N_GROUPS) + g_idx * float(EXPERTS_PER_GROUP)
    el = jnp.where((row >= lo) & (row < lo + EXPERTS_PER_GROUP), logits, ninf)
    v1 = jnp.max(el, axis=0, keepdims=True)
    i1 = jnp.min(jnp.where(el == v1, row, big), axis=0, keepdims=True)
    el2 = jnp.where(row == i1, ninf, el)
    v2 = jnp.max(el2, axis=0, keepdims=True)
    i2 = jnp.min(jnp.where(el2 == v2, row, big), axis=0, keepdims=True)
    e21 = jnp.exp(v2 - v1)
    w1 = g_w / (1.0 + e21)
    w2 = g_w * e21 / (1.0 + e21)
    return i1 - float(N_GROUPS), i2 - float(N_GROUPS), w1, w2, row


def _slot_matrix(pos1, pos2, second):
    slot = lax.broadcasted_iota(jnp.int32, (pos1.shape[0], MOE_CAP), 1)
    return jnp.where(slot == pos1.astype(jnp.int32), 1.0,
                     jnp.where(slot == pos2.astype(jnp.int32), second, 0.0)).astype(BF16)


def _slot_matrix_t(pos1, pos2, second):
    slot = lax.broadcasted_iota(jnp.int32, (MOE_CAP, pos1.shape[1]), 0)
    return jnp.where(slot == pos1.astype(jnp.int32), 1.0,
                     jnp.where(slot == pos2.astype(jnp.int32), second, 0.0)).astype(BF16)


def _split3_f32(w):
    hi = w.astype(BF16).astype(F32)
    r1 = w - hi
    mid = r1.astype(BF16).astype(F32)
    return hi, mid, r1 - mid


def _rows_to_sublanes(vals, row):
    out = jnp.zeros(row.shape, F32)
    for k, v in enumerate(vals):
        out = jnp.where(row == float(k), v, out)
    return out


def _moe_kernel(x_ref, g_ref, wrt_ref, brc_ref, wg_ref, wu_ref, wd_ref, o_ref,
                xs_ref, pos_ref, tab_v, tab_s, xt_ref, addr_ref, wgu_bf, wd_bf, sem, *, nblk):
    s = pl.program_id(1)
    dump_row = nblk * MOE_CAP

    @pl.when(s < nblk)
    def _dispatch():
        @pl.when(s == 0)
        def _init():
            xs_ref[dump_row:dump_row + MOE_UNIT, :] = jnp.zeros((MOE_UNIT, XS_W), BF16)
            xt_ref[...] = jnp.zeros_like(xt_ref)

        x = x_ref[...]
        ms = jnp.mean(x * x, axis=-1, keepdims=True)
        hn = (x * lax.rsqrt(ms + EPS) * g_ref[0]).astype(BF16)
        e1, e2, w1, w2, row = _route(hn, wrt_ref, brc_ref)
        onehot = ((row == e1) | (row == e2)).astype(F32)
        earlier = (lax.broadcasted_iota(jnp.int32, (MOE_BLK, MOE_BLK), 0)
                   < lax.broadcasted_iota(jnp.int32, (MOE_BLK, MOE_BLK), 1)).astype(BF16)
        before = jnp.dot(onehot.astype(BF16), earlier, preferred_element_type=F32)
        counts = jnp.sum(onehot, axis=1, keepdims=True)
        units = jnp.floor((counts + (MOE_UNIT - 1.0)) * (1.0 / MOE_UNIT))
        lower = (lax.broadcasted_iota(jnp.int32, (LANE, LANE), 1)
                 < lax.broadcasted_iota(jnp.int32, (LANE, LANE), 0)).astype(BF16)
        padded = jnp.broadcast_to(units * MOE_UNIT, (LANE, LANE))
        run_off = jnp.dot(lower, padded.astype(BF16), preferred_element_type=F32)[:, 0:1]
        start = run_off + before
        pos1 = jnp.sum(jnp.where(row == e1, start, 0.0), axis=0, keepdims=True)
        pos2 = jnp.sum(jnp.where(row == e2, start, 0.0), axis=0, keepdims=True)
        pt_t = _slot_matrix_t(pos1, pos2, 2.0)
        rows = pl.ds(pl.multiple_of(s * MOE_CAP, MOE_CAP), MOE_CAP)
        for c in range(D_MODEL // MOE_NC):
            cs = slice(c * MOE_NC, (c + 1) * MOE_NC)
            xs_ref[rows, cs] = jnp.dot(pt_t, hn[:, cs], preferred_element_type=F32).astype(BF16)
        w_rows = _rows_to_sublanes(
            _split3_f32(w1) + _split3_f32(w2) + (jnp.ones_like(w1),), row).astype(BF16)
        xs_ref[rows, D_MODEL:] = lax.dot_general(pt_t, w_rows, NT_DIMS,
                                                 preferred_element_type=F32).astype(BF16)
        pos_ref[s] = _rows_to_sublanes((pos1, pos2), row).T
        lane_e = lax.broadcasted_iota(jnp.int32, (LANE, LANE), 1)
        per_expert = jnp.where(lane_e == 0, run_off, jnp.where(lane_e == 1, units, 0.0))
        tab_v[s] = per_expert.T[0:SUBLANE, :].astype(jnp.int32)

        @pl.when(s == nblk - 1)
        def _publish():
            cp = pltpu.make_async_copy(tab_v, tab_s, sem)
            cp.start()
            cp.wait()

    @pl.when((s >= nblk) & (s < nblk + N_EXPERTS))
    def _experts():
        e = s - nblk
        wgu_bf[:, :D_EXPERT] = wg_ref[0].astype(BF16)
        wgu_bf[:, D_EXPERT:] = wu_ref[0].astype(BF16)
        wd_bf[...] = wd_ref[0].astype(BF16)
        total = tab_s[0, 1, e]
        for j in range(1, nblk):
            total = total + tab_s[j, 1, e]

        def chunk(c, carry):
            u0 = c * MOE_CHU
            for k in range(MOE_CHU):
                addr_ref[k] = dump_row
            cum = 0
            for j in range(nblk):
                nj = tab_s[j, 1, e]
                run0 = j * MOE_CAP + tab_s[j, 0, e]
                k_off = cum - u0

                def copy_unit(u, c2, run0=run0, k_off=k_off):
                    src = pl.multiple_of(run0 + u * MOE_UNIT, MOE_UNIT)
                    dst = pl.multiple_of((k_off + u) * MOE_UNIT, MOE_UNIT)
                    xt_ref[pl.ds(dst, MOE_UNIT), :] = xs_ref[pl.ds(src, MOE_UNIT), :]
                    addr_ref[k_off + u] = src
                    return c2
                lax.fori_loop(jnp.clip(u0 - cum, 0, nj), jnp.clip(u0 + MOE_CHU - cum, 0, nj),
                              copy_unit, 0)
                cum = cum + nj
            rows_here = jnp.minimum(MOE_CHU, total - u0) * MOE_UNIT

            def tile(t, c3):
                xt = xt_ref[pl.ds(pl.multiple_of(t * MOE_MT, MOE_MT), MOE_MT), :]
                wl = xt[:, D_MODEL:].astype(F32)
                second = wl[:, 6:7] > 1.5
                wcol = jnp.where(second, 0.5 * (wl[:, 3:4] + wl[:, 4:5] + wl[:, 5:6]),
                                 wl[:, 0:1] + wl[:, 1:2] + wl[:, 2:3])
                gu = (jnp.dot(xt[:, :D_MODEL], wgu_bf[...], preferred_element_type=F32)
                      * jnp.where(second, 0.5, 1.0))
                gate = gu[:, :D_EXPERT]
                h = gate * _sigmoid(gate) * gu[:, D_EXPERT:] * wcol
                y = jnp.dot(h.astype(BF16), wd_bf[...], preferred_element_type=F32).astype(BF16)
                for uu in range(MOE_MT // MOE_UNIT):
                    dst = pl.multiple_of(addr_ref[t * (MOE_MT // MOE_UNIT) + uu], MOE_UNIT)
                    xs_ref[pl.ds(dst, MOE_UNIT), :D_MODEL] = y[uu * MOE_UNIT:(uu + 1) * MOE_UNIT]
                return c3
            lax.fori_loop(0, (rows_here + MOE_MT - 1) // MOE_MT, tile, 0)
            return carry
        lax.fori_loop(0, (total + MOE_CHU - 1) // MOE_CHU, chunk, 0)

    @pl.when(s >= nblk + N_EXPERTS)
    def _combine():
        j = s - nblk - N_EXPERTS
        p = pos_ref[j]
        pt = _slot_matrix(p[:, 0:1], p[:, 1:2], 1.0)
        rows = pl.ds(pl.multiple_of(j * MOE_CAP, MOE_CAP), MOE_CAP)
        for c in range(D_MODEL // MOE_NC):
            cs = slice(c * MOE_NC, (c + 1) * MOE_NC)
            o_ref[:, cs] = x_ref[:, cs] + jnp.dot(pt, xs_ref[rows, cs], preferred_element_type=F32)


def _moe(x2, g_all, w_r, b_r, wg_all, wu_all, wd_all, layer, tt):
    t = x2.shape[0]
    nblk = tt // MOE_BLK
    steps = 2 * nblk + N_EXPERTS

    def tok(b, s):
        j = jnp.where(s < nblk, s, jnp.where(s < nblk + N_EXPERTS, nblk - 1, s - nblk - N_EXPERTS))
        return (b * nblk + j, 0)

    def out_tok(b, s):
        return (b * nblk + jnp.maximum(s - nblk - N_EXPERTS, 0), 0)

    wsel = lambda b, s: (layer * N_EXPERTS + jnp.clip(s - nblk, 0, N_EXPERTS - 1), 0, 0)
    const = lambda b, s: (0, 0)
    return pl.pallas_call(
        functools.partial(_moe_kernel, nblk=nblk),
        grid=(t // tt, steps),
        in_specs=[
            pl.BlockSpec((MOE_BLK, D_MODEL), tok),
            pl.BlockSpec((1, 1, D_MODEL), lambda b, s: (layer, 0, 0)),
            pl.BlockSpec((LANE, D_MODEL), const),
            pl.BlockSpec((LANE, 1), const),
            pl.BlockSpec((1, D_MODEL, D_EXPERT), wsel),
            pl.BlockSpec((1, D_MODEL, D_EXPERT), wsel),
            pl.BlockSpec((1, D_EXPERT, D_MODEL), wsel),
        ],
        out_specs=pl.BlockSpec((MOE_BLK, D_MODEL), out_tok),
        out_shape=jax.ShapeDtypeStruct((t, D_MODEL), F32),
        scratch_shapes=[
            pltpu.VMEM((nblk * MOE_CAP + MOE_UNIT, XS_W), BF16),
            pltpu.VMEM((nblk, MOE_BLK, LANE), F32),
            pltpu.VMEM((nblk, SUBLANE, LANE), jnp.int32),
            pltpu.SMEM((nblk, SUBLANE, LANE), jnp.int32),
            pltpu.VMEM((MOE_CH, XS_W), BF16),
            pltpu.SMEM((MOE_CHU,), jnp.int32),
            pltpu.VMEM((D_MODEL, 2 * D_EXPERT), BF16),
            pltpu.VMEM((D_EXPERT, D_MODEL), BF16),
            pltpu.SemaphoreType.DMA,
        ],
        compiler_params=_cparams(2),
        name="moe",
    )(x2, g_all, w_r, b_r, wg_all, wu_all, wd_all)


def _final_norm_kernel(x_ref, g_ref, o_ref):
    x = x_ref[...]
    ms = jnp.mean(x * x, axis=-1, keepdims=True)
    o_ref[...] = x * lax.rsqrt(ms + EPS) * g_ref[...]


def _final_norm(x2, g):
    t = x2.shape[0]
    tm = min(ROW_TILE, t)
    return pl.pallas_call(
        _final_norm_kernel,
        grid=(t // tm,),
        in_specs=[pl.BlockSpec((tm, D_MODEL), lambda i: (i, 0)),
                  pl.BlockSpec((1, D_MODEL), lambda i: (0, 0))],
        out_specs=pl.BlockSpec((tm, D_MODEL), lambda i: (i, 0)),
        out_shape=jax.ShapeDtypeStruct((t, D_MODEL), F32),
        compiler_params=_cparams(1),
        name="final_norm",
    )(x2, g)


def _t5_bucket(rel):
    nb = N_BUCKETS // 2
    max_exact = nb // 2
    base = jnp.where(rel > 0, nb, 0)
    n = jnp.abs(rel)
    large = max_exact + (jnp.log(jnp.maximum(n, 1).astype(jnp.float32) / max_exact)
                         / math.log(MAX_DISTANCE / max_exact) * (nb - max_exact)).astype(jnp.int32)
    large = jnp.minimum(large, nb - 1)
    return base + jnp.where(n < max_exact, n, large)


def _pad_rank(wr):
    return jnp.pad(wr, ((0, LANE - B_RANK), (0, 0))).astype(BF16)


def kernel(x, w_in, rel_bias, attn_sink, gla_wr_fwd, gla_br_fwd, gla_wr_bwd, gla_br_bwd, gla_norm, w_branch_a, w_branch_b, w_out, norm_mix, norm_ffn, router_group_w, router_group_b, router_expert_w, router_expert_b, expert_w_gate, expert_w_up, expert_w_down, norm_final):
    bsz, seq, d = x.shape
    depth = w_in.shape[0]
    t = bsz * seq
    tt = _moe_tile_tokens(t)
    q_off = jnp.arange(BLOCK)
    k_off = jnp.arange(3 * BLOCK) - BLOCK
    bucket = _t5_bucket(k_off[None, :] - q_off[:, None]).astype(jnp.int32)
    x2 = x.reshape(t, d)
    norm_mix3 = norm_mix[:, None, :]
    w_in_t = jnp.swapaxes(w_in, 1, 2)
    norm_ffn3 = norm_ffn[:, None, :]
    wg_all = expert_w_gate.reshape(depth * N_EXPERTS, d, D_EXPERT)
    wu_all = expert_w_up.reshape(depth * N_EXPERTS, d, D_EXPERT)
    wd_all = expert_w_down.reshape(depth * N_EXPERTS, D_EXPERT, d)
    for l in range(depth):
        proj = _inproj(x2, norm_mix3, w_in_t, l)
        ya = _attention(proj, bucket, rel_bias, attn_sink[l], bsz, seq)
        wrf, wrb = _pad_rank(gla_wr_fwd[l]), _pad_rank(gla_wr_bwd[l])
        brf, brb = gla_br_fwd[l][None, :], gla_br_bwd[l][None, :]
        s_f, s_b = _gla_states(proj, wrf.T, brf.T, wrb.T, brb.T, bsz, seq)
        yb = _gla_out(proj, s_f, s_b, wrf, brf, wrb, brb, gla_norm[l][None, :], bsz, seq)
        x2 = _merge(ya, yb, proj, x2, w_branch_a, w_branch_b, w_out, l)
        w_r = jnp.concatenate(
            [router_group_w[l].T, router_expert_w[l].T,
             jnp.zeros((LANE - N_GROUPS - N_EXPERTS, d), F32)], axis=0)
        b_r = jnp.concatenate(
            [router_group_b[l], router_expert_b[l],
             jnp.zeros((LANE - N_GROUPS - N_EXPERTS,), F32)])[:, None]
        x2 = _moe(x2, norm_ffn3, w_r, b_r, wg_all, wu_all, wd_all, l, tt)
    return _final_norm(x2, norm_final[None, :]).reshape(bsz, seq, d)
```

```python
import functools
import math

import numpy as np
import jax
import jax.numpy as jnp
from jax import lax
from jax.experimental import pallas as pl
from jax.experimental.pallas import tpu as pltpu

F32 = jnp.float32
BF16 = jnp.bfloat16

D_MODEL = 1024
A_HEADS = 8
A_KV_HEADS = 2
A_HEAD_DIM = 64
A_GROUP = A_HEADS // A_KV_HEADS
WINDOW = 128
BLOCK = 128
N_BUCKETS = 32
MAX_DISTANCE = 128
B_HEADS = 4
B_KEY_DIM = 64
B_VAL_DIM = 128
B_RANK = 16
GATE_TAU = 16.0
CHUNK = 64
N_GROUPS = 4
EXPERTS_PER_GROUP = 8
N_EXPERTS = N_GROUPS * EXPERTS_PER_GROUP
D_EXPERT = 256
EPS = 1e-6
NEG_INF = -1e30

LANE = 128
SUBLANE = 8
VMEM_LIMIT = 56 * 1024 * 1024

QA_W = A_HEADS * A_HEAD_DIM
KA_W = A_KV_HEADS * A_HEAD_DIM
QB_W = B_HEADS * B_KEY_DIM
VB_W = B_HEADS * B_VAL_DIM
COL_GA = 0
COL_GB = COL_GA + D_MODEL
COL_AQ = COL_GB + D_MODEL
COL_BV = COL_AQ + QA_W
COL_BG = COL_BV + VB_W
COL_BQ = COL_BG + VB_W
COL_BK = COL_BQ + QB_W
COL_AK = COL_BK + QB_W
COL_AV = COL_AK + KA_W
COL_R = COL_AV + KA_W
PROJ_W = COL_R + LANE

ROW_TILE = 512
PROJ_CHUNK = 1280
GLA_TILE = 256
CPT = GLA_TILE // CHUNK


def _cparams(n_axes):
    return pltpu.CompilerParams(
        dimension_semantics=("arbitrary",) * n_axes, vmem_limit_bytes=VMEM_LIMIT)


_W_IN_SRC = np.cumsum([0, QA_W, KA_W, KA_W, QB_W, QB_W, VB_W, VB_W, B_RANK, D_MODEL, D_MODEL])
_W_IN_SEGMENTS = tuple(zip(
    (COL_AQ, COL_AK, COL_AV, COL_BQ, COL_BK, COL_BV, COL_BG, COL_R, COL_GA, COL_GB),
    (int(s) for s in _W_IN_SRC[:-1]),
    (int(w) for w in np.diff(_W_IN_SRC))))
W_IN_COLS = int(_W_IN_SRC[-1])
W_ROWS_PER_COPY = 256


def _load_permuted_w_in(wt_ref, wbf_ref):
    wbf_ref[COL_R:COL_R + LANE, :] = jnp.zeros((LANE, D_MODEL), BF16)
    for dst, src, width in _W_IN_SEGMENTS:
        for r0 in range(0, width, W_ROWS_PER_COPY):
            n = min(W_ROWS_PER_COPY, width - r0)
            wbf_ref[dst + r0:dst + r0 + n, :] = wt_ref[0, src + r0:src + r0 + n, :].astype(BF16)


def _inproj_kernel(x_ref, g_ref, wt_ref, o_ref, wbf_ref):
    @pl.when(pl.program_id(0) == 0)
    def _prep():
        _load_permuted_w_in(wt_ref, wbf_ref)

    x = x_ref[...]
    ms = jnp.mean(x * x, axis=-1, keepdims=True)
    xn = (x * lax.rsqrt(ms + EPS) * g_ref[0]).astype(BF16)
    for c0 in range(0, PROJ_W, PROJ_CHUNK):
        sl = slice(c0, min(c0 + PROJ_CHUNK, PROJ_W))
        o_ref[:, sl] = lax.dot_general(xn, wbf_ref[sl, :], (((1,), (1,)), ((), ())),
                                       preferred_element_type=F32).astype(BF16)


def _inproj(x2, g_all, wt_all, layer):
    t = x2.shape[0]
    tm = min(ROW_TILE, t)
    return pl.pallas_call(
        _inproj_kernel,
        grid=(t // tm,),
        in_specs=[
            pl.BlockSpec((tm, D_MODEL), lambda i: (i, 0)),
            pl.BlockSpec((1, 1, D_MODEL), lambda i: (layer, 0, 0)),
            pl.BlockSpec((1, W_IN_COLS, D_MODEL), lambda i: (layer, 0, 0), pipeline_mode=pl.Buffered(1)),
        ],
        out_specs=pl.BlockSpec((tm, PROJ_W), lambda i: (i, 0)),
        out_shape=jax.ShapeDtypeStruct((t, PROJ_W), BF16),
        scratch_shapes=[pltpu.VMEM((PROJ_W, D_MODEL), BF16)],
        compiler_params=_cparams(1),
        name="inproj",
    )(x2, g_all, wt_all)


def _attn_kernel(bucket_ref, relb_ref, sink_ref, q_ref, kp_ref, kc_ref, kn_ref,
                 vp_ref, vc_ref, vn_ref, o_ref, bias_ref, *, nblk):
    b = pl.program_id(0)
    n = pl.program_id(1)
    kw = 3 * BLOCK

    @pl.when((b == 0) & (n == 0))
    def _build_bias():
        bucket = bucket_ref[...]
        row = lax.broadcasted_iota(jnp.int32, (BLOCK, kw), 0)
        col = lax.broadcasted_iota(jnp.int32, (BLOCK, kw), 1)
        win = jnp.abs(col - BLOCK - row) <= WINDOW
        keep = (win & (col >= BLOCK), win, win & (col < 2 * BLOCK))
        for hq in range(A_HEADS):
            acc = jnp.zeros((BLOCK, kw), F32)
            for bb in range(N_BUCKETS):
                acc = jnp.where(bucket == bb, relb_ref[bb, hq], acc)
            for v in range(3):
                bias_ref[v, hq] = jnp.where(keep[v], acc, NEG_INF)

    variant = jnp.where(n == 0, 0, jnp.where(n == nblk - 1, 2, 1))
    q = (q_ref[...].astype(F32) * (A_HEAD_DIM ** -0.5)).astype(BF16)
    k3 = [jnp.concatenate([kp_ref[:, hs], kc_ref[:, hs], kn_ref[:, hs]], axis=0)
          for hs in (slice(h * A_HEAD_DIM, (h + 1) * A_HEAD_DIM) for h in range(A_KV_HEADS))]
    v3 = [jnp.concatenate([vp_ref[:, hs], vc_ref[:, hs], vn_ref[:, hs]], axis=0)
          for hs in (slice(h * A_HEAD_DIM, (h + 1) * A_HEAD_DIM) for h in range(A_KV_HEADS))]
    logits = [lax.dot_general(q[:, hq * A_HEAD_DIM:(hq + 1) * A_HEAD_DIM], k3[hq // A_GROUP],
                              (((1,), (1,)), ((), ())), preferred_element_type=F32)
              for hq in range(A_HEADS)]
    probs, dens = [], []
    for hq in range(A_HEADS):
        s = logits[hq] + bias_ref[variant, hq]
        sk = sink_ref[hq]
        m = jnp.maximum(jnp.max(s, axis=-1, keepdims=True), sk)
        p = jnp.exp(s - m)
        dens.append(jnp.sum(p, axis=-1, keepdims=True) + jnp.exp(sk - m))
        probs.append(p.astype(BF16))
    outs = [jnp.dot(probs[hq], v3[hq // A_GROUP], preferred_element_type=F32) / dens[hq]
            for hq in range(A_HEADS)]
    o_ref[...] = jnp.concatenate(outs, axis=1).astype(BF16)


def _attention(proj, bucket, rel_bias, sink, bsz, seq):
    nblk = seq // BLOCK
    t = bsz * seq
    cq = COL_AQ // QA_W
    ck = COL_AK // KA_W
    cv = COL_AV // KA_W

    def prev(b, n):
        return b * nblk + jnp.maximum(n - 1, 0)

    def cur(b, n):
        return b * nblk + n

    def nxt(b, n):
        return b * nblk + jnp.minimum(n + 1, nblk - 1)

    smem = pl.BlockSpec(memory_space=pltpu.SMEM)
    kv_specs = [pl.BlockSpec((BLOCK, KA_W), functools.partial(lambda b, n, f, c: (f(b, n), c), f=f, c=c))
                for c in (ck, cv) for f in (prev, cur, nxt)]
    return pl.pallas_call(
        functools.partial(_attn_kernel, nblk=nblk),
        grid=(bsz, nblk),
        in_specs=[
            pl.BlockSpec((BLOCK, 3 * BLOCK), lambda b, n: (0, 0)),
            smem, smem,
            pl.BlockSpec((BLOCK, QA_W), lambda b, n: (b * nblk + n, cq)),
        ] + kv_specs,
        out_specs=pl.BlockSpec((BLOCK, QA_W), lambda b, n: (b * nblk + n, 0)),
        out_shape=jax.ShapeDtypeStruct((t, QA_W), BF16),
        scratch_shapes=[pltpu.VMEM((3, A_HEADS, BLOCK, 3 * BLOCK), F32)],
        compiler_params=_cparams(2),
        name="attn",
    )(bucket, rel_bias, sink, proj, proj, proj, proj, proj, proj, proj)


def _log_sigmoid(z):
    return jnp.minimum(z, 0.0) - jnp.log1p(jnp.exp(-jnp.abs(z)))


def _log_decay(r, wr, br):
    z = jnp.dot(r, wr, preferred_element_type=F32) + br
    return _log_sigmoid(z) / GATE_TAU


def _chunk_masks():
    row = lax.broadcasted_iota(jnp.int32, (GLA_TILE, GLA_TILE), 0)
    col = lax.broadcasted_iota(jnp.int32, (GLA_TILE, GLA_TILE), 1)
    sh = int(math.log2(CHUNK))
    same = jnp.right_shift(row, sh) == jnp.right_shift(col, sh)
    return same, same & (col <= row), same & (col >= row)


def _f32dot(a, b):
    return jnp.dot(a, b, preferred_element_type=F32, precision=lax.Precision.HIGHEST)


def _split3(x):
    hi = x.astype(BF16)
    r1 = x - hi.astype(F32)
    mid = r1.astype(BF16)
    lo = (r1 - mid.astype(F32)).astype(BF16)
    return hi, mid, lo


def _mask_dot(mask, x):
    m = mask.astype(BF16)
    hi, mid, lo = _split3(x)
    return (jnp.dot(m, hi, preferred_element_type=F32) + jnp.dot(m, mid, preferred_element_type=F32)
            + jnp.dot(m, lo, preferred_element_type=F32))


def _gla_state_dir(k_ref, v_ref, r_ref, wrt_ref, brc_ref, s_out_ref, state_ref, order, cum_mask, same):
    z = lax.dot_general(wrt_ref[...], r_ref[...], (((1,), (1,)), ((), ())),
                        preferred_element_type=F32) + brc_ref[:, 0:1]
    hi, mid, lo = _split3(_log_sigmoid(z) / GATE_TAU)
    m = jnp.concatenate([cum_mask, same], axis=1).astype(BF16)
    both = (jnp.dot(hi, m, preferred_element_type=F32) + jnp.dot(mid, m, preferred_element_type=F32)
            + jnp.dot(lo, m, preferred_element_type=F32))
    cum = both[:, :GLA_TILE]
    tot = both[:, GLA_TILE:]
    k_end = (k_ref[...].astype(F32).T * jnp.exp(tot - cum)).astype(BF16)
    lane_chunk = jnp.right_shift(lax.broadcasted_iota(jnp.int32, (1, GLA_TILE), 1),
                                 int(math.log2(CHUNK)))
    upd = []
    for h in range(B_HEADS):
        kh = k_end[h * B_KEY_DIM:(h + 1) * B_KEY_DIM]
        lhs = jnp.concatenate([jnp.where(lane_chunk == c, kh, 0.0).astype(BF16) for c in range(CPT)],
                              axis=0)
        upd.append(jnp.dot(lhs, v_ref[:, h * B_VAL_DIM:(h + 1) * B_VAL_DIM],
                           preferred_element_type=F32))
    for c in order:
        s_out_ref[0, c] = state_ref[...].astype(BF16)
        upd_c = jnp.concatenate([u[c * B_KEY_DIM:(c + 1) * B_KEY_DIM] for u in upd], axis=0)
        decay = jnp.exp(tot[:, c * CHUNK:c * CHUNK + 1])
        state_ref[...] = decay * state_ref[...] + upd_c


def _gla_state_kernel(kf_ref, vf_ref, rf_ref, kb_ref, vb_ref, rb_ref,
                      wrf_ref, brf_ref, wrb_ref, brb_ref,
                      sf_ref, sb_ref, stf_ref, stb_ref):
    @pl.when(pl.program_id(1) == 0)
    def _reset():
        stf_ref[...] = jnp.zeros_like(stf_ref)
        stb_ref[...] = jnp.zeros_like(stb_ref)

    same, lower, upper = _chunk_masks()
    _gla_state_dir(kf_ref, vf_ref, rf_ref, wrf_ref, brf_ref, sf_ref, stf_ref,
                   range(CPT), upper, same)
    _gla_state_dir(kb_ref, vb_ref, rb_ref, wrb_ref, brb_ref, sb_ref, stb_ref,
                   range(CPT - 1, -1, -1), lower, same)


def _gla_states(proj, wrf_t, brf_c, wrb_t, brb_c, bsz, seq):
    nt = seq // GLA_TILE
    nchunks = seq // CHUNK
    hk = B_HEADS * B_KEY_DIM
    ck, cv, cr = COL_BK // QB_W, COL_BV // VB_W, COL_R // LANE

    def fwd(b, i):
        return b * nt + i

    def bwd(b, i):
        return b * nt + (nt - 1 - i)

    def tile_specs(f):
        return [pl.BlockSpec((GLA_TILE, QB_W), lambda b, i: (f(b, i), ck)),
                pl.BlockSpec((GLA_TILE, VB_W), lambda b, i: (f(b, i), cv)),
                pl.BlockSpec((GLA_TILE, LANE), lambda b, i: (f(b, i), cr))]

    const = lambda b, i: (0, 0)
    out_sds = jax.ShapeDtypeStruct((bsz, nchunks, hk, B_VAL_DIM), BF16)
    return pl.pallas_call(
        _gla_state_kernel,
        grid=(bsz, nt),
        in_specs=tile_specs(fwd) + tile_specs(bwd) + [
            pl.BlockSpec((hk, LANE), const), pl.BlockSpec((hk, 1), const),
            pl.BlockSpec((hk, LANE), const), pl.BlockSpec((hk, 1), const)],
        out_specs=[pl.BlockSpec((1, CPT, hk, B_VAL_DIM), lambda b, i: (b, i, 0, 0)),
                   pl.BlockSpec((1, CPT, hk, B_VAL_DIM), lambda b, i: (b, nt - 1 - i, 0, 0))],
        out_shape=[out_sds, out_sds],
        scratch_shapes=[pltpu.VMEM((hk, B_VAL_DIM), F32), pltpu.VMEM((hk, B_VAL_DIM), F32)],
        compiler_params=_cparams(2),
        name="gla_state",
    )(proj, proj, proj, proj, proj, proj, wrf_t, brf_c, wrb_t, brb_c)


def _gla_out_kernel(q_ref, k_ref, v_ref, r_ref, g_ref, sf_ref, sb_ref,
                    wrf_ref, brf_ref, wrb_ref, brb_ref, ng_ref, o_ref):
    same, lower, upper = _chunk_masks()
    r = r_ref[...]
    cum_f = _mask_dot(lower, _log_decay(r, wrf_ref[...], brf_ref[...]))
    cum_b = _mask_dot(upper, _log_decay(r, wrb_ref[...], brb_ref[...]))
    q = q_ref[...].astype(F32) * (B_KEY_DIM ** -0.5)
    k = k_ref[...].astype(F32)
    qd_f = q * jnp.exp(cum_f)
    qd_b = q * jnp.exp(cum_b)
    ki_f = (k * jnp.exp(-cum_f)).astype(BF16)
    ki_b = (k * jnp.exp(-cum_b)).astype(BF16)
    lane_head = jnp.right_shift(
        lax.broadcasted_iota(jnp.int32, (1, B_HEADS * B_KEY_DIM), 1), int(math.log2(B_KEY_DIM)))

    def stack_heads(x):
        return jnp.concatenate(
            [jnp.where(lane_head == h, x, 0.0) for h in range(B_HEADS)], axis=0).astype(BF16)

    qs_f = stack_heads(qd_f)
    qs_b = stack_heads(qd_b)
    nt_dims = (((1,), (1,)), ((), ()))
    sc_f = lax.dot_general(qs_f, ki_f, nt_dims, preferred_element_type=F32)
    sc_b = lax.dot_general(qs_b, ki_b, nt_dims, preferred_element_type=F32)
    lower_s = jnp.concatenate([lower] * B_HEADS, axis=0)
    strict_upper_s = jnp.concatenate([upper & ~lower] * B_HEADS, axis=0)
    p = jnp.where(lower_s, sc_f, jnp.where(strict_upper_s, sc_b, 0.0)).astype(BF16)
    v = v_ref[...]
    inter = []
    for c in range(CPT):
        lhs_f = jnp.concatenate(
            [qs_f[h * GLA_TILE + c * CHUNK:h * GLA_TILE + (c + 1) * CHUNK] for h in range(B_HEADS)], axis=0)
        lhs_b = jnp.concatenate(
            [qs_b[h * GLA_TILE + c * CHUNK:h * GLA_TILE + (c + 1) * CHUNK] for h in range(B_HEADS)], axis=0)
        inter.append(jnp.dot(lhs_f, sf_ref[0, c], preferred_element_type=F32)
                     + jnp.dot(lhs_b, sb_ref[0, c], preferred_element_type=F32))
    ng = ng_ref[...]
    g = g_ref[...].astype(F32)
    for h in range(B_HEADS):
        vs = slice(h * B_VAL_DIM, (h + 1) * B_VAL_DIM)
        o = jnp.dot(p[h * GLA_TILE:(h + 1) * GLA_TILE], v[:, vs], preferred_element_type=F32)
        o = o + jnp.concatenate([inter[c][h * CHUNK:(h + 1) * CHUNK] for c in range(CPT)], axis=0)
        o = o * lax.rsqrt(jnp.mean(o * o, axis=-1, keepdims=True) + EPS) * ng
        gh = g[:, vs]
        o_ref[:, vs] = (o * (gh / (1.0 + jnp.exp(-gh)))).astype(BF16)


def _gla_out(proj, s_f, s_b, wrf, brf, wrb, brb, ng, bsz, seq):
    nt = seq // GLA_TILE
    t = bsz * seq
    hk = B_HEADS * B_KEY_DIM
    const = lambda b, i: (0, 0)
    row = lambda c: (lambda b, i: (b * nt + i, c))
    return pl.pallas_call(
        _gla_out_kernel,
        grid=(bsz, nt),
        in_specs=[
            pl.BlockSpec((GLA_TILE, QB_W), row(COL_BQ // QB_W)),
            pl.BlockSpec((GLA_TILE, QB_W), row(COL_BK // QB_W)),
            pl.BlockSpec((GLA_TILE, VB_W), row(COL_BV // VB_W)),
            pl.BlockSpec((GLA_TILE, LANE), row(COL_R // LANE)),
            pl.BlockSpec((GLA_TILE, VB_W), row(COL_BG // VB_W)),
            pl.BlockSpec((1, CPT, hk, B_VAL_DIM), lambda b, i: (b, i, 0, 0)),
            pl.BlockSpec((1, CPT, hk, B_VAL_DIM), lambda b, i: (b, i, 0, 0)),
            pl.BlockSpec((LANE, hk), const), pl.BlockSpec((1, hk), const),
            pl.BlockSpec((LANE, hk), const), pl.BlockSpec((1, hk), const),
            pl.BlockSpec((1, B_VAL_DIM), const),
        ],
        out_specs=pl.BlockSpec((GLA_TILE, VB_W), lambda b, i: (b * nt + i, 0)),
        out_shape=jax.ShapeDtypeStruct((t, VB_W), BF16),
        compiler_params=_cparams(2),
        name="gla_out",
    )(proj, proj, proj, proj, proj, s_f, s_b, wrf, brf, wrb, brb, ng)


def _sigmoid(x):
    return 1.0 / (1.0 + jnp.exp(-x))


def _merge_kernel(ya_ref, yb_ref, ga_ref, gb_ref, x_ref, wa_ref, wb_ref, wo_ref, o_ref,
                  wa_bf, wb_bf, wo_bf):
    @pl.when(pl.program_id(0) == 0)
    def _prep():
        wa_bf[...] = wa_ref[0].astype(BF16)
        wb_bf[...] = wb_ref[0].astype(BF16)
        wo_bf[...] = wo_ref[0].astype(BF16)

    a = jnp.dot(ya_ref[...], wa_bf[...], preferred_element_type=F32)
    b = jnp.dot(yb_ref[...], wb_bf[...], preferred_element_type=F32)
    merged = _sigmoid(ga_ref[...].astype(F32)) * a + _sigmoid(gb_ref[...].astype(F32)) * b
    o_ref[...] = x_ref[...] + jnp.dot(merged.astype(BF16), wo_bf[...], preferred_element_type=F32)


def _merge(ya, yb, proj, x2, wa_all, wb_all, wo_all, layer):
    t = x2.shape[0]
    tm = min(ROW_TILE, t)
    wsel = lambda i: (layer, 0, 0)
    return pl.pallas_call(
        _merge_kernel,
        grid=(t // tm,),
        in_specs=[
            pl.BlockSpec((tm, QA_W), lambda i: (i, 0)),
            pl.BlockSpec((tm, VB_W), lambda i: (i, 0)),
            pl.BlockSpec((tm, D_MODEL), lambda i: (i, COL_GA // D_MODEL)),
            pl.BlockSpec((tm, D_MODEL), lambda i: (i, COL_GB // D_MODEL)),
            pl.BlockSpec((tm, D_MODEL), lambda i: (i, 0)),
            pl.BlockSpec((1, QA_W, D_MODEL), wsel, pipeline_mode=pl.Buffered(1)),
            pl.BlockSpec((1, VB_W, D_MODEL), wsel, pipeline_mode=pl.Buffered(1)),
            pl.BlockSpec((1, D_MODEL, D_MODEL), wsel, pipeline_mode=pl.Buffered(1)),
        ],
        out_specs=pl.BlockSpec((tm, D_MODEL), lambda i: (i, 0)),
        out_shape=jax.ShapeDtypeStruct((t, D_MODEL), F32),
        scratch_shapes=[pltpu.VMEM((QA_W, D_MODEL), BF16), pltpu.VMEM((VB_W, D_MODEL), BF16),
                        pltpu.VMEM((D_MODEL, D_MODEL), BF16)],
        compiler_params=_cparams(1),
        name="merge",
    )(ya, yb, proj, proj, x2, wa_all, wb_all, wo_all)


def _moe_tile_tokens(t):
    return min(4096, t)


MOE_BLK = 512
MOE_UNIT = 16
MOE_CAP = 1536
XS_W = D_MODEL + LANE
MOE_MT = 384
MOE_CH = 2 * MOE_MT
MOE_CHU = MOE_CH // MOE_UNIT
MOE_NC = 256
assert MOE_CAP >= 2 * MOE_BLK + N_EXPERTS * (MOE_UNIT - 1) and MOE_CAP % MOE_UNIT == 0


NT_DIMS = (((1,), (1,)), ((), ()))


def _route(hn_bf, wrt_ref, brc_ref):
    logits = lax.dot_general(wrt_ref[...].astype(BF16), hn_bf, NT_DIMS,
                             preferred_element_type=F32) + brc_ref[:, 0:1]
    row = lax.broadcasted_iota(jnp.int32, logits.shape, 0).astype(F32)
    big = float(LANE)
    ninf = -jnp.inf
    gl = jnp.where(row < N_GROUPS, logits, ninf)
    gmax = jnp.max(gl, axis=0, keepdims=True)
    g_idx = jnp.min(jnp.where(gl == gmax, row, big), axis=0, keepdims=True)
    g_w = 1.0 / jnp.sum(jnp.exp(gl - gmax), axis=0, keepdims=True)
    lo = float(N_GROUPS) + g_idx * float(EXPERTS_PER_GROUP)
    el = jnp.where((row >= lo) & (row < lo + EXPERTS_PER_GROUP), logits, ninf)
    v1 = jnp.max(el, axis=0, keepdims=True)
    i1 = jnp.min(jnp.where(el == v1, row, big), axis=0, keepdims=True)
    el2 = jnp.where(row == i1, ninf, el)
    v2 = jnp.max(el2, axis=0, keepdims=True)
    i2 = jnp.min(jnp.where(el2 == v2, row, big), axis=0, keepdims=True)
    e21 = jnp.exp(v2 - v1)
    w1 = g_w / (1.0 + e21)
    w2 = g_w * e21 / (1.0 + e21)
    return i1 - float(N_GROUPS), i2 - float(N_GROUPS), w1, w2, row


def _slot_matrix(pos1, pos2, second):
    slot = lax.broadcasted_iota(jnp.int32, (pos1.shape[0], MOE_CAP), 1)
    return jnp.where(slot == pos1.astype(jnp.int32), 1.0,
                     jnp.where(slot == pos2.astype(jnp.int32), second, 0.0)).astype(BF16)


def _slot_matrix_t(pos1, pos2, second):
    slot = lax.broadcasted_iota(jnp.int32, (MOE_CAP, pos1.shape[1]), 0)
    return jnp.where(slot == pos1.astype(jnp.int32), 1.0,
                     jnp.where(slot == pos2.astype(jnp.int32), second, 0.0)).astype(BF16)


def _split3_f32(w):
    hi = w.astype(BF16).astype(F32)
    r1 = w - hi
    mid = r1.astype(BF16).astype(F32)
    return hi, mid, r1 - mid


def _rows_to_sublanes(vals, row):
    out = jnp.zeros(row.shape, F32)
    for k, v in enumerate(vals):
        out = jnp.where(row == float(k), v, out)
    return out


def _moe_kernel(x_ref, g_ref, wrt_ref, brc_ref, wg_ref, wu_ref, wd_ref, gf_ref, o_ref,
                xs_ref, pos_ref, tab_v, tab_s, xt_ref, addr_ref, wgu_bf, wd_bf, sem,
                *, nblk, final_norm):
    s = pl.program_id(1)
    dump_row = nblk * MOE_CAP

    @pl.when(s < nblk)
    def _dispatch():
        @pl.when(s == 0)
        def _init():
            xs_ref[dump_row:dump_row + MOE_UNIT, :] = jnp.zeros((MOE_UNIT, XS_W), BF16)
            xt_ref[...] = jnp.zeros_like(xt_ref)

        x = x_ref[...]
        ms = jnp.mean(x * x, axis=-1, keepdims=True)
        hn = (x * lax.rsqrt(ms + EPS) * g_ref[0]).astype(BF16)
        e1, e2, w1, w2, row = _route(hn, wrt_ref, brc_ref)
        onehot = ((row == e1) | (row == e2)).astype(F32)
        earlier = (lax.broadcasted_iota(jnp.int32, (MOE_BLK, MOE_BLK), 0)
                   < lax.broadcasted_iota(jnp.int32, (MOE_BLK, MOE_BLK), 1)).astype(BF16)
        before = jnp.dot(onehot.astype(BF16), earlier, preferred_element_type=F32)
        counts = jnp.sum(onehot, axis=1, keepdims=True)
        units = jnp.floor((counts + (MOE_UNIT - 1.0)) * (1.0 / MOE_UNIT))
        lower = (lax.broadcasted_iota(jnp.int32, (LANE, LANE), 1)
                 < lax.broadcasted_iota(jnp.int32, (LANE, LANE), 0)).astype(BF16)
        padded = jnp.broadcast_to(units * MOE_UNIT, (LANE, LANE))
        run_off = jnp.dot(lower, padded.astype(BF16), preferred_element_type=F32)[:, 0:1]
        start = run_off + before
        pos1 = jnp.sum(jnp.where(row == e1, start, 0.0), axis=0, keepdims=True)
        pos2 = jnp.sum(jnp.where(row == e2, start, 0.0), axis=0, keepdims=True)
        pt_t = _slot_matrix_t(pos1, pos2, 2.0)
        rows = pl.ds(pl.multiple_of(s * MOE_CAP, MOE_CAP), MOE_CAP)
        for c in range(D_MODEL // MOE_NC):
            cs = slice(c * MOE_NC, (c + 1) * MOE_NC)
            xs_ref[rows, cs] = jnp.dot(pt_t, hn[:, cs], preferred_element_type=F32).astype(BF16)
        w_rows = _rows_to_sublanes(
            _split3_f32(w1) + _split3_f32(w2) + (jnp.ones_like(w1),), row).astype(BF16)
        xs_ref[rows, D_MODEL:] = lax.dot_general(pt_t, w_rows, NT_DIMS,
                                                 preferred_element_type=F32).astype(BF16)
        pos_ref[s] = _rows_to_sublanes((pos1, pos2), row).T
        lane_e = lax.broadcasted_iota(jnp.int32, (LANE, LANE), 1)
        per_expert = jnp.where(lane_e == 0, run_off, jnp.where(lane_e == 1, units, 0.0))
        tab_v[s] = per_expert.T[0:SUBLANE, :].astype(jnp.int32)

        @pl.when(s == nblk - 1)
        def _publish():
            cp = pltpu.make_async_copy(tab_v, tab_s, sem)
            cp.start()
            cp.wait()

    @pl.when((s >= nblk) & (s < nblk + N_EXPERTS))
    def _experts():
        e = s - nblk
        wgu_bf[:, :D_EXPERT] = wg_ref[0].astype(BF16)
        wgu_bf[:, D_EXPERT:] = wu_ref[0].astype(BF16)
        wd_bf[...] = wd_ref[0].astype(BF16)
        total = tab_s[0, 1, e]
        for j in range(1, nblk):
            total = total + tab_s[j, 1, e]

        def chunk(c, carry):
            u0 = c * MOE_CHU
            for k in range(MOE_CHU):
                addr_ref[k] = dump_row
            cum = 0
            for j in range(nblk):
                nj = tab_s[j, 1, e]
                run0 = j * MOE_CAP + tab_s[j, 0, e]
                k_off = cum - u0

                def copy_unit(u, c2, run0=run0, k_off=k_off):
                    src = pl.multiple_of(run0 + u * MOE_UNIT, MOE_UNIT)
                    dst = pl.multiple_of((k_off + u) * MOE_UNIT, MOE_UNIT)
                    xt_ref[pl.ds(dst, MOE_UNIT), :] = xs_ref[pl.ds(src, MOE_UNIT), :]
                    addr_ref[k_off + u] = src
                    return c2
                lax.fori_loop(jnp.clip(u0 - cum, 0, nj), jnp.clip(u0 + MOE_CHU - cum, 0, nj),
                              copy_unit, 0)
                cum = cum + nj
            rows_here = jnp.minimum(MOE_CHU, total - u0) * MOE_UNIT

            def tile(t, c3):
                xt = xt_ref[pl.ds(pl.multiple_of(t * MOE_MT, MOE_MT), MOE_MT), :]
                wl = xt[:, D_MODEL:].astype(F32)
                second = wl[:, 6:7] > 1.5
                wcol = jnp.where(second, 0.5 * (wl[:, 3:4] + wl[:, 4:5] + wl[:, 5:6]),
                                 wl[:, 0:1] + wl[:, 1:2] + wl[:, 2:3])
                gu = (jnp.dot(xt[:, :D_MODEL], wgu_bf[...], preferred_element_type=F32)
                      * jnp.where(second, 0.5, 1.0))
                gate = gu[:, :D_EXPERT]
                h = gate * _sigmoid(gate) * gu[:, D_EXPERT:] * wcol
                y = jnp.dot(h.astype(BF16), wd_bf[...], preferred_element_type=F32).astype(BF16)
                for uu in range(MOE_MT // MOE_UNIT):
                    dst = pl.multiple_of(addr_ref[t * (MOE_MT // MOE_UNIT) + uu], MOE_UNIT)
                    xs_ref[pl.ds(dst, MOE_UNIT), :D_MODEL] = y[uu * MOE_UNIT:(uu + 1) * MOE_UNIT]
                return c3
            lax.fori_loop(0, (rows_here + MOE_MT - 1) // MOE_MT, tile, 0)
            return carry
        lax.fori_loop(0, (total + MOE_CHU - 1) // MOE_CHU, chunk, 0)

    @pl.when(s >= nblk + N_EXPERTS)
    def _combine():
        j = s - nblk - N_EXPERTS
        p = pos_ref[j]
        pt = _slot_matrix(p[:, 0:1], p[:, 1:2], 1.0)
        rows = pl.ds(pl.multiple_of(j * MOE_CAP, MOE_CAP), MOE_CAP)
        for c in range(D_MODEL // MOE_NC):
            cs = slice(c * MOE_NC, (c + 1) * MOE_NC)
            o_ref[:, cs] = x_ref[:, cs] + jnp.dot(pt, xs_ref[rows, cs], preferred_element_type=F32)
        if final_norm:
            y = o_ref[...]
            ms = jnp.mean(y * y, axis=-1, keepdims=True)
            o_ref[...] = y * lax.rsqrt(ms + EPS) * gf_ref[...]


def _moe(x2, g_all, w_r, b_r, wg_all, wu_all, wd_all, g_final, layer, tt, final_norm):
    t = x2.shape[0]
    nblk = tt // MOE_BLK
    steps = 2 * nblk + N_EXPERTS

    def tok(b, s):
        j = jnp.where(s < nblk, s, jnp.where(s < nblk + N_EXPERTS, nblk - 1, s - nblk - N_EXPERTS))
        return (b * nblk + j, 0)

    def out_tok(b, s):
        return (b * nblk + jnp.maximum(s - nblk - N_EXPERTS, 0), 0)

    wsel = lambda b, s: (layer * N_EXPERTS + jnp.clip(s - nblk, 0, N_EXPERTS - 1), 0, 0)
    const = lambda b, s: (0, 0)
    return pl.pallas_call(
        functools.partial(_moe_kernel, nblk=nblk, final_norm=final_norm),
        grid=(t // tt, steps),
        in_specs=[
            pl.BlockSpec((MOE_BLK, D_MODEL), tok),
            pl.BlockSpec((1, 1, D_MODEL), lambda b, s: (layer, 0, 0)),
            pl.BlockSpec((LANE, D_MODEL), const),
            pl.BlockSpec((LANE, 1), const),
            pl.BlockSpec((1, D_MODEL, D_EXPERT), wsel),
            pl.BlockSpec((1, D_MODEL, D_EXPERT), wsel),
            pl.BlockSpec((1, D_EXPERT, D_MODEL), wsel),
            pl.BlockSpec((1, D_MODEL), const),
        ],
        out_specs=pl.BlockSpec((MOE_BLK, D_MODEL), out_tok),
        out_shape=jax.ShapeDtypeStruct((t, D_MODEL), F32),
        scratch_shapes=[
            pltpu.VMEM((nblk * MOE_CAP + MOE_UNIT, XS_W), BF16),
            pltpu.VMEM((nblk, MOE_BLK, LANE), F32),
            pltpu.VMEM((nblk, SUBLANE, LANE), jnp.int32),
            pltpu.SMEM((nblk, SUBLANE, LANE), jnp.int32),
            pltpu.VMEM((MOE_CH, XS_W), BF16),
            pltpu.SMEM((MOE_CHU,), jnp.int32),
            pltpu.VMEM((D_MODEL, 2 * D_EXPERT), BF16),
            pltpu.VMEM((D_EXPERT, D_MODEL), BF16),
            pltpu.SemaphoreType.DMA,
        ],
        compiler_params=_cparams(2),
        name="moe",
    )(x2, g_all, w_r, b_r, wg_all, wu_all, wd_all, g_final)


def _t5_bucket(rel):
    nb = N_BUCKETS // 2
    max_exact = nb // 2
    base = jnp.where(rel > 0, nb, 0)
    n = jnp.abs(rel)
    large = max_exact + (jnp.log(jnp.maximum(n, 1).astype(jnp.float32) / max_exact)
                         / math.log(MAX_DISTANCE / max_exact) * (nb - max_exact)).astype(jnp.int32)
    large = jnp.minimum(large, nb - 1)
    return base + jnp.where(n < max_exact, n, large)


def _pad_rank(wr):
    return jnp.pad(wr, ((0, LANE - B_RANK), (0, 0))).astype(BF16)


def kernel(x, w_in, rel_bias, attn_sink, gla_wr_fwd, gla_br_fwd, gla_wr_bwd, gla_br_bwd, gla_norm, w_branch_a, w_branch_b, w_out, norm_mix, norm_ffn, router_group_w, router_group_b, router_expert_w, router_expert_b, expert_w_gate, expert_w_up, expert_w_down, norm_final):
    bsz, seq, d = x.shape
    depth = w_in.shape[0]
    t = bsz * seq
    tt = _moe_tile_tokens(t)
    q_off = jnp.arange(BLOCK)
    k_off = jnp.arange(3 * BLOCK) - BLOCK
    bucket = _t5_bucket(k_off[None, :] - q_off[:, None]).astype(jnp.int32)
    x2 = x.reshape(t, d)
    norm_mix3 = norm_mix[:, None, :]
    w_in_t = jnp.swapaxes(w_in, 1, 2)
    norm_ffn3 = norm_ffn[:, None, :]
    wg_all = expert_w_gate.reshape(depth * N_EXPERTS, d, D_EXPERT)
    wu_all = expert_w_up.reshape(depth * N_EXPERTS, d, D_EXPERT)
    wd_all = expert_w_down.reshape(depth * N_EXPERTS, D_EXPERT, d)
    for l in range(depth):
        proj = _inproj(x2, norm_mix3, w_in_t, l)
        ya = _attention(proj, bucket, rel_bias, attn_sink[l], bsz, seq)
        wrf, wrb = _pad_rank(gla_wr_fwd[l]), _pad_rank(gla_wr_bwd[l])
        brf, brb = gla_br_fwd[l][None, :], gla_br_bwd[l][None, :]
        s_f, s_b = _gla_states(proj, wrf.T, brf.T, wrb.T, brb.T, bsz, seq)
        yb = _gla_out(proj, s_f, s_b, wrf, brf, wrb, brb, gla_norm[l][None, :], bsz, seq)
        x2 = _merge(ya, yb, proj, x2, w_branch_a, w_branch_b, w_out, l)
        w_r = jnp.concatenate(
            [router_group_w[l].T, router_expert_w[l].T,
             jnp.zeros((LANE - N_GROUPS - N_EXPERTS, d), F32)], axis=0)
        b_r = jnp.concatenate(
            [router_group_b[l], router_expert_b[l],
             jnp.zeros((LANE - N_GROUPS - N_EXPERTS,), F32)])[:, None]
        x2 = _moe(x2, norm_ffn3, w_r, b_r, wg_all, wu_all, wd_all, norm_final[None, :], l, tt,
                  final_norm=(l == depth - 1))
    return x2.reshape(bsz, seq, d)
```

```python
import functools
import math

import numpy as np
import jax
import jax.numpy as jnp
from jax import lax
from jax.experimental import pallas as pl
from jax.experimental.pallas import tpu as pltpu

F32 = jnp.float32
BF16 = jnp.bfloat16

D_MODEL = 1024
A_HEADS = 8
A_KV_HEADS = 2
A_HEAD_DIM = 64
A_GROUP = A_HEADS // A_KV_HEADS
WINDOW = 128
BLOCK = 128
N_BUCKETS = 32
MAX_DISTANCE = 128
B_HEADS = 4
B_KEY_DIM = 64
B_VAL_DIM = 128
B_RANK = 16
GATE_TAU = 16.0
CHUNK = 64
N_GROUPS = 4
EXPERTS_PER_GROUP = 8
N_EXPERTS = N_GROUPS * EXPERTS_PER_GROUP
D_EXPERT = 256
EPS = 1e-6
NEG_INF = -1e30

LANE = 128
SUBLANE = 8
VMEM_LIMIT = 56 * 1024 * 1024

QA_W = A_HEADS * A_HEAD_DIM
KA_W = A_KV_HEADS * A_HEAD_DIM
QB_W = B_HEADS * B_KEY_DIM
VB_W = B_HEADS * B_VAL_DIM
COL_GA = 0
COL_GB = COL_GA + D_MODEL
COL_AQ = COL_GB + D_MODEL
COL_BV = COL_AQ + QA_W
COL_BG = COL_BV + VB_W
COL_BQ = COL_BG + VB_W
COL_BK = COL_BQ + QB_W
COL_AK = COL_BK + QB_W
COL_AV = COL_AK + KA_W
COL_R = COL_AV + KA_W
PROJ_W = COL_R + LANE

ROW_TILE = 512
PROJ_CHUNK = 1280
GLA_TILE = 256
CPT = GLA_TILE // CHUNK


def _cparams(n_axes):
    return pltpu.CompilerParams(
        dimension_semantics=("arbitrary",) * n_axes, vmem_limit_bytes=VMEM_LIMIT)


_W_IN_SRC = np.cumsum([0, QA_W, KA_W, KA_W, QB_W, QB_W, VB_W, VB_W, B_RANK, D_MODEL, D_MODEL])
_W_IN_SEGMENTS = tuple(zip(
    (COL_AQ, COL_AK, COL_AV, COL_BQ, COL_BK, COL_BV, COL_BG, COL_R, COL_GA, COL_GB),
    (int(s) for s in _W_IN_SRC[:-1]),
    (int(w) for w in np.diff(_W_IN_SRC))))
W_IN_COLS = int(_W_IN_SRC[-1])
W_ROWS_PER_COPY = 256


def _load_permuted_w_in(wt_ref, wbf_ref):
    wbf_ref[COL_R:COL_R + LANE, :] = jnp.zeros((LANE, D_MODEL), BF16)
    for dst, src, width in _W_IN_SEGMENTS:
        for r0 in range(0, width, W_ROWS_PER_COPY):
            n = min(W_ROWS_PER_COPY, width - r0)
            wbf_ref[dst + r0:dst + r0 + n, :] = wt_ref[0, src + r0:src + r0 + n, :].astype(BF16)


def _inproj_kernel(x_ref, g_ref, wt_ref, o_ref, wbf_ref):
    @pl.when(pl.program_id(0) == 0)
    def _prep():
        _load_permuted_w_in(wt_ref, wbf_ref)

    x = x_ref[...]
    ms = jnp.mean(x * x, axis=-1, keepdims=True)
    xn = (x * lax.rsqrt(ms + EPS) * g_ref[0]).astype(BF16)
    for c0 in range(0, PROJ_W, PROJ_CHUNK):
        sl = slice(c0, min(c0 + PROJ_CHUNK, PROJ_W))
        o_ref[:, sl] = lax.dot_general(xn, wbf_ref[sl, :], (((1,), (1,)), ((), ())),
                                       preferred_element_type=F32).astype(BF16)


def _inproj(x2, g_all, wt_all, layer):
    t = x2.shape[0]
    tm = min(ROW_TILE, t)
    return pl.pallas_call(
        _inproj_kernel,
        grid=(t // tm,),
        in_specs=[
            pl.BlockSpec((tm, D_MODEL), lambda i: (i, 0)),
            pl.BlockSpec((1, 1, D_MODEL), lambda i: (layer, 0, 0)),
            pl.BlockSpec((1, W_IN_COLS, D_MODEL), lambda i: (layer, 0, 0), pipeline_mode=pl.Buffered(1)),
        ],
        out_specs=pl.BlockSpec((tm, PROJ_W), lambda i: (i, 0)),
        out_shape=jax.ShapeDtypeStruct((t, PROJ_W), BF16),
        scratch_shapes=[pltpu.VMEM((PROJ_W, D_MODEL), BF16)],
        compiler_params=_cparams(1),
        name="inproj",
    )(x2, g_all, wt_all)


ATT_QB = 2


def _attn_kernel(bucket_ref, relb_ref, sink_ref, q_ref, kp_ref, kc_ref, kn_ref,
                 vp_ref, vc_ref, vn_ref, o_ref, bias_ref, *, nblk):
    b = pl.program_id(0)
    n = pl.program_id(1)
    kw = 3 * BLOCK

    @pl.when((b == 0) & (n == 0))
    def _build_bias():
        bucket = bucket_ref[...]
        row = lax.broadcasted_iota(jnp.int32, (BLOCK, kw), 0)
        col = lax.broadcasted_iota(jnp.int32, (BLOCK, kw), 1)
        win = jnp.abs(col - BLOCK - row) <= WINDOW
        keep = (win & (col >= BLOCK), win, win & (col < 2 * BLOCK))
        for hq in range(A_HEADS):
            acc = jnp.zeros((BLOCK, kw), F32)
            for bb in range(N_BUCKETS):
                acc = jnp.where(bucket == bb, relb_ref[bb, hq], acc)
            for v in range(3):
                bias_ref[v, hq] = jnp.where(keep[v], acc, NEG_INF)

    q = (q_ref[...].astype(F32) * (A_HEAD_DIM ** -0.5)).astype(BF16)
    kall = jnp.concatenate([kp_ref[...], kc_ref[...], kn_ref[...]], axis=0)
    vall = jnp.concatenate([vp_ref[...], vc_ref[...], vn_ref[...]], axis=0)
    jobs = [(j, hq) for j in range(ATT_QB) for hq in range(A_HEADS)]

    def keys(x, j, hq):
        h = hq // A_GROUP
        return x[j * BLOCK:(j + 3) * BLOCK, h * A_HEAD_DIM:(h + 1) * A_HEAD_DIM]

    logits = [lax.dot_general(q[j * BLOCK:(j + 1) * BLOCK, hq * A_HEAD_DIM:(hq + 1) * A_HEAD_DIM],
                              keys(kall, j, hq), (((1,), (1,)), ((), ())), preferred_element_type=F32)
              for j, hq in jobs]
    probs, dens = [], []
    for i, (j, hq) in enumerate(jobs):
        blk = n * ATT_QB + j
        variant = jnp.where(blk == 0, 0, jnp.where(blk == nblk - 1, 2, 1))
        s = logits[i] + bias_ref[variant, hq]
        sk = sink_ref[hq]
        m = jnp.maximum(jnp.max(s, axis=-1, keepdims=True), sk)
        p = jnp.exp(s - m)
        dens.append(jnp.sum(p, axis=-1, keepdims=True) + jnp.exp(sk - m))
        probs.append(p.astype(BF16))
    outs = [jnp.dot(probs[i], keys(vall, j, hq), preferred_element_type=F32) / dens[i]
            for i, (j, hq) in enumerate(jobs)]
    for j in range(ATT_QB):
        o_ref[j * BLOCK:(j + 1) * BLOCK, :] = jnp.concatenate(
            outs[j * A_HEADS:(j + 1) * A_HEADS], axis=1).astype(BF16)


def _attention(proj, bucket, rel_bias, sink, bsz, seq):
    nblk = seq // BLOCK
    t = bsz * seq
    cq = COL_AQ // QA_W
    ck = COL_AK // KA_W
    cv = COL_AV // KA_W

    nsteps = nblk // ATT_QB
    rows = ATT_QB * BLOCK

    def kv_specs(c):
        return [pl.BlockSpec((BLOCK, KA_W), lambda b, n: (b * nblk + jnp.maximum(n * ATT_QB - 1, 0), c)),
                pl.BlockSpec((rows, KA_W), lambda b, n: (b * nsteps + n, c)),
                pl.BlockSpec((BLOCK, KA_W),
                             lambda b, n: (b * nblk + jnp.minimum((n + 1) * ATT_QB, nblk - 1), c))]

    smem = pl.BlockSpec(memory_space=pltpu.SMEM)
    return pl.pallas_call(
        functools.partial(_attn_kernel, nblk=nblk),
        grid=(bsz, nsteps),
        in_specs=[
            pl.BlockSpec((BLOCK, 3 * BLOCK), lambda b, n: (0, 0)),
            smem, smem,
            pl.BlockSpec((rows, QA_W), lambda b, n: (b * nsteps + n, cq)),
        ] + kv_specs(ck) + kv_specs(cv),
        out_specs=pl.BlockSpec((rows, QA_W), lambda b, n: (b * nsteps + n, 0)),
        out_shape=jax.ShapeDtypeStruct((t, QA_W), BF16),
        scratch_shapes=[pltpu.VMEM((3, A_HEADS, BLOCK, 3 * BLOCK), F32)],
        compiler_params=_cparams(2),
        name="attn",
    )(bucket, rel_bias, sink, proj, proj, proj, proj, proj, proj, proj)


def _log_sigmoid(z):
    return jnp.minimum(z, 0.0) - jnp.log1p(jnp.exp(-jnp.abs(z)))


def _log_decay(r, wr, br):
    z = jnp.dot(r, wr, preferred_element_type=F32) + br
    return _log_sigmoid(z) / GATE_TAU


def _chunk_masks():
    row = lax.broadcasted_iota(jnp.int32, (GLA_TILE, GLA_TILE), 0)
    col = lax.broadcasted_iota(jnp.int32, (GLA_TILE, GLA_TILE), 1)
    sh = int(math.log2(CHUNK))
    same = jnp.right_shift(row, sh) == jnp.right_shift(col, sh)
    return same, same & (col <= row), same & (col >= row)


def _f32dot(a, b):
    return jnp.dot(a, b, preferred_element_type=F32, precision=lax.Precision.HIGHEST)


def _split3(x):
    hi = x.astype(BF16)
    r1 = x - hi.astype(F32)
    mid = r1.astype(BF16)
    lo = (r1 - mid.astype(F32)).astype(BF16)
    return hi, mid, lo


def _mask_dot(mask, x):
    m = mask.astype(BF16)
    hi, mid, lo = _split3(x)
    return (jnp.dot(m, hi, preferred_element_type=F32) + jnp.dot(m, mid, preferred_element_type=F32)
            + jnp.dot(m, lo, preferred_element_type=F32))


def _gla_state_dir(k_ref, v_ref, r_ref, wrt_ref, brc_ref, s_out_ref, state_ref, order, cum_mask, same):
    z = lax.dot_general(wrt_ref[...], r_ref[...], (((1,), (1,)), ((), ())),
                        preferred_element_type=F32) + brc_ref[:, 0:1]
    hi, mid, lo = _split3(_log_sigmoid(z) / GATE_TAU)
    m = jnp.concatenate([cum_mask, same], axis=1).astype(BF16)
    both = (jnp.dot(hi, m, preferred_element_type=F32) + jnp.dot(mid, m, preferred_element_type=F32)
            + jnp.dot(lo, m, preferred_element_type=F32))
    cum = both[:, :GLA_TILE]
    tot = both[:, GLA_TILE:]
    k_end = (k_ref[...].astype(F32).T * jnp.exp(tot - cum)).astype(BF16)
    lane_chunk = jnp.right_shift(lax.broadcasted_iota(jnp.int32, (1, GLA_TILE), 1),
                                 int(math.log2(CHUNK)))
    upd = []
    for h in range(B_HEADS):
        kh = k_end[h * B_KEY_DIM:(h + 1) * B_KEY_DIM]
        lhs = jnp.concatenate([jnp.where(lane_chunk == c, kh, 0.0).astype(BF16) for c in range(CPT)],
                              axis=0)
        upd.append(jnp.dot(lhs, v_ref[:, h * B_VAL_DIM:(h + 1) * B_VAL_DIM],
                           preferred_element_type=F32))
    for c in order:
        s_out_ref[0, c] = state_ref[...].astype(BF16)
        upd_c = jnp.concatenate([u[c * B_KEY_DIM:(c + 1) * B_KEY_DIM] for u in upd], axis=0)
        decay = jnp.exp(tot[:, c * CHUNK:c * CHUNK + 1])
        state_ref[...] = decay * state_ref[...] + upd_c


def _gla_state_kernel(kf_ref, vf_ref, rf_ref, kb_ref, vb_ref, rb_ref,
                      wrf_ref, brf_ref, wrb_ref, brb_ref,
                      sf_ref, sb_ref, stf_ref, stb_ref):
    @pl.when(pl.program_id(1) == 0)
    def _reset():
        stf_ref[...] = jnp.zeros_like(stf_ref)
        stb_ref[...] = jnp.zeros_like(stb_ref)

    same, lower, upper = _chunk_masks()
    _gla_state_dir(kf_ref, vf_ref, rf_ref, wrf_ref, brf_ref, sf_ref, stf_ref,
                   range(CPT), upper, same)
    _gla_state_dir(kb_ref, vb_ref, rb_ref, wrb_ref, brb_ref, sb_ref, stb_ref,
                   range(CPT - 1, -1, -1), lower, same)


def _gla_states(proj, wrf_t, brf_c, wrb_t, brb_c, bsz, seq):
    nt = seq // GLA_TILE
    nchunks = seq // CHUNK
    hk = B_HEADS * B_KEY_DIM
    ck, cv, cr = COL_BK // QB_W, COL_BV // VB_W, COL_R // LANE

    def fwd(b, i):
        return b * nt + i

    def bwd(b, i):
        return b * nt + (nt - 1 - i)

    def tile_specs(f):
        return [pl.BlockSpec((GLA_TILE, QB_W), lambda b, i: (f(b, i), ck)),
                pl.BlockSpec((GLA_TILE, VB_W), lambda b, i: (f(b, i), cv)),
                pl.BlockSpec((GLA_TILE, LANE), lambda b, i: (f(b, i), cr))]

    const = lambda b, i: (0, 0)
    out_sds = jax.ShapeDtypeStruct((bsz, nchunks, hk, B_VAL_DIM), BF16)
    return pl.pallas_call(
        _gla_state_kernel,
        grid=(bsz, nt),
        in_specs=tile_specs(fwd) + tile_specs(bwd) + [
            pl.BlockSpec((hk, LANE), const), pl.BlockSpec((hk, 1), const),
            pl.BlockSpec((hk, LANE), const), pl.BlockSpec((hk, 1), const)],
        out_specs=[pl.BlockSpec((1, CPT, hk, B_VAL_DIM), lambda b, i: (b, i, 0, 0)),
                   pl.BlockSpec((1, CPT, hk, B_VAL_DIM), lambda b, i: (b, nt - 1 - i, 0, 0))],
        out_shape=[out_sds, out_sds],
        scratch_shapes=[pltpu.VMEM((hk, B_VAL_DIM), F32), pltpu.VMEM((hk, B_VAL_DIM), F32)],
        compiler_params=_cparams(2),
        name="gla_state",
    )(proj, proj, proj, proj, proj, proj, wrf_t, brf_c, wrb_t, brb_c)


def _gla_out_kernel(q_ref, k_ref, v_ref, r_ref, g_ref, sf_ref, sb_ref,
                    wrf_ref, brf_ref, wrb_ref, brb_ref, ng_ref, o_ref):
    same, lower, upper = _chunk_masks()
    r = r_ref[...]
    cum_f = _mask_dot(lower, _log_decay(r, wrf_ref[...], brf_ref[...]))
    cum_b = _mask_dot(upper, _log_decay(r, wrb_ref[...], brb_ref[...]))
    q = q_ref[...].astype(F32) * (B_KEY_DIM ** -0.5)
    k = k_ref[...].astype(F32)
    qd_f = q * jnp.exp(cum_f)
    qd_b = q * jnp.exp(cum_b)
    ki_f = (k * jnp.exp(-cum_f)).astype(BF16)
    ki_b = (k * jnp.exp(-cum_b)).astype(BF16)
    lane_head = jnp.right_shift(
        lax.broadcasted_iota(jnp.int32, (1, B_HEADS * B_KEY_DIM), 1), int(math.log2(B_KEY_DIM)))

    def stack_heads(x):
        return jnp.concatenate(
            [jnp.where(lane_head == h, x, 0.0) for h in range(B_HEADS)], axis=0).astype(BF16)

    qs_f = stack_heads(qd_f)
    qs_b = stack_heads(qd_b)
    nt_dims = (((1,), (1,)), ((), ()))
    sc_f = lax.dot_general(qs_f, ki_f, nt_dims, preferred_element_type=F32)
    sc_b = lax.dot_general(qs_b, ki_b, nt_dims, preferred_element_type=F32)
    lower_s = jnp.concatenate([lower] * B_HEADS, axis=0)
    strict_upper_s = jnp.concatenate([upper & ~lower] * B_HEADS, axis=0)
    p = jnp.where(lower_s, sc_f, jnp.where(strict_upper_s, sc_b, 0.0)).astype(BF16)
    v = v_ref[...]
    inter = []
    for c in range(CPT):
        lhs_f = jnp.concatenate(
            [qs_f[h * GLA_TILE + c * CHUNK:h * GLA_TILE + (c + 1) * CHUNK] for h in range(B_HEADS)], axis=0)
        lhs_b = jnp.concatenate(
            [qs_b[h * GLA_TILE + c * CHUNK:h * GLA_TILE + (c + 1) * CHUNK] for h in range(B_HEADS)], axis=0)
        inter.append(jnp.dot(lhs_f, sf_ref[0, c], preferred_element_type=F32)
                     + jnp.dot(lhs_b, sb_ref[0, c], preferred_element_type=F32))
    ng = ng_ref[...]
    g = g_ref[...].astype(F32)
    for h in range(B_HEADS):
        vs = slice(h * B_VAL_DIM, (h + 1) * B_VAL_DIM)
        o = jnp.dot(p[h * GLA_TILE:(h + 1) * GLA_TILE], v[:, vs], preferred_element_type=F32)
        o = o + jnp.concatenate([inter[c][h * CHUNK:(h + 1) * CHUNK] for c in range(CPT)], axis=0)
        o = o * lax.rsqrt(jnp.mean(o * o, axis=-1, keepdims=True) + EPS) * ng
        gh = g[:, vs]
        o_ref[:, vs] = (o * (gh / (1.0 + jnp.exp(-gh)))).astype(BF16)


def _gla_out(proj, s_f, s_b, wrf, brf, wrb, brb, ng, bsz, seq):
    nt = seq // GLA_TILE
    t = bsz * seq
    hk = B_HEADS * B_KEY_DIM
    const = lambda b, i: (0, 0)
    row = lambda c: (lambda b, i: (b * nt + i, c))
    return pl.pallas_call(
        _gla_out_kernel,
        grid=(bsz, nt),
        in_specs=[
            pl.BlockSpec((GLA_TILE, QB_W), row(COL_BQ // QB_W)),
            pl.BlockSpec((GLA_TILE, QB_W), row(COL_BK // QB_W)),
            pl.BlockSpec((GLA_TILE, VB_W), row(COL_BV // VB_W)),
            pl.BlockSpec((GLA_TILE, LANE), row(COL_R // LANE)),
            pl.BlockSpec((GLA_TILE, VB_W), row(COL_BG // VB_W)),
            pl.BlockSpec((1, CPT, hk, B_VAL_DIM), lambda b, i: (b, i, 0, 0)),
            pl.BlockSpec((1, CPT, hk, B_VAL_DIM), lambda b, i: (b, i, 0, 0)),
            pl.BlockSpec((LANE, hk), const), pl.BlockSpec((1, hk), const),
            pl.BlockSpec((LANE, hk), const), pl.BlockSpec((1, hk), const),
            pl.BlockSpec((1, B_VAL_DIM), const),
        ],
        out_specs=pl.BlockSpec((GLA_TILE, VB_W), lambda b, i: (b * nt + i, 0)),
        out_shape=jax.ShapeDtypeStruct((t, VB_W), BF16),
        compiler_params=_cparams(2),
        name="gla_out",
    )(proj, proj, proj, proj, proj, s_f, s_b, wrf, brf, wrb, brb, ng)


def _sigmoid(x):
    return 1.0 / (1.0 + jnp.exp(-x))


def _merge_kernel(ya_ref, yb_ref, ga_ref, gb_ref, x_ref, wa_ref, wb_ref, wo_ref, o_ref,
                  wa_bf, wb_bf, wo_bf):
    @pl.when(pl.program_id(0) == 0)
    def _prep():
        wa_bf[...] = wa_ref[0].astype(BF16)
        wb_bf[...] = wb_ref[0].astype(BF16)
        wo_bf[...] = wo_ref[0].astype(BF16)

    a = jnp.dot(ya_ref[...], wa_bf[...], preferred_element_type=F32)
    b = jnp.dot(yb_ref[...], wb_bf[...], preferred_element_type=F32)
    merged = _sigmoid(ga_ref[...].astype(F32)) * a + _sigmoid(gb_ref[...].astype(F32)) * b
    o_ref[...] = x_ref[...] + jnp.dot(merged.astype(BF16), wo_bf[...], preferred_element_type=F32)


def _merge(ya, yb, proj, x2, wa_all, wb_all, wo_all, layer):
    t = x2.shape[0]
    tm = min(ROW_TILE, t)
    wsel = lambda i: (layer, 0, 0)
    return pl.pallas_call(
        _merge_kernel,
        grid=(t // tm,),
        in_specs=[
            pl.BlockSpec((tm, QA_W), lambda i: (i, 0)),
            pl.BlockSpec((tm, VB_W), lambda i: (i, 0)),
            pl.BlockSpec((tm, D_MODEL), lambda i: (i, COL_GA // D_MODEL)),
            pl.BlockSpec((tm, D_MODEL), lambda i: (i, COL_GB // D_MODEL)),
            pl.BlockSpec((tm, D_MODEL), lambda i: (i, 0)),
            pl.BlockSpec((1, QA_W, D_MODEL), wsel, pipeline_mode=pl.Buffered(1)),
            pl.BlockSpec((1, VB_W, D_MODEL), wsel, pipeline_mode=pl.Buffered(1)),
            pl.BlockSpec((1, D_MODEL, D_MODEL), wsel, pipeline_mode=pl.Buffered(1)),
        ],
        out_specs=pl.BlockSpec((tm, D_MODEL), lambda i: (i, 0)),
        out_shape=jax.ShapeDtypeStruct((t, D_MODEL), F32),
        scratch_shapes=[pltpu.VMEM((QA_W, D_MODEL), BF16), pltpu.VMEM((VB_W, D_MODEL), BF16),
                        pltpu.VMEM((D_MODEL, D_MODEL), BF16)],
        compiler_params=_cparams(1),
        name="merge",
    )(ya, yb, proj, proj, x2, wa_all, wb_all, wo_all)


def _moe_tile_tokens(t):
    return min(4096, t)


MOE_BLK = 512
MOE_UNIT = 16
MOE_CAP = 1536
XS_W = D_MODEL + LANE
MOE_MT = 384
MOE_CH = 2 * MOE_MT
MOE_CHU = MOE_CH // MOE_UNIT
MOE_NC = 256
assert MOE_CAP >= 2 * MOE_BLK + N_EXPERTS * (MOE_UNIT - 1) and MOE_CAP % MOE_UNIT == 0


NT_DIMS = (((1,), (1,)), ((), ()))


def _route(hn_bf, wrt_ref, brc_ref):
    logits = lax.dot_general(wrt_ref[...].astype(BF16), hn_bf, NT_DIMS,
                             preferred_element_type=F32) + brc_ref[:, 0:1]
    row = lax.broadcasted_iota(jnp.int32, logits.shape, 0).astype(F32)
    big = float(LANE)
    ninf = -jnp.inf
    gl = jnp.where(row < N_GROUPS, logits, ninf)
    gmax = jnp.max(gl, axis=0, keepdims=True)
    g_idx = jnp.min(jnp.where(gl == gmax, row, big), axis=0, keepdims=True)
    g_w = 1.0 / jnp.sum(jnp.exp(gl - gmax), axis=0, keepdims=True)
    lo = float(N_GROUPS) + g_idx * float(EXPERTS_PER_GROUP)
    el = jnp.where((row >= lo) & (row < lo + EXPERTS_PER_GROUP), logits, ninf)
    v1 = jnp.max(el, axis=0, keepdims=True)
    i1 = jnp.min(jnp.where(el == v1, row, big), axis=0, keepdims=True)
    el2 = jnp.where(row == i1, ninf, el)
    v2 = jnp.max(el2, axis=0, keepdims=True)
    i2 = jnp.min(jnp.where(el2 == v2, row, big), axis=0, keepdims=True)
    e21 = jnp.exp(v2 - v1)
    w1 = g_w / (1.0 + e21)
    w2 = g_w * e21 / (1.0 + e21)
    return i1 - float(N_GROUPS), i2 - float(N_GROUPS), w1, w2, row


def _slot_matrix(pos1, pos2, second):
    slot = lax.broadcasted_iota(jnp.int32, (pos1.shape[0], MOE_CAP), 1)
    return jnp.where(slot == pos1.astype(jnp.int32), 1.0,
                     jnp.where(slot == pos2.astype(jnp.int32), second, 0.0)).astype(BF16)


def _slot_matrix_t(pos1, pos2, second):
    slot = lax.broadcasted_iota(jnp.int32, (MOE_CAP, pos1.shape[1]), 0)
    return jnp.where(slot == pos1.astype(jnp.int32), 1.0,
                     jnp.where(slot == pos2.astype(jnp.int32), second, 0.0)).astype(BF16)


def _split3_f32(w):
    hi = w.astype(BF16).astype(F32)
    r1 = w - hi
    mid = r1.astype(BF16).astype(F32)
    return hi, mid, r1 - mid


def _rows_to_sublanes(vals, row):
    out = jnp.zeros(row.shape, F32)
    for k, v in enumerate(vals):
        out = jnp.where(row == float(k), v, out)
    return out


def _moe_kernel(x_ref, g_ref, wrt_ref, brc_ref, wg_ref, wu_ref, wd_ref, gf_ref, o_ref,
                xs_ref, pos_ref, tab_v, tab_s, xt_ref, addr_ref, wgu_bf, wd_bf, sem,
                *, nblk, final_norm):
    s = pl.program_id(1)
    dump_row = nblk * MOE_CAP

    @pl.when(s < nblk)
    def _dispatch():
        @pl.when(s == 0)
        def _init():
            xs_ref[dump_row:dump_row + MOE_UNIT, :] = jnp.zeros((MOE_UNIT, XS_W), BF16)
            xt_ref[...] = jnp.zeros_like(xt_ref)

        x = x_ref[...]
        ms = jnp.mean(x * x, axis=-1, keepdims=True)
        hn = (x * lax.rsqrt(ms + EPS) * g_ref[0]).astype(BF16)
        e1, e2, w1, w2, row = _route(hn, wrt_ref, brc_ref)
        onehot = ((row == e1) | (row == e2)).astype(F32)
        earlier = (lax.broadcasted_iota(jnp.int32, (MOE_BLK, MOE_BLK), 0)
                   < lax.broadcasted_iota(jnp.int32, (MOE_BLK, MOE_BLK), 1)).astype(BF16)
        before = jnp.dot(onehot.astype(BF16), earlier, preferred_element_type=F32)
        counts = jnp.sum(onehot, axis=1, keepdims=True)
        units = jnp.floor((counts + (MOE_UNIT - 1.0)) * (1.0 / MOE_UNIT))
        lower = (lax.broadcasted_iota(jnp.int32, (LANE, LANE), 1)
                 < lax.broadcasted_iota(jnp.int32, (LANE, LANE), 0)).astype(BF16)
        padded = jnp.broadcast_to(units * MOE_UNIT, (LANE, LANE))
        run_off = jnp.dot(lower, padded.astype(BF16), preferred_element_type=F32)[:, 0:1]
        start = run_off + before
        pos1 = jnp.sum(jnp.where(row == e1, start, 0.0), axis=0, keepdims=True)
        pos2 = jnp.sum(jnp.where(row == e2, start, 0.0), axis=0, keepdims=True)
        pt_t = _slot_matrix_t(pos1, pos2, 2.0)
        rows = pl.ds(pl.multiple_of(s * MOE_CAP, MOE_CAP), MOE_CAP)
        for c in range(D_MODEL // MOE_NC):
            cs = slice(c * MOE_NC, (c + 1) * MOE_NC)
            xs_ref[rows, cs] = jnp.dot(pt_t, hn[:, cs], preferred_element_type=F32).astype(BF16)
        w_rows = _rows_to_sublanes(
            _split3_f32(w1) + _split3_f32(w2) + (jnp.ones_like(w1),), row).astype(BF16)
        xs_ref[rows, D_MODEL:] = lax.dot_general(pt_t, w_rows, NT_DIMS,
                                                 preferred_element_type=F32).astype(BF16)
        pos_ref[s] = _rows_to_sublanes((pos1, pos2), row).T
        lane_e = lax.broadcasted_iota(jnp.int32, (LANE, LANE), 1)
        per_expert = jnp.where(lane_e == 0, run_off, jnp.where(lane_e == 1, units, 0.0))
        tab_v[s] = per_expert.T[0:SUBLANE, :].astype(jnp.int32)

        @pl.when(s == nblk - 1)
        def _publish():
            cp = pltpu.make_async_copy(tab_v, tab_s, sem)
            cp.start()
            cp.wait()

    @pl.when((s >= nblk) & (s < nblk + N_EXPERTS))
    def _experts():
        e = s - nblk
        wgu_bf[:, :D_EXPERT] = wg_ref[0].astype(BF16)
        wgu_bf[:, D_EXPERT:] = wu_ref[0].astype(BF16)
        wd_bf[...] = wd_ref[0].astype(BF16)
        total = tab_s[0, 1, e]
        for j in range(1, nblk):
            total = total + tab_s[j, 1, e]

        def chunk(c, carry):
            u0 = c * MOE_CHU
            for k in range(MOE_CHU):
                addr_ref[k] = dump_row
            cum = 0
            for j in range(nblk):
                nj = tab_s[j, 1, e]
                run0 = j * MOE_CAP + tab_s[j, 0, e]
                k_off = cum - u0

                def copy_unit(u, c2, run0=run0, k_off=k_off):
                    src = pl.multiple_of(run0 + u * MOE_UNIT, MOE_UNIT)
                    dst = pl.multiple_of((k_off + u) * MOE_UNIT, MOE_UNIT)
                    xt_ref[pl.ds(dst, MOE_UNIT), :] = xs_ref[pl.ds(src, MOE_UNIT), :]
                    addr_ref[k_off + u] = src
                    return c2
                lax.fori_loop(jnp.clip(u0 - cum, 0, nj), jnp.clip(u0 + MOE_CHU - cum, 0, nj),
                              copy_unit, 0)
                cum = cum + nj
            rows_here = jnp.minimum(MOE_CHU, total - u0) * MOE_UNIT

            def tile(t, c3):
                xt = xt_ref[pl.ds(pl.multiple_of(t * MOE_MT, MOE_MT), MOE_MT), :]
                wl = xt[:, D_MODEL:].astype(F32)
                second = wl[:, 6:7] > 1.5
                wcol = jnp.where(second, 0.5 * (wl[:, 3:4] + wl[:, 4:5] + wl[:, 5:6]),
                                 wl[:, 0:1] + wl[:, 1:2] + wl[:, 2:3])
                gu = (jnp.dot(xt[:, :D_MODEL], wgu_bf[...], preferred_element_type=F32)
                      * jnp.where(second, 0.5, 1.0))
                gate = gu[:, :D_EXPERT]
                h = gate * _sigmoid(gate) * gu[:, D_EXPERT:] * wcol
                y = jnp.dot(h.astype(BF16), wd_bf[...], preferred_element_type=F32).astype(BF16)
                for uu in range(MOE_MT // MOE_UNIT):
                    dst = pl.multiple_of(addr_ref[t * (MOE_MT // MOE_UNIT) + uu], MOE_UNIT)
                    xs_ref[pl.ds(dst, MOE_UNIT), :D_MODEL] = y[uu * MOE_UNIT:(uu + 1) * MOE_UNIT]
                return c3
            lax.fori_loop(0, (rows_here + MOE_MT - 1) // MOE_MT, tile, 0)
            return carry
        lax.fori_loop(0, (total + MOE_CHU - 1) // MOE_CHU, chunk, 0)

    @pl.when(s >= nblk + N_EXPERTS)
    def _combine():
        j = s - nblk - N_EXPERTS
        p = pos_ref[j]
        pt = _slot_matrix(p[:, 0:1], p[:, 1:2], 1.0)
        rows = pl.ds(pl.multiple_of(j * MOE_CAP, MOE_CAP), MOE_CAP)
        for c in range(D_MODEL // MOE_NC):
            cs = slice(c * MOE_NC, (c + 1) * MOE_NC)
            o_ref[:, cs] = x_ref[:, cs] + jnp.dot(pt, xs_ref[rows, cs], preferred_element_type=F32)
        if final_norm:
            y = o_ref[...]
            ms = jnp.mean(y * y, axis=-1, keepdims=True)
            o_ref[...] = y * lax.rsqrt(ms + EPS) * gf_ref[...]


def _moe(x2, g_all, w_r, b_r, wg_all, wu_all, wd_all, g_final, layer, tt, final_norm):
    t = x2.shape[0]
    nblk = tt // MOE_BLK
    steps = 2 * nblk + N_EXPERTS

    def tok(b, s):
        j = jnp.where(s < nblk, s, jnp.where(s < nblk + N_EXPERTS, nblk - 1, s - nblk - N_EXPERTS))
        return (b * nblk + j, 0)

    def out_tok(b, s):
        return (b * nblk + jnp.maximum(s - nblk - N_EXPERTS, 0), 0)

    wsel = lambda b, s: (layer * N_EXPERTS + jnp.clip(s - nblk, 0, N_EXPERTS - 1), 0, 0)
    const = lambda b, s: (0, 0)
    return pl.pallas_call(
        functools.partial(_moe_kernel, nblk=nblk, final_norm=final_norm),
        grid=(t // tt, steps),
        in_specs=[
            pl.BlockSpec((MOE_BLK, D_MODEL), tok),
            pl.BlockSpec((1, 1, D_MODEL), lambda b, s: (layer, 0, 0)),
            pl.BlockSpec((LANE, D_MODEL), const),
            pl.BlockSpec((LANE, 1), const),
            pl.BlockSpec((1, D_MODEL, D_EXPERT), wsel),
            pl.BlockSpec((1, D_MODEL, D_EXPERT), wsel),
            pl.BlockSpec((1, D_EXPERT, D_MODEL), wsel),
            pl.BlockSpec((1, D_MODEL), const),
        ],
        out_specs=pl.BlockSpec((MOE_BLK, D_MODEL), out_tok),
        out_shape=jax.ShapeDtypeStruct((t, D_MODEL), F32),
        scratch_shapes=[
            pltpu.VMEM((nblk * MOE_CAP + MOE_UNIT, XS_W), BF16),
            pltpu.VMEM((nblk, MOE_BLK, LANE), F32),
            pltpu.VMEM((nblk, SUBLANE, LANE), jnp.int32),
            pltpu.SMEM((nblk, SUBLANE, LANE), jnp.int32),
            pltpu.VMEM((MOE_CH, XS_W), BF16),
            pltpu.SMEM((MOE_CHU,), jnp.int32),
            pltpu.VMEM((D_MODEL, 2 * D_EXPERT), BF16),
            pltpu.VMEM((D_EXPERT, D_MODEL), BF16),
            pltpu.SemaphoreType.DMA,
        ],
        compiler_params=_cparams(2),
        name="moe",
    )(x2, g_all, w_r, b_r, wg_all, wu_all, wd_all, g_final)


def _t5_bucket(rel):
    nb = N_BUCKETS // 2
    max_exact = nb // 2
    base = jnp.where(rel > 0, nb, 0)
    n = jnp.abs(rel)
    large = max_exact + (jnp.log(jnp.maximum(n, 1).astype(jnp.float32) / max_exact)
                         / math.log(MAX_DISTANCE / max_exact) * (nb - max_exact)).astype(jnp.int32)
    large = jnp.minimum(large, nb - 1)
    return base + jnp.where(n < max_exact, n, large)


def _pad_rank(wr):
    return jnp.pad(wr, ((0, LANE - B_RANK), (0, 0))).astype(BF16)


def kernel(x, w_in, rel_bias, attn_sink, gla_wr_fwd, gla_br_fwd, gla_wr_bwd, gla_br_bwd, gla_norm, w_branch_a, w_branch_b, w_out, norm_mix, norm_ffn, router_group_w, router_group_b, router_expert_w, router_expert_b, expert_w_gate, expert_w_up, expert_w_down, norm_final):
    bsz, seq, d = x.shape
    depth = w_in.shape[0]
    t = bsz * seq
    tt = _moe_tile_tokens(t)
    q_off = jnp.arange(BLOCK)
    k_off = jnp.arange(3 * BLOCK) - BLOCK
    bucket = _t5_bucket(k_off[None, :] - q_off[:, None]).astype(jnp.int32)
    x2 = x.reshape(t, d)
    norm_mix3 = norm_mix[:, None, :]
    w_in_t = jnp.swapaxes(w_in, 1, 2)
    norm_ffn3 = norm_ffn[:, None, :]
    wg_all = expert_w_gate.reshape(depth * N_EXPERTS, d, D_EXPERT)
    wu_all = expert_w_up.reshape(depth * N_EXPERTS, d, D_EXPERT)
    wd_all = expert_w_down.reshape(depth * N_EXPERTS, D_EXPERT, d)
    for l in range(depth):
        proj = _inproj(x2, norm_mix3, w_in_t, l)
        ya = _attention(proj, bucket, rel_bias, attn_sink[l], bsz, seq)
        wrf, wrb = _pad_rank(gla_wr_fwd[l]), _pad_rank(gla_wr_bwd[l])
        brf, brb = gla_br_fwd[l][None, :], gla_br_bwd[l][None, :]
        s_f, s_b = _gla_states(proj, wrf.T, brf.T, wrb.T, brb.T, bsz, seq)
        yb = _gla_out(proj, s_f, s_b, wrf, brf, wrb, brb, gla_norm[l][None, :], bsz, seq)
        x2 = _merge(ya, yb, proj, x2, w_branch_a, w_branch_b, w_out, l)
        w_r = jnp.concatenate(
            [router_group_w[l].T, router_expert_w[l].T,
             jnp.zeros((LANE - N_GROUPS - N_EXPERTS, d), F32)], axis=0)
        b_r = jnp.concatenate(
            [router_group_b[l], router_expert_b[l],
             jnp.zeros((LANE - N_GROUPS - N_EXPERTS,), F32)])[:, None]
        x2 = _moe(x2, norm_ffn3, w_r, b_r, wg_all, wu_all, wd_all, norm_final[None, :], l, tt,
                  final_norm=(l == depth - 1))
    return x2.reshape(bsz, seq, d)
```

```python
import functools
import math

import numpy as np
import jax
import jax.numpy as jnp
from jax import lax
from jax.experimental import pallas as pl
from jax.experimental.pallas import tpu as pltpu

F32 = jnp.float32
BF16 = jnp.bfloat16

D_MODEL = 1024
A_HEADS = 8
A_KV_HEADS = 2
A_HEAD_DIM = 64
A_GROUP = A_HEADS // A_KV_HEADS
WINDOW = 128
BLOCK = 128
N_BUCKETS = 32
MAX_DISTANCE = 128
B_HEADS = 4
B_KEY_DIM = 64
B_VAL_DIM = 128
B_RANK = 16
GATE_TAU = 16.0
CHUNK = 64
N_GROUPS = 4
EXPERTS_PER_GROUP = 8
N_EXPERTS = N_GROUPS * EXPERTS_PER_GROUP
D_EXPERT = 256
EPS = 1e-6
NEG_INF = -1e30

LANE = 128
SUBLANE = 8
VMEM_LIMIT = 56 * 1024 * 1024

QA_W = A_HEADS * A_HEAD_DIM
KA_W = A_KV_HEADS * A_HEAD_DIM
QB_W = B_HEADS * B_KEY_DIM
VB_W = B_HEADS * B_VAL_DIM
COL_GA = 0
COL_GB = COL_GA + D_MODEL
COL_AQ = COL_GB + D_MODEL
COL_BV = COL_AQ + QA_W
COL_BG = COL_BV + VB_W
COL_BQ = COL_BG + VB_W
COL_BK = COL_BQ + QB_W
COL_AK = COL_BK + QB_W
COL_AV = COL_AK + KA_W
COL_R = COL_AV + KA_W
PROJ_W = COL_R + LANE

ROW_TILE = 512
PROJ_CHUNK = 1280
GLA_TILE = 256
CPT = GLA_TILE // CHUNK


def _cparams(n_axes):
    return pltpu.CompilerParams(
        dimension_semantics=("arbitrary",) * n_axes, vmem_limit_bytes=VMEM_LIMIT)


_W_IN_SRC = np.cumsum([0, QA_W, KA_W, KA_W, QB_W, QB_W, VB_W, VB_W, B_RANK, D_MODEL, D_MODEL])
_W_IN_SEGMENTS = tuple(zip(
    (COL_AQ, COL_AK, COL_AV, COL_BQ, COL_BK, COL_BV, COL_BG, COL_R, COL_GA, COL_GB),
    (int(s) for s in _W_IN_SRC[:-1]),
    (int(w) for w in np.diff(_W_IN_SRC))))
W_IN_COLS = int(_W_IN_SRC[-1])
W_ROWS_PER_COPY = 256


def _load_permuted_w_in(wt_ref, wbf_ref):
    wbf_ref[COL_R:COL_R + LANE, :] = jnp.zeros((LANE, D_MODEL), BF16)
    for dst, src, width in _W_IN_SEGMENTS:
        for r0 in range(0, width, W_ROWS_PER_COPY):
            n = min(W_ROWS_PER_COPY, width - r0)
            wbf_ref[dst + r0:dst + r0 + n, :] = wt_ref[0, src + r0:src + r0 + n, :].astype(BF16)


def _inproj_kernel(x_ref, g_ref, wt_ref, o_ref, wbf_ref):
    @pl.when(pl.program_id(0) == 0)
    def _prep():
        _load_permuted_w_in(wt_ref, wbf_ref)

    x = x_ref[...]
    ms = jnp.mean(x * x, axis=-1, keepdims=True)
    xn = (x * lax.rsqrt(ms + EPS) * g_ref[0]).astype(BF16)
    for c0 in range(0, PROJ_W, PROJ_CHUNK):
        sl = slice(c0, min(c0 + PROJ_CHUNK, PROJ_W))
        o_ref[:, sl] = lax.dot_general(xn, wbf_ref[sl, :], (((1,), (1,)), ((), ())),
                                       preferred_element_type=F32).astype(BF16)


def _inproj(x2, g_all, wt_all, layer):
    t = x2.shape[0]
    tm = min(ROW_TILE, t)
    return pl.pallas_call(
        _inproj_kernel,
        grid=(t // tm,),
        in_specs=[
            pl.BlockSpec((tm, D_MODEL), lambda i: (i, 0)),
            pl.BlockSpec((1, 1, D_MODEL), lambda i: (layer, 0, 0)),
            pl.BlockSpec((1, W_IN_COLS, D_MODEL), lambda i: (layer, 0, 0), pipeline_mode=pl.Buffered(1)),
        ],
        out_specs=pl.BlockSpec((tm, PROJ_W), lambda i: (i, 0)),
        out_shape=jax.ShapeDtypeStruct((t, PROJ_W), BF16),
        scratch_shapes=[pltpu.VMEM((PROJ_W, D_MODEL), BF16)],
        compiler_params=_cparams(1),
        name="inproj",
    )(x2, g_all, wt_all)


ATT_QB = 2


def _attn_kernel(bucket_ref, relb_ref, sink_ref, q_ref, kp_ref, kc_ref, kn_ref,
                 vp_ref, vc_ref, vn_ref, o_ref, bias_ref, *, nblk):
    b = pl.program_id(0)
    n = pl.program_id(1)
    kw = 3 * BLOCK

    @pl.when((b == 0) & (n == 0))
    def _build_bias():
        bucket = bucket_ref[...]
        row = lax.broadcasted_iota(jnp.int32, (BLOCK, kw), 0)
        col = lax.broadcasted_iota(jnp.int32, (BLOCK, kw), 1)
        win = jnp.abs(col - BLOCK - row) <= WINDOW
        keep = (win & (col >= BLOCK), win, win & (col < 2 * BLOCK))
        for hq in range(A_HEADS):
            acc = jnp.zeros((BLOCK, kw), F32)
            for bb in range(N_BUCKETS):
                acc = jnp.where(bucket == bb, relb_ref[bb, hq], acc)
            for v in range(3):
                bias_ref[v, hq] = jnp.where(keep[v], acc, NEG_INF)

    q = (q_ref[...].astype(F32) * (A_HEAD_DIM ** -0.5)).astype(BF16)
    kall = jnp.concatenate([kp_ref[...], kc_ref[...], kn_ref[...]], axis=0)
    vall = jnp.concatenate([vp_ref[...], vc_ref[...], vn_ref[...]], axis=0)
    jobs = [(j, hq) for j in range(ATT_QB) for hq in range(A_HEADS)]

    def keys(x, j, hq):
        h = hq // A_GROUP
        return x[j * BLOCK:(j + 3) * BLOCK, h * A_HEAD_DIM:(h + 1) * A_HEAD_DIM]

    logits = [lax.dot_general(q[j * BLOCK:(j + 1) * BLOCK, hq * A_HEAD_DIM:(hq + 1) * A_HEAD_DIM],
                              keys(kall, j, hq), (((1,), (1,)), ((), ())), preferred_element_type=F32)
              for j, hq in jobs]
    probs, dens = [], []
    for i, (j, hq) in enumerate(jobs):
        blk = n * ATT_QB + j
        variant = jnp.where(blk == 0, 0, jnp.where(blk == nblk - 1, 2, 1))
        s = logits[i] + bias_ref[variant, hq]
        sk = sink_ref[hq]
        m = jnp.maximum(jnp.max(s, axis=-1, keepdims=True), sk)
        p = jnp.exp(s - m)
        dens.append(jnp.sum(p, axis=-1, keepdims=True) + jnp.exp(sk - m))
        probs.append(p.astype(BF16))
    outs = [jnp.dot(probs[i], keys(vall, j, hq), preferred_element_type=F32) / dens[i]
            for i, (j, hq) in enumerate(jobs)]
    for j in range(ATT_QB):
        o_ref[j * BLOCK:(j + 1) * BLOCK, :] = jnp.concatenate(
            outs[j * A_HEADS:(j + 1) * A_HEADS], axis=1).astype(BF16)


def _attention(proj, bucket, rel_bias, sink, bsz, seq):
    nblk = seq // BLOCK
    t = bsz * seq
    cq = COL_AQ // QA_W
    ck = COL_AK // KA_W
    cv = COL_AV // KA_W

    nsteps = nblk // ATT_QB
    rows = ATT_QB * BLOCK

    def kv_specs(c):
        return [pl.BlockSpec((BLOCK, KA_W), lambda b, n: (b * nblk + jnp.maximum(n * ATT_QB - 1, 0), c)),
                pl.BlockSpec((rows, KA_W), lambda b, n: (b * nsteps + n, c)),
                pl.BlockSpec((BLOCK, KA_W),
                             lambda b, n: (b * nblk + jnp.minimum((n + 1) * ATT_QB, nblk - 1), c))]

    smem = pl.BlockSpec(memory_space=pltpu.SMEM)
    return pl.pallas_call(
        functools.partial(_attn_kernel, nblk=nblk),
        grid=(bsz, nsteps),
        in_specs=[
            pl.BlockSpec((BLOCK, 3 * BLOCK), lambda b, n: (0, 0)),
            smem, smem,
            pl.BlockSpec((rows, QA_W), lambda b, n: (b * nsteps + n, cq)),
        ] + kv_specs(ck) + kv_specs(cv),
        out_specs=pl.BlockSpec((rows, QA_W), lambda b, n: (b * nsteps + n, 0)),
        out_shape=jax.ShapeDtypeStruct((t, QA_W), BF16),
        scratch_shapes=[pltpu.VMEM((3, A_HEADS, BLOCK, 3 * BLOCK), F32)],
        compiler_params=_cparams(2),
        name="attn",
    )(bucket, rel_bias, sink, proj, proj, proj, proj, proj, proj, proj)


def _log_sigmoid(z):
    return jnp.minimum(z, 0.0) - jnp.log(1.0 + jnp.exp(-jnp.abs(z)))


def _log_decay(r, wr, br):
    z = jnp.dot(r, wr, preferred_element_type=F32) + br
    return _log_sigmoid(z) / GATE_TAU


def _chunk_masks():
    row = lax.broadcasted_iota(jnp.int32, (GLA_TILE, GLA_TILE), 0)
    col = lax.broadcasted_iota(jnp.int32, (GLA_TILE, GLA_TILE), 1)
    sh = int(math.log2(CHUNK))
    same = jnp.right_shift(row, sh) == jnp.right_shift(col, sh)
    return same, same & (col <= row), same & (col >= row)


def _f32dot(a, b):
    return jnp.dot(a, b, preferred_element_type=F32, precision=lax.Precision.HIGHEST)


def _split3(x):
    hi = x.astype(BF16)
    r1 = x - hi.astype(F32)
    mid = r1.astype(BF16)
    lo = (r1 - mid.astype(F32)).astype(BF16)
    return hi, mid, lo


def _mask_dot(mask, x):
    m = mask.astype(BF16)
    hi, mid, lo = _split3(x)
    return (jnp.dot(m, hi, preferred_element_type=F32) + jnp.dot(m, mid, preferred_element_type=F32)
            + jnp.dot(m, lo, preferred_element_type=F32))


def _gla_state_dir(k_ref, v_ref, r_ref, wrt_ref, brc_ref, s_out_ref, state_ref, order, cum_mask, same):
    z = lax.dot_general(wrt_ref[...], r_ref[...], (((1,), (1,)), ((), ())),
                        preferred_element_type=F32) + brc_ref[:, 0:1]
    hi, mid, lo = _split3(_log_sigmoid(z) / GATE_TAU)
    m = jnp.concatenate([cum_mask, same], axis=1).astype(BF16)
    both = (jnp.dot(hi, m, preferred_element_type=F32) + jnp.dot(mid, m, preferred_element_type=F32)
            + jnp.dot(lo, m, preferred_element_type=F32))
    cum = both[:, :GLA_TILE]
    tot = both[:, GLA_TILE:]
    k_end = (k_ref[...].astype(F32).T * jnp.exp(tot - cum)).astype(BF16)
    lane_chunk = jnp.right_shift(lax.broadcasted_iota(jnp.int32, (1, GLA_TILE), 1),
                                 int(math.log2(CHUNK)))
    upd = []
    for h in range(B_HEADS):
        kh = k_end[h * B_KEY_DIM:(h + 1) * B_KEY_DIM]
        lhs = jnp.concatenate([jnp.where(lane_chunk == c, kh, 0.0).astype(BF16) for c in range(CPT)],
                              axis=0)
        upd.append(jnp.dot(lhs, v_ref[:, h * B_VAL_DIM:(h + 1) * B_VAL_DIM],
                           preferred_element_type=F32))
    for c in order:
        s_out_ref[0, c] = state_ref[...].astype(BF16)
        upd_c = jnp.concatenate([u[c * B_KEY_DIM:(c + 1) * B_KEY_DIM] for u in upd], axis=0)
        decay = jnp.exp(tot[:, c * CHUNK:c * CHUNK + 1])
        state_ref[...] = decay * state_ref[...] + upd_c


def _gla_state_kernel(kf_ref, vf_ref, rf_ref, kb_ref, vb_ref, rb_ref,
                      wrf_ref, brf_ref, wrb_ref, brb_ref,
                      sf_ref, sb_ref, stf_ref, stb_ref):
    @pl.when(pl.program_id(1) == 0)
    def _reset():
        stf_ref[...] = jnp.zeros_like(stf_ref)
        stb_ref[...] = jnp.zeros_like(stb_ref)

    same, lower, upper = _chunk_masks()
    _gla_state_dir(kf_ref, vf_ref, rf_ref, wrf_ref, brf_ref, sf_ref, stf_ref,
                   range(CPT), upper, same)
    _gla_state_dir(kb_ref, vb_ref, rb_ref, wrb_ref, brb_ref, sb_ref, stb_ref,
                   range(CPT - 1, -1, -1), lower, same)


def _gla_states(proj, wrf_t, brf_c, wrb_t, brb_c, bsz, seq):
    nt = seq // GLA_TILE
    nchunks = seq // CHUNK
    hk = B_HEADS * B_KEY_DIM
    ck, cv, cr = COL_BK // QB_W, COL_BV // VB_W, COL_R // LANE

    def fwd(b, i):
        return b * nt + i

    def bwd(b, i):
        return b * nt + (nt - 1 - i)

    def tile_specs(f):
        return [pl.BlockSpec((GLA_TILE, QB_W), lambda b, i: (f(b, i), ck)),
                pl.BlockSpec((GLA_TILE, VB_W), lambda b, i: (f(b, i), cv)),
                pl.BlockSpec((GLA_TILE, LANE), lambda b, i: (f(b, i), cr))]

    const = lambda b, i: (0, 0)
    out_sds = jax.ShapeDtypeStruct((bsz, nchunks, hk, B_VAL_DIM), BF16)
    return pl.pallas_call(
        _gla_state_kernel,
        grid=(bsz, nt),
        in_specs=tile_specs(fwd) + tile_specs(bwd) + [
            pl.BlockSpec((hk, LANE), const), pl.BlockSpec((hk, 1), const),
            pl.BlockSpec((hk, LANE), const), pl.BlockSpec((hk, 1), const)],
        out_specs=[pl.BlockSpec((1, CPT, hk, B_VAL_DIM), lambda b, i: (b, i, 0, 0)),
                   pl.BlockSpec((1, CPT, hk, B_VAL_DIM), lambda b, i: (b, nt - 1 - i, 0, 0))],
        out_shape=[out_sds, out_sds],
        scratch_shapes=[pltpu.VMEM((hk, B_VAL_DIM), F32), pltpu.VMEM((hk, B_VAL_DIM), F32)],
        compiler_params=_cparams(2),
        name="gla_state",
    )(proj, proj, proj, proj, proj, proj, wrf_t, brf_c, wrb_t, brb_c)


def _gla_out_kernel(q_ref, k_ref, v_ref, r_ref, g_ref, sf_ref, sb_ref,
                    wrf_ref, brf_ref, wrb_ref, brb_ref, ng_ref, o_ref):
    same, lower, upper = _chunk_masks()
    r = r_ref[...]
    cum_f = _mask_dot(lower, _log_decay(r, wrf_ref[...], brf_ref[...]))
    cum_b = _mask_dot(upper, _log_decay(r, wrb_ref[...], brb_ref[...]))
    q = q_ref[...].astype(F32) * (B_KEY_DIM ** -0.5)
    k = k_ref[...].astype(F32)
    qd_f = q * jnp.exp(cum_f)
    qd_b = q * jnp.exp(cum_b)
    ki_f = (k * jnp.exp(-cum_f)).astype(BF16)
    ki_b = (k * jnp.exp(-cum_b)).astype(BF16)
    lane_head = jnp.right_shift(
        lax.broadcasted_iota(jnp.int32, (1, B_HEADS * B_KEY_DIM), 1), int(math.log2(B_KEY_DIM)))

    def stack_heads(x):
        return jnp.concatenate(
            [jnp.where(lane_head == h, x, 0.0) for h in range(B_HEADS)], axis=0).astype(BF16)

    qs_f = stack_heads(qd_f)
    qs_b = stack_heads(qd_b)
    nt_dims = (((1,), (1,)), ((), ()))
    sc_f = lax.dot_general(qs_f, ki_f, nt_dims, preferred_element_type=F32)
    sc_b = lax.dot_general(qs_b, ki_b, nt_dims, preferred_element_type=F32)
    lower_s = jnp.concatenate([lower] * B_HEADS, axis=0)
    strict_upper_s = jnp.concatenate([upper & ~lower] * B_HEADS, axis=0)
    p = jnp.where(lower_s, sc_f, jnp.where(strict_upper_s, sc_b, 0.0)).astype(BF16)
    v = v_ref[...]
    inter = []
    for c in range(CPT):
        lhs_f = jnp.concatenate(
            [qs_f[h * GLA_TILE + c * CHUNK:h * GLA_TILE + (c + 1) * CHUNK] for h in range(B_HEADS)], axis=0)
        lhs_b = jnp.concatenate(
            [qs_b[h * GLA_TILE + c * CHUNK:h * GLA_TILE + (c + 1) * CHUNK] for h in range(B_HEADS)], axis=0)
        inter.append(jnp.dot(lhs_f, sf_ref[0, c], preferred_element_type=F32)
                     + jnp.dot(lhs_b, sb_ref[0, c], preferred_element_type=F32))
    ng = ng_ref[...]
    g = g_ref[...].astype(F32)
    for h in range(B_HEADS):
        vs = slice(h * B_VAL_DIM, (h + 1) * B_VAL_DIM)
        o = jnp.dot(p[h * GLA_TILE:(h + 1) * GLA_TILE], v[:, vs], preferred_element_type=F32)
        o = o + jnp.concatenate([inter[c][h * CHUNK:(h + 1) * CHUNK] for c in range(CPT)], axis=0)
        o = o * lax.rsqrt(jnp.mean(o * o, axis=-1, keepdims=True) + EPS) * ng
        gh = g[:, vs]
        o_ref[:, vs] = (o * (gh / (1.0 + jnp.exp(-gh)))).astype(BF16)


def _gla_out(proj, s_f, s_b, wrf, brf, wrb, brb, ng, bsz, seq):
    nt = seq // GLA_TILE
    t = bsz * seq
    hk = B_HEADS * B_KEY_DIM
    const = lambda b, i: (0, 0)
    row = lambda c: (lambda b, i: (b * nt + i, c))
    return pl.pallas_call(
        _gla_out_kernel,
        grid=(bsz, nt),
        in_specs=[
            pl.BlockSpec((GLA_TILE, QB_W), row(COL_BQ // QB_W)),
            pl.BlockSpec((GLA_TILE, QB_W), row(COL_BK // QB_W)),
            pl.BlockSpec((GLA_TILE, VB_W), row(COL_BV // VB_W)),
            pl.BlockSpec((GLA_TILE, LANE), row(COL_R // LANE)),
            pl.BlockSpec((GLA_TILE, VB_W), row(COL_BG // VB_W)),
            pl.BlockSpec((1, CPT, hk, B_VAL_DIM), lambda b, i: (b, i, 0, 0)),
            pl.BlockSpec((1, CPT, hk, B_VAL_DIM), lambda b, i: (b, i, 0, 0)),
            pl.BlockSpec((LANE, hk), const), pl.BlockSpec((1, hk), const),
            pl.BlockSpec((LANE, hk), const), pl.BlockSpec((1, hk), const),
            pl.BlockSpec((1, B_VAL_DIM), const),
        ],
        out_specs=pl.BlockSpec((GLA_TILE, VB_W), lambda b, i: (b * nt + i, 0)),
        out_shape=jax.ShapeDtypeStruct((t, VB_W), BF16),
        compiler_params=_cparams(2),
        name="gla_out",
    )(proj, proj, proj, proj, proj, s_f, s_b, wrf, brf, wrb, brb, ng)


def _sigmoid(x):
    return 1.0 / (1.0 + jnp.exp(-x))


def _merge_kernel(ya_ref, yb_ref, ga_ref, gb_ref, x_ref, wa_ref, wb_ref, wo_ref, o_ref,
                  wa_bf, wb_bf, wo_bf):
    @pl.when(pl.program_id(0) == 0)
    def _prep():
        wa_bf[...] = wa_ref[0].astype(BF16)
        wb_bf[...] = wb_ref[0].astype(BF16)
        wo_bf[...] = wo_ref[0].astype(BF16)

    a = jnp.dot(ya_ref[...], wa_bf[...], preferred_element_type=F32)
    b = jnp.dot(yb_ref[...], wb_bf[...], preferred_element_type=F32)
    merged = _sigmoid(ga_ref[...].astype(F32)) * a + _sigmoid(gb_ref[...].astype(F32)) * b
    o_ref[...] = x_ref[...] + jnp.dot(merged.astype(BF16), wo_bf[...], preferred_element_type=F32)


def _merge(ya, yb, proj, x2, wa_all, wb_all, wo_all, layer):
    t = x2.shape[0]
    tm = min(ROW_TILE, t)
    wsel = lambda i: (layer, 0, 0)
    return pl.pallas_call(
        _merge_kernel,
        grid=(t // tm,),
        in_specs=[
            pl.BlockSpec((tm, QA_W), lambda i: (i, 0)),
            pl.BlockSpec((tm, VB_W), lambda i: (i, 0)),
            pl.BlockSpec((tm, D_MODEL), lambda i: (i, COL_GA // D_MODEL)),
            pl.BlockSpec((tm, D_MODEL), lambda i: (i, COL_GB // D_MODEL)),
            pl.BlockSpec((tm, D_MODEL), lambda i: (i, 0)),
            pl.BlockSpec((1, QA_W, D_MODEL), wsel, pipeline_mode=pl.Buffered(1)),
            pl.BlockSpec((1, VB_W, D_MODEL), wsel, pipeline_mode=pl.Buffered(1)),
            pl.BlockSpec((1, D_MODEL, D_MODEL), wsel, pipeline_mode=pl.Buffered(1)),
        ],
        out_specs=pl.BlockSpec((tm, D_MODEL), lambda i: (i, 0)),
        out_shape=jax.ShapeDtypeStruct((t, D_MODEL), F32),
        scratch_shapes=[pltpu.VMEM((QA_W, D_MODEL), BF16), pltpu.VMEM((VB_W, D_MODEL), BF16),
                        pltpu.VMEM((D_MODEL, D_MODEL), BF16)],
        compiler_params=_cparams(1),
        name="merge",
    )(ya, yb, proj, proj, x2, wa_all, wb_all, wo_all)


def _moe_tile_tokens(t):
    return min(4096, t)


MOE_BLK = 512
MOE_UNIT = 16
MOE_CAP = 1536
XS_W = D_MODEL + LANE
MOE_MT = 384
MOE_CH = 2 * MOE_MT
MOE_CHU = MOE_CH // MOE_UNIT
MOE_NC = 256
assert MOE_CAP >= 2 * MOE_BLK + N_EXPERTS * (MOE_UNIT - 1) and MOE_CAP % MOE_UNIT == 0


NT_DIMS = (((1,), (1,)), ((), ()))


def _route(hn_bf, wrt_ref, brc_ref):
    logits = lax.dot_general(wrt_ref[...].astype(BF16), hn_bf, NT_DIMS,
                             preferred_element_type=F32) + brc_ref[:, 0:1]
    row = lax.broadcasted_iota(jnp.int32, logits.shape, 0).astype(F32)
    big = float(LANE)
    ninf = -jnp.inf
    gl = jnp.where(row < N_GROUPS, logits, ninf)
    gmax = jnp.max(gl, axis=0, keepdims=True)
    g_idx = jnp.min(jnp.where(gl == gmax, row, big), axis=0, keepdims=True)
    g_w = 1.0 / jnp.sum(jnp.exp(gl - gmax), axis=0, keepdims=True)
    lo = float(N_GROUPS) + g_idx * float(EXPERTS_PER_GROUP)
    el = jnp.where((row >= lo) & (row < lo + EXPERTS_PER_GROUP), logits, ninf)
    v1 = jnp.max(el, axis=0, keepdims=True)
    i1 = jnp.min(jnp.where(el == v1, row, big), axis=0, keepdims=True)
    el2 = jnp.where(row == i1, ninf, el)
    v2 = jnp.max(el2, axis=0, keepdims=True)
    i2 = jnp.min(jnp.where(el2 == v2, row, big), axis=0, keepdims=True)
    e21 = jnp.exp(v2 - v1)
    w1 = g_w / (1.0 + e21)
    w2 = g_w * e21 / (1.0 + e21)
    return i1 - float(N_GROUPS), i2 - float(N_GROUPS), w1, w2, row


def _slot_matrix(pos1, pos2, second):
    slot = lax.broadcasted_iota(jnp.int32, (pos1.shape[0], MOE_CAP), 1)
    return jnp.where(slot == pos1.astype(jnp.int32), 1.0,
                     jnp.where(slot == pos2.astype(jnp.int32), second, 0.0)).astype(BF16)


def _slot_matrix_t(pos1, pos2, second):
    slot = lax.broadcasted_iota(jnp.int32, (MOE_CAP, pos1.shape[1]), 0)
    return jnp.where(slot == pos1.astype(jnp.int32), 1.0,
                     jnp.where(slot == pos2.astype(jnp.int32), second, 0.0)).astype(BF16)


def _split3_f32(w):
    hi = w.astype(BF16).astype(F32)
    r1 = w - hi
    mid = r1.astype(BF16).astype(F32)
    return hi, mid, r1 - mid


def _rows_to_sublanes(vals, row):
    out = jnp.zeros(row.shape, F32)
    for k, v in enumerate(vals):
        out = jnp.where(row == float(k), v, out)
    return out


def _moe_kernel(x_ref, g_ref, wrt_ref, brc_ref, wg_ref, wu_ref, wd_ref, gf_ref, o_ref,
                xs_ref, pos_ref, tab_v, tab_s, xt_ref, addr_ref, wgu_bf, wd_bf, sem,
                *, nblk, final_norm):
    s = pl.program_id(1)
    dump_row = nblk * MOE_CAP

    @pl.when(s < nblk)
    def _dispatch():
        @pl.when(s == 0)
        def _init():
            xs_ref[dump_row:dump_row + MOE_UNIT, :] = jnp.zeros((MOE_UNIT, XS_W), BF16)
            xt_ref[...] = jnp.zeros_like(xt_ref)

        x = x_ref[...]
        ms = jnp.mean(x * x, axis=-1, keepdims=True)
        hn = (x * lax.rsqrt(ms + EPS) * g_ref[0]).astype(BF16)
        e1, e2, w1, w2, row = _route(hn, wrt_ref, brc_ref)
        onehot = ((row == e1) | (row == e2)).astype(F32)
        earlier = (lax.broadcasted_iota(jnp.int32, (MOE_BLK, MOE_BLK), 0)
                   < lax.broadcasted_iota(jnp.int32, (MOE_BLK, MOE_BLK), 1)).astype(BF16)
        before = jnp.dot(onehot.astype(BF16), earlier, preferred_element_type=F32)
        counts = jnp.sum(onehot, axis=1, keepdims=True)
        units = jnp.floor((counts + (MOE_UNIT - 1.0)) * (1.0 / MOE_UNIT))
        lower = (lax.broadcasted_iota(jnp.int32, (LANE, LANE), 1)
                 < lax.broadcasted_iota(jnp.int32, (LANE, LANE), 0)).astype(BF16)
        padded = jnp.broadcast_to(units * MOE_UNIT, (LANE, LANE))
        run_off = jnp.dot(lower, padded.astype(BF16), preferred_element_type=F32)[:, 0:1]
        start = run_off + before
        pos1 = jnp.sum(jnp.where(row == e1, start, 0.0), axis=0, keepdims=True)
        pos2 = jnp.sum(jnp.where(row == e2, start, 0.0), axis=0, keepdims=True)
        pt_t = _slot_matrix_t(pos1, pos2, 2.0)
        rows = pl.ds(pl.multiple_of(s * MOE_CAP, MOE_CAP), MOE_CAP)
        for c in range(D_MODEL // MOE_NC):
            cs = slice(c * MOE_NC, (c + 1) * MOE_NC)
            xs_ref[rows, cs] = jnp.dot(pt_t, hn[:, cs], preferred_element_type=F32).astype(BF16)
        w_rows = _rows_to_sublanes(
            _split3_f32(w1) + _split3_f32(w2) + (jnp.ones_like(w1),), row).astype(BF16)
        xs_ref[rows, D_MODEL:] = lax.dot_general(pt_t, w_rows, NT_DIMS,
                                                 preferred_element_type=F32).astype(BF16)
        pos_ref[s] = _rows_to_sublanes((pos1, pos2), row).T
        lane_e = lax.broadcasted_iota(jnp.int32, (LANE, LANE), 1)
        per_expert = jnp.where(lane_e == 0, run_off, jnp.where(lane_e == 1, units, 0.0))
        tab_v[s] = per_expert.T[0:SUBLANE, :].astype(jnp.int32)

        @pl.when(s == nblk - 1)
        def _publish():
            cp = pltpu.make_async_copy(tab_v, tab_s, sem)
            cp.start()
            cp.wait()

    @pl.when((s >= nblk) & (s < nblk + N_EXPERTS))
    def _experts():
        e = s - nblk
        wgu_bf[:, :D_EXPERT] = wg_ref[0].astype(BF16)
        wgu_bf[:, D_EXPERT:] = wu_ref[0].astype(BF16)
        wd_bf[...] = wd_ref[0].astype(BF16)
        total = tab_s[0, 1, e]
        for j in range(1, nblk):
            total = total + tab_s[j, 1, e]

        def chunk(c, carry):
            u0 = c * MOE_CHU
            for k in range(MOE_CHU):
                addr_ref[k] = dump_row
            cum = 0
            for j in range(nblk):
                nj = tab_s[j, 1, e]
                run0 = j * MOE_CAP + tab_s[j, 0, e]
                k_off = cum - u0

                def copy_unit(u, c2, run0=run0, k_off=k_off):
                    src = pl.multiple_of(run0 + u * MOE_UNIT, MOE_UNIT)
                    dst = pl.multiple_of((k_off + u) * MOE_UNIT, MOE_UNIT)
                    xt_ref[pl.ds(dst, MOE_UNIT), :] = xs_ref[pl.ds(src, MOE_UNIT), :]
                    addr_ref[k_off + u] = src
                    return c2
                lax.fori_loop(jnp.clip(u0 - cum, 0, nj), jnp.clip(u0 + MOE_CHU - cum, 0, nj),
                              copy_unit, 0)
                cum = cum + nj
            rows_here = jnp.minimum(MOE_CHU, total - u0) * MOE_UNIT

            def tile(t, c3):
                xt = xt_ref[pl.ds(pl.multiple_of(t * MOE_MT, MOE_MT), MOE_MT), :]
                wl = xt[:, D_MODEL:].astype(F32)
                second = wl[:, 6:7] > 1.5
                wcol = jnp.where(second, 0.5 * (wl[:, 3:4] + wl[:, 4:5] + wl[:, 5:6]),
                                 wl[:, 0:1] + wl[:, 1:2] + wl[:, 2:3])
                gu = (jnp.dot(xt[:, :D_MODEL], wgu_bf[...], preferred_element_type=F32)
                      * jnp.where(second, 0.5, 1.0))
                gate = gu[:, :D_EXPERT]
                h = gate * _sigmoid(gate) * gu[:, D_EXPERT:] * wcol
                y = jnp.dot(h.astype(BF16), wd_bf[...], preferred_element_type=F32).astype(BF16)
                for uu in range(MOE_MT // MOE_UNIT):
                    dst = pl.multiple_of(addr_ref[t * (MOE_MT // MOE_UNIT) + uu], MOE_UNIT)
                    xs_ref[pl.ds(dst, MOE_UNIT), :D_MODEL] = y[uu * MOE_UNIT:(uu + 1) * MOE_UNIT]
                return c3
            lax.fori_loop(0, (rows_here + MOE_MT - 1) // MOE_MT, tile, 0)
            return carry
        lax.fori_loop(0, (total + MOE_CHU - 1) // MOE_CHU, chunk, 0)

    @pl.when(s >= nblk + N_EXPERTS)
    def _combine():
        j = s - nblk - N_EXPERTS
        p = pos_ref[j]
        pt = _slot_matrix(p[:, 0:1], p[:, 1:2], 1.0)
        rows = pl.ds(pl.multiple_of(j * MOE_CAP, MOE_CAP), MOE_CAP)
        for c in range(D_MODEL // MOE_NC):
            cs = slice(c * MOE_NC, (c + 1) * MOE_NC)
            o_ref[:, cs] = x_ref[:, cs] + jnp.dot(pt, xs_ref[rows, cs], preferred_element_type=F32)
        if final_norm:
            y = o_ref[...]
            ms = jnp.mean(y * y, axis=-1, keepdims=True)
            o_ref[...] = y * lax.rsqrt(ms + EPS) * gf_ref[...]


def _moe(x2, g_all, w_r, b_r, wg_all, wu_all, wd_all, g_final, layer, tt, final_norm):
    t = x2.shape[0]
    nblk = tt // MOE_BLK
    steps = 2 * nblk + N_EXPERTS

    def tok(b, s):
        j = jnp.where(s < nblk, s, jnp.where(s < nblk + N_EXPERTS, nblk - 1, s - nblk - N_EXPERTS))
        return (b * nblk + j, 0)

    def out_tok(b, s):
        return (b * nblk + jnp.maximum(s - nblk - N_EXPERTS, 0), 0)

    wsel = lambda b, s: (layer * N_EXPERTS + jnp.clip(s - nblk, 0, N_EXPERTS - 1), 0, 0)
    const = lambda b, s: (0, 0)
    return pl.pallas_call(
        functools.partial(_moe_kernel, nblk=nblk, final_norm=final_norm),
        grid=(t // tt, steps),
        in_specs=[
            pl.BlockSpec((MOE_BLK, D_MODEL), tok),
            pl.BlockSpec((1, 1, D_MODEL), lambda b, s: (layer, 0, 0)),
            pl.BlockSpec((LANE, D_MODEL), const),
            pl.BlockSpec((LANE, 1), const),
            pl.BlockSpec((1, D_MODEL, D_EXPERT), wsel),
            pl.BlockSpec((1, D_MODEL, D_EXPERT), wsel),
            pl.BlockSpec((1, D_EXPERT, D_MODEL), wsel),
            pl.BlockSpec((1, D_MODEL), const),
        ],
        out_specs=pl.BlockSpec((MOE_BLK, D_MODEL), out_tok),
        out_shape=jax.ShapeDtypeStruct((t, D_MODEL), F32),
        scratch_shapes=[
            pltpu.VMEM((nblk * MOE_CAP + MOE_UNIT, XS_W), BF16),
            pltpu.VMEM((nblk, MOE_BLK, LANE), F32),
            pltpu.VMEM((nblk, SUBLANE, LANE), jnp.int32),
            pltpu.SMEM((nblk, SUBLANE, LANE), jnp.int32),
            pltpu.VMEM((MOE_CH, XS_W), BF16),
            pltpu.SMEM((MOE_CHU,), jnp.int32),
            pltpu.VMEM((D_MODEL, 2 * D_EXPERT), BF16),
            pltpu.VMEM((D_EXPERT, D_MODEL), BF16),
            pltpu.SemaphoreType.DMA,
        ],
        compiler_params=_cparams(2),
        name="moe",
    )(x2, g_all, w_r, b_r, wg_all, wu_all, wd_all, g_final)


def _t5_bucket(rel):
    nb = N_BUCKETS // 2
    max_exact = nb // 2
    base = jnp.where(rel > 0, nb, 0)
    n = jnp.abs(rel)
    large = max_exact + (jnp.log(jnp.maximum(n, 1).astype(jnp.float32) / max_exact)
                         / math.log(MAX_DISTANCE / max_exact) * (nb - max_exact)).astype(jnp.int32)
    large = jnp.minimum(large, nb - 1)
    return base + jnp.where(n < max_exact, n, large)


def _pad_rank(wr):
    return jnp.pad(wr, ((0, LANE - B_RANK), (0, 0))).astype(BF16)


def kernel(x, w_in, rel_bias, attn_sink, gla_wr_fwd, gla_br_fwd, gla_wr_bwd, gla_br_bwd, gla_norm, w_branch_a, w_branch_b, w_out, norm_mix, norm_ffn, router_group_w, router_group_b, router_expert_w, router_expert_b, expert_w_gate, expert_w_up, expert_w_down, norm_final):
    bsz, seq, d = x.shape
    depth = w_in.shape[0]
    t = bsz * seq
    tt = _moe_tile_tokens(t)
    q_off = jnp.arange(BLOCK)
    k_off = jnp.arange(3 * BLOCK) - BLOCK
    bucket = _t5_bucket(k_off[None, :] - q_off[:, None]).astype(jnp.int32)
    x2 = x.reshape(t, d)
    norm_mix3 = norm_mix[:, None, :]
    w_in_t = jnp.swapaxes(w_in, 1, 2)
    norm_ffn3 = norm_ffn[:, None, :]
    wg_all = expert_w_gate.reshape(depth * N_EXPERTS, d, D_EXPERT)
    wu_all = expert_w_up.reshape(depth * N_EXPERTS, d, D_EXPERT)
    wd_all = expert_w_down.reshape(depth * N_EXPERTS, D_EXPERT, d)
    for l in range(depth):
        proj = _inproj(x2, norm_mix3, w_in_t, l)
        ya = _attention(proj, bucket, rel_bias, attn_sink[l], bsz, seq)
        wrf, wrb = _pad_rank(gla_wr_fwd[l]), _pad_rank(gla_wr_bwd[l])
        brf, brb = gla_br_fwd[l][None, :], gla_br_bwd[l][None, :]
        s_f, s_b = _gla_states(proj, wrf.T, brf.T, wrb.T, brb.T, bsz, seq)
        yb = _gla_out(proj, s_f, s_b, wrf, brf, wrb, brb, gla_norm[l][None, :], bsz, seq)
        x2 = _merge(ya, yb, proj, x2, w_branch_a, w_branch_b, w_out, l)
        w_r = jnp.concatenate(
            [router_group_w[l].T, router_expert_w[l].T,
             jnp.zeros((LANE - N_GROUPS - N_EXPERTS, d), F32)], axis=0)
        b_r = jnp.concatenate(
            [router_group_b[l], router_expert_b[l],
             jnp.zeros((LANE - N_GROUPS - N_EXPERTS,), F32)])[:, None]
        x2 = _moe(x2, norm_ffn3, w_r, b_r, wg_all, wu_all, wd_all, norm_final[None, :], l, tt,
                  final_norm=(l == depth - 1))
    return x2.reshape(bsz, seq, d)
```

```python
import functools
import math

import numpy as np
import jax
import jax.numpy as jnp
from jax import lax
from jax.experimental import pallas as pl
from jax.experimental.pallas import tpu as pltpu

F32 = jnp.float32
BF16 = jnp.bfloat16

D_MODEL = 1024
A_HEADS = 8
A_KV_HEADS = 2
A_HEAD_DIM = 64
A_GROUP = A_HEADS // A_KV_HEADS
WINDOW = 128
BLOCK = 128
N_BUCKETS = 32
MAX_DISTANCE = 128
B_HEADS = 4
B_KEY_DIM = 64
B_VAL_DIM = 128
B_RANK = 16
GATE_TAU = 16.0
CHUNK = 64
N_GROUPS = 4
EXPERTS_PER_GROUP = 8
N_EXPERTS = N_GROUPS * EXPERTS_PER_GROUP
D_EXPERT = 256
EPS = 1e-6
NEG_INF = -1e30

LANE = 128
SUBLANE = 8
VMEM_LIMIT = 56 * 1024 * 1024

QA_W = A_HEADS * A_HEAD_DIM
KA_W = A_KV_HEADS * A_HEAD_DIM
QB_W = B_HEADS * B_KEY_DIM
VB_W = B_HEADS * B_VAL_DIM
COL_GA = 0
COL_GB = COL_GA + D_MODEL
COL_AQ = COL_GB + D_MODEL
COL_BV = COL_AQ + QA_W
COL_BG = COL_BV + VB_W
COL_BQ = COL_BG + VB_W
COL_BK = COL_BQ + QB_W
COL_AK = COL_BK + QB_W
COL_AV = COL_AK + KA_W
COL_R = COL_AV + KA_W
PROJ_W = COL_R + LANE

ROW_TILE = 512
PROJ_CHUNK = 1280
GLA_TILE = 256
CPT = GLA_TILE // CHUNK


def _cparams(n_axes):
    return pltpu.CompilerParams(
        dimension_semantics=("arbitrary",) * n_axes, vmem_limit_bytes=VMEM_LIMIT)


_W_IN_SRC = np.cumsum([0, QA_W, KA_W, KA_W, QB_W, QB_W, VB_W, VB_W, B_RANK, D_MODEL, D_MODEL])
_W_IN_SEGMENTS = tuple(zip(
    (COL_AQ, COL_AK, COL_AV, COL_BQ, COL_BK, COL_BV, COL_BG, COL_R, COL_GA, COL_GB),
    (int(s) for s in _W_IN_SRC[:-1]),
    (int(w) for w in np.diff(_W_IN_SRC))))
W_IN_COLS = int(_W_IN_SRC[-1])
W_ROWS_PER_COPY = 256


def _load_permuted_w_in(wt_ref, wbf_ref):
    wbf_ref[COL_R:COL_R + LANE, :] = jnp.zeros((LANE, D_MODEL), BF16)
    for dst, src, width in _W_IN_SEGMENTS:
        for r0 in range(0, width, W_ROWS_PER_COPY):
            n = min(W_ROWS_PER_COPY, width - r0)
            wbf_ref[dst + r0:dst + r0 + n, :] = wt_ref[0, src + r0:src + r0 + n, :].astype(BF16)


def _inproj_kernel(x_ref, g_ref, wt_ref, o_ref, wbf_ref):
    @pl.when(pl.program_id(0) == 0)
    def _prep():
        _load_permuted_w_in(wt_ref, wbf_ref)

    x = x_ref[...]
    ms = jnp.mean(x * x, axis=-1, keepdims=True)
    xn = (x * lax.rsqrt(ms + EPS) * g_ref[0]).astype(BF16)
    for c0 in range(0, PROJ_W, PROJ_CHUNK):
        sl = slice(c0, min(c0 + PROJ_CHUNK, PROJ_W))
        o_ref[:, sl] = lax.dot_general(xn, wbf_ref[sl, :], (((1,), (1,)), ((), ())),
                                       preferred_element_type=F32).astype(BF16)


def _inproj(x2, g_all, wt_all, layer):
    t = x2.shape[0]
    tm = min(ROW_TILE, t)
    return pl.pallas_call(
        _inproj_kernel,
        grid=(t // tm,),
        in_specs=[
            pl.BlockSpec((tm, D_MODEL), lambda i: (i, 0)),
            pl.BlockSpec((1, 1, D_MODEL), lambda i: (layer, 0, 0)),
            pl.BlockSpec((1, W_IN_COLS, D_MODEL), lambda i: (layer, 0, 0), pipeline_mode=pl.Buffered(1)),
        ],
        out_specs=pl.BlockSpec((tm, PROJ_W), lambda i: (i, 0)),
        out_shape=jax.ShapeDtypeStruct((t, PROJ_W), BF16),
        scratch_shapes=[pltpu.VMEM((PROJ_W, D_MODEL), BF16)],
        compiler_params=_cparams(1),
        name="inproj",
    )(x2, g_all, wt_all)


ATT_QB = 2


def _attn_kernel(bucket_ref, relb_ref, sink_ref, q_ref, kp_ref, kc_ref, kn_ref,
                 vp_ref, vc_ref, vn_ref, o_ref, bias_ref, logit_ref, prob_ref, *, nblk):
    b = pl.program_id(0)
    n = pl.program_id(1)
    kw = 3 * BLOCK

    @pl.when((b == 0) & (n == 0))
    def _build_bias():
        bucket = bucket_ref[...]
        row = lax.broadcasted_iota(jnp.int32, (BLOCK, kw), 0)
        col = lax.broadcasted_iota(jnp.int32, (BLOCK, kw), 1)
        win = jnp.abs(col - BLOCK - row) <= WINDOW
        keep = (win & (col >= BLOCK), win, win & (col < 2 * BLOCK))
        for hq in range(A_HEADS):
            acc = jnp.zeros((BLOCK, kw), F32)
            for bb in range(N_BUCKETS):
                acc = jnp.where(bucket == bb, relb_ref[bb, hq], acc)
            for v in range(3):
                bias_ref[v, hq] = jnp.where(keep[v], acc, NEG_INF)

    q = (q_ref[...].astype(F32) * (A_HEAD_DIM ** -0.5)).astype(BF16)
    kall = jnp.concatenate([kp_ref[...], kc_ref[...], kn_ref[...]], axis=0)
    vall = jnp.concatenate([vp_ref[...], vc_ref[...], vn_ref[...]], axis=0)
    jobs = [(j, hq) for j in range(ATT_QB) for hq in range(A_HEADS)]

    def keys(x, j, hq):
        h = hq // A_GROUP
        return x[j * BLOCK:(j + 3) * BLOCK, h * A_HEAD_DIM:(h + 1) * A_HEAD_DIM]

    for i, (j, hq) in enumerate(jobs):
        logit_ref[i] = lax.dot_general(
            q[j * BLOCK:(j + 1) * BLOCK, hq * A_HEAD_DIM:(hq + 1) * A_HEAD_DIM], keys(kall, j, hq),
            (((1,), (1,)), ((), ())), preferred_element_type=F32)
    dens = []
    for i, (j, hq) in enumerate(jobs):
        blk = n * ATT_QB + j
        variant = jnp.where(blk == 0, 0, jnp.where(blk == nblk - 1, 2, 1))
        s = logit_ref[i] + bias_ref[variant, hq]
        sk = sink_ref[hq]
        m = jnp.maximum(jnp.max(s, axis=-1, keepdims=True), sk)
        p = jnp.exp(s - m)
        dens.append(jnp.sum(p, axis=-1, keepdims=True) + jnp.exp(sk - m))
        prob_ref[i] = p.astype(BF16)
    outs = [jnp.dot(prob_ref[i], keys(vall, j, hq), preferred_element_type=F32) / dens[i]
            for i, (j, hq) in enumerate(jobs)]
    for j in range(ATT_QB):
        o_ref[j * BLOCK:(j + 1) * BLOCK, :] = jnp.concatenate(
            outs[j * A_HEADS:(j + 1) * A_HEADS], axis=1).astype(BF16)


def _attention(proj, bucket, rel_bias, sink, bsz, seq):
    nblk = seq // BLOCK
    t = bsz * seq
    cq = COL_AQ // QA_W
    ck = COL_AK // KA_W
    cv = COL_AV // KA_W

    nsteps = nblk // ATT_QB
    rows = ATT_QB * BLOCK

    def kv_specs(c):
        return [pl.BlockSpec((BLOCK, KA_W), lambda b, n: (b * nblk + jnp.maximum(n * ATT_QB - 1, 0), c)),
                pl.BlockSpec((rows, KA_W), lambda b, n: (b * nsteps + n, c)),
                pl.BlockSpec((BLOCK, KA_W),
                             lambda b, n: (b * nblk + jnp.minimum((n + 1) * ATT_QB, nblk - 1), c))]

    smem = pl.BlockSpec(memory_space=pltpu.SMEM)
    return pl.pallas_call(
        functools.partial(_attn_kernel, nblk=nblk),
        grid=(bsz, nsteps),
        in_specs=[
            pl.BlockSpec((BLOCK, 3 * BLOCK), lambda b, n: (0, 0)),
            smem, smem,
            pl.BlockSpec((rows, QA_W), lambda b, n: (b * nsteps + n, cq)),
        ] + kv_specs(ck) + kv_specs(cv),
        out_specs=pl.BlockSpec((rows, QA_W), lambda b, n: (b * nsteps + n, 0)),
        out_shape=jax.ShapeDtypeStruct((t, QA_W), BF16),
        scratch_shapes=[pltpu.VMEM((3, A_HEADS, BLOCK, 3 * BLOCK), F32),
                        pltpu.VMEM((ATT_QB * A_HEADS, BLOCK, 3 * BLOCK), F32),
                        pltpu.VMEM((ATT_QB * A_HEADS, BLOCK, 3 * BLOCK), BF16)],
        compiler_params=_cparams(2),
        name="attn",
    )(bucket, rel_bias, sink, proj, proj, proj, proj, proj, proj, proj)


def _log_sigmoid(z):
    return jnp.minimum(z, 0.0) - jnp.log(1.0 + jnp.exp(-jnp.abs(z)))


def _log_decay(r, wr, br):
    z = jnp.dot(r, wr, preferred_element_type=F32) + br
    return _log_sigmoid(z) / GATE_TAU


def _chunk_masks():
    row = lax.broadcasted_iota(jnp.int32, (GLA_TILE, GLA_TILE), 0)
    col = lax.broadcasted_iota(jnp.int32, (GLA_TILE, GLA_TILE), 1)
    sh = int(math.log2(CHUNK))
    same = jnp.right_shift(row, sh) == jnp.right_shift(col, sh)
    return same, same & (col <= row), same & (col >= row)


def _f32dot(a, b):
    return jnp.dot(a, b, preferred_element_type=F32, precision=lax.Precision.HIGHEST)


def _split3(x):
    hi = x.astype(BF16)
    r1 = x - hi.astype(F32)
    mid = r1.astype(BF16)
    lo = (r1 - mid.astype(F32)).astype(BF16)
    return hi, mid, lo


def _mask_dot(mask, x):
    m = mask.astype(BF16)
    hi, mid, lo = _split3(x)
    return (jnp.dot(m, hi, preferred_element_type=F32) + jnp.dot(m, mid, preferred_element_type=F32)
            + jnp.dot(m, lo, preferred_element_type=F32))


def _gla_state_dir(k_ref, v_ref, r_ref, wrt_ref, brc_ref, s_out_ref, state_ref, order, cum_mask, same):
    z = lax.dot_general(wrt_ref[...], r_ref[...], (((1,), (1,)), ((), ())),
                        preferred_element_type=F32) + brc_ref[:, 0:1]
    hi, mid, lo = _split3(_log_sigmoid(z) / GATE_TAU)
    m = jnp.concatenate([cum_mask, same], axis=1).astype(BF16)
    both = (jnp.dot(hi, m, preferred_element_type=F32) + jnp.dot(mid, m, preferred_element_type=F32)
            + jnp.dot(lo, m, preferred_element_type=F32))
    cum = both[:, :GLA_TILE]
    tot = both[:, GLA_TILE:]
    k_end = (k_ref[...].astype(F32).T * jnp.exp(tot - cum)).astype(BF16)
    lane_chunk = jnp.right_shift(lax.broadcasted_iota(jnp.int32, (1, GLA_TILE), 1),
                                 int(math.log2(CHUNK)))
    upd = []
    for h in range(B_HEADS):
        kh = k_end[h * B_KEY_DIM:(h + 1) * B_KEY_DIM]
        lhs = jnp.concatenate([jnp.where(lane_chunk == c, kh, 0.0).astype(BF16) for c in range(CPT)],
                              axis=0)
        upd.append(jnp.dot(lhs, v_ref[:, h * B_VAL_DIM:(h + 1) * B_VAL_DIM],
                           preferred_element_type=F32))
    for c in order:
        s_out_ref[0, c] = state_ref[...].astype(BF16)
        upd_c = jnp.concatenate([u[c * B_KEY_DIM:(c + 1) * B_KEY_DIM] for u in upd], axis=0)
        decay = jnp.exp(tot[:, c * CHUNK:c * CHUNK + 1])
        state_ref[...] = decay * state_ref[...] + upd_c


def _gla_state_kernel(kf_ref, vf_ref, rf_ref, kb_ref, vb_ref, rb_ref,
                      wrf_ref, brf_ref, wrb_ref, brb_ref,
                      sf_ref, sb_ref, stf_ref, stb_ref):
    @pl.when(pl.program_id(1) == 0)
    def _reset():
        stf_ref[...] = jnp.zeros_like(stf_ref)
        stb_ref[...] = jnp.zeros_like(stb_ref)

    same, lower, upper = _chunk_masks()
    _gla_state_dir(kf_ref, vf_ref, rf_ref, wrf_ref, brf_ref, sf_ref, stf_ref,
                   range(CPT), upper, same)
    _gla_state_dir(kb_ref, vb_ref, rb_ref, wrb_ref, brb_ref, sb_ref, stb_ref,
                   range(CPT - 1, -1, -1), lower, same)


def _gla_states(proj, wrf_t, brf_c, wrb_t, brb_c, bsz, seq):
    nt = seq // GLA_TILE
    nchunks = seq // CHUNK
    hk = B_HEADS * B_KEY_DIM
    ck, cv, cr = COL_BK // QB_W, COL_BV // VB_W, COL_R // LANE

    def fwd(b, i):
        return b * nt + i

    def bwd(b, i):
        return b * nt + (nt - 1 - i)

    def tile_specs(f):
        return [pl.BlockSpec((GLA_TILE, QB_W), lambda b, i: (f(b, i), ck)),
                pl.BlockSpec((GLA_TILE, VB_W), lambda b, i: (f(b, i), cv)),
                pl.BlockSpec((GLA_TILE, LANE), lambda b, i: (f(b, i), cr))]

    const = lambda b, i: (0, 0)
    out_sds = jax.ShapeDtypeStruct((bsz, nchunks, hk, B_VAL_DIM), BF16)
    return pl.pallas_call(
        _gla_state_kernel,
        grid=(bsz, nt),
        in_specs=tile_specs(fwd) + tile_specs(bwd) + [
            pl.BlockSpec((hk, LANE), const), pl.BlockSpec((hk, 1), const),
            pl.BlockSpec((hk, LANE), const), pl.BlockSpec((hk, 1), const)],
        out_specs=[pl.BlockSpec((1, CPT, hk, B_VAL_DIM), lambda b, i: (b, i, 0, 0)),
                   pl.BlockSpec((1, CPT, hk, B_VAL_DIM), lambda b, i: (b, nt - 1 - i, 0, 0))],
        out_shape=[out_sds, out_sds],
        scratch_shapes=[pltpu.VMEM((hk, B_VAL_DIM), F32), pltpu.VMEM((hk, B_VAL_DIM), F32)],
        compiler_params=_cparams(2),
        name="gla_state",
    )(proj, proj, proj, proj, proj, proj, wrf_t, brf_c, wrb_t, brb_c)


def _gla_out_kernel(q_ref, k_ref, v_ref, r_ref, g_ref, sf_ref, sb_ref,
                    wrf_ref, brf_ref, wrb_ref, brb_ref, ng_ref, o_ref):
    same, lower, upper = _chunk_masks()
    r = r_ref[...]
    cum_f = _mask_dot(lower, _log_decay(r, wrf_ref[...], brf_ref[...]))
    cum_b = _mask_dot(upper, _log_decay(r, wrb_ref[...], brb_ref[...]))
    q = q_ref[...].astype(F32) * (B_KEY_DIM ** -0.5)
    k = k_ref[...].astype(F32)
    qd_f = q * jnp.exp(cum_f)
    qd_b = q * jnp.exp(cum_b)
    ki_f = (k * jnp.exp(-cum_f)).astype(BF16)
    ki_b = (k * jnp.exp(-cum_b)).astype(BF16)
    lane_head = jnp.right_shift(
        lax.broadcasted_iota(jnp.int32, (1, B_HEADS * B_KEY_DIM), 1), int(math.log2(B_KEY_DIM)))

    def stack_heads(x):
        return jnp.concatenate(
            [jnp.where(lane_head == h, x, 0.0) for h in range(B_HEADS)], axis=0).astype(BF16)

    qs_f = stack_heads(qd_f)
    qs_b = stack_heads(qd_b)
    nt_dims = (((1,), (1,)), ((), ()))
    sc_f = lax.dot_general(qs_f, ki_f, nt_dims, preferred_element_type=F32)
    sc_b = lax.dot_general(qs_b, ki_b, nt_dims, preferred_element_type=F32)
    lower_s = jnp.concatenate([lower] * B_HEADS, axis=0)
    strict_upper_s = jnp.concatenate([upper & ~lower] * B_HEADS, axis=0)
    p = jnp.where(lower_s, sc_f, jnp.where(strict_upper_s, sc_b, 0.0)).astype(BF16)
    v = v_ref[...]
    inter = []
    for c in range(CPT):
        lhs_f = jnp.concatenate(
            [qs_f[h * GLA_TILE + c * CHUNK:h * GLA_TILE + (c + 1) * CHUNK] for h in range(B_HEADS)], axis=0)
        lhs_b = jnp.concatenate(
            [qs_b[h * GLA_TILE + c * CHUNK:h * GLA_TILE + (c + 1) * CHUNK] for h in range(B_HEADS)], axis=0)
        inter.append(jnp.dot(lhs_f, sf_ref[0, c], preferred_element_type=F32)
                     + jnp.dot(lhs_b, sb_ref[0, c], preferred_element_type=F32))
    ng = ng_ref[...]
    g = g_ref[...].astype(F32)
    for h in range(B_HEADS):
        vs = slice(h * B_VAL_DIM, (h + 1) * B_VAL_DIM)
        o = jnp.dot(p[h * GLA_TILE:(h + 1) * GLA_TILE], v[:, vs], preferred_element_type=F32)
        o = o + jnp.concatenate([inter[c][h * CHUNK:(h + 1) * CHUNK] for c in range(CPT)], axis=0)
        o = o * lax.rsqrt(jnp.mean(o * o, axis=-1, keepdims=True) + EPS) * ng
        gh = g[:, vs]
        o_ref[:, vs] = (o * (gh / (1.0 + jnp.exp(-gh)))).astype(BF16)


def _gla_out(proj, s_f, s_b, wrf, brf, wrb, brb, ng, bsz, seq):
    nt = seq // GLA_TILE
    t = bsz * seq
    hk = B_HEADS * B_KEY_DIM
    const = lambda b, i: (0, 0)
    row = lambda c: (lambda b, i: (b * nt + i, c))
    return pl.pallas_call(
        _gla_out_kernel,
        grid=(bsz, nt),
        in_specs=[
            pl.BlockSpec((GLA_TILE, QB_W), row(COL_BQ // QB_W)),
            pl.BlockSpec((GLA_TILE, QB_W), row(COL_BK // QB_W)),
            pl.BlockSpec((GLA_TILE, VB_W), row(COL_BV // VB_W)),
            pl.BlockSpec((GLA_TILE, LANE), row(COL_R // LANE)),
            pl.BlockSpec((GLA_TILE, VB_W), row(COL_BG // VB_W)),
            pl.BlockSpec((1, CPT, hk, B_VAL_DIM), lambda b, i: (b, i, 0, 0)),
            pl.BlockSpec((1, CPT, hk, B_VAL_DIM), lambda b, i: (b, i, 0, 0)),
            pl.BlockSpec((LANE, hk), const), pl.BlockSpec((1, hk), const),
            pl.BlockSpec((LANE, hk), const), pl.BlockSpec((1, hk), const),
            pl.BlockSpec((1, B_VAL_DIM), const),
        ],
        out_specs=pl.BlockSpec((GLA_TILE, VB_W), lambda b, i: (b * nt + i, 0)),
        out_shape=jax.ShapeDtypeStruct((t, VB_W), BF16),
        compiler_params=_cparams(2),
        name="gla_out",
    )(proj, proj, proj, proj, proj, s_f, s_b, wrf, brf, wrb, brb, ng)


def _sigmoid(x):
    return 1.0 / (1.0 + jnp.exp(-x))


def _merge_kernel(ya_ref, yb_ref, ga_ref, gb_ref, x_ref, wa_ref, wb_ref, wo_ref, o_ref,
                  wa_bf, wb_bf, wo_bf):
    @pl.when(pl.program_id(0) == 0)
    def _prep():
        wa_bf[...] = wa_ref[0].astype(BF16)
        wb_bf[...] = wb_ref[0].astype(BF16)
        wo_bf[...] = wo_ref[0].astype(BF16)

    a = jnp.dot(ya_ref[...], wa_bf[...], preferred_element_type=F32)
    b = jnp.dot(yb_ref[...], wb_bf[...], preferred_element_type=F32)
    merged = _sigmoid(ga_ref[...].astype(F32)) * a + _sigmoid(gb_ref[...].astype(F32)) * b
    o_ref[...] = x_ref[...] + jnp.dot(merged.astype(BF16), wo_bf[...], preferred_element_type=F32)


def _merge(ya, yb, proj, x2, wa_all, wb_all, wo_all, layer):
    t = x2.shape[0]
    tm = min(ROW_TILE, t)
    wsel = lambda i: (layer, 0, 0)
    return pl.pallas_call(
        _merge_kernel,
        grid=(t // tm,),
        in_specs=[
            pl.BlockSpec((tm, QA_W), lambda i: (i, 0)),
            pl.BlockSpec((tm, VB_W), lambda i: (i, 0)),
            pl.BlockSpec((tm, D_MODEL), lambda i: (i, COL_GA // D_MODEL)),
            pl.BlockSpec((tm, D_MODEL), lambda i: (i, COL_GB // D_MODEL)),
            pl.BlockSpec((tm, D_MODEL), lambda i: (i, 0)),
            pl.BlockSpec((1, QA_W, D_MODEL), wsel, pipeline_mode=pl.Buffered(1)),
            pl.BlockSpec((1, VB_W, D_MODEL), wsel, pipeline_mode=pl.Buffered(1)),
            pl.BlockSpec((1, D_MODEL, D_MODEL), wsel, pipeline_mode=pl.Buffered(1)),
        ],
        out_specs=pl.BlockSpec((tm, D_MODEL), lambda i: (i, 0)),
        out_shape=jax.ShapeDtypeStruct((t, D_MODEL), F32),
        scratch_shapes=[pltpu.VMEM((QA_W, D_MODEL), BF16), pltpu.VMEM((VB_W, D_MODEL), BF16),
                        pltpu.VMEM((D_MODEL, D_MODEL), BF16)],
        compiler_params=_cparams(1),
        name="merge",
    )(ya, yb, proj, proj, x2, wa_all, wb_all, wo_all)


def _moe_tile_tokens(t):
    return min(4096, t)


MOE_BLK = 512
MOE_UNIT = 16
MOE_CAP = 1536
MOE_CAP_COMMON = 1280
XS_W = D_MODEL + LANE
MOE_MT = 384
MOE_CH = 2 * MOE_MT
MOE_CHU = MOE_CH // MOE_UNIT
MOE_NC = 256
assert MOE_CAP >= 2 * MOE_BLK + N_EXPERTS * (MOE_UNIT - 1) and MOE_CAP % MOE_UNIT == 0


NT_DIMS = (((1,), (1,)), ((), ()))


def _route(hn_bf, wrt_ref, brc_ref):
    logits = lax.dot_general(wrt_ref[...].astype(BF16), hn_bf, NT_DIMS,
                             preferred_element_type=F32) + brc_ref[:, 0:1]
    row = lax.broadcasted_iota(jnp.int32, logits.shape, 0).astype(F32)
    big = float(LANE)
    ninf = -jnp.inf
    gl = jnp.where(row < N_GROUPS, logits, ninf)
    gmax = jnp.max(gl, axis=0, keepdims=True)
    g_idx = jnp.min(jnp.where(gl == gmax, row, big), axis=0, keepdims=True)
    g_w = 1.0 / jnp.sum(jnp.exp(gl - gmax), axis=0, keepdims=True)
    lo = float(N_GROUPS) + g_idx * float(EXPERTS_PER_GROUP)
    el = jnp.where((row >= lo) & (row < lo + EXPERTS_PER_GROUP), logits, ninf)
    v1 = jnp.max(el, axis=0, keepdims=True)
    i1 = jnp.min(jnp.where(el == v1, row, big), axis=0, keepdims=True)
    el2 = jnp.where(row == i1, ninf, el)
    v2 = jnp.max(el2, axis=0, keepdims=True)
    i2 = jnp.min(jnp.where(el2 == v2, row, big), axis=0, keepdims=True)
    e21 = jnp.exp(v2 - v1)
    w1 = g_w / (1.0 + e21)
    w2 = g_w * e21 / (1.0 + e21)
    return i1 - float(N_GROUPS), i2 - float(N_GROUPS), w1, w2, row


def _slot_matrix(pos1, pos2, second):
    slot = lax.broadcasted_iota(jnp.int32, (pos1.shape[0], MOE_CAP), 1)
    return jnp.where(slot == pos1.astype(jnp.int32), 1.0,
                     jnp.where(slot == pos2.astype(jnp.int32), second, 0.0)).astype(BF16)


def _slot_matrix_t(pos1, pos2, second):
    slot = lax.broadcasted_iota(jnp.int32, (MOE_CAP, pos1.shape[1]), 0)
    return jnp.where(slot == pos1.astype(jnp.int32), 1.0,
                     jnp.where(slot == pos2.astype(jnp.int32), second, 0.0)).astype(BF16)


def _split3_f32(w):
    hi = w.astype(BF16).astype(F32)
    r1 = w - hi
    mid = r1.astype(BF16).astype(F32)
    return hi, mid, r1 - mid


def _rows_to_sublanes(vals, row):
    out = jnp.zeros(row.shape, F32)
    for k, v in enumerate(vals):
        out = jnp.where(row == float(k), v, out)
    return out


def _moe_kernel(x_ref, g_ref, wrt_ref, brc_ref, wg_ref, wu_ref, wd_ref, gf_ref, o_ref,
                xs_ref, pos_ref, tab_v, tab_s, xt_ref, addr_ref, wgu_bf, wd_bf, sem,
                *, nblk, final_norm):
    s = pl.program_id(1)
    dump_row = nblk * MOE_CAP

    @pl.when(s < nblk)
    def _dispatch():
        @pl.when(s == 0)
        def _init():
            xs_ref[dump_row:dump_row + MOE_UNIT, :] = jnp.zeros((MOE_UNIT, XS_W), BF16)
            xt_ref[...] = jnp.zeros_like(xt_ref)

        x = x_ref[...]
        ms = jnp.mean(x * x, axis=-1, keepdims=True)
        hn = (x * lax.rsqrt(ms + EPS) * g_ref[0]).astype(BF16)
        e1, e2, w1, w2, row = _route(hn, wrt_ref, brc_ref)
        onehot = ((row == e1) | (row == e2)).astype(F32)
        earlier = (lax.broadcasted_iota(jnp.int32, (MOE_BLK, MOE_BLK), 0)
                   < lax.broadcasted_iota(jnp.int32, (MOE_BLK, MOE_BLK), 1)).astype(BF16)
        before = jnp.dot(onehot.astype(BF16), earlier, preferred_element_type=F32)
        counts = jnp.sum(onehot, axis=1, keepdims=True)
        units = jnp.floor((counts + (MOE_UNIT - 1.0)) * (1.0 / MOE_UNIT))
        lower = (lax.broadcasted_iota(jnp.int32, (LANE, LANE), 1)
                 < lax.broadcasted_iota(jnp.int32, (LANE, LANE), 0)).astype(BF16)
        padded = jnp.broadcast_to(units * MOE_UNIT, (LANE, LANE))
        run_off = jnp.dot(lower, padded.astype(BF16), preferred_element_type=F32)[:, 0:1]
        start = run_off + before
        pos1 = jnp.sum(jnp.where(row == e1, start, 0.0), axis=0, keepdims=True)
        pos2 = jnp.sum(jnp.where(row == e2, start, 0.0), axis=0, keepdims=True)
        w_rows = _rows_to_sublanes(
            _split3_f32(w1) + _split3_f32(w2) + (jnp.ones_like(w1),), row).astype(BF16)
        used = jnp.sum(units, axis=0, keepdims=True) * MOE_UNIT
        pt_t = _slot_matrix_t(pos1, pos2, 2.0)
        rows = pl.ds(pl.multiple_of(s * MOE_CAP, MOE_CAP), MOE_CAP)
        for c in range(D_MODEL // MOE_NC):
            cs = slice(c * MOE_NC, (c + 1) * MOE_NC)
            xs_ref[rows, cs] = jnp.dot(pt_t, hn[:, cs], preferred_element_type=F32).astype(BF16)
        xs_ref[rows, D_MODEL:] = lax.dot_general(pt_t, w_rows, NT_DIMS,
                                                 preferred_element_type=F32).astype(BF16)
        pos_ref[s] = _rows_to_sublanes((pos1, pos2), row).T
        lane_e = lax.broadcasted_iota(jnp.int32, (LANE, LANE), 1)
        per_expert = jnp.where(lane_e == 0, run_off,
                               jnp.where(lane_e == 1, units, jnp.where(lane_e == 2, used, 0.0)))
        tab_v[s] = per_expert.T[0:SUBLANE, :].astype(jnp.int32)

        @pl.when(s == nblk - 1)
        def _publish():
            cp = pltpu.make_async_copy(tab_v, tab_s, sem)
            cp.start()
            cp.wait()

    @pl.when((s >= nblk) & (s < nblk + N_EXPERTS))
    def _experts():
        e = s - nblk
        wgu_bf[:, :D_EXPERT] = wg_ref[0].astype(BF16)
        wgu_bf[:, D_EXPERT:] = wu_ref[0].astype(BF16)
        wd_bf[...] = wd_ref[0].astype(BF16)
        total = tab_s[0, 1, e]
        for j in range(1, nblk):
            total = total + tab_s[j, 1, e]

        def chunk(c, carry):
            u0 = c * MOE_CHU
            for k in range(MOE_CHU):
                addr_ref[k] = dump_row
            cum = 0
            for j in range(nblk):
                nj = tab_s[j, 1, e]
                run0 = j * MOE_CAP + tab_s[j, 0, e]
                k_off = cum - u0

                def copy_unit(u, c2, run0=run0, k_off=k_off):
                    src = pl.multiple_of(run0 + u * MOE_UNIT, MOE_UNIT)
                    dst = pl.multiple_of((k_off + u) * MOE_UNIT, MOE_UNIT)
                    xt_ref[pl.ds(dst, MOE_UNIT), :] = xs_ref[pl.ds(src, MOE_UNIT), :]
                    addr_ref[k_off + u] = src
                    return c2
                lax.fori_loop(jnp.clip(u0 - cum, 0, nj), jnp.clip(u0 + MOE_CHU - cum, 0, nj),
                              copy_unit, 0)
                cum = cum + nj
            rows_here = jnp.minimum(MOE_CHU, total - u0) * MOE_UNIT

            def tile(t, c3):
                xt = xt_ref[pl.ds(pl.multiple_of(t * MOE_MT, MOE_MT), MOE_MT), :]
                wl = xt[:, D_MODEL:].astype(F32)
                second = wl[:, 6:7] > 1.5
                wcol = jnp.where(second, 0.5 * (wl[:, 3:4] + wl[:, 4:5] + wl[:, 5:6]),
                                 wl[:, 0:1] + wl[:, 1:2] + wl[:, 2:3])
                gu = (jnp.dot(xt[:, :D_MODEL], wgu_bf[...], preferred_element_type=F32)
                      * jnp.where(second, 0.5, 1.0))
                gate = gu[:, :D_EXPERT]
                h = gate * _sigmoid(gate) * gu[:, D_EXPERT:] * wcol
                y = jnp.dot(h.astype(BF16), wd_bf[...], preferred_element_type=F32).astype(BF16)
                for uu in range(MOE_MT // MOE_UNIT):
                    dst = pl.multiple_of(addr_ref[t * (MOE_MT // MOE_UNIT) + uu], MOE_UNIT)
                    xs_ref[pl.ds(dst, MOE_UNIT), :D_MODEL] = y[uu * MOE_UNIT:(uu + 1) * MOE_UNIT]
                return c3
            lax.fori_loop(0, (rows_here + MOE_MT - 1) // MOE_MT, tile, 0)
            return carry
        lax.fori_loop(0, (total + MOE_CHU - 1) // MOE_CHU, chunk, 0)

    @pl.when(s >= nblk + N_EXPERTS)
    def _combine():
        j = s - nblk - N_EXPERTS
        p = pos_ref[j]
        pt = _slot_matrix(p[:, 0:1], p[:, 1:2], 1.0)
        base = pl.multiple_of(j * MOE_CAP, MOE_CAP)
        head = pl.ds(base, MOE_CAP_COMMON)
        for c in range(D_MODEL // MOE_NC):
            cs = slice(c * MOE_NC, (c + 1) * MOE_NC)
            o_ref[:, cs] = x_ref[:, cs] + jnp.dot(pt[:, :MOE_CAP_COMMON], xs_ref[head, cs],
                                                  preferred_element_type=F32)

        @pl.when(tab_s[j, 2, 0] > MOE_CAP_COMMON)
        def _tail():
            tail = pl.ds(base + MOE_CAP_COMMON, MOE_CAP - MOE_CAP_COMMON)
            for c in range(D_MODEL // MOE_NC):
                cs = slice(c * MOE_NC, (c + 1) * MOE_NC)
                o_ref[:, cs] += jnp.dot(pt[:, MOE_CAP_COMMON:], xs_ref[tail, cs],
                                        preferred_element_type=F32)
        if final_norm:
            y = o_ref[...]
            ms = jnp.mean(y * y, axis=-1, keepdims=True)
            o_ref[...] = y * lax.rsqrt(ms + EPS) * gf_ref[...]


def _moe(x2, g_all, w_r, b_r, wg_all, wu_all, wd_all, g_final, layer, tt, final_norm):
    t = x2.shape[0]
    nblk = tt // MOE_BLK
    steps = 2 * nblk + N_EXPERTS

    def tok(b, s):
        j = jnp.where(s < nblk, s, jnp.where(s < nblk + N_EXPERTS, nblk - 1, s - nblk - N_EXPERTS))
        return (b * nblk + j, 0)

    def out_tok(b, s):
        return (b * nblk + jnp.maximum(s - nblk - N_EXPERTS, 0), 0)

    wsel = lambda b, s: (layer * N_EXPERTS + jnp.clip(s - nblk, 0, N_EXPERTS - 1), 0, 0)
    const = lambda b, s: (0, 0)
    return pl.pallas_call(
        functools.partial(_moe_kernel, nblk=nblk, final_norm=final_norm),
        grid=(t // tt, steps),
        in_specs=[
            pl.BlockSpec((MOE_BLK, D_MODEL), tok),
            pl.BlockSpec((1, 1, D_MODEL), lambda b, s: (layer, 0, 0)),
            pl.BlockSpec((LANE, D_MODEL), const),
            pl.BlockSpec((LANE, 1), const),
            pl.BlockSpec((1, D_MODEL, D_EXPERT), wsel),
            pl.BlockSpec((1, D_MODEL, D_EXPERT), wsel),
            pl.BlockSpec((1, D_EXPERT, D_MODEL), wsel),
            pl.BlockSpec((1, D_MODEL), const),
        ],
        out_specs=pl.BlockSpec((MOE_BLK, D_MODEL), out_tok),
        out_shape=jax.ShapeDtypeStruct((t, D_MODEL), F32),
        scratch_shapes=[
            pltpu.VMEM((nblk * MOE_CAP + MOE_UNIT, XS_W), BF16),
            pltpu.VMEM((nblk, MOE_BLK, LANE), F32),
            pltpu.VMEM((nblk, SUBLANE, LANE), jnp.int32),
            pltpu.SMEM((nblk, SUBLANE, LANE), jnp.int32),
            pltpu.VMEM((MOE_CH, XS_W), BF16),
            pltpu.SMEM((MOE_CHU,), jnp.int32),
            pltpu.VMEM((D_MODEL, 2 * D_EXPERT), BF16),
            pltpu.VMEM((D_EXPERT, D_MODEL), BF16),
            pltpu.SemaphoreType.DMA,
        ],
        compiler_params=_cparams(2),
        name="moe",
    )(x2, g_all, w_r, b_r, wg_all, wu_all, wd_all, g_final)


def _t5_bucket(rel):
    nb = N_BUCKETS // 2
    max_exact = nb // 2
    base = jnp.where(rel > 0, nb, 0)
    n = jnp.abs(rel)
    large = max_exact + (jnp.log(jnp.maximum(n, 1).astype(jnp.float32) / max_exact)
                         / math.log(MAX_DISTANCE / max_exact) * (nb - max_exact)).astype(jnp.int32)
    large = jnp.minimum(large, nb - 1)
    return base + jnp.where(n < max_exact, n, large)


def _pad_rank(wr):
    return jnp.pad(wr, ((0, LANE - B_RANK), (0, 0))).astype(BF16)


def kernel(x, w_in, rel_bias, attn_sink, gla_wr_fwd, gla_br_fwd, gla_wr_bwd, gla_br_bwd, gla_norm, w_branch_a, w_branch_b, w_out, norm_mix, norm_ffn, router_group_w, router_group_b, router_expert_w, router_expert_b, expert_w_gate, expert_w_up, expert_w_down, norm_final):
    bsz, seq, d = x.shape
    depth = w_in.shape[0]
    t = bsz * seq
    tt = _moe_tile_tokens(t)
    q_off = jnp.arange(BLOCK)
    k_off = jnp.arange(3 * BLOCK) - BLOCK
    bucket = _t5_bucket(k_off[None, :] - q_off[:, None]).astype(jnp.int32) & (N_BUCKETS - 1)
    x2 = x.reshape(t, d)
    norm_mix3 = norm_mix[:, None, :]
    w_in_t = jnp.swapaxes(w_in, 1, 2)
    norm_ffn3 = norm_ffn[:, None, :]
    wg_all = expert_w_gate.reshape(depth * N_EXPERTS, d, D_EXPERT)
    wu_all = expert_w_up.reshape(depth * N_EXPERTS, d, D_EXPERT)
    wd_all = expert_w_down.reshape(depth * N_EXPERTS, D_EXPERT, d)
    for l in range(depth):
        proj = _inproj(x2, norm_mix3, w_in_t, l)
        ya = _attention(proj, bucket, rel_bias, attn_sink[l], bsz, seq)
        wrf, wrb = _pad_rank(gla_wr_fwd[l]), _pad_rank(gla_wr_bwd[l])
        brf, brb = gla_br_fwd[l][None, :], gla_br_bwd[l][None, :]
        s_f, s_b = _gla_states(proj, wrf.T, brf.T, wrb.T, brb.T, bsz, seq)
        yb = _gla_out(proj, s_f, s_b, wrf, brf, wrb, brb, gla_norm[l][None, :], bsz, seq)
        x2 = _merge(ya, yb, proj, x2, w_branch_a, w_branch_b, w_out, l)
        w_r = jnp.concatenate(
            [router_group_w[l].T, router_expert_w[l].T,
             jnp.zeros((LANE - N_GROUPS - N_EXPERTS, d), F32)], axis=0)
        b_r = jnp.concatenate(
            [router_group_b[l], router_expert_b[l],
             jnp.zeros((LANE - N_GROUPS - N_EXPERTS,), F32)])[:, None]
        x2 = _moe(x2, norm_ffn3, w_r, b_r, wg_all, wu_all, wd_all, norm_final[None, :], l, tt,
                  final_norm=(l == depth - 1))
    return x2.reshape(bsz, seq, d)
```

```python
import functools
import math

import numpy as np
import jax
import jax.numpy as jnp
from jax import lax
from jax.experimental import pallas as pl
from jax.experimental.pallas import tpu as pltpu

F32 = jnp.float32
BF16 = jnp.bfloat16

D_MODEL = 1024
A_HEADS = 8
A_KV_HEADS = 2
A_HEAD_DIM = 64
A_GROUP = A_HEADS // A_KV_HEADS
WINDOW = 128
BLOCK = 128
N_BUCKETS = 32
MAX_DISTANCE = 128
B_HEADS = 4
B_KEY_DIM = 64
B_VAL_DIM = 128
B_RANK = 16
GATE_TAU = 16.0
CHUNK = 64
N_GROUPS = 4
EXPERTS_PER_GROUP = 8
N_EXPERTS = N_GROUPS * EXPERTS_PER_GROUP
D_EXPERT = 256
EPS = 1e-6
NEG_INF = -1e30

LANE = 128
SUBLANE = 8
VMEM_LIMIT = 56 * 1024 * 1024

QA_W = A_HEADS * A_HEAD_DIM
KA_W = A_KV_HEADS * A_HEAD_DIM
QB_W = B_HEADS * B_KEY_DIM
VB_W = B_HEADS * B_VAL_DIM
COL_GA = 0
COL_GB = COL_GA + D_MODEL
COL_AQ = COL_GB + D_MODEL
COL_BV = COL_AQ + QA_W
COL_BG = COL_BV + VB_W
COL_BQ = COL_BG + VB_W
COL_BK = COL_BQ + QB_W
COL_AK = COL_BK + QB_W
COL_AV = COL_AK + KA_W
COL_R = COL_AV + KA_W
PROJ_W = COL_R + LANE

ROW_TILE = 512
PROJ_CHUNK = 1280
GLA_TILE = 256
CPT = GLA_TILE // CHUNK


def _cparams(n_axes):
    return pltpu.CompilerParams(
        dimension_semantics=("arbitrary",) * n_axes, vmem_limit_bytes=VMEM_LIMIT)


_W_IN_SRC = np.cumsum([0, QA_W, KA_W, KA_W, QB_W, QB_W, VB_W, VB_W, B_RANK, D_MODEL, D_MODEL])
_W_IN_SEGMENTS = tuple(zip(
    (COL_AQ, COL_AK, COL_AV, COL_BQ, COL_BK, COL_BV, COL_BG, COL_R, COL_GA, COL_GB),
    (int(s) for s in _W_IN_SRC[:-1]),
    (int(w) for w in np.diff(_W_IN_SRC))))
W_IN_COLS = int(_W_IN_SRC[-1])
W_ROWS_PER_COPY = 256


def _load_permuted_w_in(wt_ref, wbf_ref):
    wbf_ref[COL_R:COL_R + LANE, :] = jnp.zeros((LANE, D_MODEL), BF16)
    for dst, src, width in _W_IN_SEGMENTS:
        for r0 in range(0, width, W_ROWS_PER_COPY):
            n = min(W_ROWS_PER_COPY, width - r0)
            wbf_ref[dst + r0:dst + r0 + n, :] = wt_ref[0, src + r0:src + r0 + n, :].astype(BF16)


def _inproj_kernel(x_ref, g_ref, wt_ref, o_ref, wbf_ref):
    @pl.when(pl.program_id(0) == 0)
    def _prep():
        _load_permuted_w_in(wt_ref, wbf_ref)

    x = x_ref[...]
    ms = jnp.mean(x * x, axis=-1, keepdims=True)
    xn = (x * lax.rsqrt(ms + EPS) * g_ref[0]).astype(BF16)
    for c0 in range(0, PROJ_W, PROJ_CHUNK):
        sl = slice(c0, min(c0 + PROJ_CHUNK, PROJ_W))
        o_ref[:, sl] = lax.dot_general(xn, wbf_ref[sl, :], (((1,), (1,)), ((), ())),
                                       preferred_element_type=F32).astype(BF16)


def _inproj(x2, g_all, wt_all, layer):
    t = x2.shape[0]
    tm = min(ROW_TILE, t)
    return pl.pallas_call(
        _inproj_kernel,
        grid=(t // tm,),
        in_specs=[
            pl.BlockSpec((tm, D_MODEL), lambda i: (i, 0)),
            pl.BlockSpec((1, 1, D_MODEL), lambda i: (layer, 0, 0)),
            pl.BlockSpec((1, W_IN_COLS, D_MODEL), lambda i: (layer, 0, 0), pipeline_mode=pl.Buffered(1)),
        ],
        out_specs=pl.BlockSpec((tm, PROJ_W), lambda i: (i, 0)),
        out_shape=jax.ShapeDtypeStruct((t, PROJ_W), BF16),
        scratch_shapes=[pltpu.VMEM((PROJ_W, D_MODEL), BF16)],
        compiler_params=_cparams(1),
        name="inproj",
    )(x2, g_all, wt_all)


ATT_QB = 2


def _attn_kernel(bucket_ref, relb_ref, sink_ref, q_ref, kp_ref, kc_ref, kn_ref,
                 vp_ref, vc_ref, vn_ref, o_ref, bias_ref, logit_ref, prob_ref, *, nblk):
    b = pl.program_id(0)
    n = pl.program_id(1)
    kw = 3 * BLOCK

    @pl.when((b == 0) & (n == 0))
    def _build_bias():
        bucket = bucket_ref[...]
        row = lax.broadcasted_iota(jnp.int32, (BLOCK, kw), 0)
        col = lax.broadcasted_iota(jnp.int32, (BLOCK, kw), 1)
        win = jnp.abs(col - BLOCK - row) <= WINDOW
        keep = (win & (col >= BLOCK), win, win & (col < 2 * BLOCK))
        for hq in range(A_HEADS):
            acc = jnp.zeros((BLOCK, kw), F32)
            for bb in range(N_BUCKETS):
                acc = jnp.where(bucket == bb, relb_ref[bb, hq], acc)
            for v in range(3):
                bias_ref[v, hq] = jnp.where(keep[v], acc, NEG_INF)

    q = (q_ref[...].astype(F32) * (A_HEAD_DIM ** -0.5)).astype(BF16)
    kall = jnp.concatenate([kp_ref[...], kc_ref[...], kn_ref[...]], axis=0)
    vall = jnp.concatenate([vp_ref[...], vc_ref[...], vn_ref[...]], axis=0)
    jobs = [(j, hq) for j in range(ATT_QB) for hq in range(A_HEADS)]

    def keys(x, j, hq):
        h = hq // A_GROUP
        return x[j * BLOCK:(j + 3) * BLOCK, h * A_HEAD_DIM:(h + 1) * A_HEAD_DIM]

    for i, (j, hq) in enumerate(jobs):
        logit_ref[i] = lax.dot_general(
            q[j * BLOCK:(j + 1) * BLOCK, hq * A_HEAD_DIM:(hq + 1) * A_HEAD_DIM], keys(kall, j, hq),
            (((1,), (1,)), ((), ())), preferred_element_type=F32)
    dens = []
    for i, (j, hq) in enumerate(jobs):
        blk = n * ATT_QB + j
        variant = jnp.where(blk == 0, 0, jnp.where(blk == nblk - 1, 2, 1))
        s = logit_ref[i] + bias_ref[variant, hq]
        sk = sink_ref[hq]
        m = jnp.maximum(jnp.max(s, axis=-1, keepdims=True), sk)
        p = jnp.exp(s - m)
        dens.append(jnp.sum(p, axis=-1, keepdims=True) + jnp.exp(sk - m))
        prob_ref[i] = p.astype(BF16)
    outs = [jnp.dot(prob_ref[i], keys(vall, j, hq), preferred_element_type=F32) / dens[i]
            for i, (j, hq) in enumerate(jobs)]
    for j in range(ATT_QB):
        o_ref[j * BLOCK:(j + 1) * BLOCK, :] = jnp.concatenate(
            outs[j * A_HEADS:(j + 1) * A_HEADS], axis=1).astype(BF16)


def _attention(proj, bucket, rel_bias, sink, bsz, seq):
    nblk = seq // BLOCK
    t = bsz * seq
    cq = COL_AQ // QA_W
    ck = COL_AK // KA_W
    cv = COL_AV // KA_W

    nsteps = nblk // ATT_QB
    rows = ATT_QB * BLOCK

    def kv_specs(c):
        return [pl.BlockSpec((BLOCK, KA_W), lambda b, n: (b * nblk + jnp.maximum(n * ATT_QB - 1, 0), c)),
                pl.BlockSpec((rows, KA_W), lambda b, n: (b * nsteps + n, c)),
                pl.BlockSpec((BLOCK, KA_W),
                             lambda b, n: (b * nblk + jnp.minimum((n + 1) * ATT_QB, nblk - 1), c))]

    smem = pl.BlockSpec(memory_space=pltpu.SMEM)
    return pl.pallas_call(
        functools.partial(_attn_kernel, nblk=nblk),
        grid=(bsz, nsteps),
        in_specs=[
            pl.BlockSpec((BLOCK, 3 * BLOCK), lambda b, n: (0, 0)),
            smem, smem,
            pl.BlockSpec((rows, QA_W), lambda b, n: (b * nsteps + n, cq)),
        ] + kv_specs(ck) + kv_specs(cv),
        out_specs=pl.BlockSpec((rows, QA_W), lambda b, n: (b * nsteps + n, 0)),
        out_shape=jax.ShapeDtypeStruct((t, QA_W), BF16),
        scratch_shapes=[pltpu.VMEM((3, A_HEADS, BLOCK, 3 * BLOCK), F32),
                        pltpu.VMEM((ATT_QB * A_HEADS, BLOCK, 3 * BLOCK), F32),
                        pltpu.VMEM((ATT_QB * A_HEADS, BLOCK, 3 * BLOCK), BF16)],
        compiler_params=_cparams(2),
        name="attn",
    )(bucket, rel_bias, sink, proj, proj, proj, proj, proj, proj, proj)


def _log_sigmoid(z):
    return jnp.minimum(z, 0.0) - jnp.log(1.0 + jnp.exp(-jnp.abs(z)))


def _log_decay(r, wr, br):
    z = jnp.dot(r, wr, preferred_element_type=F32) + br
    return _log_sigmoid(z) / GATE_TAU


def _chunk_masks():
    row = lax.broadcasted_iota(jnp.int32, (GLA_TILE, GLA_TILE), 0)
    col = lax.broadcasted_iota(jnp.int32, (GLA_TILE, GLA_TILE), 1)
    sh = int(math.log2(CHUNK))
    same = jnp.right_shift(row, sh) == jnp.right_shift(col, sh)
    return same, same & (col <= row), same & (col >= row)


def _split3(x):
    hi = x.astype(BF16)
    r1 = x - hi.astype(F32)
    mid = r1.astype(BF16)
    lo = (r1 - mid.astype(F32)).astype(BF16)
    return hi, mid, lo


def _mask_dot(mask, x):
    m = mask.astype(BF16)
    hi, mid, lo = _split3(x)
    return (jnp.dot(m, hi, preferred_element_type=F32) + jnp.dot(m, mid, preferred_element_type=F32)
            + jnp.dot(m, lo, preferred_element_type=F32))


def _gla_state_dir(k_ref, v_ref, r_ref, wrt_ref, brc_ref, s_out_ref, state_ref, order, cum_mask, same):
    z = lax.dot_general(wrt_ref[...], r_ref[...], (((1,), (1,)), ((), ())),
                        preferred_element_type=F32) + brc_ref[:, 0:1]
    hi, mid, lo = _split3(_log_sigmoid(z) / GATE_TAU)
    m = jnp.concatenate([cum_mask, same], axis=1).astype(BF16)
    both = (jnp.dot(hi, m, preferred_element_type=F32) + jnp.dot(mid, m, preferred_element_type=F32)
            + jnp.dot(lo, m, preferred_element_type=F32))
    cum = both[:, :GLA_TILE]
    tot = both[:, GLA_TILE:]
    k_end = (k_ref[...].astype(F32).T * jnp.exp(tot - cum)).astype(BF16)
    lane_chunk = jnp.right_shift(lax.broadcasted_iota(jnp.int32, (1, GLA_TILE), 1),
                                 int(math.log2(CHUNK)))
    upd = []
    for h in range(B_HEADS):
        kh = k_end[h * B_KEY_DIM:(h + 1) * B_KEY_DIM]
        lhs = jnp.concatenate([jnp.where(lane_chunk == c, kh, 0.0).astype(BF16) for c in range(CPT)],
                              axis=0)
        upd.append(jnp.dot(lhs, v_ref[:, h * B_VAL_DIM:(h + 1) * B_VAL_DIM],
                           preferred_element_type=F32))
    for c in order:
        s_out_ref[0, c] = state_ref[...].astype(BF16)
        upd_c = jnp.concatenate([u[c * B_KEY_DIM:(c + 1) * B_KEY_DIM] for u in upd], axis=0)
        decay = jnp.exp(tot[:, c * CHUNK:c * CHUNK + 1])
        state_ref[...] = decay * state_ref[...] + upd_c


def _gla_state_kernel(kf_ref, vf_ref, rf_ref, kb_ref, vb_ref, rb_ref,
                      wrf_ref, brf_ref, wrb_ref, brb_ref,
                      sf_ref, sb_ref, stf_ref, stb_ref):
    @pl.when(pl.program_id(1) == 0)
    def _reset():
        stf_ref[...] = jnp.zeros_like(stf_ref)
        stb_ref[...] = jnp.zeros_like(stb_ref)

    same, lower, upper = _chunk_masks()
    _gla_state_dir(kf_ref, vf_ref, rf_ref, wrf_ref, brf_ref, sf_ref, stf_ref,
                   range(CPT), upper, same)
    _gla_state_dir(kb_ref, vb_ref, rb_ref, wrb_ref, brb_ref, sb_ref, stb_ref,
                   range(CPT - 1, -1, -1), lower, same)


def _gla_states(proj, wrf_t, brf_c, wrb_t, brb_c, bsz, seq):
    nt = seq // GLA_TILE
    nchunks = seq // CHUNK
    hk = B_HEADS * B_KEY_DIM
    ck, cv, cr = COL_BK // QB_W, COL_BV // VB_W, COL_R // LANE

    def fwd(b, i):
        return b * nt + i

    def bwd(b, i):
        return b * nt + (nt - 1 - i)

    def tile_specs(f):
        return [pl.BlockSpec((GLA_TILE, QB_W), lambda b, i: (f(b, i), ck)),
                pl.BlockSpec((GLA_TILE, VB_W), lambda b, i: (f(b, i), cv)),
                pl.BlockSpec((GLA_TILE, LANE), lambda b, i: (f(b, i), cr))]

    const = lambda b, i: (0, 0)
    out_sds = jax.ShapeDtypeStruct((bsz, nchunks, hk, B_VAL_DIM), BF16)
    return pl.pallas_call(
        _gla_state_kernel,
        grid=(bsz, nt),
        in_specs=tile_specs(fwd) + tile_specs(bwd) + [
            pl.BlockSpec((hk, LANE), const), pl.BlockSpec((hk, 1), const),
            pl.BlockSpec((hk, LANE), const), pl.BlockSpec((hk, 1), const)],
        out_specs=[pl.BlockSpec((1, CPT, hk, B_VAL_DIM), lambda b, i: (b, i, 0, 0)),
                   pl.BlockSpec((1, CPT, hk, B_VAL_DIM), lambda b, i: (b, nt - 1 - i, 0, 0))],
        out_shape=[out_sds, out_sds],
        scratch_shapes=[pltpu.VMEM((hk, B_VAL_DIM), F32), pltpu.VMEM((hk, B_VAL_DIM), F32)],
        compiler_params=_cparams(2),
        name="gla_state",
    )(proj, proj, proj, proj, proj, proj, wrf_t, brf_c, wrb_t, brb_c)


def _gla_out_kernel(q_ref, k_ref, v_ref, r_ref, g_ref, sf_ref, sb_ref,
                    wrf_ref, brf_ref, wrb_ref, brb_ref, ng_ref, o_ref):
    same, lower, upper = _chunk_masks()
    r = r_ref[...]
    cum_f = _mask_dot(lower, _log_decay(r, wrf_ref[...], brf_ref[...]))
    cum_b = _mask_dot(upper, _log_decay(r, wrb_ref[...], brb_ref[...]))
    q = q_ref[...].astype(F32) * (B_KEY_DIM ** -0.5)
    k = k_ref[...].astype(F32)
    qd_f = q * jnp.exp(cum_f)
    qd_b = q * jnp.exp(cum_b)
    ki_f = (k * jnp.exp(-cum_f)).astype(BF16)
    ki_b = (k * jnp.exp(-cum_b)).astype(BF16)
    lane_head = jnp.right_shift(
        lax.broadcasted_iota(jnp.int32, (1, B_HEADS * B_KEY_DIM), 1), int(math.log2(B_KEY_DIM)))

    def stack_heads(x):
        return jnp.concatenate(
            [jnp.where(lane_head == h, x, 0.0) for h in range(B_HEADS)], axis=0).astype(BF16)

    qs_f = stack_heads(qd_f)
    qs_b = stack_heads(qd_b)
    nt_dims = (((1,), (1,)), ((), ()))
    sc_f = lax.dot_general(qs_f, ki_f, nt_dims, preferred_element_type=F32)
    sc_b = lax.dot_general(qs_b, ki_b, nt_dims, preferred_element_type=F32)
    lower_s = jnp.concatenate([lower] * B_HEADS, axis=0)
    strict_upper_s = jnp.concatenate([upper & ~lower] * B_HEADS, axis=0)
    p = jnp.where(lower_s, sc_f, jnp.where(strict_upper_s, sc_b, 0.0)).astype(BF16)
    v = v_ref[...]
    inter = []
    for c in range(CPT):
        lhs_f = jnp.concatenate(
            [qs_f[h * GLA_TILE + c * CHUNK:h * GLA_TILE + (c + 1) * CHUNK] for h in range(B_HEADS)], axis=0)
        lhs_b = jnp.concatenate(
            [qs_b[h * GLA_TILE + c * CHUNK:h * GLA_TILE + (c + 1) * CHUNK] for h in range(B_HEADS)], axis=0)
        inter.append(jnp.dot(lhs_f, sf_ref[0, c], preferred_element_type=F32)
                     + jnp.dot(lhs_b, sb_ref[0, c], preferred_element_type=F32))
    ng = ng_ref[...]
    g = g_ref[...].astype(F32)
    for h in range(B_HEADS):
        vs = slice(h * B_VAL_DIM, (h + 1) * B_VAL_DIM)
        o = jnp.dot(p[h * GLA_TILE:(h + 1) * GLA_TILE], v[:, vs], preferred_element_type=F32)
        o = o + jnp.concatenate([inter[c][h * CHUNK:(h + 1) * CHUNK] for c in range(CPT)], axis=0)
        o = o * lax.rsqrt(jnp.mean(o * o, axis=-1, keepdims=True) + EPS) * ng
        gh = g[:, vs]
        o_ref[:, vs] = (o * (gh / (1.0 + jnp.exp(-gh)))).astype(BF16)


def _gla_out(proj, s_f, s_b, wrf, brf, wrb, brb, ng, bsz, seq):
    nt = seq // GLA_TILE
    t = bsz * seq
    hk = B_HEADS * B_KEY_DIM
    const = lambda b, i: (0, 0)
    row = lambda c: (lambda b, i: (b * nt + i, c))
    return pl.pallas_call(
        _gla_out_kernel,
        grid=(bsz, nt),
        in_specs=[
            pl.BlockSpec((GLA_TILE, QB_W), row(COL_BQ // QB_W)),
            pl.BlockSpec((GLA_TILE, QB_W), row(COL_BK // QB_W)),
            pl.BlockSpec((GLA_TILE, VB_W), row(COL_BV // VB_W)),
            pl.BlockSpec((GLA_TILE, LANE), row(COL_R // LANE)),
            pl.BlockSpec((GLA_TILE, VB_W), row(COL_BG // VB_W)),
            pl.BlockSpec((1, CPT, hk, B_VAL_DIM), lambda b, i: (b, i, 0, 0)),
            pl.BlockSpec((1, CPT, hk, B_VAL_DIM), lambda b, i: (b, i, 0, 0)),
            pl.BlockSpec((LANE, hk), const), pl.BlockSpec((1, hk), const),
            pl.BlockSpec((LANE, hk), const), pl.BlockSpec((1, hk), const),
            pl.BlockSpec((1, B_VAL_DIM), const),
        ],
        out_specs=pl.BlockSpec((GLA_TILE, VB_W), lambda b, i: (b * nt + i, 0)),
        out_shape=jax.ShapeDtypeStruct((t, VB_W), BF16),
        compiler_params=_cparams(2),
        name="gla_out",
    )(proj, proj, proj, proj, proj, s_f, s_b, wrf, brf, wrb, brb, ng)


def _sigmoid(x):
    return 1.0 / (1.0 + jnp.exp(-x))


def _merge_kernel(ya_ref, yb_ref, ga_ref, gb_ref, x_ref, wa_ref, wb_ref, wo_ref, o_ref,
                  wa_bf, wb_bf, wo_bf):
    @pl.when(pl.program_id(0) == 0)
    def _prep():
        wa_bf[...] = wa_ref[0].astype(BF16)
        wb_bf[...] = wb_ref[0].astype(BF16)
        wo_bf[...] = wo_ref[0].astype(BF16)

    a = jnp.dot(ya_ref[...], wa_bf[...], preferred_element_type=F32)
    b = jnp.dot(yb_ref[...], wb_bf[...], preferred_element_type=F32)
    merged = _sigmoid(ga_ref[...].astype(F32)) * a + _sigmoid(gb_ref[...].astype(F32)) * b
    o_ref[...] = x_ref[...] + jnp.dot(merged.astype(BF16), wo_bf[...], preferred_element_type=F32)


def _merge(ya, yb, proj, x2, wa_all, wb_all, wo_all, layer):
    t = x2.shape[0]
    tm = min(ROW_TILE, t)
    wsel = lambda i: (layer, 0, 0)
    return pl.pallas_call(
        _merge_kernel,
        grid=(t // tm,),
        in_specs=[
            pl.BlockSpec((tm, QA_W), lambda i: (i, 0)),
            pl.BlockSpec((tm, VB_W), lambda i: (i, 0)),
            pl.BlockSpec((tm, D_MODEL), lambda i: (i, COL_GA // D_MODEL)),
            pl.BlockSpec((tm, D_MODEL), lambda i: (i, COL_GB // D_MODEL)),
            pl.BlockSpec((tm, D_MODEL), lambda i: (i, 0)),
            pl.BlockSpec((1, QA_W, D_MODEL), wsel, pipeline_mode=pl.Buffered(1)),
            pl.BlockSpec((1, VB_W, D_MODEL), wsel, pipeline_mode=pl.Buffered(1)),
            pl.BlockSpec((1, D_MODEL, D_MODEL), wsel, pipeline_mode=pl.Buffered(1)),
        ],
        out_specs=pl.BlockSpec((tm, D_MODEL), lambda i: (i, 0)),
        out_shape=jax.ShapeDtypeStruct((t, D_MODEL), F32),
        scratch_shapes=[pltpu.VMEM((QA_W, D_MODEL), BF16), pltpu.VMEM((VB_W, D_MODEL), BF16),
                        pltpu.VMEM((D_MODEL, D_MODEL), BF16)],
        compiler_params=_cparams(1),
        name="merge",
    )(ya, yb, proj, proj, x2, wa_all, wb_all, wo_all)


def _moe_tile_tokens(t):
    return min(4096, t)


MOE_BLK = 512
MOE_UNIT = 16
MOE_CAP = 1536
MOE_CAP_COMMON = 1280
XS_W = D_MODEL + LANE
MOE_MT = 384
MOE_CH = 2 * MOE_MT
MOE_CHU = MOE_CH // MOE_UNIT
MOE_NC = 256
assert MOE_CAP >= 2 * MOE_BLK + N_EXPERTS * (MOE_UNIT - 1) and MOE_CAP % MOE_UNIT == 0


NT_DIMS = (((1,), (1,)), ((), ()))


def _route(hn_bf, wrt_ref, brc_ref):
    logits = lax.dot_general(wrt_ref[...].astype(BF16), hn_bf, NT_DIMS,
                             preferred_element_type=F32) + brc_ref[:, 0:1]
    row = lax.broadcasted_iota(jnp.int32, logits.shape, 0).astype(F32)
    big = float(LANE)
    ninf = -jnp.inf
    gl = jnp.where(row < N_GROUPS, logits, ninf)
    gmax = jnp.max(gl, axis=0, keepdims=True)
    g_idx = jnp.min(jnp.where(gl == gmax, row, big), axis=0, keepdims=True)
    g_w = 1.0 / jnp.sum(jnp.exp(gl - gmax), axis=0, keepdims=True)
    lo = float(N_GROUPS) + g_idx * float(EXPERTS_PER_GROUP)
    el = jnp.where((row >= lo) & (row < lo + EXPERTS_PER_GROUP), logits, ninf)
    v1 = jnp.max(el, axis=0, keepdims=True)
    i1 = jnp.min(jnp.where(el == v1, row, big), axis=0, keepdims=True)
    el2 = jnp.where(row == i1, ninf, el)
    v2 = jnp.max(el2, axis=0, keepdims=True)
    i2 = jnp.min(jnp.where(el2 == v2, row, big), axis=0, keepdims=True)
    e21 = jnp.exp(v2 - v1)
    w1 = g_w / (1.0 + e21)
    w2 = g_w * e21 / (1.0 + e21)
    return i1 - float(N_GROUPS), i2 - float(N_GROUPS), w1, w2, row


def _slot_matrix(pos1, pos2, second):
    slot = lax.broadcasted_iota(jnp.int32, (pos1.shape[0], MOE_CAP), 1)
    return jnp.where(slot == pos1.astype(jnp.int32), 1.0,
                     jnp.where(slot == pos2.astype(jnp.int32), second, 0.0)).astype(BF16)


def _slot_matrix_t(pos1, pos2, second):
    slot = lax.broadcasted_iota(jnp.int32, (MOE_CAP, pos1.shape[1]), 0)
    return jnp.where(slot == pos1.astype(jnp.int32), 1.0,
                     jnp.where(slot == pos2.astype(jnp.int32), second, 0.0)).astype(BF16)


def _split3_f32(w):
    hi = w.astype(BF16).astype(F32)
    r1 = w - hi
    mid = r1.astype(BF16).astype(F32)
    return hi, mid, r1 - mid


def _rows_to_sublanes(vals, row):
    out = jnp.zeros(row.shape, F32)
    for k, v in enumerate(vals):
        out = jnp.where(row == float(k), v, out)
    return out


def _moe_kernel(x_ref, g_ref, wrt_ref, brc_ref, wg_ref, wu_ref, wd_ref, gf_ref, o_ref,
                xs_ref, pos_ref, tab_v, tab_s, xt_ref, addr_ref, sem,
                *, nblk, final_norm):
    s = pl.program_id(1)
    dump_row = nblk * MOE_CAP

    @pl.when(s < nblk)
    def _dispatch():
        @pl.when(s == 0)
        def _init():
            xs_ref[dump_row:dump_row + MOE_UNIT, :] = jnp.zeros((MOE_UNIT, XS_W), BF16)
            xt_ref[...] = jnp.zeros_like(xt_ref)

        x = x_ref[...]
        ms = jnp.mean(x * x, axis=-1, keepdims=True)
        hn = (x * lax.rsqrt(ms + EPS) * g_ref[0]).astype(BF16)
        e1, e2, w1, w2, row = _route(hn, wrt_ref, brc_ref)
        onehot = ((row == e1) | (row == e2)).astype(F32)
        earlier = (lax.broadcasted_iota(jnp.int32, (MOE_BLK, MOE_BLK), 0)
                   < lax.broadcasted_iota(jnp.int32, (MOE_BLK, MOE_BLK), 1)).astype(BF16)
        before = jnp.dot(onehot.astype(BF16), earlier, preferred_element_type=F32)
        counts = jnp.sum(onehot, axis=1, keepdims=True)
        units = jnp.floor((counts + (MOE_UNIT - 1.0)) * (1.0 / MOE_UNIT))
        lower = (lax.broadcasted_iota(jnp.int32, (LANE, LANE), 1)
                 < lax.broadcasted_iota(jnp.int32, (LANE, LANE), 0)).astype(BF16)
        padded = jnp.broadcast_to(units * MOE_UNIT, (LANE, LANE))
        run_off = jnp.dot(lower, padded.astype(BF16), preferred_element_type=F32)[:, 0:1]
        start = run_off + before
        pos1 = jnp.sum(jnp.where(row == e1, start, 0.0), axis=0, keepdims=True)
        pos2 = jnp.sum(jnp.where(row == e2, start, 0.0), axis=0, keepdims=True)
        w_rows = _rows_to_sublanes(
            _split3_f32(w1) + _split3_f32(w2) + (jnp.ones_like(w1),), row).astype(BF16)
        used = jnp.sum(units, axis=0, keepdims=True) * MOE_UNIT
        pt_t = _slot_matrix_t(pos1, pos2, 2.0)
        rows = pl.ds(pl.multiple_of(s * MOE_CAP, MOE_CAP), MOE_CAP)
        for c in range(D_MODEL // MOE_NC):
            cs = slice(c * MOE_NC, (c + 1) * MOE_NC)
            xs_ref[rows, cs] = jnp.dot(pt_t, hn[:, cs], preferred_element_type=F32).astype(BF16)
        xs_ref[rows, D_MODEL:] = lax.dot_general(pt_t, w_rows, NT_DIMS,
                                                 preferred_element_type=F32).astype(BF16)
        pos_ref[s] = _rows_to_sublanes((pos1, pos2), row).T
        lane_e = lax.broadcasted_iota(jnp.int32, (LANE, LANE), 1)
        per_expert = jnp.where(lane_e == 0, run_off,
                               jnp.where(lane_e == 1, units, jnp.where(lane_e == 2, used, 0.0)))
        tab_v[s] = per_expert.T[0:SUBLANE, :].astype(jnp.int32)

        @pl.when(s == nblk - 1)
        def _publish():
            cp = pltpu.make_async_copy(tab_v, tab_s, sem)
            cp.start()
            cp.wait()

    @pl.when((s >= nblk) & (s < nblk + N_EXPERTS))
    def _experts():
        e = s - nblk
        total = tab_s[0, 1, e]
        for j in range(1, nblk):
            total = total + tab_s[j, 1, e]

        def chunk(c, carry):
            u0 = c * MOE_CHU
            for k in range(MOE_CHU):
                addr_ref[k] = dump_row
            cum = 0
            for j in range(nblk):
                nj = tab_s[j, 1, e]
                run0 = j * MOE_CAP + tab_s[j, 0, e]
                k_off = cum - u0

                def copy_unit(u, c2, run0=run0, k_off=k_off):
                    src = pl.multiple_of(run0 + u * MOE_UNIT, MOE_UNIT)
                    dst = pl.multiple_of((k_off + u) * MOE_UNIT, MOE_UNIT)
                    xt_ref[pl.ds(dst, MOE_UNIT), :] = xs_ref[pl.ds(src, MOE_UNIT), :]
                    addr_ref[k_off + u] = src
                    return c2
                lax.fori_loop(jnp.clip(u0 - cum, 0, nj), jnp.clip(u0 + MOE_CHU - cum, 0, nj),
                              copy_unit, 0)
                cum = cum + nj
            rows_here = jnp.minimum(MOE_CHU, total - u0) * MOE_UNIT

            def tile(t, c3):
                xt = xt_ref[pl.ds(pl.multiple_of(t * MOE_MT, MOE_MT), MOE_MT), :]
                wl = xt[:, D_MODEL:].astype(F32)
                second = wl[:, 6:7] > 1.5
                wcol = jnp.where(second, 0.5 * (wl[:, 3:4] + wl[:, 4:5] + wl[:, 5:6]),
                                 wl[:, 0:1] + wl[:, 1:2] + wl[:, 2:3])
                undo = jnp.where(second, 0.5, 1.0)
                xf = xt[:, :D_MODEL]
                gate = jnp.dot(xf, wg_ref[0].astype(BF16), preferred_element_type=F32) * undo
                up = jnp.dot(xf, wu_ref[0].astype(BF16), preferred_element_type=F32) * undo
                h = gate * _sigmoid(gate) * up * wcol
                y = jnp.dot(h.astype(BF16), wd_ref[0].astype(BF16),
                            preferred_element_type=F32).astype(BF16)
                for uu in range(MOE_MT // MOE_UNIT):
                    dst = pl.multiple_of(addr_ref[t * (MOE_MT // MOE_UNIT) + uu], MOE_UNIT)
                    xs_ref[pl.ds(dst, MOE_UNIT), :D_MODEL] = y[uu * MOE_UNIT:(uu + 1) * MOE_UNIT]
                return c3
            lax.fori_loop(0, (rows_here + MOE_MT - 1) // MOE_MT, tile, 0)
            return carry
        lax.fori_loop(0, (total + MOE_CHU - 1) // MOE_CHU, chunk, 0)

    @pl.when(s >= nblk + N_EXPERTS)
    def _combine():
        j = s - nblk - N_EXPERTS
        p = pos_ref[j]
        pt = _slot_matrix(p[:, 0:1], p[:, 1:2], 1.0)
        base = pl.multiple_of(j * MOE_CAP, MOE_CAP)
        head = pl.ds(base, MOE_CAP_COMMON)
        for c in range(D_MODEL // MOE_NC):
            cs = slice(c * MOE_NC, (c + 1) * MOE_NC)
            o_ref[:, cs] = x_ref[:, cs] + jnp.dot(pt[:, :MOE_CAP_COMMON], xs_ref[head, cs],
                                                  preferred_element_type=F32)

        @pl.when(tab_s[j, 2, 0] > MOE_CAP_COMMON)
        def _tail():
            tail = pl.ds(base + MOE_CAP_COMMON, MOE_CAP - MOE_CAP_COMMON)
            for c in range(D_MODEL // MOE_NC):
                cs = slice(c * MOE_NC, (c + 1) * MOE_NC)
                o_ref[:, cs] += jnp.dot(pt[:, MOE_CAP_COMMON:], xs_ref[tail, cs],
                                        preferred_element_type=F32)
        if final_norm:
            y = o_ref[...]
            ms = jnp.mean(y * y, axis=-1, keepdims=True)
            o_ref[...] = y * lax.rsqrt(ms + EPS) * gf_ref[...]


def _moe(x2, g_all, w_r, b_r, wg_all, wu_all, wd_all, g_final, layer, tt, final_norm):
    t = x2.shape[0]
    nblk = tt // MOE_BLK
    steps = 2 * nblk + N_EXPERTS

    def tok(b, s):
        j = jnp.where(s < nblk, s, jnp.where(s < nblk + N_EXPERTS, nblk - 1, s - nblk - N_EXPERTS))
        return (b * nblk + j, 0)

    def out_tok(b, s):
        return (b * nblk + jnp.maximum(s - nblk - N_EXPERTS, 0), 0)

    wsel = lambda b, s: (layer * N_EXPERTS + jnp.clip(s - nblk, 0, N_EXPERTS - 1), 0, 0)
    const = lambda b, s: (0, 0)
    return pl.pallas_call(
        functools.partial(_moe_kernel, nblk=nblk, final_norm=final_norm),
        grid=(t // tt, steps),
        in_specs=[
            pl.BlockSpec((MOE_BLK, D_MODEL), tok),
            pl.BlockSpec((1, 1, D_MODEL), lambda b, s: (layer, 0, 0)),
            pl.BlockSpec((LANE, D_MODEL), const),
            pl.BlockSpec((LANE, 1), const),
            pl.BlockSpec((1, D_MODEL, D_EXPERT), wsel),
            pl.BlockSpec((1, D_MODEL, D_EXPERT), wsel),
            pl.BlockSpec((1, D_EXPERT, D_MODEL), wsel),
            pl.BlockSpec((1, D_MODEL), const),
        ],
        out_specs=pl.BlockSpec((MOE_BLK, D_MODEL), out_tok),
        out_shape=jax.ShapeDtypeStruct((t, D_MODEL), F32),
        scratch_shapes=[
            pltpu.VMEM((nblk * MOE_CAP + MOE_UNIT, XS_W), BF16),
            pltpu.VMEM((nblk, MOE_BLK, LANE), F32),
            pltpu.VMEM((nblk, SUBLANE, LANE), jnp.int32),
            pltpu.SMEM((nblk, SUBLANE, LANE), jnp.int32),
            pltpu.VMEM((MOE_CH, XS_W), BF16),
            pltpu.SMEM((MOE_CHU,), jnp.int32),
            pltpu.SemaphoreType.DMA,
        ],
        compiler_params=_cparams(2),
        name="moe",
    )(x2, g_all, w_r, b_r, wg_all, wu_all, wd_all, g_final)


def _t5_bucket(rel):
    nb = N_BUCKETS // 2
    max_exact = nb // 2
    base = jnp.where(rel > 0, nb, 0)
    n = jnp.abs(rel)
    large = max_exact + (jnp.log(jnp.maximum(n, 1).astype(jnp.float32) / max_exact)
                         / math.log(MAX_DISTANCE / max_exact) * (nb - max_exact)).astype(jnp.int32)
    large = jnp.minimum(large, nb - 1)
    return base + jnp.where(n < max_exact, n, large)


def _pad_rank(wr):
    return jnp.pad(wr, ((0, LANE - B_RANK), (0, 0))).astype(BF16)


def kernel(x, w_in, rel_bias, attn_sink, gla_wr_fwd, gla_br_fwd, gla_wr_bwd, gla_br_bwd, gla_norm, w_branch_a, w_branch_b, w_out, norm_mix, norm_ffn, router_group_w, router_group_b, router_expert_w, router_expert_b, expert_w_gate, expert_w_up, expert_w_down, norm_final):
    bsz, seq, d = x.shape
    depth = w_in.shape[0]
    t = bsz * seq
    tt = _moe_tile_tokens(t)
    q_off = jnp.arange(BLOCK)
    k_off = jnp.arange(3 * BLOCK) - BLOCK
    bucket = _t5_bucket(k_off[None, :] - q_off[:, None]).astype(jnp.int32) & (N_BUCKETS - 1)
    x2 = x.reshape(t, d)
    norm_mix3 = norm_mix[:, None, :]
    w_in_t = jnp.swapaxes(w_in, 1, 2)
    norm_ffn3 = norm_ffn[:, None, :]
    wg_all = expert_w_gate.reshape(depth * N_EXPERTS, d, D_EXPERT)
    wu_all = expert_w_up.reshape(depth * N_EXPERTS, d, D_EXPERT)
    wd_all = expert_w_down.reshape(depth * N_EXPERTS, D_EXPERT, d)
    for l in range(depth):
        proj = _inproj(x2, norm_mix3, w_in_t, l)
        ya = _attention(proj, bucket, rel_bias, attn_sink[l], bsz, seq)
        wrf, wrb = _pad_rank(gla_wr_fwd[l]), _pad_rank(gla_wr_bwd[l])
        brf, brb = gla_br_fwd[l][None, :], gla_br_bwd[l][None, :]
        s_f, s_b = _gla_states(proj, wrf.T, brf.T, wrb.T, brb.T, bsz, seq)
        yb = _gla_out(proj, s_f, s_b, wrf, brf, wrb, brb, gla_norm[l][None, :], bsz, seq)
        x2 = _merge(ya, yb, proj, x2, w_branch_a, w_branch_b, w_out, l)
        w_r = jnp.concatenate(
            [router_group_w[l].T, router_expert_w[l].T,
             jnp.zeros((LANE - N_GROUPS - N_EXPERTS, d), F32)], axis=0)
        b_r = jnp.concatenate(
            [router_group_b[l], router_expert_b[l],
             jnp.zeros((LANE - N_GROUPS - N_EXPERTS,), F32)])[:, None]
        x2 = _moe(x2, norm_ffn3, w_r, b_r, wg_all, wu_all, wd_all, norm_final[None, :], l, tt,
                  final_norm=(l == depth - 1))
    return x2.reshape(bsz, seq, d)
```

```python
import functools
import math

import numpy as np
import jax
import jax.numpy as jnp
from jax import lax
from jax.experimental import pallas as pl
from jax.experimental.pallas import tpu as pltpu

F32 = jnp.float32
BF16 = jnp.bfloat16

D_MODEL = 1024
A_HEADS = 8
A_KV_HEADS = 2
A_HEAD_DIM = 64
A_GROUP = A_HEADS // A_KV_HEADS
WINDOW = 128
BLOCK = 128
N_BUCKETS = 32
MAX_DISTANCE = 128
B_HEADS = 4
B_KEY_DIM = 64
B_VAL_DIM = 128
B_RANK = 16
GATE_TAU = 16.0
CHUNK = 64
N_GROUPS = 4
EXPERTS_PER_GROUP = 8
N_EXPERTS = N_GROUPS * EXPERTS_PER_GROUP
D_EXPERT = 256
EPS = 1e-6
NEG_INF = -1e30

LANE = 128
SUBLANE = 8
VMEM_LIMIT = 56 * 1024 * 1024

QA_W = A_HEADS * A_HEAD_DIM
KA_W = A_KV_HEADS * A_HEAD_DIM
QB_W = B_HEADS * B_KEY_DIM
VB_W = B_HEADS * B_VAL_DIM
COL_GA = 0
COL_GB = COL_GA + D_MODEL
COL_AQ = COL_GB + D_MODEL
COL_BV = COL_AQ + QA_W
COL_BG = COL_BV + VB_W
COL_BQ = COL_BG + VB_W
COL_BK = COL_BQ + QB_W
COL_AK = COL_BK + QB_W
COL_AV = COL_AK + KA_W
COL_R = COL_AV + KA_W
PROJ_W = COL_R + LANE

ROW_TILE = 512
PROJ_CHUNK = 1280
GLA_TILE = 256
CPT = GLA_TILE // CHUNK


def _cparams(n_axes):
    return pltpu.CompilerParams(
        dimension_semantics=("arbitrary",) * n_axes, vmem_limit_bytes=VMEM_LIMIT)


_W_IN_SRC = np.cumsum([0, QA_W, KA_W, KA_W, QB_W, QB_W, VB_W, VB_W, B_RANK, D_MODEL, D_MODEL])
_W_IN_SEGMENTS = tuple(zip(
    (COL_AQ, COL_AK, COL_AV, COL_BQ, COL_BK, COL_BV, COL_BG, COL_R, COL_GA, COL_GB),
    (int(s) for s in _W_IN_SRC[:-1]),
    (int(w) for w in np.diff(_W_IN_SRC))))
W_IN_COLS = int(_W_IN_SRC[-1])
W_ROWS_PER_COPY = 256


def _load_permuted_w_in(wt_ref, wbf_ref):
    wbf_ref[COL_R:COL_R + LANE, :] = jnp.zeros((LANE, D_MODEL), BF16)
    for dst, src, width in _W_IN_SEGMENTS:
        for r0 in range(0, width, W_ROWS_PER_COPY):
            n = min(W_ROWS_PER_COPY, width - r0)
            wbf_ref[dst + r0:dst + r0 + n, :] = wt_ref[0, src + r0:src + r0 + n, :].astype(BF16)


def _inproj_kernel(x_ref, g_ref, wt_ref, o_ref, wbf_ref):
    @pl.when(pl.program_id(0) == 0)
    def _prep():
        _load_permuted_w_in(wt_ref, wbf_ref)

    x = x_ref[...]
    ms = jnp.mean(x * x, axis=-1, keepdims=True)
    xn = (x * lax.rsqrt(ms + EPS) * g_ref[0]).astype(BF16)
    for c0 in range(0, PROJ_W, PROJ_CHUNK):
        sl = slice(c0, min(c0 + PROJ_CHUNK, PROJ_W))
        o_ref[:, sl] = lax.dot_general(xn, wbf_ref[sl, :], (((1,), (1,)), ((), ())),
                                       preferred_element_type=F32).astype(BF16)


def _inproj(x2, g_all, wt_all, layer):
    t = x2.shape[0]
    tm = min(ROW_TILE, t)
    return pl.pallas_call(
        _inproj_kernel,
        grid=(t // tm,),
        in_specs=[
            pl.BlockSpec((tm, D_MODEL), lambda i: (i, 0)),
            pl.BlockSpec((1, 1, D_MODEL), lambda i: (layer, 0, 0)),
            pl.BlockSpec((1, W_IN_COLS, D_MODEL), lambda i: (layer, 0, 0), pipeline_mode=pl.Buffered(1)),
        ],
        out_specs=pl.BlockSpec((tm, PROJ_W), lambda i: (i, 0)),
        out_shape=jax.ShapeDtypeStruct((t, PROJ_W), BF16),
        scratch_shapes=[pltpu.VMEM((PROJ_W, D_MODEL), BF16)],
        compiler_params=_cparams(1),
        name="inproj",
    )(x2, g_all, wt_all)


ATT_QB = 2


def _attn_kernel(bucket_ref, relb_ref, sink_ref, q_ref, kp_ref, kc_ref, kn_ref,
                 vp_ref, vc_ref, vn_ref, o_ref, bias_ref, logit_ref, prob_ref, *, nblk):
    b = pl.program_id(0)
    n = pl.program_id(1)
    kw = 3 * BLOCK

    @pl.when((b == 0) & (n == 0))
    def _build_bias():
        bucket = bucket_ref[...]
        row = lax.broadcasted_iota(jnp.int32, (BLOCK, kw), 0)
        col = lax.broadcasted_iota(jnp.int32, (BLOCK, kw), 1)
        win = jnp.abs(col - BLOCK - row) <= WINDOW
        keep = (win & (col >= BLOCK), win, win & (col < 2 * BLOCK))
        for hq in range(A_HEADS):
            acc = jnp.zeros((BLOCK, kw), F32)
            for bb in range(N_BUCKETS):
                acc = jnp.where(bucket == bb, relb_ref[bb, hq], acc)
            for v in range(3):
                bias_ref[v, hq] = jnp.where(keep[v], acc, NEG_INF)

    q = (q_ref[...].astype(F32) * (A_HEAD_DIM ** -0.5)).astype(BF16)
    kall = jnp.concatenate([kp_ref[...], kc_ref[...], kn_ref[...]], axis=0)
    vall = jnp.concatenate([vp_ref[...], vc_ref[...], vn_ref[...]], axis=0)
    jobs = [(j, hq) for j in range(ATT_QB) for hq in range(A_HEADS)]

    def keys(x, j, hq):
        h = hq // A_GROUP
        return x[j * BLOCK:(j + 3) * BLOCK, h * A_HEAD_DIM:(h + 1) * A_HEAD_DIM]

    for i, (j, hq) in enumerate(jobs):
        logit_ref[i] = lax.dot_general(
            q[j * BLOCK:(j + 1) * BLOCK, hq * A_HEAD_DIM:(hq + 1) * A_HEAD_DIM], keys(kall, j, hq),
            (((1,), (1,)), ((), ())), preferred_element_type=F32)
    dens = []
    for i, (j, hq) in enumerate(jobs):
        blk = n * ATT_QB + j
        variant = jnp.where(blk == 0, 0, jnp.where(blk == nblk - 1, 2, 1))
        s = logit_ref[i] + bias_ref[variant, hq]
        sk = sink_ref[hq]
        m = jnp.maximum(jnp.max(s, axis=-1, keepdims=True), sk)
        p = jnp.exp(s - m)
        dens.append(jnp.sum(p, axis=-1, keepdims=True) + jnp.exp(sk - m))
        prob_ref[i] = p.astype(BF16)
    outs = [jnp.dot(prob_ref[i], keys(vall, j, hq), preferred_element_type=F32) / dens[i]
            for i, (j, hq) in enumerate(jobs)]
    for j in range(ATT_QB):
        o_ref[j * BLOCK:(j + 1) * BLOCK, :] = jnp.concatenate(
            outs[j * A_HEADS:(j + 1) * A_HEADS], axis=1).astype(BF16)


def _attention(proj, bucket, rel_bias, sink, bsz, seq):
    nblk = seq // BLOCK
    t = bsz * seq
    cq = COL_AQ // QA_W
    ck = COL_AK // KA_W
    cv = COL_AV // KA_W

    nsteps = nblk // ATT_QB
    rows = ATT_QB * BLOCK

    def kv_specs(c):
        return [pl.BlockSpec((BLOCK, KA_W), lambda b, n: (b * nblk + jnp.maximum(n * ATT_QB - 1, 0), c)),
                pl.BlockSpec((rows, KA_W), lambda b, n: (b * nsteps + n, c)),
                pl.BlockSpec((BLOCK, KA_W),
                             lambda b, n: (b * nblk + jnp.minimum((n + 1) * ATT_QB, nblk - 1), c))]

    smem = pl.BlockSpec(memory_space=pltpu.SMEM)
    return pl.pallas_call(
        functools.partial(_attn_kernel, nblk=nblk),
        grid=(bsz, nsteps),
        in_specs=[
            pl.BlockSpec((BLOCK, 3 * BLOCK), lambda b, n: (0, 0)),
            smem, smem,
            pl.BlockSpec((rows, QA_W), lambda b, n: (b * nsteps + n, cq)),
        ] + kv_specs(ck) + kv_specs(cv),
        out_specs=pl.BlockSpec((rows, QA_W), lambda b, n: (b * nsteps + n, 0)),
        out_shape=jax.ShapeDtypeStruct((t, QA_W), BF16),
        scratch_shapes=[pltpu.VMEM((3, A_HEADS, BLOCK, 3 * BLOCK), F32),
                        pltpu.VMEM((ATT_QB * A_HEADS, BLOCK, 3 * BLOCK), F32),
                        pltpu.VMEM((ATT_QB * A_HEADS, BLOCK, 3 * BLOCK), BF16)],
        compiler_params=_cparams(2),
        name="attn",
    )(bucket, rel_bias, sink, proj, proj, proj, proj, proj, proj, proj)


def _log_sigmoid(z):
    return jnp.minimum(z, 0.0) - jnp.log(1.0 + jnp.exp(-jnp.abs(z)))


def _log_decay(r, wr, br):
    z = jnp.dot(r, wr, preferred_element_type=F32) + br
    return _log_sigmoid(z) / GATE_TAU


def _chunk_masks():
    row = lax.broadcasted_iota(jnp.int32, (GLA_TILE, GLA_TILE), 0)
    col = lax.broadcasted_iota(jnp.int32, (GLA_TILE, GLA_TILE), 1)
    sh = int(math.log2(CHUNK))
    same = jnp.right_shift(row, sh) == jnp.right_shift(col, sh)
    return same, same & (col <= row), same & (col >= row)


def _split3(x):
    hi = x.astype(BF16)
    r1 = x - hi.astype(F32)
    mid = r1.astype(BF16)
    lo = (r1 - mid.astype(F32)).astype(BF16)
    return hi, mid, lo


def _mask_dot(mask, x):
    m = mask.astype(BF16)
    hi, mid, lo = _split3(x)
    return (jnp.dot(m, hi, preferred_element_type=F32) + jnp.dot(m, mid, preferred_element_type=F32)
            + jnp.dot(m, lo, preferred_element_type=F32))


def _gla_state_dir(k_ref, v_ref, r_ref, wrt_ref, brc_ref, s_out_ref, state_ref, order, cum_mask, same):
    z = lax.dot_general(wrt_ref[...], r_ref[...], (((1,), (1,)), ((), ())),
                        preferred_element_type=F32) + brc_ref[:, 0:1]
    hi, mid, lo = _split3(_log_sigmoid(z) / GATE_TAU)
    m = jnp.concatenate([cum_mask, same], axis=1).astype(BF16)
    both = (jnp.dot(hi, m, preferred_element_type=F32) + jnp.dot(mid, m, preferred_element_type=F32)
            + jnp.dot(lo, m, preferred_element_type=F32))
    cum = both[:, :GLA_TILE]
    tot = both[:, GLA_TILE:]
    k_end = (k_ref[...].astype(F32).T * jnp.exp(tot - cum)).astype(BF16)
    lane_chunk = jnp.right_shift(lax.broadcasted_iota(jnp.int32, (1, GLA_TILE), 1),
                                 int(math.log2(CHUNK)))
    upd = []
    for h in range(B_HEADS):
        kh = k_end[h * B_KEY_DIM:(h + 1) * B_KEY_DIM]
        lhs = jnp.concatenate([jnp.where(lane_chunk == c, kh, 0.0).astype(BF16) for c in range(CPT)],
                              axis=0)
        upd.append(jnp.dot(lhs, v_ref[:, h * B_VAL_DIM:(h + 1) * B_VAL_DIM],
                           preferred_element_type=F32))
    for c in order:
        s_out_ref[0, c] = state_ref[...].astype(BF16)
        upd_c = jnp.concatenate([u[c * B_KEY_DIM:(c + 1) * B_KEY_DIM] for u in upd], axis=0)
        decay = jnp.exp(tot[:, c * CHUNK:c * CHUNK + 1])
        state_ref[...] = decay * state_ref[...] + upd_c


def _gla_state_kernel(kf_ref, vf_ref, rf_ref, kb_ref, vb_ref, rb_ref,
                      wrf_ref, brf_ref, wrb_ref, brb_ref,
                      sf_ref, sb_ref, stf_ref, stb_ref):
    @pl.when(pl.program_id(1) == 0)
    def _reset():
        stf_ref[...] = jnp.zeros_like(stf_ref)
        stb_ref[...] = jnp.zeros_like(stb_ref)

    same, lower, upper = _chunk_masks()
    _gla_state_dir(kf_ref, vf_ref, rf_ref, wrf_ref, brf_ref, sf_ref, stf_ref,
                   range(CPT), upper, same)
    _gla_state_dir(kb_ref, vb_ref, rb_ref, wrb_ref, brb_ref, sb_ref, stb_ref,
                   range(CPT - 1, -1, -1), lower, same)


def _gla_states(proj, wrf_t, brf_c, wrb_t, brb_c, bsz, seq):
    nt = seq // GLA_TILE
    nchunks = seq // CHUNK
    hk = B_HEADS * B_KEY_DIM
    ck, cv, cr = COL_BK // QB_W, COL_BV // VB_W, COL_R // LANE

    def fwd(b, i):
        return b * nt + i

    def bwd(b, i):
        return b * nt + (nt - 1 - i)

    def tile_specs(f):
        return [pl.BlockSpec((GLA_TILE, QB_W), lambda b, i: (f(b, i), ck)),
                pl.BlockSpec((GLA_TILE, VB_W), lambda b, i: (f(b, i), cv)),
                pl.BlockSpec((GLA_TILE, LANE), lambda b, i: (f(b, i), cr))]

    const = lambda b, i: (0, 0)
    out_sds = jax.ShapeDtypeStruct((bsz, nchunks, hk, B_VAL_DIM), BF16)
    return pl.pallas_call(
        _gla_state_kernel,
        grid=(bsz, nt),
        in_specs=tile_specs(fwd) + tile_specs(bwd) + [
            pl.BlockSpec((hk, LANE), const), pl.BlockSpec((hk, 1), const),
            pl.BlockSpec((hk, LANE), const), pl.BlockSpec((hk, 1), const)],
        out_specs=[pl.BlockSpec((1, CPT, hk, B_VAL_DIM), lambda b, i: (b, i, 0, 0)),
                   pl.BlockSpec((1, CPT, hk, B_VAL_DIM), lambda b, i: (b, nt - 1 - i, 0, 0))],
        out_shape=[out_sds, out_sds],
        scratch_shapes=[pltpu.VMEM((hk, B_VAL_DIM), F32), pltpu.VMEM((hk, B_VAL_DIM), F32)],
        compiler_params=_cparams(2),
        name="gla_state",
    )(proj, proj, proj, proj, proj, proj, wrf_t, brf_c, wrb_t, brb_c)


def _gla_out_kernel(q_ref, k_ref, v_ref, r_ref, g_ref, sf_ref, sb_ref,
                    wrf_ref, brf_ref, wrb_ref, brb_ref, ng_ref, o_ref):
    same, lower, upper = _chunk_masks()
    r = r_ref[...]
    cum_f = _mask_dot(lower, _log_decay(r, wrf_ref[...], brf_ref[...]))
    cum_b = _mask_dot(upper, _log_decay(r, wrb_ref[...], brb_ref[...]))
    q = q_ref[...].astype(F32) * (B_KEY_DIM ** -0.5)
    k = k_ref[...].astype(F32)
    qd_f = q * jnp.exp(cum_f)
    qd_b = q * jnp.exp(cum_b)
    ki_f = (k * jnp.exp(-cum_f)).astype(BF16)
    ki_b = (k * jnp.exp(-cum_b)).astype(BF16)
    lane_head = jnp.right_shift(
        lax.broadcasted_iota(jnp.int32, (1, B_HEADS * B_KEY_DIM), 1), int(math.log2(B_KEY_DIM)))

    def stack_heads(x):
        return jnp.concatenate(
            [jnp.where(lane_head == h, x, 0.0) for h in range(B_HEADS)], axis=0).astype(BF16)

    qs_f = stack_heads(qd_f)
    qs_b = stack_heads(qd_b)
    nt_dims = (((1,), (1,)), ((), ()))
    sc_f = lax.dot_general(qs_f, ki_f, nt_dims, preferred_element_type=F32)
    sc_b = lax.dot_general(qs_b, ki_b, nt_dims, preferred_element_type=F32)
    lower_s = jnp.concatenate([lower] * B_HEADS, axis=0)
    strict_upper_s = jnp.concatenate([upper & ~lower] * B_HEADS, axis=0)
    p = jnp.where(lower_s, sc_f, jnp.where(strict_upper_s, sc_b, 0.0)).astype(BF16)
    v = v_ref[...]
    inter = []
    for c in range(CPT):
        lhs_f = jnp.concatenate(
            [qs_f[h * GLA_TILE + c * CHUNK:h * GLA_TILE + (c + 1) * CHUNK] for h in range(B_HEADS)], axis=0)
        lhs_b = jnp.concatenate(
            [qs_b[h * GLA_TILE + c * CHUNK:h * GLA_TILE + (c + 1) * CHUNK] for h in range(B_HEADS)], axis=0)
        inter.append(jnp.dot(lhs_f, sf_ref[0, c], preferred_element_type=F32)
                     + jnp.dot(lhs_b, sb_ref[0, c], preferred_element_type=F32))
    ng = ng_ref[...]
    g = g_ref[...].astype(F32)
    for h in range(B_HEADS):
        vs = slice(h * B_VAL_DIM, (h + 1) * B_VAL_DIM)
        o = jnp.dot(p[h * GLA_TILE:(h + 1) * GLA_TILE], v[:, vs], preferred_element_type=F32)
        o = o + jnp.concatenate([inter[c][h * CHUNK:(h + 1) * CHUNK] for c in range(CPT)], axis=0)
        o = o * lax.rsqrt(jnp.mean(o * o, axis=-1, keepdims=True) + EPS) * ng
        gh = g[:, vs]
        o_ref[:, vs] = (o * (gh / (1.0 + jnp.exp(-gh)))).astype(BF16)


def _gla_out(proj, s_f, s_b, wrf, brf, wrb, brb, ng, bsz, seq):
    nt = seq // GLA_TILE
    t = bsz * seq
    hk = B_HEADS * B_KEY_DIM
    const = lambda b, i: (0, 0)
    row = lambda c: (lambda b, i: (b * nt + i, c))
    return pl.pallas_call(
        _gla_out_kernel,
        grid=(bsz, nt),
        in_specs=[
            pl.BlockSpec((GLA_TILE, QB_W), row(COL_BQ // QB_W)),
            pl.BlockSpec((GLA_TILE, QB_W), row(COL_BK // QB_W)),
            pl.BlockSpec((GLA_TILE, VB_W), row(COL_BV // VB_W)),
            pl.BlockSpec((GLA_TILE, LANE), row(COL_R // LANE)),
            pl.BlockSpec((GLA_TILE, VB_W), row(COL_BG // VB_W)),
            pl.BlockSpec((1, CPT, hk, B_VAL_DIM), lambda b, i: (b, i, 0, 0)),
            pl.BlockSpec((1, CPT, hk, B_VAL_DIM), lambda b, i: (b, i, 0, 0)),
            pl.BlockSpec((LANE, hk), const), pl.BlockSpec((1, hk), const),
            pl.BlockSpec((LANE, hk), const), pl.BlockSpec((1, hk), const),
            pl.BlockSpec((1, B_VAL_DIM), const),
        ],
        out_specs=pl.BlockSpec((GLA_TILE, VB_W), lambda b, i: (b * nt + i, 0)),
        out_shape=jax.ShapeDtypeStruct((t, VB_W), BF16),
        compiler_params=_cparams(2),
        name="gla_out",
    )(proj, proj, proj, proj, proj, s_f, s_b, wrf, brf, wrb, brb, ng)


def _sigmoid(x):
    return 1.0 / (1.0 + jnp.exp(-x))


def _merge_kernel(ya_ref, yb_ref, ga_ref, gb_ref, x_ref, wa_ref, wb_ref, wo_ref, o_ref,
                  wa_bf, wb_bf, wo_bf):
    @pl.when(pl.program_id(0) == 0)
    def _prep():
        wa_bf[...] = wa_ref[0].astype(BF16)
        wb_bf[...] = wb_ref[0].astype(BF16)
        wo_bf[...] = wo_ref[0].astype(BF16)

    a = jnp.dot(ya_ref[...], wa_bf[...], preferred_element_type=F32)
    b = jnp.dot(yb_ref[...], wb_bf[...], preferred_element_type=F32)
    merged = _sigmoid(ga_ref[...].astype(F32)) * a + _sigmoid(gb_ref[...].astype(F32)) * b
    o_ref[...] = x_ref[...] + jnp.dot(merged.astype(BF16), wo_bf[...], preferred_element_type=F32)


def _merge(ya, yb, proj, x2, wa_all, wb_all, wo_all, layer):
    t = x2.shape[0]
    tm = min(ROW_TILE, t)
    wsel = lambda i: (layer, 0, 0)
    return pl.pallas_call(
        _merge_kernel,
        grid=(t // tm,),
        in_specs=[
            pl.BlockSpec((tm, QA_W), lambda i: (i, 0)),
            pl.BlockSpec((tm, VB_W), lambda i: (i, 0)),
            pl.BlockSpec((tm, D_MODEL), lambda i: (i, COL_GA // D_MODEL)),
            pl.BlockSpec((tm, D_MODEL), lambda i: (i, COL_GB // D_MODEL)),
            pl.BlockSpec((tm, D_MODEL), lambda i: (i, 0)),
            pl.BlockSpec((1, QA_W, D_MODEL), wsel, pipeline_mode=pl.Buffered(1)),
            pl.BlockSpec((1, VB_W, D_MODEL), wsel, pipeline_mode=pl.Buffered(1)),
            pl.BlockSpec((1, D_MODEL, D_MODEL), wsel, pipeline_mode=pl.Buffered(1)),
        ],
        out_specs=pl.BlockSpec((tm, D_MODEL), lambda i: (i, 0)),
        out_shape=jax.ShapeDtypeStruct((t, D_MODEL), F32),
        scratch_shapes=[pltpu.VMEM((QA_W, D_MODEL), BF16), pltpu.VMEM((VB_W, D_MODEL), BF16),
                        pltpu.VMEM((D_MODEL, D_MODEL), BF16)],
        compiler_params=_cparams(1),
        name="merge",
    )(ya, yb, proj, proj, x2, wa_all, wb_all, wo_all)


def _moe_tile_tokens(t):
    return min(4096, t)


MOE_BLK = 512
MOE_UNIT = 16
MOE_CAP = 1536
MOE_CAP_COMMON = 1280
XS_W = D_MODEL + LANE
MOE_MT = 384
MOE_CH = 2 * MOE_MT
MOE_CHU = MOE_CH // MOE_UNIT
MOE_NC = 256
MOE_EPS = 2
MOE_ESTEPS = N_EXPERTS // MOE_EPS
assert MOE_CAP >= 2 * MOE_BLK + N_EXPERTS * (MOE_UNIT - 1) and MOE_CAP % MOE_UNIT == 0


NT_DIMS = (((1,), (1,)), ((), ()))


def _route(hn_bf, wrt_ref, brc_ref):
    logits = lax.dot_general(wrt_ref[...].astype(BF16), hn_bf, NT_DIMS,
                             preferred_element_type=F32) + brc_ref[:, 0:1]
    row = lax.broadcasted_iota(jnp.int32, logits.shape, 0).astype(F32)
    big = float(LANE)
    ninf = -jnp.inf
    gl = jnp.where(row < N_GROUPS, logits, ninf)
    gmax = jnp.max(gl, axis=0, keepdims=True)
    g_idx = jnp.min(jnp.where(gl == gmax, row, big), axis=0, keepdims=True)
    g_w = 1.0 / jnp.sum(jnp.exp(gl - gmax), axis=0, keepdims=True)
    lo = float(N_GROUPS) + g_idx * float(EXPERTS_PER_GROUP)
    el = jnp.where((row >= lo) & (row < lo + EXPERTS_PER_GROUP), logits, ninf)
    v1 = jnp.max(el, axis=0, keepdims=True)
    i1 = jnp.min(jnp.where(el == v1, row, big), axis=0, keepdims=True)
    el2 = jnp.where(row == i1, ninf, el)
    v2 = jnp.max(el2, axis=0, keepdims=True)
    i2 = jnp.min(jnp.where(el2 == v2, row, big), axis=0, keepdims=True)
    e21 = jnp.exp(v2 - v1)
    w1 = g_w / (1.0 + e21)
    w2 = g_w * e21 / (1.0 + e21)
    return i1 - float(N_GROUPS), i2 - float(N_GROUPS), w1, w2, row


def _slot_matrix(pos1, pos2, second):
    slot = lax.broadcasted_iota(jnp.int32, (pos1.shape[0], MOE_CAP), 1)
    return jnp.where(slot == pos1.astype(jnp.int32), 1.0,
                     jnp.where(slot == pos2.astype(jnp.int32), second, 0.0)).astype(BF16)


def _slot_matrix_t(pos1, pos2, second):
    slot = lax.broadcasted_iota(jnp.int32, (MOE_CAP, pos1.shape[1]), 0)
    return jnp.where(slot == pos1.astype(jnp.int32), 1.0,
                     jnp.where(slot == pos2.astype(jnp.int32), second, 0.0)).astype(BF16)


def _split3_f32(w):
    hi = w.astype(BF16).astype(F32)
    r1 = w - hi
    mid = r1.astype(BF16).astype(F32)
    return hi, mid, r1 - mid


def _rows_to_sublanes(vals, row):
    out = jnp.zeros(row.shape, F32)
    for k, v in enumerate(vals):
        out = jnp.where(row == float(k), v, out)
    return out


def _moe_kernel(x_ref, g_ref, wrt_ref, brc_ref, wg_ref, wu_ref, wd_ref, gf_ref, o_ref,
                xs_ref, pos_ref, tab_v, tab_s, xt_ref, addr_ref, sem,
                *, nblk, final_norm):
    s = pl.program_id(1)
    dump_row = nblk * MOE_CAP

    @pl.when(s < nblk)
    def _dispatch():
        @pl.when(s == 0)
        def _init():
            xs_ref[dump_row:dump_row + MOE_UNIT, :] = jnp.zeros((MOE_UNIT, XS_W), BF16)
            xt_ref[...] = jnp.zeros_like(xt_ref)

        x = x_ref[...]
        ms = jnp.mean(x * x, axis=-1, keepdims=True)
        hn = (x * lax.rsqrt(ms + EPS) * g_ref[0]).astype(BF16)
        e1, e2, w1, w2, row = _route(hn, wrt_ref, brc_ref)
        onehot = ((row == e1) | (row == e2)).astype(F32)
        earlier = (lax.broadcasted_iota(jnp.int32, (MOE_BLK, MOE_BLK), 0)
                   < lax.broadcasted_iota(jnp.int32, (MOE_BLK, MOE_BLK), 1)).astype(BF16)
        before = jnp.dot(onehot.astype(BF16), earlier, preferred_element_type=F32)
        counts = jnp.sum(onehot, axis=1, keepdims=True)
        units = jnp.floor((counts + (MOE_UNIT - 1.0)) * (1.0 / MOE_UNIT))
        lower = (lax.broadcasted_iota(jnp.int32, (LANE, LANE), 1)
                 < lax.broadcasted_iota(jnp.int32, (LANE, LANE), 0)).astype(BF16)
        padded = jnp.broadcast_to(units * MOE_UNIT, (LANE, LANE))
        run_off = jnp.dot(lower, padded.astype(BF16), preferred_element_type=F32)[:, 0:1]
        start = run_off + before
        pos1 = jnp.sum(jnp.where(row == e1, start, 0.0), axis=0, keepdims=True)
        pos2 = jnp.sum(jnp.where(row == e2, start, 0.0), axis=0, keepdims=True)
        w_rows = _rows_to_sublanes(
            _split3_f32(w1) + _split3_f32(w2) + (jnp.ones_like(w1),), row).astype(BF16)
        used = jnp.sum(units, axis=0, keepdims=True) * MOE_UNIT
        pt_t = _slot_matrix_t(pos1, pos2, 2.0)
        rows = pl.ds(pl.multiple_of(s * MOE_CAP, MOE_CAP), MOE_CAP)
        for c in range(D_MODEL // MOE_NC):
            cs = slice(c * MOE_NC, (c + 1) * MOE_NC)
            xs_ref[rows, cs] = jnp.dot(pt_t, hn[:, cs], preferred_element_type=F32).astype(BF16)
        xs_ref[rows, D_MODEL:] = lax.dot_general(pt_t, w_rows, NT_DIMS,
                                                 preferred_element_type=F32).astype(BF16)
        pos_ref[s] = _rows_to_sublanes((pos1, pos2), row).T
        lane_e = lax.broadcasted_iota(jnp.int32, (LANE, LANE), 1)
        per_expert = jnp.where(lane_e == 0, run_off,
                               jnp.where(lane_e == 1, units, jnp.where(lane_e == 2, used, 0.0)))
        tab_v[s] = per_expert.T[0:SUBLANE, :].astype(jnp.int32)

        @pl.when(s == nblk - 1)
        def _publish():
            cp = pltpu.make_async_copy(tab_v, tab_s, sem)
            cp.start()
            cp.wait()

    def one_expert(e, wi):
        total = tab_s[0, 1, e]
        for j in range(1, nblk):
            total = total + tab_s[j, 1, e]

        def chunk(c, carry):
            u0 = c * MOE_CHU
            for k in range(MOE_CHU):
                addr_ref[k] = dump_row
            cum = 0
            for j in range(nblk):
                nj = tab_s[j, 1, e]
                run0 = j * MOE_CAP + tab_s[j, 0, e]
                k_off = cum - u0

                def copy_unit(u, c2, run0=run0, k_off=k_off):
                    src = pl.multiple_of(run0 + u * MOE_UNIT, MOE_UNIT)
                    dst = pl.multiple_of((k_off + u) * MOE_UNIT, MOE_UNIT)
                    xt_ref[pl.ds(dst, MOE_UNIT), :] = xs_ref[pl.ds(src, MOE_UNIT), :]
                    addr_ref[k_off + u] = src
                    return c2
                lax.fori_loop(jnp.clip(u0 - cum, 0, nj), jnp.clip(u0 + MOE_CHU - cum, 0, nj),
                              copy_unit, 0)
                cum = cum + nj
            rows_here = jnp.minimum(MOE_CHU, total - u0) * MOE_UNIT

            def tile(t, c3):
                xt = xt_ref[pl.ds(pl.multiple_of(t * MOE_MT, MOE_MT), MOE_MT), :]
                wl = xt[:, D_MODEL:].astype(F32)
                second = wl[:, 6:7] > 1.5
                wcol = jnp.where(second, 0.5 * (wl[:, 3:4] + wl[:, 4:5] + wl[:, 5:6]),
                                 wl[:, 0:1] + wl[:, 1:2] + wl[:, 2:3])
                undo = jnp.where(second, 0.5, 1.0)
                xf = xt[:, :D_MODEL]
                gate = jnp.dot(xf, wg_ref[wi].astype(BF16), preferred_element_type=F32) * undo
                up = jnp.dot(xf, wu_ref[wi].astype(BF16), preferred_element_type=F32) * undo
                h = gate * _sigmoid(gate) * up * wcol
                y = jnp.dot(h.astype(BF16), wd_ref[wi].astype(BF16),
                            preferred_element_type=F32).astype(BF16)
                for uu in range(MOE_MT // MOE_UNIT):
                    dst = pl.multiple_of(addr_ref[t * (MOE_MT // MOE_UNIT) + uu], MOE_UNIT)
                    xs_ref[pl.ds(dst, MOE_UNIT), :D_MODEL] = y[uu * MOE_UNIT:(uu + 1) * MOE_UNIT]
                return c3
            lax.fori_loop(0, (rows_here + MOE_MT - 1) // MOE_MT, tile, 0)
            return carry
        lax.fori_loop(0, (total + MOE_CHU - 1) // MOE_CHU, chunk, 0)

    @pl.when((s >= nblk) & (s < nblk + MOE_ESTEPS))
    def _experts():
        for wi in range(MOE_EPS):
            one_expert((s - nblk) * MOE_EPS + wi, wi)

    @pl.when(s >= nblk + MOE_ESTEPS)
    def _combine():
        j = s - nblk - MOE_ESTEPS
        p = pos_ref[j]
        pt = _slot_matrix(p[:, 0:1], p[:, 1:2], 1.0)
        base = pl.multiple_of(j * MOE_CAP, MOE_CAP)
        head = pl.ds(base, MOE_CAP_COMMON)
        for c in range(D_MODEL // MOE_NC):
            cs = slice(c * MOE_NC, (c + 1) * MOE_NC)
            o_ref[:, cs] = x_ref[:, cs] + jnp.dot(pt[:, :MOE_CAP_COMMON], xs_ref[head, cs],
                                                  preferred_element_type=F32)

        @pl.when(tab_s[j, 2, 0] > MOE_CAP_COMMON)
        def _tail():
            tail = pl.ds(base + MOE_CAP_COMMON, MOE_CAP - MOE_CAP_COMMON)
            for c in range(D_MODEL // MOE_NC):
                cs = slice(c * MOE_NC, (c + 1) * MOE_NC)
                o_ref[:, cs] += jnp.dot(pt[:, MOE_CAP_COMMON:], xs_ref[tail, cs],
                                        preferred_element_type=F32)
        if final_norm:
            y = o_ref[...]
            ms = jnp.mean(y * y, axis=-1, keepdims=True)
            o_ref[...] = y * lax.rsqrt(ms + EPS) * gf_ref[...]


def _moe(x2, g_all, w_r, b_r, wg_all, wu_all, wd_all, g_final, layer, tt, final_norm):
    t = x2.shape[0]
    nblk = tt // MOE_BLK
    steps = 2 * nblk + MOE_ESTEPS

    def tok(b, s):
        j = jnp.where(s < nblk, s, jnp.where(s < nblk + MOE_ESTEPS, nblk - 1, s - nblk - MOE_ESTEPS))
        return (b * nblk + j, 0)

    def out_tok(b, s):
        return (b * nblk + jnp.maximum(s - nblk - MOE_ESTEPS, 0), 0)

    wsel = lambda b, s: (layer * MOE_ESTEPS + jnp.clip(s - nblk, 0, MOE_ESTEPS - 1), 0, 0)
    const = lambda b, s: (0, 0)
    return pl.pallas_call(
        functools.partial(_moe_kernel, nblk=nblk, final_norm=final_norm),
        grid=(t // tt, steps),
        in_specs=[
            pl.BlockSpec((MOE_BLK, D_MODEL), tok),
            pl.BlockSpec((1, 1, D_MODEL), lambda b, s: (layer, 0, 0)),
            pl.BlockSpec((LANE, D_MODEL), const),
            pl.BlockSpec((LANE, 1), const),
            pl.BlockSpec((MOE_EPS, D_MODEL, D_EXPERT), wsel),
            pl.BlockSpec((MOE_EPS, D_MODEL, D_EXPERT), wsel),
            pl.BlockSpec((MOE_EPS, D_EXPERT, D_MODEL), wsel),
            pl.BlockSpec((1, D_MODEL), const),
        ],
        out_specs=pl.BlockSpec((MOE_BLK, D_MODEL), out_tok),
        out_shape=jax.ShapeDtypeStruct((t, D_MODEL), F32),
        scratch_shapes=[
            pltpu.VMEM((nblk * MOE_CAP + MOE_UNIT, XS_W), BF16),
            pltpu.VMEM((nblk, MOE_BLK, LANE), F32),
            pltpu.VMEM((nblk, SUBLANE, LANE), jnp.int32),
            pltpu.SMEM((nblk, SUBLANE, LANE), jnp.int32),
            pltpu.VMEM((MOE_CH, XS_W), BF16),
            pltpu.SMEM((MOE_CHU,), jnp.int32),
            pltpu.SemaphoreType.DMA,
        ],
        compiler_params=_cparams(2),
        name="moe",
    )(x2, g_all, w_r, b_r, wg_all, wu_all, wd_all, g_final)


def _t5_bucket(rel):
    nb = N_BUCKETS // 2
    max_exact = nb // 2
    base = jnp.where(rel > 0, nb, 0)
    n = jnp.abs(rel)
    large = max_exact + (jnp.log(jnp.maximum(n, 1).astype(jnp.float32) / max_exact)
                         / math.log(MAX_DISTANCE / max_exact) * (nb - max_exact)).astype(jnp.int32)
    large = jnp.minimum(large, nb - 1)
    return base + jnp.where(n < max_exact, n, large)


def _pad_rank(wr):
    return jnp.pad(wr, ((0, LANE - B_RANK), (0, 0))).astype(BF16)


def kernel(x, w_in, rel_bias, attn_sink, gla_wr_fwd, gla_br_fwd, gla_wr_bwd, gla_br_bwd, gla_norm, w_branch_a, w_branch_b, w_out, norm_mix, norm_ffn, router_group_w, router_group_b, router_expert_w, router_expert_b, expert_w_gate, expert_w_up, expert_w_down, norm_final):
    bsz, seq, d = x.shape
    depth = w_in.shape[0]
    t = bsz * seq
    tt = _moe_tile_tokens(t)
    q_off = jnp.arange(BLOCK)
    k_off = jnp.arange(3 * BLOCK) - BLOCK
    bucket = _t5_bucket(k_off[None, :] - q_off[:, None]).astype(jnp.int32) & (N_BUCKETS - 1)
    x2 = x.reshape(t, d)
    norm_mix3 = norm_mix[:, None, :]
    w_in_t = jnp.swapaxes(w_in, 1, 2)
    norm_ffn3 = norm_ffn[:, None, :]
    wg_all = expert_w_gate.reshape(depth * N_EXPERTS, d, D_EXPERT)
    wu_all = expert_w_up.reshape(depth * N_EXPERTS, d, D_EXPERT)
    wd_all = expert_w_down.reshape(depth * N_EXPERTS, D_EXPERT, d)
    for l in range(depth):
        proj = _inproj(x2, norm_mix3, w_in_t, l)
        ya = _attention(proj, bucket, rel_bias, attn_sink[l], bsz, seq)
        wrf, wrb = _pad_rank(gla_wr_fwd[l]), _pad_rank(gla_wr_bwd[l])
        brf, brb = gla_br_fwd[l][None, :], gla_br_bwd[l][None, :]
        s_f, s_b = _gla_states(proj, wrf.T, brf.T, wrb.T, brb.T, bsz, seq)
        yb = _gla_out(proj, s_f, s_b, wrf, brf, wrb, brb, gla_norm[l][None, :], bsz, seq)
        x2 = _merge(ya, yb, proj, x2, w_branch_a, w_branch_b, w_out, l)
        w_r = jnp.concatenate(
            [router_group_w[l].T, router_expert_w[l].T,
             jnp.zeros((LANE - N_GROUPS - N_EXPERTS, d), F32)], axis=0)
        b_r = jnp.concatenate(
            [router_group_b[l], router_expert_b[l],
             jnp.zeros((LANE - N_GROUPS - N_EXPERTS,), F32)])[:, None]
        x2 = _moe(x2, norm_ffn3, w_r, b_r, wg_all, wu_all, wd_all, norm_final[None, :], l, tt,
                  final_norm=(l == depth - 1))
    return x2.reshape(bsz, seq, d)
```

```python
import functools
import math

import numpy as np
import jax
import jax.numpy as jnp
from jax import lax
from jax.experimental import pallas as pl
from jax.experimental.pallas import tpu as pltpu

F32 = jnp.float32
BF16 = jnp.bfloat16

D_MODEL = 1024
A_HEADS = 8
A_KV_HEADS = 2
A_HEAD_DIM = 64
A_GROUP = A_HEADS // A_KV_HEADS
WINDOW = 128
BLOCK = 128
N_BUCKETS = 32
MAX_DISTANCE = 128
B_HEADS = 4
B_KEY_DIM = 64
B_VAL_DIM = 128
B_RANK = 16
GATE_TAU = 16.0
CHUNK = 64
N_GROUPS = 4
EXPERTS_PER_GROUP = 8
N_EXPERTS = N_GROUPS * EXPERTS_PER_GROUP
D_EXPERT = 256
EPS = 1e-6
NEG_INF = -1e30

LANE = 128
SUBLANE = 8
VMEM_LIMIT = 56 * 1024 * 1024

QA_W = A_HEADS * A_HEAD_DIM
KA_W = A_KV_HEADS * A_HEAD_DIM
QB_W = B_HEADS * B_KEY_DIM
VB_W = B_HEADS * B_VAL_DIM
COL_GA = 0
COL_GB = COL_GA + D_MODEL
COL_AQ = COL_GB + D_MODEL
COL_BV = COL_AQ + QA_W
COL_BG = COL_BV + VB_W
COL_BQ = COL_BG + VB_W
COL_BK = COL_BQ + QB_W
COL_AK = COL_BK + QB_W
COL_AV = COL_AK + KA_W
COL_R = COL_AV + KA_W
PROJ_W = COL_R + LANE

ROW_TILE = 512
PROJ_CHUNK = 1280
GLA_TILE = 256
CPT = GLA_TILE // CHUNK


def _cparams(n_axes):
    return pltpu.CompilerParams(
        dimension_semantics=("arbitrary",) * n_axes, vmem_limit_bytes=VMEM_LIMIT)


_W_IN_SRC = np.cumsum([0, QA_W, KA_W, KA_W, QB_W, QB_W, VB_W, VB_W, B_RANK, D_MODEL, D_MODEL])
_W_IN_SEGMENTS = tuple(zip(
    (COL_AQ, COL_AK, COL_AV, COL_BQ, COL_BK, COL_BV, COL_BG, COL_R, COL_GA, COL_GB),
    (int(s) for s in _W_IN_SRC[:-1]),
    (int(w) for w in np.diff(_W_IN_SRC))))
W_IN_COLS = int(_W_IN_SRC[-1])
W_ROWS_PER_COPY = 256


def _load_permuted_w_in(wt_ref, wbf_ref):
    wbf_ref[COL_R:COL_R + LANE, :] = jnp.zeros((LANE, D_MODEL), BF16)
    for dst, src, width in _W_IN_SEGMENTS:
        for r0 in range(0, width, W_ROWS_PER_COPY):
            n = min(W_ROWS_PER_COPY, width - r0)
            wbf_ref[dst + r0:dst + r0 + n, :] = wt_ref[0, src + r0:src + r0 + n, :].astype(BF16)


def _inproj_kernel(x_ref, g_ref, wt_ref, wg_ref, wu_ref, wd_ref, o_ref, wgb_ref, wub_ref, wdb_ref, wbf_ref):
    @pl.when(pl.program_id(0) == 0)
    def _prep():
        _load_permuted_w_in(wt_ref, wbf_ref)

    wgb_ref[...] = wg_ref[...].astype(BF16)
    wub_ref[...] = wu_ref[...].astype(BF16)
    wdb_ref[...] = wd_ref[...].astype(BF16)

    x = x_ref[...]
    ms = jnp.mean(x * x, axis=-1, keepdims=True)
    xn = (x * lax.rsqrt(ms + EPS) * g_ref[0]).astype(BF16)
    for c0 in range(0, PROJ_W, PROJ_CHUNK):
        sl = slice(c0, min(c0 + PROJ_CHUNK, PROJ_W))
        o_ref[:, sl] = lax.dot_general(xn, wbf_ref[sl, :], (((1,), (1,)), ((), ())),
                                       preferred_element_type=F32).astype(BF16)


def _inproj(x2, g_all, wt_all, wg_all, wu_all, wd_all, layer):
    t = x2.shape[0]
    tm = min(ROW_TILE, t)
    nsteps = t // tm
    epi = -(-N_EXPERTS // nsteps)
    while N_EXPERTS % epi:
        epi += 1
    nwb = N_EXPERTS // epi
    wsel = lambda i: (layer * nwb + jnp.minimum(i, nwb - 1), 0, 0)
    wout = lambda i: (jnp.minimum(i, nwb - 1), 0, 0)
    return pl.pallas_call(
        _inproj_kernel,
        grid=(nsteps,),
        in_specs=[
            pl.BlockSpec((tm, D_MODEL), lambda i: (i, 0)),
            pl.BlockSpec((1, 1, D_MODEL), lambda i: (layer, 0, 0)),
            pl.BlockSpec((1, W_IN_COLS, D_MODEL), lambda i: (layer, 0, 0), pipeline_mode=pl.Buffered(1)),
            pl.BlockSpec((epi, D_MODEL, D_EXPERT), wsel),
            pl.BlockSpec((epi, D_MODEL, D_EXPERT), wsel),
            pl.BlockSpec((epi, D_EXPERT, D_MODEL), wsel),
        ],
        out_specs=[
            pl.BlockSpec((tm, PROJ_W), lambda i: (i, 0)),
            pl.BlockSpec((epi, D_MODEL, D_EXPERT), wout),
            pl.BlockSpec((epi, D_MODEL, D_EXPERT), wout),
            pl.BlockSpec((epi, D_EXPERT, D_MODEL), wout),
        ],
        out_shape=[
            jax.ShapeDtypeStruct((t, PROJ_W), BF16),
            jax.ShapeDtypeStruct((N_EXPERTS, D_MODEL, D_EXPERT), BF16),
            jax.ShapeDtypeStruct((N_EXPERTS, D_MODEL, D_EXPERT), BF16),
            jax.ShapeDtypeStruct((N_EXPERTS, D_EXPERT, D_MODEL), BF16),
        ],
        scratch_shapes=[pltpu.VMEM((PROJ_W, D_MODEL), BF16)],
        compiler_params=_cparams(1),
        name="inproj",
    )(x2, g_all, wt_all, wg_all, wu_all, wd_all)


ATT_QB = 2


def _attn_kernel(bucket_ref, relb_ref, sink_ref, q_ref, kp_ref, kc_ref, kn_ref,
                 vp_ref, vc_ref, vn_ref, o_ref, bias_ref, logit_ref, prob_ref, *, nblk):
    b = pl.program_id(0)
    n = pl.program_id(1)
    kw = 3 * BLOCK

    @pl.when((b == 0) & (n == 0))
    def _build_bias():
        bucket = bucket_ref[...]
        row = lax.broadcasted_iota(jnp.int32, (BLOCK, kw), 0)
        col = lax.broadcasted_iota(jnp.int32, (BLOCK, kw), 1)
        win = jnp.abs(col - BLOCK - row) <= WINDOW
        keep = (win & (col >= BLOCK), win, win & (col < 2 * BLOCK))
        for hq in range(A_HEADS):
            acc = jnp.zeros((BLOCK, kw), F32)
            for bb in range(N_BUCKETS):
                acc = jnp.where(bucket == bb, relb_ref[bb, hq], acc)
            for v in range(3):
                bias_ref[v, hq] = jnp.where(keep[v], acc, NEG_INF)

    q = (q_ref[...].astype(F32) * (A_HEAD_DIM ** -0.5)).astype(BF16)
    kall = jnp.concatenate([kp_ref[...], kc_ref[...], kn_ref[...]], axis=0)
    vall = jnp.concatenate([vp_ref[...], vc_ref[...], vn_ref[...]], axis=0)
    jobs = [(j, hq) for j in range(ATT_QB) for hq in range(A_HEADS)]

    def keys(x, j, hq):
        h = hq // A_GROUP
        return x[j * BLOCK:(j + 3) * BLOCK, h * A_HEAD_DIM:(h + 1) * A_HEAD_DIM]

    for i, (j, hq) in enumerate(jobs):
        logit_ref[i] = lax.dot_general(
            q[j * BLOCK:(j + 1) * BLOCK, hq * A_HEAD_DIM:(hq + 1) * A_HEAD_DIM], keys(kall, j, hq),
            (((1,), (1,)), ((), ())), preferred_element_type=F32)
    dens = []
    for i, (j, hq) in enumerate(jobs):
        blk = n * ATT_QB + j
        variant = jnp.where(blk == 0, 0, jnp.where(blk == nblk - 1, 2, 1))
        s = logit_ref[i] + bias_ref[variant, hq]
        sk = sink_ref[hq]
        m = jnp.maximum(jnp.max(s, axis=-1, keepdims=True), sk)
        p = jnp.exp(s - m)
        dens.append(jnp.sum(p, axis=-1, keepdims=True) + jnp.exp(sk - m))
        prob_ref[i] = p.astype(BF16)
    outs = [jnp.dot(prob_ref[i], keys(vall, j, hq), preferred_element_type=F32) / dens[i]
            for i, (j, hq) in enumerate(jobs)]
    for j in range(ATT_QB):
        o_ref[j * BLOCK:(j + 1) * BLOCK, :] = jnp.concatenate(
            outs[j * A_HEADS:(j + 1) * A_HEADS], axis=1).astype(BF16)


def _attention(proj, bucket, rel_bias, sink, bsz, seq):
    nblk = seq // BLOCK
    t = bsz * seq
    cq = COL_AQ // QA_W
    ck = COL_AK // KA_W
    cv = COL_AV // KA_W

    nsteps = nblk // ATT_QB
    rows = ATT_QB * BLOCK

    def kv_specs(c):
        return [pl.BlockSpec((BLOCK, KA_W), lambda b, n: (b * nblk + jnp.maximum(n * ATT_QB - 1, 0), c)),
                pl.BlockSpec((rows, KA_W), lambda b, n: (b * nsteps + n, c)),
                pl.BlockSpec((BLOCK, KA_W),
                             lambda b, n: (b * nblk + jnp.minimum((n + 1) * ATT_QB, nblk - 1), c))]

    smem = pl.BlockSpec(memory_space=pltpu.SMEM)
    return pl.pallas_call(
        functools.partial(_attn_kernel, nblk=nblk),
        grid=(bsz, nsteps),
        in_specs=[
            pl.BlockSpec((BLOCK, 3 * BLOCK), lambda b, n: (0, 0)),
            smem, smem,
            pl.BlockSpec((rows, QA_W), lambda b, n: (b * nsteps + n, cq)),
        ] + kv_specs(ck) + kv_specs(cv),
        out_specs=pl.BlockSpec((rows, QA_W), lambda b, n: (b * nsteps + n, 0)),
        out_shape=jax.ShapeDtypeStruct((t, QA_W), BF16),
        scratch_shapes=[pltpu.VMEM((3, A_HEADS, BLOCK, 3 * BLOCK), F32),
                        pltpu.VMEM((ATT_QB * A_HEADS, BLOCK, 3 * BLOCK), F32),
                        pltpu.VMEM((ATT_QB * A_HEADS, BLOCK, 3 * BLOCK), BF16)],
        compiler_params=_cparams(2),
        name="attn",
    )(bucket, rel_bias, sink, proj, proj, proj, proj, proj, proj, proj)


def _log_sigmoid(z):
    return jnp.minimum(z, 0.0) - jnp.log(1.0 + jnp.exp(-jnp.abs(z)))


def _log_decay(r, wr, br):
    z = jnp.dot(r, wr, preferred_element_type=F32) + br
    return _log_sigmoid(z) / GATE_TAU


def _chunk_masks():
    row = lax.broadcasted_iota(jnp.int32, (GLA_TILE, GLA_TILE), 0)
    col = lax.broadcasted_iota(jnp.int32, (GLA_TILE, GLA_TILE), 1)
    sh = int(math.log2(CHUNK))
    same = jnp.right_shift(row, sh) == jnp.right_shift(col, sh)
    return same, same & (col <= row), same & (col >= row)


def _split3(x):
    hi = x.astype(BF16)
    r1 = x - hi.astype(F32)
    mid = r1.astype(BF16)
    lo = (r1 - mid.astype(F32)).astype(BF16)
    return hi, mid, lo


def _mask_dot(mask, x):
    m = mask.astype(BF16)
    hi, mid, lo = _split3(x)
    return (jnp.dot(m, hi, preferred_element_type=F32) + jnp.dot(m, mid, preferred_element_type=F32)
            + jnp.dot(m, lo, preferred_element_type=F32))


def _gla_state_dir(k_ref, v_ref, r_ref, wrt_ref, brc_ref, s_out_ref, state_ref, order, cum_mask, same):
    z = lax.dot_general(wrt_ref[...], r_ref[...], (((1,), (1,)), ((), ())),
                        preferred_element_type=F32) + brc_ref[:, 0:1]
    hi, mid, lo = _split3(_log_sigmoid(z) / GATE_TAU)
    m = jnp.concatenate([cum_mask, same], axis=1).astype(BF16)
    both = (jnp.dot(hi, m, preferred_element_type=F32) + jnp.dot(mid, m, preferred_element_type=F32)
            + jnp.dot(lo, m, preferred_element_type=F32))
    cum = both[:, :GLA_TILE]
    tot = both[:, GLA_TILE:]
    k_end = (k_ref[...].astype(F32).T * jnp.exp(tot - cum)).astype(BF16)
    lane_chunk = jnp.right_shift(lax.broadcasted_iota(jnp.int32, (1, GLA_TILE), 1),
                                 int(math.log2(CHUNK)))
    upd = []
    for h in range(B_HEADS):
        kh = k_end[h * B_KEY_DIM:(h + 1) * B_KEY_DIM]
        lhs = jnp.concatenate([jnp.where(lane_chunk == c, kh, 0.0).astype(BF16) for c in range(CPT)],
                              axis=0)
        upd.append(jnp.dot(lhs, v_ref[:, h * B_VAL_DIM:(h + 1) * B_VAL_DIM],
                           preferred_element_type=F32))
    for c in order:
        s_out_ref[0, c] = state_ref[...].astype(BF16)
        upd_c = jnp.concatenate([u[c * B_KEY_DIM:(c + 1) * B_KEY_DIM] for u in upd], axis=0)
        decay = jnp.exp(tot[:, c * CHUNK:c * CHUNK + 1])
        state_ref[...] = decay * state_ref[...] + upd_c


def _gla_state_kernel(kf_ref, vf_ref, rf_ref, kb_ref, vb_ref, rb_ref,
                      wrf_ref, brf_ref, wrb_ref, brb_ref,
                      sf_ref, sb_ref, stf_ref, stb_ref):
    @pl.when(pl.program_id(1) == 0)
    def _reset():
        stf_ref[...] = jnp.zeros_like(stf_ref)
        stb_ref[...] = jnp.zeros_like(stb_ref)

    same, lower, upper = _chunk_masks()
    _gla_state_dir(kf_ref, vf_ref, rf_ref, wrf_ref, brf_ref, sf_ref, stf_ref,
                   range(CPT), upper, same)
    _gla_state_dir(kb_ref, vb_ref, rb_ref, wrb_ref, brb_ref, sb_ref, stb_ref,
                   range(CPT - 1, -1, -1), lower, same)


def _gla_states(proj, wrf_t, brf_c, wrb_t, brb_c, bsz, seq):
    nt = seq // GLA_TILE
    nchunks = seq // CHUNK
    hk = B_HEADS * B_KEY_DIM
    ck, cv, cr = COL_BK // QB_W, COL_BV // VB_W, COL_R // LANE

    def fwd(b, i):
        return b * nt + i

    def bwd(b, i):
        return b * nt + (nt - 1 - i)

    def tile_specs(f):
        return [pl.BlockSpec((GLA_TILE, QB_W), lambda b, i: (f(b, i), ck)),
                pl.BlockSpec((GLA_TILE, VB_W), lambda b, i: (f(b, i), cv)),
                pl.BlockSpec((GLA_TILE, LANE), lambda b, i: (f(b, i), cr))]

    const = lambda b, i: (0, 0)
    out_sds = jax.ShapeDtypeStruct((bsz, nchunks, hk, B_VAL_DIM), BF16)
    return pl.pallas_call(
        _gla_state_kernel,
        grid=(bsz, nt),
        in_specs=tile_specs(fwd) + tile_specs(bwd) + [
            pl.BlockSpec((hk, LANE), const), pl.BlockSpec((hk, 1), const),
            pl.BlockSpec((hk, LANE), const), pl.BlockSpec((hk, 1), const)],
        out_specs=[pl.BlockSpec((1, CPT, hk, B_VAL_DIM), lambda b, i: (b, i, 0, 0)),
                   pl.BlockSpec((1, CPT, hk, B_VAL_DIM), lambda b, i: (b, nt - 1 - i, 0, 0))],
        out_shape=[out_sds, out_sds],
        scratch_shapes=[pltpu.VMEM((hk, B_VAL_DIM), F32), pltpu.VMEM((hk, B_VAL_DIM), F32)],
        compiler_params=_cparams(2),
        name="gla_state",
    )(proj, proj, proj, proj, proj, proj, wrf_t, brf_c, wrb_t, brb_c)


def _gla_out_kernel(q_ref, k_ref, v_ref, r_ref, g_ref, sf_ref, sb_ref,
                    wrf_ref, brf_ref, wrb_ref, brb_ref, ng_ref, o_ref):
    same, lower, upper = _chunk_masks()
    r = r_ref[...]
    cum_f = _mask_dot(lower, _log_decay(r, wrf_ref[...], brf_ref[...]))
    cum_b = _mask_dot(upper, _log_decay(r, wrb_ref[...], brb_ref[...]))
    q = q_ref[...].astype(F32) * (B_KEY_DIM ** -0.5)
    k = k_ref[...].astype(F32)
    qd_f = q * jnp.exp(cum_f)
    qd_b = q * jnp.exp(cum_b)
    ki_f = (k * jnp.exp(-cum_f)).astype(BF16)
    ki_b = (k * jnp.exp(-cum_b)).astype(BF16)
    lane_head = jnp.right_shift(
        lax.broadcasted_iota(jnp.int32, (1, B_HEADS * B_KEY_DIM), 1), int(math.log2(B_KEY_DIM)))

    def stack_heads(x):
        return jnp.concatenate(
            [jnp.where(lane_head == h, x, 0.0) for h in range(B_HEADS)], axis=0).astype(BF16)

    qs_f = stack_heads(qd_f)
    qs_b = stack_heads(qd_b)
    nt_dims = (((1,), (1,)), ((), ()))
    sc_f = lax.dot_general(qs_f, ki_f, nt_dims, preferred_element_type=F32)
    sc_b = lax.dot_general(qs_b, ki_b, nt_dims, preferred_element_type=F32)
    lower_s = jnp.concatenate([lower] * B_HEADS, axis=0)
    strict_upper_s = jnp.concatenate([upper & ~lower] * B_HEADS, axis=0)
    p = jnp.where(lower_s, sc_f, jnp.where(strict_upper_s, sc_b, 0.0)).astype(BF16)
    v = v_ref[...]
    inter = []
    for c in range(CPT):
        lhs_f = jnp.concatenate(
            [qs_f[h * GLA_TILE + c * CHUNK:h * GLA_TILE + (c + 1) * CHUNK] for h in range(B_HEADS)], axis=0)
        lhs_b = jnp.concatenate(
            [qs_b[h * GLA_TILE + c * CHUNK:h * GLA_TILE + (c + 1) * CHUNK] for h in range(B_HEADS)], axis=0)
        inter.append(jnp.dot(lhs_f, sf_ref[0, c], preferred_element_type=F32)
                     + jnp.dot(lhs_b, sb_ref[0, c], preferred_element_type=F32))
    ng = ng_ref[...]
    g = g_ref[...].astype(F32)
    for h in range(B_HEADS):
        vs = slice(h * B_VAL_DIM, (h + 1) * B_VAL_DIM)
        o = jnp.dot(p[h * GLA_TILE:(h + 1) * GLA_TILE], v[:, vs], preferred_element_type=F32)
        o = o + jnp.concatenate([inter[c][h * CHUNK:(h + 1) * CHUNK] for c in range(CPT)], axis=0)
        o = o * lax.rsqrt(jnp.mean(o * o, axis=-1, keepdims=True) + EPS) * ng
        gh = g[:, vs]
        o_ref[:, vs] = (o * (gh / (1.0 + jnp.exp(-gh)))).astype(BF16)


def _gla_out(proj, s_f, s_b, wrf, brf, wrb, brb, ng, bsz, seq):
    nt = seq // GLA_TILE
    t = bsz * seq
    hk = B_HEADS * B_KEY_DIM
    const = lambda b, i: (0, 0)
    row = lambda c: (lambda b, i: (b * nt + i, c))
    return pl.pallas_call(
        _gla_out_kernel,
        grid=(bsz, nt),
        in_specs=[
            pl.BlockSpec((GLA_TILE, QB_W), row(COL_BQ // QB_W)),
            pl.BlockSpec((GLA_TILE, QB_W), row(COL_BK // QB_W)),
            pl.BlockSpec((GLA_TILE, VB_W), row(COL_BV // VB_W)),
            pl.BlockSpec((GLA_TILE, LANE), row(COL_R // LANE)),
            pl.BlockSpec((GLA_TILE, VB_W), row(COL_BG // VB_W)),
            pl.BlockSpec((1, CPT, hk, B_VAL_DIM), lambda b, i: (b, i, 0, 0)),
            pl.BlockSpec((1, CPT, hk, B_VAL_DIM), lambda b, i: (b, i, 0, 0)),
            pl.BlockSpec((LANE, hk), const), pl.BlockSpec((1, hk), const),
            pl.BlockSpec((LANE, hk), const), pl.BlockSpec((1, hk), const),
            pl.BlockSpec((1, B_VAL_DIM), const),
        ],
        out_specs=pl.BlockSpec((GLA_TILE, VB_W), lambda b, i: (b * nt + i, 0)),
        out_shape=jax.ShapeDtypeStruct((t, VB_W), BF16),
        compiler_params=_cparams(2),
        name="gla_out",
    )(proj, proj, proj, proj, proj, s_f, s_b, wrf, brf, wrb, brb, ng)


def _sigmoid(x):
    return 1.0 / (1.0 + jnp.exp(-x))


def _merge_kernel(ya_ref, yb_ref, ga_ref, gb_ref, x_ref, wa_ref, wb_ref, wo_ref, o_ref,
                  wa_bf, wb_bf, wo_bf):
    @pl.when(pl.program_id(0) == 0)
    def _prep():
        wa_bf[...] = wa_ref[0].astype(BF16)
        wb_bf[...] = wb_ref[0].astype(BF16)
        wo_bf[...] = wo_ref[0].astype(BF16)

    a = jnp.dot(ya_ref[...], wa_bf[...], preferred_element_type=F32)
    b = jnp.dot(yb_ref[...], wb_bf[...], preferred_element_type=F32)
    merged = _sigmoid(ga_ref[...].astype(F32)) * a + _sigmoid(gb_ref[...].astype(F32)) * b
    o_ref[...] = x_ref[...] + jnp.dot(merged.astype(BF16), wo_bf[...], preferred_element_type=F32)


def _merge(ya, yb, proj, x2, wa_all, wb_all, wo_all, layer):
    t = x2.shape[0]
    tm = min(ROW_TILE, t)
    wsel = lambda i: (layer, 0, 0)
    return pl.pallas_call(
        _merge_kernel,
        grid=(t // tm,),
        in_specs=[
            pl.BlockSpec((tm, QA_W), lambda i: (i, 0)),
            pl.BlockSpec((tm, VB_W), lambda i: (i, 0)),
            pl.BlockSpec((tm, D_MODEL), lambda i: (i, COL_GA // D_MODEL)),
            pl.BlockSpec((tm, D_MODEL), lambda i: (i, COL_GB // D_MODEL)),
            pl.BlockSpec((tm, D_MODEL), lambda i: (i, 0)),
            pl.BlockSpec((1, QA_W, D_MODEL), wsel, pipeline_mode=pl.Buffered(1)),
            pl.BlockSpec((1, VB_W, D_MODEL), wsel, pipeline_mode=pl.Buffered(1)),
            pl.BlockSpec((1, D_MODEL, D_MODEL), wsel, pipeline_mode=pl.Buffered(1)),
        ],
        out_specs=pl.BlockSpec((tm, D_MODEL), lambda i: (i, 0)),
        out_shape=jax.ShapeDtypeStruct((t, D_MODEL), F32),
        scratch_shapes=[pltpu.VMEM((QA_W, D_MODEL), BF16), pltpu.VMEM((VB_W, D_MODEL), BF16),
                        pltpu.VMEM((D_MODEL, D_MODEL), BF16)],
        compiler_params=_cparams(1),
        name="merge",
    )(ya, yb, proj, proj, x2, wa_all, wb_all, wo_all)


def _moe_tile_tokens(t):
    return min(4096, t)


MOE_BLK = 512
MOE_UNIT = 16
MOE_CAP = 1536
MOE_CAP_COMMON = 1280
XS_W = D_MODEL + LANE
MOE_MT = 384
MOE_CH = 2 * MOE_MT
MOE_CHU = MOE_CH // MOE_UNIT
MOE_NC = 256
MOE_EPS = 2
MOE_ESTEPS = N_EXPERTS // MOE_EPS
assert MOE_CAP >= 2 * MOE_BLK + N_EXPERTS * (MOE_UNIT - 1) and MOE_CAP % MOE_UNIT == 0


NT_DIMS = (((1,), (1,)), ((), ()))


def _route(hn_bf, wrt_ref, brc_ref):
    logits = lax.dot_general(wrt_ref[...].astype(BF16), hn_bf, NT_DIMS,
                             preferred_element_type=F32) + brc_ref[:, 0:1]
    row = lax.broadcasted_iota(jnp.int32, logits.shape, 0).astype(F32)
    big = float(LANE)
    ninf = -jnp.inf
    gl = jnp.where(row < N_GROUPS, logits, ninf)
    gmax = jnp.max(gl, axis=0, keepdims=True)
    g_idx = jnp.min(jnp.where(gl == gmax, row, big), axis=0, keepdims=True)
    g_w = 1.0 / jnp.sum(jnp.exp(gl - gmax), axis=0, keepdims=True)
    lo = float(N_GROUPS) + g_idx * float(EXPERTS_PER_GROUP)
    el = jnp.where((row >= lo) & (row < lo + EXPERTS_PER_GROUP), logits, ninf)
    v1 = jnp.max(el, axis=0, keepdims=True)
    i1 = jnp.min(jnp.where(el == v1, row, big), axis=0, keepdims=True)
    el2 = jnp.where(row == i1, ninf, el)
    v2 = jnp.max(el2, axis=0, keepdims=True)
    i2 = jnp.min(jnp.where(el2 == v2, row, big), axis=0, keepdims=True)
    e21 = jnp.exp(v2 - v1)
    w1 = g_w / (1.0 + e21)
    w2 = g_w * e21 / (1.0 + e21)
    return i1 - float(N_GROUPS), i2 - float(N_GROUPS), w1, w2, row


def _slot_matrix(pos1, pos2, second):
    slot = lax.broadcasted_iota(jnp.int32, (pos1.shape[0], MOE_CAP), 1)
    return jnp.where(slot == pos1.astype(jnp.int32), 1.0,
                     jnp.where(slot == pos2.astype(jnp.int32), second, 0.0)).astype(BF16)


def _slot_matrix_t(pos1, pos2, second):
    slot = lax.broadcasted_iota(jnp.int32, (MOE_CAP, pos1.shape[1]), 0)
    return jnp.where(slot == pos1.astype(jnp.int32), 1.0,
                     jnp.where(slot == pos2.astype(jnp.int32), second, 0.0)).astype(BF16)


def _split3_f32(w):
    hi = w.astype(BF16).astype(F32)
    r1 = w - hi
    mid = r1.astype(BF16).astype(F32)
    return hi, mid, r1 - mid


def _rows_to_sublanes(vals, row):
    out = jnp.zeros(row.shape, F32)
    for k, v in enumerate(vals):
        out = jnp.where(row == float(k), v, out)
    return out


def _moe_kernel(x_ref, g_ref, wrt_ref, brc_ref, wg_ref, wu_ref, wd_ref, gf_ref, o_ref,
                xs_ref, pos_ref, tab_v, tab_s, xt_ref, addr_ref, sem,
                *, nblk, final_norm):
    s = pl.program_id(1)
    dump_row = nblk * MOE_CAP

    @pl.when(s < nblk)
    def _dispatch():
        @pl.when(s == 0)
        def _init():
            xs_ref[dump_row:dump_row + MOE_UNIT, :] = jnp.zeros((MOE_UNIT, XS_W), BF16)
            xt_ref[...] = jnp.zeros_like(xt_ref)

        x = x_ref[...]
        ms = jnp.mean(x * x, axis=-1, keepdims=True)
        hn = (x * lax.rsqrt(ms + EPS) * g_ref[0]).astype(BF16)
        e1, e2, w1, w2, row = _route(hn, wrt_ref, brc_ref)
        onehot = ((row == e1) | (row == e2)).astype(F32)
        earlier = (lax.broadcasted_iota(jnp.int32, (MOE_BLK, MOE_BLK), 0)
                   < lax.broadcasted_iota(jnp.int32, (MOE_BLK, MOE_BLK), 1)).astype(BF16)
        before = jnp.dot(onehot.astype(BF16), earlier, preferred_element_type=F32)
        counts = jnp.sum(onehot, axis=1, keepdims=True)
        units = jnp.floor((counts + (MOE_UNIT - 1.0)) * (1.0 / MOE_UNIT))
        lower = (lax.broadcasted_iota(jnp.int32, (LANE, LANE), 1)
                 < lax.broadcasted_iota(jnp.int32, (LANE, LANE), 0)).astype(BF16)
        padded = jnp.broadcast_to(units * MOE_UNIT, (LANE, LANE))
        run_off = jnp.dot(lower, padded.astype(BF16), preferred_element_type=F32)[:, 0:1]
        start = run_off + before
        pos1 = jnp.sum(jnp.where(row == e1, start, 0.0), axis=0, keepdims=True)
        pos2 = jnp.sum(jnp.where(row == e2, start, 0.0), axis=0, keepdims=True)
        w_rows = _rows_to_sublanes(
            _split3_f32(w1) + _split3_f32(w2) + (jnp.ones_like(w1),), row).astype(BF16)
        used = jnp.sum(units, axis=0, keepdims=True) * MOE_UNIT
        pt_t = _slot_matrix_t(pos1, pos2, 2.0)
        rows = pl.ds(pl.multiple_of(s * MOE_CAP, MOE_CAP), MOE_CAP)
        for c in range(D_MODEL // MOE_NC):
            cs = slice(c * MOE_NC, (c + 1) * MOE_NC)
            xs_ref[rows, cs] = jnp.dot(pt_t, hn[:, cs], preferred_element_type=F32).astype(BF16)
        xs_ref[rows, D_MODEL:] = lax.dot_general(pt_t, w_rows, NT_DIMS,
                                                 preferred_element_type=F32).astype(BF16)
        pos_ref[s] = _rows_to_sublanes((pos1, pos2), row).T
        lane_e = lax.broadcasted_iota(jnp.int32, (LANE, LANE), 1)
        per_expert = jnp.where(lane_e == 0, run_off,
                               jnp.where(lane_e == 1, units, jnp.where(lane_e == 2, used, 0.0)))
        tab_v[s] = per_expert.T[0:SUBLANE, :].astype(jnp.int32)

        @pl.when(s == nblk - 1)
        def _publish():
            cp = pltpu.make_async_copy(tab_v, tab_s, sem)
            cp.start()
            cp.wait()

    def one_expert(e, wi):
        total = tab_s[0, 1, e]
        for j in range(1, nblk):
            total = total + tab_s[j, 1, e]

        def chunk(c, carry):
            u0 = c * MOE_CHU
            for k in range(MOE_CHU):
                addr_ref[k] = dump_row
            cum = 0
            for j in range(nblk):
                nj = tab_s[j, 1, e]
                run0 = j * MOE_CAP + tab_s[j, 0, e]
                k_off = cum - u0

                def copy_unit(u, c2, run0=run0, k_off=k_off):
                    src = pl.multiple_of(run0 + u * MOE_UNIT, MOE_UNIT)
                    dst = pl.multiple_of((k_off + u) * MOE_UNIT, MOE_UNIT)
                    xt_ref[pl.ds(dst, MOE_UNIT), :] = xs_ref[pl.ds(src, MOE_UNIT), :]
                    addr_ref[k_off + u] = src
                    return c2
                lax.fori_loop(jnp.clip(u0 - cum, 0, nj), jnp.clip(u0 + MOE_CHU - cum, 0, nj),
                              copy_unit, 0)
                cum = cum + nj
            rows_here = jnp.minimum(MOE_CHU, total - u0) * MOE_UNIT

            def tile(t, c3):
                xt = xt_ref[pl.ds(pl.multiple_of(t * MOE_MT, MOE_MT), MOE_MT), :]
                wl = xt[:, D_MODEL:].astype(F32)
                second = wl[:, 6:7] > 1.5
                wcol = jnp.where(second, 0.5 * (wl[:, 3:4] + wl[:, 4:5] + wl[:, 5:6]),
                                 wl[:, 0:1] + wl[:, 1:2] + wl[:, 2:3])
                undo = jnp.where(second, 0.5, 1.0)
                xf = xt[:, :D_MODEL]
                gate = jnp.dot(xf, wg_ref[wi], preferred_element_type=F32) * undo
                up = jnp.dot(xf, wu_ref[wi], preferred_element_type=F32) * undo
                h = gate * _sigmoid(gate) * up * wcol
                y = jnp.dot(h.astype(BF16), wd_ref[wi], preferred_element_type=F32).astype(BF16)
                for uu in range(MOE_MT // MOE_UNIT):
                    dst = pl.multiple_of(addr_ref[t * (MOE_MT // MOE_UNIT) + uu], MOE_UNIT)
                    xs_ref[pl.ds(dst, MOE_UNIT), :D_MODEL] = y[uu * MOE_UNIT:(uu + 1) * MOE_UNIT]
                return c3
            lax.fori_loop(0, (rows_here + MOE_MT - 1) // MOE_MT, tile, 0)
            return carry
        lax.fori_loop(0, (total + MOE_CHU - 1) // MOE_CHU, chunk, 0)

    @pl.when((s >= nblk) & (s < nblk + MOE_ESTEPS))
    def _experts():
        for wi in range(MOE_EPS):
            one_expert((s - nblk) * MOE_EPS + wi, wi)

    @pl.when(s >= nblk + MOE_ESTEPS)
    def _combine():
        j = s - nblk - MOE_ESTEPS
        p = pos_ref[j]
        pt = _slot_matrix(p[:, 0:1], p[:, 1:2], 1.0)
        base = pl.multiple_of(j * MOE_CAP, MOE_CAP)
        head = pl.ds(base, MOE_CAP_COMMON)
        for c in range(D_MODEL // MOE_NC):
            cs = slice(c * MOE_NC, (c + 1) * MOE_NC)
            o_ref[:, cs] = x_ref[:, cs] + jnp.dot(pt[:, :MOE_CAP_COMMON], xs_ref[head, cs],
                                                  preferred_element_type=F32)

        @pl.when(tab_s[j, 2, 0] > MOE_CAP_COMMON)
        def _tail():
            tail = pl.ds(base + MOE_CAP_COMMON, MOE_CAP - MOE_CAP_COMMON)
            for c in range(D_MODEL // MOE_NC):
                cs = slice(c * MOE_NC, (c + 1) * MOE_NC)
                o_ref[:, cs] += jnp.dot(pt[:, MOE_CAP_COMMON:], xs_ref[tail, cs],
                                        preferred_element_type=F32)
        if final_norm:
            y = o_ref[...]
            ms = jnp.mean(y * y, axis=-1, keepdims=True)
            o_ref[...] = y * lax.rsqrt(ms + EPS) * gf_ref[...]


def _moe(x2, g_all, w_r, b_r, wg_all, wu_all, wd_all, g_final, layer, tt, final_norm):
    t = x2.shape[0]
    nblk = tt // MOE_BLK
    steps = 2 * nblk + MOE_ESTEPS

    def tok(b, s):
        j = jnp.where(s < nblk, s, jnp.where(s < nblk + MOE_ESTEPS, nblk - 1, s - nblk - MOE_ESTEPS))
        return (b * nblk + j, 0)

    def out_tok(b, s):
        return (b * nblk + jnp.maximum(s - nblk - MOE_ESTEPS, 0), 0)

    wsel = lambda b, s: (jnp.clip(s - nblk, 0, MOE_ESTEPS - 1), 0, 0)
    const = lambda b, s: (0, 0)
    return pl.pallas_call(
        functools.partial(_moe_kernel, nblk=nblk, final_norm=final_norm),
        grid=(t // tt, steps),
        in_specs=[
            pl.BlockSpec((MOE_BLK, D_MODEL), tok),
            pl.BlockSpec((1, 1, D_MODEL), lambda b, s: (layer, 0, 0)),
            pl.BlockSpec((LANE, D_MODEL), const),
            pl.BlockSpec((LANE, 1), const),
            pl.BlockSpec((MOE_EPS, D_MODEL, D_EXPERT), wsel),
            pl.BlockSpec((MOE_EPS, D_MODEL, D_EXPERT), wsel),
            pl.BlockSpec((MOE_EPS, D_EXPERT, D_MODEL), wsel),
            pl.BlockSpec((1, D_MODEL), const),
        ],
        out_specs=pl.BlockSpec((MOE_BLK, D_MODEL), out_tok),
        out_shape=jax.ShapeDtypeStruct((t, D_MODEL), F32),
        scratch_shapes=[
            pltpu.VMEM((nblk * MOE_CAP + MOE_UNIT, XS_W), BF16),
            pltpu.VMEM((nblk, MOE_BLK, LANE), F32),
            pltpu.VMEM((nblk, SUBLANE, LANE), jnp.int32),
            pltpu.SMEM((nblk, SUBLANE, LANE), jnp.int32),
            pltpu.VMEM((MOE_CH, XS_W), BF16),
            pltpu.SMEM((MOE_CHU,), jnp.int32),
            pltpu.SemaphoreType.DMA,
        ],
        compiler_params=_cparams(2),
        name="moe",
    )(x2, g_all, w_r, b_r, wg_all, wu_all, wd_all, g_final)


def _t5_bucket(rel):
    nb = N_BUCKETS // 2
    max_exact = nb // 2
    base = jnp.where(rel > 0, nb, 0)
    n = jnp.abs(rel)
    large = max_exact + (jnp.log(jnp.maximum(n, 1).astype(jnp.float32) / max_exact)
                         / math.log(MAX_DISTANCE / max_exact) * (nb - max_exact)).astype(jnp.int32)
    large = jnp.minimum(large, nb - 1)
    return base + jnp.where(n < max_exact, n, large)


def _pad_rank(wr):
    return jnp.pad(wr, ((0, LANE - B_RANK), (0, 0))).astype(BF16)


def kernel(x, w_in, rel_bias, attn_sink, gla_wr_fwd, gla_br_fwd, gla_wr_bwd, gla_br_bwd, gla_norm, w_branch_a, w_branch_b, w_out, norm_mix, norm_ffn, router_group_w, router_group_b, router_expert_w, router_expert_b, expert_w_gate, expert_w_up, expert_w_down, norm_final):
    bsz, seq, d = x.shape
    depth = w_in.shape[0]
    t = bsz * seq
    tt = _moe_tile_tokens(t)
    q_off = jnp.arange(BLOCK)
    k_off = jnp.arange(3 * BLOCK) - BLOCK
    bucket = _t5_bucket(k_off[None, :] - q_off[:, None]).astype(jnp.int32) & (N_BUCKETS - 1)
    x2 = x.reshape(t, d)
    norm_mix3 = norm_mix[:, None, :]
    w_in_t = jnp.swapaxes(w_in, 1, 2)
    norm_ffn3 = norm_ffn[:, None, :]
    wg_all = expert_w_gate.reshape(depth * N_EXPERTS, d, D_EXPERT)
    wu_all = expert_w_up.reshape(depth * N_EXPERTS, d, D_EXPERT)
    wd_all = expert_w_down.reshape(depth * N_EXPERTS, D_EXPERT, d)
    for l in range(depth):
        proj, wg_bf, wu_bf, wd_bf = _inproj(x2, norm_mix3, w_in_t, wg_all, wu_all, wd_all, l)
        ya = _attention(proj, bucket, rel_bias, attn_sink[l], bsz, seq)
        wrf, wrb = _pad_rank(gla_wr_fwd[l]), _pad_rank(gla_wr_bwd[l])
        brf, brb = gla_br_fwd[l][None, :], gla_br_bwd[l][None, :]
        s_f, s_b = _gla_states(proj, wrf.T, brf.T, wrb.T, brb.T, bsz, seq)
        yb = _gla_out(proj, s_f, s_b, wrf, brf, wrb, brb, gla_norm[l][None, :], bsz, seq)
        x2 = _merge(ya, yb, proj, x2, w_branch_a, w_branch_b, w_out, l)
        w_r = jnp.concatenate(
            [router_group_w[l].T, router_expert_w[l].T,
             jnp.zeros((LANE - N_GROUPS - N_EXPERTS, d), F32)], axis=0)
        b_r = jnp.concatenate(
            [router_group_b[l], router_expert_b[l],
             jnp.zeros((LANE - N_GROUPS - N_EXPERTS,), F32)])[:, None]
        x2 = _moe(x2, norm_ffn3, w_r, b_r, wg_bf, wu_bf, wd_bf, norm_final[None, :], l, tt,
                  final_norm=(l == depth - 1))
    return x2.reshape(bsz, seq, d)
```

```python
import functools
import math

import numpy as np
import jax
import jax.numpy as jnp
from jax import lax
from jax.experimental import pallas as pl
from jax.experimental.pallas import tpu as pltpu

F32 = jnp.float32
BF16 = jnp.bfloat16

D_MODEL = 1024
A_HEADS = 8
A_KV_HEADS = 2
A_HEAD_DIM = 64
A_GROUP = A_HEADS // A_KV_HEADS
WINDOW = 128
BLOCK = 128
N_BUCKETS = 32
MAX_DISTANCE = 128
B_HEADS = 4
B_KEY_DIM = 64
B_VAL_DIM = 128
B_RANK = 16
GATE_TAU = 16.0
CHUNK = 64
N_GROUPS = 4
EXPERTS_PER_GROUP = 8
N_EXPERTS = N_GROUPS * EXPERTS_PER_GROUP
D_EXPERT = 256
EPS = 1e-6
NEG_INF = -1e30

LANE = 128
SUBLANE = 8
VMEM_LIMIT = 56 * 1024 * 1024

QA_W = A_HEADS * A_HEAD_DIM
KA_W = A_KV_HEADS * A_HEAD_DIM
QB_W = B_HEADS * B_KEY_DIM
VB_W = B_HEADS * B_VAL_DIM
COL_GA = 0
COL_GB = COL_GA + D_MODEL
COL_AQ = COL_GB + D_MODEL
COL_BV = COL_AQ + QA_W
COL_BG = COL_BV + VB_W
COL_BQ = COL_BG + VB_W
COL_BK = COL_BQ + QB_W
COL_AK = COL_BK + QB_W
COL_AV = COL_AK + KA_W
COL_R = COL_AV + KA_W
PROJ_W = COL_R + LANE

ROW_TILE = 512
PROJ_CHUNK = 1280
GLA_TILE = 256
CPT = GLA_TILE // CHUNK


def _cparams(n_axes):
    return pltpu.CompilerParams(
        dimension_semantics=("arbitrary",) * n_axes, vmem_limit_bytes=VMEM_LIMIT)


_W_IN_SRC = np.cumsum([0, QA_W, KA_W, KA_W, QB_W, QB_W, VB_W, VB_W, B_RANK, D_MODEL, D_MODEL])
_W_IN_SEGMENTS = tuple(zip(
    (COL_AQ, COL_AK, COL_AV, COL_BQ, COL_BK, COL_BV, COL_BG, COL_R, COL_GA, COL_GB),
    (int(s) for s in _W_IN_SRC[:-1]),
    (int(w) for w in np.diff(_W_IN_SRC))))
W_IN_COLS = int(_W_IN_SRC[-1])
W_ROWS_PER_COPY = 256


def _load_permuted_w_in(wt_ref, wbf_ref):
    wbf_ref[COL_R:COL_R + LANE, :] = jnp.zeros((LANE, D_MODEL), BF16)
    for dst, src, width in _W_IN_SEGMENTS:
        for r0 in range(0, width, W_ROWS_PER_COPY):
            n = min(W_ROWS_PER_COPY, width - r0)
            wbf_ref[dst + r0:dst + r0 + n, :] = wt_ref[0, src + r0:src + r0 + n, :].astype(BF16)


def _inproj_kernel(x_ref, g_ref, wt_ref, wg_ref, wu_ref, wd_ref, o_ref, wgb_ref, wub_ref, wdb_ref, wbf_ref):
    @pl.when(pl.program_id(0) == 0)
    def _prep():
        _load_permuted_w_in(wt_ref, wbf_ref)

    wgb_ref[...] = wg_ref[...].astype(BF16)
    wub_ref[...] = wu_ref[...].astype(BF16)
    wdb_ref[...] = wd_ref[...].astype(BF16)

    x = x_ref[...]
    ms = jnp.mean(x * x, axis=-1, keepdims=True)
    xn = (x * lax.rsqrt(ms + EPS) * g_ref[0]).astype(BF16)
    for c0 in range(0, PROJ_W, PROJ_CHUNK):
        sl = slice(c0, min(c0 + PROJ_CHUNK, PROJ_W))
        o_ref[:, sl] = lax.dot_general(xn, wbf_ref[sl, :], (((1,), (1,)), ((), ())),
                                       preferred_element_type=F32).astype(BF16)


def _inproj(x2, g_all, wt_all, wg_all, wu_all, wd_all, layer):
    t = x2.shape[0]
    tm = min(ROW_TILE, t)
    nsteps = t // tm
    epi = -(-N_EXPERTS // nsteps)
    while N_EXPERTS % epi:
        epi += 1
    nwb = N_EXPERTS // epi
    wsel = lambda i: (layer * nwb + jnp.minimum(i, nwb - 1), 0, 0)
    wout = lambda i: (jnp.minimum(i, nwb - 1), 0, 0)
    return pl.pallas_call(
        _inproj_kernel,
        grid=(nsteps,),
        in_specs=[
            pl.BlockSpec((tm, D_MODEL), lambda i: (i, 0)),
            pl.BlockSpec((1, 1, D_MODEL), lambda i: (layer, 0, 0)),
            pl.BlockSpec((1, W_IN_COLS, D_MODEL), lambda i: (layer, 0, 0), pipeline_mode=pl.Buffered(1)),
            pl.BlockSpec((epi, D_MODEL, D_EXPERT), wsel),
            pl.BlockSpec((epi, D_MODEL, D_EXPERT), wsel),
            pl.BlockSpec((epi, D_EXPERT, D_MODEL), wsel),
        ],
        out_specs=[
            pl.BlockSpec((tm, PROJ_W), lambda i: (i, 0)),
            pl.BlockSpec((epi, D_MODEL, D_EXPERT), wout),
            pl.BlockSpec((epi, D_MODEL, D_EXPERT), wout),
            pl.BlockSpec((epi, D_EXPERT, D_MODEL), wout),
        ],
        out_shape=[
            jax.ShapeDtypeStruct((t, PROJ_W), BF16),
            jax.ShapeDtypeStruct((N_EXPERTS, D_MODEL, D_EXPERT), BF16),
            jax.ShapeDtypeStruct((N_EXPERTS, D_MODEL, D_EXPERT), BF16),
            jax.ShapeDtypeStruct((N_EXPERTS, D_EXPERT, D_MODEL), BF16),
        ],
        scratch_shapes=[pltpu.VMEM((PROJ_W, D_MODEL), BF16)],
        compiler_params=_cparams(1),
        name="inproj",
    )(x2, g_all, wt_all, wg_all, wu_all, wd_all)


ATT_QB = 2


def _attn_kernel(bucket_ref, relb_ref, sink_ref, q_ref, kp_ref, kc_ref, kn_ref,
                 vp_ref, vc_ref, vn_ref, o_ref, bias_ref, logit_ref, prob_ref, *, nblk):
    b = pl.program_id(0)
    n = pl.program_id(1)
    kw = 3 * BLOCK

    @pl.when((b == 0) & (n == 0))
    def _build_bias():
        bucket = bucket_ref[...]
        row = lax.broadcasted_iota(jnp.int32, (BLOCK, kw), 0)
        col = lax.broadcasted_iota(jnp.int32, (BLOCK, kw), 1)
        win = jnp.abs(col - BLOCK - row) <= WINDOW
        keep = (win & (col >= BLOCK), win, win & (col < 2 * BLOCK))
        for hq in range(A_HEADS):
            acc = jnp.zeros((BLOCK, kw), F32)
            for bb in range(N_BUCKETS):
                acc = jnp.where(bucket == bb, relb_ref[bb, hq], acc)
            for v in range(3):
                bias_ref[v, hq] = jnp.where(keep[v], acc, NEG_INF)

    q = (q_ref[...].astype(F32) * (A_HEAD_DIM ** -0.5)).astype(BF16)
    kall = jnp.concatenate([kp_ref[...], kc_ref[...], kn_ref[...]], axis=0)
    vall = jnp.concatenate([vp_ref[...], vc_ref[...], vn_ref[...]], axis=0)
    jobs = [(j, hq) for j in range(ATT_QB) for hq in range(A_HEADS)]

    def keys(x, j, hq):
        h = hq // A_GROUP
        return x[j * BLOCK:(j + 3) * BLOCK, h * A_HEAD_DIM:(h + 1) * A_HEAD_DIM]

    for i, (j, hq) in enumerate(jobs):
        logit_ref[i] = lax.dot_general(
            q[j * BLOCK:(j + 1) * BLOCK, hq * A_HEAD_DIM:(hq + 1) * A_HEAD_DIM], keys(kall, j, hq),
            (((1,), (1,)), ((), ())), preferred_element_type=F32)
    dens = []
    for i, (j, hq) in enumerate(jobs):
        blk = n * ATT_QB + j
        variant = jnp.where(blk == 0, 0, jnp.where(blk == nblk - 1, 2, 1))
        s = logit_ref[i] + bias_ref[variant, hq]
        sk = sink_ref[hq]
        m = jnp.maximum(jnp.max(s, axis=-1, keepdims=True), sk)
        p = jnp.exp(s - m)
        dens.append(jnp.sum(p, axis=-1, keepdims=True) + jnp.exp(sk - m))
        prob_ref[i] = p.astype(BF16)
    outs = [jnp.dot(prob_ref[i], keys(vall, j, hq), preferred_element_type=F32) / dens[i]
            for i, (j, hq) in enumerate(jobs)]
    for j in range(ATT_QB):
        o_ref[j * BLOCK:(j + 1) * BLOCK, :] = jnp.concatenate(
            outs[j * A_HEADS:(j + 1) * A_HEADS], axis=1).astype(BF16)


def _attention(proj, bucket, rel_bias, sink, bsz, seq):
    nblk = seq // BLOCK
    t = bsz * seq
    cq = COL_AQ // QA_W
    ck = COL_AK // KA_W
    cv = COL_AV // KA_W

    nsteps = nblk // ATT_QB
    rows = ATT_QB * BLOCK

    def kv_specs(c):
        return [pl.BlockSpec((BLOCK, KA_W), lambda b, n: (b * nblk + jnp.maximum(n * ATT_QB - 1, 0), c)),
                pl.BlockSpec((rows, KA_W), lambda b, n: (b * nsteps + n, c)),
                pl.BlockSpec((BLOCK, KA_W),
                             lambda b, n: (b * nblk + jnp.minimum((n + 1) * ATT_QB, nblk - 1), c))]

    smem = pl.BlockSpec(memory_space=pltpu.SMEM)
    return pl.pallas_call(
        functools.partial(_attn_kernel, nblk=nblk),
        grid=(bsz, nsteps),
        in_specs=[
            pl.BlockSpec((BLOCK, 3 * BLOCK), lambda b, n: (0, 0)),
            smem, smem,
            pl.BlockSpec((rows, QA_W), lambda b, n: (b * nsteps + n, cq)),
        ] + kv_specs(ck) + kv_specs(cv),
        out_specs=pl.BlockSpec((rows, QA_W), lambda b, n: (b * nsteps + n, 0)),
        out_shape=jax.ShapeDtypeStruct((t, QA_W), BF16),
        scratch_shapes=[pltpu.VMEM((3, A_HEADS, BLOCK, 3 * BLOCK), F32),
                        pltpu.VMEM((ATT_QB * A_HEADS, BLOCK, 3 * BLOCK), F32),
                        pltpu.VMEM((ATT_QB * A_HEADS, BLOCK, 3 * BLOCK), BF16)],
        compiler_params=_cparams(2),
        name="attn",
    )(bucket, rel_bias, sink, proj, proj, proj, proj, proj, proj, proj)


def _log_sigmoid(z):
    return jnp.minimum(z, 0.0) - jnp.log(1.0 + jnp.exp(-jnp.abs(z)))


def _log_decay(r, wr, br):
    z = jnp.dot(r, wr, preferred_element_type=F32) + br
    return _log_sigmoid(z) / GATE_TAU


def _chunk_masks():
    row = lax.broadcasted_iota(jnp.int32, (GLA_TILE, GLA_TILE), 0)
    col = lax.broadcasted_iota(jnp.int32, (GLA_TILE, GLA_TILE), 1)
    sh = int(math.log2(CHUNK))
    same = jnp.right_shift(row, sh) == jnp.right_shift(col, sh)
    return same, same & (col <= row), same & (col >= row)


def _split3(x):
    hi = x.astype(BF16)
    r1 = x - hi.astype(F32)
    mid = r1.astype(BF16)
    lo = (r1 - mid.astype(F32)).astype(BF16)
    return hi, mid, lo


def _mask_dot(mask, x):
    m = mask.astype(BF16)
    hi, mid, lo = _split3(x)
    return (jnp.dot(m, hi, preferred_element_type=F32) + jnp.dot(m, mid, preferred_element_type=F32)
            + jnp.dot(m, lo, preferred_element_type=F32))


def _gla_state_dir(k_ref, v_ref, r_ref, wrt_ref, brc_ref, s_out_ref, state_ref, order, cum_mask, same):
    z = lax.dot_general(wrt_ref[...], r_ref[...], (((1,), (1,)), ((), ())),
                        preferred_element_type=F32) + brc_ref[:, 0:1]
    hi, mid, lo = _split3(_log_sigmoid(z) / GATE_TAU)
    m = jnp.concatenate([cum_mask, same], axis=1).astype(BF16)
    both = (jnp.dot(hi, m, preferred_element_type=F32) + jnp.dot(mid, m, preferred_element_type=F32)
            + jnp.dot(lo, m, preferred_element_type=F32))
    cum = both[:, :GLA_TILE]
    tot = both[:, GLA_TILE:]
    k_end = (k_ref[...].astype(F32).T * jnp.exp(tot - cum)).astype(BF16)
    lane_chunk = jnp.right_shift(lax.broadcasted_iota(jnp.int32, (1, GLA_TILE), 1),
                                 int(math.log2(CHUNK)))
    upd = []
    for h in range(B_HEADS):
        kh = k_end[h * B_KEY_DIM:(h + 1) * B_KEY_DIM]
        lhs = jnp.concatenate([jnp.where(lane_chunk == c, kh, 0.0).astype(BF16) for c in range(CPT)],
                              axis=0)
        upd.append(jnp.dot(lhs, v_ref[:, h * B_VAL_DIM:(h + 1) * B_VAL_DIM],
                           preferred_element_type=F32))
    for c in order:
        s_out_ref[0, c] = state_ref[...].astype(BF16)
        upd_c = jnp.concatenate([u[c * B_KEY_DIM:(c + 1) * B_KEY_DIM] for u in upd], axis=0)
        decay = jnp.exp(tot[:, c * CHUNK:c * CHUNK + 1])
        state_ref[...] = decay * state_ref[...] + upd_c


def _gla_state_kernel(kf_ref, vf_ref, rf_ref, kb_ref, vb_ref, rb_ref,
                      wrf_ref, brf_ref, wrb_ref, brb_ref,
                      sf_ref, sb_ref, stf_ref, stb_ref):
    @pl.when(pl.program_id(1) == 0)
    def _reset():
        stf_ref[...] = jnp.zeros_like(stf_ref)
        stb_ref[...] = jnp.zeros_like(stb_ref)

    same, lower, upper = _chunk_masks()
    _gla_state_dir(kf_ref, vf_ref, rf_ref, wrf_ref, brf_ref, sf_ref, stf_ref,
                   range(CPT), upper, same)
    _gla_state_dir(kb_ref, vb_ref, rb_ref, wrb_ref, brb_ref, sb_ref, stb_ref,
                   range(CPT - 1, -1, -1), lower, same)


def _gla_states(proj, wrf_t, brf_c, wrb_t, brb_c, bsz, seq):
    nt = seq // GLA_TILE
    nchunks = seq // CHUNK
    hk = B_HEADS * B_KEY_DIM
    ck, cv, cr = COL_BK // QB_W, COL_BV // VB_W, COL_R // LANE

    def fwd(b, i):
        return b * nt + i

    def bwd(b, i):
        return b * nt + (nt - 1 - i)

    def tile_specs(f):
        return [pl.BlockSpec((GLA_TILE, QB_W), lambda b, i: (f(b, i), ck)),
                pl.BlockSpec((GLA_TILE, VB_W), lambda b, i: (f(b, i), cv)),
                pl.BlockSpec((GLA_TILE, LANE), lambda b, i: (f(b, i), cr))]

    const = lambda b, i: (0, 0)
    out_sds = jax.ShapeDtypeStruct((bsz, nchunks, hk, B_VAL_DIM), BF16)
    return pl.pallas_call(
        _gla_state_kernel,
        grid=(bsz, nt),
        in_specs=tile_specs(fwd) + tile_specs(bwd) + [
            pl.BlockSpec((hk, LANE), const), pl.BlockSpec((hk, 1), const),
            pl.BlockSpec((hk, LANE), const), pl.BlockSpec((hk, 1), const)],
        out_specs=[pl.BlockSpec((1, CPT, hk, B_VAL_DIM), lambda b, i: (b, i, 0, 0)),
                   pl.BlockSpec((1, CPT, hk, B_VAL_DIM), lambda b, i: (b, nt - 1 - i, 0, 0))],
        out_shape=[out_sds, out_sds],
        scratch_shapes=[pltpu.VMEM((hk, B_VAL_DIM), F32), pltpu.VMEM((hk, B_VAL_DIM), F32)],
        compiler_params=_cparams(2),
        name="gla_state",
    )(proj, proj, proj, proj, proj, proj, wrf_t, brf_c, wrb_t, brb_c)


def _gla_out_kernel(q_ref, k_ref, v_ref, r_ref, g_ref, sf_ref, sb_ref,
                    wrf_ref, brf_ref, wrb_ref, brb_ref, ng_ref, o_ref):
    same, lower, upper = _chunk_masks()
    r = r_ref[...]
    cum_f = _mask_dot(lower, _log_decay(r, wrf_ref[...], brf_ref[...]))
    cum_b = _mask_dot(upper, _log_decay(r, wrb_ref[...], brb_ref[...]))
    q = q_ref[...].astype(F32) * (B_KEY_DIM ** -0.5)
    k = k_ref[...].astype(F32)
    qd_f = q * jnp.exp(cum_f)
    qd_b = q * jnp.exp(cum_b)
    ki_f = (k * jnp.exp(-cum_f)).astype(BF16)
    ki_b = (k * jnp.exp(-cum_b)).astype(BF16)
    lane_head = jnp.right_shift(
        lax.broadcasted_iota(jnp.int32, (1, B_HEADS * B_KEY_DIM), 1), int(math.log2(B_KEY_DIM)))

    def stack_heads(x):
        return jnp.concatenate(
            [jnp.where(lane_head == h, x, 0.0) for h in range(B_HEADS)], axis=0).astype(BF16)

    qs_f = stack_heads(qd_f)
    qs_b = stack_heads(qd_b)
    nt_dims = (((1,), (1,)), ((), ()))
    sc_f = lax.dot_general(qs_f, ki_f, nt_dims, preferred_element_type=F32)
    sc_b = lax.dot_general(qs_b, ki_b, nt_dims, preferred_element_type=F32)
    lower_s = jnp.concatenate([lower] * B_HEADS, axis=0)
    strict_upper_s = jnp.concatenate([upper & ~lower] * B_HEADS, axis=0)
    p = jnp.where(lower_s, sc_f, jnp.where(strict_upper_s, sc_b, 0.0)).astype(BF16)
    v = v_ref[...]
    inter = []
    for c in range(CPT):
        lhs_f = jnp.concatenate(
            [qs_f[h * GLA_TILE + c * CHUNK:h * GLA_TILE + (c + 1) * CHUNK] for h in range(B_HEADS)], axis=0)
        lhs_b = jnp.concatenate(
            [qs_b[h * GLA_TILE + c * CHUNK:h * GLA_TILE + (c + 1) * CHUNK] for h in range(B_HEADS)], axis=0)
        inter.append(jnp.dot(lhs_f, sf_ref[0, c], preferred_element_type=F32)
                     + jnp.dot(lhs_b, sb_ref[0, c], preferred_element_type=F32))
    ng = ng_ref[...]
    g = g_ref[...].astype(F32)
    for h in range(B_HEADS):
        vs = slice(h * B_VAL_DIM, (h + 1) * B_VAL_DIM)
        o = jnp.dot(p[h * GLA_TILE:(h + 1) * GLA_TILE], v[:, vs], preferred_element_type=F32)
        o = o + jnp.concatenate([inter[c][h * CHUNK:(h + 1) * CHUNK] for c in range(CPT)], axis=0)
        o = o * lax.rsqrt(jnp.mean(o * o, axis=-1, keepdims=True) + EPS) * ng
        gh = g[:, vs]
        o_ref[:, vs] = (o * (gh / (1.0 + jnp.exp(-gh)))).astype(BF16)


def _gla_out(proj, s_f, s_b, wrf, brf, wrb, brb, ng, bsz, seq):
    nt = seq // GLA_TILE
    t = bsz * seq
    hk = B_HEADS * B_KEY_DIM
    const = lambda b, i: (0, 0)
    row = lambda c: (lambda b, i: (b * nt + i, c))
    return pl.pallas_call(
        _gla_out_kernel,
        grid=(bsz, nt),
        in_specs=[
            pl.BlockSpec((GLA_TILE, QB_W), row(COL_BQ // QB_W)),
            pl.BlockSpec((GLA_TILE, QB_W), row(COL_BK // QB_W)),
            pl.BlockSpec((GLA_TILE, VB_W), row(COL_BV // VB_W)),
            pl.BlockSpec((GLA_TILE, LANE), row(COL_R // LANE)),
            pl.BlockSpec((GLA_TILE, VB_W), row(COL_BG // VB_W)),
            pl.BlockSpec((1, CPT, hk, B_VAL_DIM), lambda b, i: (b, i, 0, 0)),
            pl.BlockSpec((1, CPT, hk, B_VAL_DIM), lambda b, i: (b, i, 0, 0)),
            pl.BlockSpec((LANE, hk), const), pl.BlockSpec((1, hk), const),
            pl.BlockSpec((LANE, hk), const), pl.BlockSpec((1, hk), const),
            pl.BlockSpec((1, B_VAL_DIM), const),
        ],
        out_specs=pl.BlockSpec((GLA_TILE, VB_W), lambda b, i: (b * nt + i, 0)),
        out_shape=jax.ShapeDtypeStruct((t, VB_W), BF16),
        compiler_params=_cparams(2),
        name="gla_out",
    )(proj, proj, proj, proj, proj, s_f, s_b, wrf, brf, wrb, brb, ng)


def _sigmoid(x):
    return 1.0 / (1.0 + jnp.exp(-x))


def _merge_kernel(ya_ref, yb_ref, ga_ref, gb_ref, x_ref, wa_ref, wb_ref, wo_ref, o_ref,
                  wa_bf, wb_bf, wo_bf):
    @pl.when(pl.program_id(0) == 0)
    def _prep():
        wa_bf[...] = wa_ref[0].astype(BF16)
        wb_bf[...] = wb_ref[0].astype(BF16)
        wo_bf[...] = wo_ref[0].astype(BF16)

    a = jnp.dot(ya_ref[...], wa_bf[...], preferred_element_type=F32)
    b = jnp.dot(yb_ref[...], wb_bf[...], preferred_element_type=F32)
    merged = _sigmoid(ga_ref[...].astype(F32)) * a + _sigmoid(gb_ref[...].astype(F32)) * b
    o_ref[...] = x_ref[...] + jnp.dot(merged.astype(BF16), wo_bf[...], preferred_element_type=F32)


def _merge(ya, yb, proj, x2, wa_all, wb_all, wo_all, layer):
    t = x2.shape[0]
    tm = min(ROW_TILE, t)
    wsel = lambda i: (layer, 0, 0)
    return pl.pallas_call(
        _merge_kernel,
        grid=(t // tm,),
        in_specs=[
            pl.BlockSpec((tm, QA_W), lambda i: (i, 0)),
            pl.BlockSpec((tm, VB_W), lambda i: (i, 0)),
            pl.BlockSpec((tm, D_MODEL), lambda i: (i, COL_GA // D_MODEL)),
            pl.BlockSpec((tm, D_MODEL), lambda i: (i, COL_GB // D_MODEL)),
            pl.BlockSpec((tm, D_MODEL), lambda i: (i, 0)),
            pl.BlockSpec((1, QA_W, D_MODEL), wsel, pipeline_mode=pl.Buffered(1)),
            pl.BlockSpec((1, VB_W, D_MODEL), wsel, pipeline_mode=pl.Buffered(1)),
            pl.BlockSpec((1, D_MODEL, D_MODEL), wsel, pipeline_mode=pl.Buffered(1)),
        ],
        out_specs=pl.BlockSpec((tm, D_MODEL), lambda i: (i, 0)),
        out_shape=jax.ShapeDtypeStruct((t, D_MODEL), F32),
        scratch_shapes=[pltpu.VMEM((QA_W, D_MODEL), BF16), pltpu.VMEM((VB_W, D_MODEL), BF16),
                        pltpu.VMEM((D_MODEL, D_MODEL), BF16)],
        compiler_params=_cparams(1),
        name="merge",
    )(ya, yb, proj, proj, x2, wa_all, wb_all, wo_all)


def _moe_tile_tokens(t):
    return min(4096, t)


MOE_BLK = 512
MOE_UNIT = 16
MOE_CAP = 1536
MOE_CAP_COMMON = 1280
XS_W = D_MODEL + LANE
MOE_MT = 384
MOE_CH = 2 * MOE_MT
MOE_CHU = MOE_CH // MOE_UNIT
MOE_NC = 256
MOE_EPS = 4
MOE_ESTEPS = N_EXPERTS // MOE_EPS
assert MOE_CAP >= 2 * MOE_BLK + N_EXPERTS * (MOE_UNIT - 1) and MOE_CAP % MOE_UNIT == 0


NT_DIMS = (((1,), (1,)), ((), ()))


def _route(hn_bf, wrt_ref, brc_ref):
    logits = lax.dot_general(wrt_ref[...].astype(BF16), hn_bf, NT_DIMS,
                             preferred_element_type=F32) + brc_ref[:, 0:1]
    row = lax.broadcasted_iota(jnp.int32, logits.shape, 0).astype(F32)
    big = float(LANE)
    ninf = -jnp.inf
    gl = jnp.where(row < N_GROUPS, logits, ninf)
    gmax = jnp.max(gl, axis=0, keepdims=True)
    g_idx = jnp.min(jnp.where(gl == gmax, row, big), axis=0, keepdims=True)
    g_w = 1.0 / jnp.sum(jnp.exp(gl - gmax), axis=0, keepdims=True)
    lo = float(N_GROUPS) + g_idx * float(EXPERTS_PER_GROUP)
    el = jnp.where((row >= lo) & (row < lo + EXPERTS_PER_GROUP), logits, ninf)
    v1 = jnp.max(el, axis=0, keepdims=True)
    i1 = jnp.min(jnp.where(el == v1, row, big), axis=0, keepdims=True)
    el2 = jnp.where(row == i1, ninf, el)
    v2 = jnp.max(el2, axis=0, keepdims=True)
    i2 = jnp.min(jnp.where(el2 == v2, row, big), axis=0, keepdims=True)
    e21 = jnp.exp(v2 - v1)
    w1 = g_w / (1.0 + e21)
    w2 = g_w * e21 / (1.0 + e21)
    return i1 - float(N_GROUPS), i2 - float(N_GROUPS), w1, w2, row


def _slot_matrix(pos1, pos2, second):
    slot = lax.broadcasted_iota(jnp.int32, (pos1.shape[0], MOE_CAP), 1)
    return jnp.where(slot == pos1.astype(jnp.int32), 1.0,
                     jnp.where(slot == pos2.astype(jnp.int32), second, 0.0)).astype(BF16)


def _slot_matrix_t(pos1, pos2, second):
    slot = lax.broadcasted_iota(jnp.int32, (MOE_CAP, pos1.shape[1]), 0)
    return jnp.where(slot == pos1.astype(jnp.int32), 1.0,
                     jnp.where(slot == pos2.astype(jnp.int32), second, 0.0)).astype(BF16)


def _split3_f32(w):
    hi = w.astype(BF16).astype(F32)
    r1 = w - hi
    mid = r1.astype(BF16).astype(F32)
    return hi, mid, r1 - mid


def _rows_to_sublanes(vals, row):
    out = jnp.zeros(row.shape, F32)
    for k, v in enumerate(vals):
        out = jnp.where(row == float(k), v, out)
    return out


def _moe_kernel(x_ref, g_ref, wrt_ref, brc_ref, wg_ref, wu_ref, wd_ref, gf_ref, o_ref,
                xs_ref, pos_ref, tab_v, tab_s, xt_ref, addr_ref, sem,
                *, nblk, final_norm):
    s = pl.program_id(1)
    dump_row = nblk * MOE_CAP

    @pl.when(s < nblk)
    def _dispatch():
        @pl.when(s == 0)
        def _init():
            xs_ref[dump_row:dump_row + MOE_UNIT, :] = jnp.zeros((MOE_UNIT, XS_W), BF16)
            xt_ref[...] = jnp.zeros_like(xt_ref)

        x = x_ref[...]
        ms = jnp.mean(x * x, axis=-1, keepdims=True)
        hn = (x * lax.rsqrt(ms + EPS) * g_ref[0]).astype(BF16)
        e1, e2, w1, w2, row = _route(hn, wrt_ref, brc_ref)
        onehot = ((row == e1) | (row == e2)).astype(F32)
        earlier = (lax.broadcasted_iota(jnp.int32, (MOE_BLK, MOE_BLK), 0)
                   < lax.broadcasted_iota(jnp.int32, (MOE_BLK, MOE_BLK), 1)).astype(BF16)
        before = jnp.dot(onehot.astype(BF16), earlier, preferred_element_type=F32)
        counts = jnp.sum(onehot, axis=1, keepdims=True)
        units = jnp.floor((counts + (MOE_UNIT - 1.0)) * (1.0 / MOE_UNIT))
        lower = (lax.broadcasted_iota(jnp.int32, (LANE, LANE), 1)
                 < lax.broadcasted_iota(jnp.int32, (LANE, LANE), 0)).astype(BF16)
        padded = jnp.broadcast_to(units * MOE_UNIT, (LANE, LANE))
        run_off = jnp.dot(lower, padded.astype(BF16), preferred_element_type=F32)[:, 0:1]
        start = run_off + before
        pos1 = jnp.sum(jnp.where(row == e1, start, 0.0), axis=0, keepdims=True)
        pos2 = jnp.sum(jnp.where(row == e2, start, 0.0), axis=0, keepdims=True)
        w_rows = _rows_to_sublanes(
            _split3_f32(w1) + _split3_f32(w2) + (jnp.ones_like(w1),), row).astype(BF16)
        used = jnp.sum(units, axis=0, keepdims=True) * MOE_UNIT
        pt_t = _slot_matrix_t(pos1, pos2, 2.0)
        rows = pl.ds(pl.multiple_of(s * MOE_CAP, MOE_CAP), MOE_CAP)
        for c in range(D_MODEL // MOE_NC):
            cs = slice(c * MOE_NC, (c + 1) * MOE_NC)
            xs_ref[rows, cs] = jnp.dot(pt_t, hn[:, cs], preferred_element_type=F32).astype(BF16)
        xs_ref[rows, D_MODEL:] = lax.dot_general(pt_t, w_rows, NT_DIMS,
                                                 preferred_element_type=F32).astype(BF16)
        pos_ref[s] = _rows_to_sublanes((pos1, pos2), row).T
        lane_e = lax.broadcasted_iota(jnp.int32, (LANE, LANE), 1)
        per_expert = jnp.where(lane_e == 0, run_off,
                               jnp.where(lane_e == 1, units, jnp.where(lane_e == 2, used, 0.0)))
        tab_v[s] = per_expert.T[0:SUBLANE, :].astype(jnp.int32)

        @pl.when(s == nblk - 1)
        def _publish():
            cp = pltpu.make_async_copy(tab_v, tab_s, sem)
            cp.start()
            cp.wait()

    def one_expert(e, wi):
        total = tab_s[0, 1, e]
        for j in range(1, nblk):
            total = total + tab_s[j, 1, e]

        def chunk(c, carry):
            u0 = c * MOE_CHU
            for k in range(MOE_CHU):
                addr_ref[k] = dump_row
            cum = 0
            for j in range(nblk):
                nj = tab_s[j, 1, e]
                run0 = j * MOE_CAP + tab_s[j, 0, e]
                k_off = cum - u0

                def copy_unit(u, c2, run0=run0, k_off=k_off):
                    src = pl.multiple_of(run0 + u * MOE_UNIT, MOE_UNIT)
                    dst = pl.multiple_of((k_off + u) * MOE_UNIT, MOE_UNIT)
                    xt_ref[pl.ds(dst, MOE_UNIT), :] = xs_ref[pl.ds(src, MOE_UNIT), :]
                    addr_ref[k_off + u] = src
                    return c2
                lax.fori_loop(jnp.clip(u0 - cum, 0, nj), jnp.clip(u0 + MOE_CHU - cum, 0, nj),
                              copy_unit, 0)
                cum = cum + nj
            rows_here = jnp.minimum(MOE_CHU, total - u0) * MOE_UNIT

            def tile(t, c3):
                xt = xt_ref[pl.ds(pl.multiple_of(t * MOE_MT, MOE_MT), MOE_MT), :]
                wl = xt[:, D_MODEL:].astype(F32)
                second = wl[:, 6:7] > 1.5
                wcol = jnp.where(second, 0.5 * (wl[:, 3:4] + wl[:, 4:5] + wl[:, 5:6]),
                                 wl[:, 0:1] + wl[:, 1:2] + wl[:, 2:3])
                undo = jnp.where(second, 0.5, 1.0)
                xf = xt[:, :D_MODEL]
                gate = jnp.dot(xf, wg_ref[wi], preferred_element_type=F32) * undo
                up = jnp.dot(xf, wu_ref[wi], preferred_element_type=F32) * undo
                h = gate * _sigmoid(gate) * up * wcol
                y = jnp.dot(h.astype(BF16), wd_ref[wi], preferred_element_type=F32).astype(BF16)
                for uu in range(MOE_MT // MOE_UNIT):
                    dst = pl.multiple_of(addr_ref[t * (MOE_MT // MOE_UNIT) + uu], MOE_UNIT)
                    xs_ref[pl.ds(dst, MOE_UNIT), :D_MODEL] = y[uu * MOE_UNIT:(uu + 1) * MOE_UNIT]
                return c3
            lax.fori_loop(0, (rows_here + MOE_MT - 1) // MOE_MT, tile, 0)
            return carry
        lax.fori_loop(0, (total + MOE_CHU - 1) // MOE_CHU, chunk, 0)

    @pl.when((s >= nblk) & (s < nblk + MOE_ESTEPS))
    def _experts():
        for wi in range(MOE_EPS):
            one_expert((s - nblk) * MOE_EPS + wi, wi)

    @pl.when(s >= nblk + MOE_ESTEPS)
    def _combine():
        j = s - nblk - MOE_ESTEPS
        p = pos_ref[j]
        pt = _slot_matrix(p[:, 0:1], p[:, 1:2], 1.0)
        base = pl.multiple_of(j * MOE_CAP, MOE_CAP)
        head = pl.ds(base, MOE_CAP_COMMON)
        for c in range(D_MODEL // MOE_NC):
            cs = slice(c * MOE_NC, (c + 1) * MOE_NC)
            o_ref[:, cs] = x_ref[:, cs] + jnp.dot(pt[:, :MOE_CAP_COMMON], xs_ref[head, cs],
                                                  preferred_element_type=F32)

        @pl.when(tab_s[j, 2, 0] > MOE_CAP_COMMON)
        def _tail():
            tail = pl.ds(base + MOE_CAP_COMMON, MOE_CAP - MOE_CAP_COMMON)
            for c in range(D_MODEL // MOE_NC):
                cs = slice(c * MOE_NC, (c + 1) * MOE_NC)
                o_ref[:, cs] += jnp.dot(pt[:, MOE_CAP_COMMON:], xs_ref[tail, cs],
                                        preferred_element_type=F32)
        if final_norm:
            y = o_ref[...]
            ms = jnp.mean(y * y, axis=-1, keepdims=True)
            o_ref[...] = y * lax.rsqrt(ms + EPS) * gf_ref[...]


def _moe(x2, g_all, w_r, b_r, wg_all, wu_all, wd_all, g_final, layer, tt, final_norm):
    t = x2.shape[0]
    nblk = tt // MOE_BLK
    steps = 2 * nblk + MOE_ESTEPS

    def tok(b, s):
        j = jnp.where(s < nblk, s, jnp.where(s < nblk + MOE_ESTEPS, nblk - 1, s - nblk - MOE_ESTEPS))
        return (b * nblk + j, 0)

    def out_tok(b, s):
        return (b * nblk + jnp.maximum(s - nblk - MOE_ESTEPS, 0), 0)

    wsel = lambda b, s: (jnp.clip(s - nblk, 0, MOE_ESTEPS - 1), 0, 0)
    const = lambda b, s: (0, 0)
    return pl.pallas_call(
        functools.partial(_moe_kernel, nblk=nblk, final_norm=final_norm),
        grid=(t // tt, steps),
        in_specs=[
            pl.BlockSpec((MOE_BLK, D_MODEL), tok),
            pl.BlockSpec((1, 1, D_MODEL), lambda b, s: (layer, 0, 0)),
            pl.BlockSpec((LANE, D_MODEL), const),
            pl.BlockSpec((LANE, 1), const),
            pl.BlockSpec((MOE_EPS, D_MODEL, D_EXPERT), wsel),
            pl.BlockSpec((MOE_EPS, D_MODEL, D_EXPERT), wsel),
            pl.BlockSpec((MOE_EPS, D_EXPERT, D_MODEL), wsel),
            pl.BlockSpec((1, D_MODEL), const),
        ],
        out_specs=pl.BlockSpec((MOE_BLK, D_MODEL), out_tok),
        out_shape=jax.ShapeDtypeStruct((t, D_MODEL), F32),
        scratch_shapes=[
            pltpu.VMEM((nblk * MOE_CAP + MOE_UNIT, XS_W), BF16),
            pltpu.VMEM((nblk, MOE_BLK, LANE), F32),
            pltpu.VMEM((nblk, SUBLANE, LANE), jnp.int32),
            pltpu.SMEM((nblk, SUBLANE, LANE), jnp.int32),
            pltpu.VMEM((MOE_CH, XS_W), BF16),
            pltpu.SMEM((MOE_CHU,), jnp.int32),
            pltpu.SemaphoreType.DMA,
        ],
        compiler_params=_cparams(2),
        name="moe",
    )(x2, g_all, w_r, b_r, wg_all, wu_all, wd_all, g_final)


def _t5_bucket(rel):
    nb = N_BUCKETS // 2
    max_exact = nb // 2
    base = jnp.where(rel > 0, nb, 0)
    n = jnp.abs(rel)
    large = max_exact + (jnp.log(jnp.maximum(n, 1).astype(jnp.float32) / max_exact)
                         / math.log(MAX_DISTANCE / max_exact) * (nb - max_exact)).astype(jnp.int32)
    large = jnp.minimum(large, nb - 1)
    return base + jnp.where(n < max_exact, n, large)


def _pad_rank(wr):
    return jnp.pad(wr, ((0, LANE - B_RANK), (0, 0))).astype(BF16)


def kernel(x, w_in, rel_bias, attn_sink, gla_wr_fwd, gla_br_fwd, gla_wr_bwd, gla_br_bwd, gla_norm, w_branch_a, w_branch_b, w_out, norm_mix, norm_ffn, router_group_w, router_group_b, router_expert_w, router_expert_b, expert_w_gate, expert_w_up, expert_w_down, norm_final):
    bsz, seq, d = x.shape
    depth = w_in.shape[0]
    t = bsz * seq
    tt = _moe_tile_tokens(t)
    q_off = jnp.arange(BLOCK)
    k_off = jnp.arange(3 * BLOCK) - BLOCK
    bucket = _t5_bucket(k_off[None, :] - q_off[:, None]).astype(jnp.int32) & (N_BUCKETS - 1)
    x2 = x.reshape(t, d)
    norm_mix3 = norm_mix[:, None, :]
    w_in_t = jnp.swapaxes(w_in, 1, 2)
    norm_ffn3 = norm_ffn[:, None, :]
    wg_all = expert_w_gate.reshape(depth * N_EXPERTS, d, D_EXPERT)
    wu_all = expert_w_up.reshape(depth * N_EXPERTS, d, D_EXPERT)
    wd_all = expert_w_down.reshape(depth * N_EXPERTS, D_EXPERT, d)
    for l in range(depth):
        proj, wg_bf, wu_bf, wd_bf = _inproj(x2, norm_mix3, w_in_t, wg_all, wu_all, wd_all, l)
        ya = _attention(proj, bucket, rel_bias, attn_sink[l], bsz, seq)
        wrf, wrb = _pad_rank(gla_wr_fwd[l]), _pad_rank(gla_wr_bwd[l])
        brf, brb = gla_br_fwd[l][None, :], gla_br_bwd[l][None, :]
        s_f, s_b = _gla_states(proj, wrf.T, brf.T, wrb.T, brb.T, bsz, seq)
        yb = _gla_out(proj, s_f, s_b, wrf, brf, wrb, brb, gla_norm[l][None, :], bsz, seq)
        x2 = _merge(ya, yb, proj, x2, w_branch_a, w_branch_b, w_out, l)
        w_r = jnp.concatenate(
            [router_group_w[l].T, router_expert_w[l].T,
             jnp.zeros((LANE - N_GROUPS - N_EXPERTS, d), F32)], axis=0)
        b_r = jnp.concatenate(
            [router_group_b[l], router_expert_b[l],
             jnp.zeros((LANE - N_GROUPS - N_EXPERTS,), F32)])[:, None]
        x2 = _moe(x2, norm_ffn3, w_r, b_r, wg_bf, wu_bf, wd_bf, norm_final[None, :], l, tt,
                  final_norm=(l == depth - 1))
    return x2.reshape(bsz, seq, d)
```

```python
import functools
import math

import numpy as np
import jax
import jax.numpy as jnp
from jax import lax
from jax.experimental import pallas as pl
from jax.experimental.pallas import tpu as pltpu

F32 = jnp.float32
BF16 = jnp.bfloat16

D_MODEL = 1024
A_HEADS = 8
A_KV_HEADS = 2
A_HEAD_DIM = 64
A_GROUP = A_HEADS // A_KV_HEADS
WINDOW = 128
BLOCK = 128
N_BUCKETS = 32
MAX_DISTANCE = 128
B_HEADS = 4
B_KEY_DIM = 64
B_VAL_DIM = 128
B_RANK = 16
GATE_TAU = 16.0
CHUNK = 64
N_GROUPS = 4
EXPERTS_PER_GROUP = 8
N_EXPERTS = N_GROUPS * EXPERTS_PER_GROUP
D_EXPERT = 256
EPS = 1e-6
NEG_INF = -1e30

LANE = 128
SUBLANE = 8
VMEM_LIMIT = 56 * 1024 * 1024

QA_W = A_HEADS * A_HEAD_DIM
KA_W = A_KV_HEADS * A_HEAD_DIM
QB_W = B_HEADS * B_KEY_DIM
VB_W = B_HEADS * B_VAL_DIM
COL_GA = 0
COL_GB = COL_GA + D_MODEL
COL_AQ = COL_GB + D_MODEL
COL_BV = COL_AQ + QA_W
COL_BG = COL_BV + VB_W
COL_BQ = COL_BG + VB_W
COL_BK = COL_BQ + QB_W
COL_AK = COL_BK + QB_W
COL_AV = COL_AK + KA_W
COL_R = COL_AV + KA_W
PROJ_W = COL_R + LANE

ROW_TILE = 512
PROJ_CHUNK = 1280
GLA_TILE = 256
CPT = GLA_TILE // CHUNK


def _cparams(n_axes):
    return pltpu.CompilerParams(
        dimension_semantics=("arbitrary",) * n_axes, vmem_limit_bytes=VMEM_LIMIT)


_W_IN_SRC = np.cumsum([0, QA_W, KA_W, KA_W, QB_W, QB_W, VB_W, VB_W, B_RANK, D_MODEL, D_MODEL])
_W_IN_SEGMENTS = tuple(zip(
    (COL_AQ, COL_AK, COL_AV, COL_BQ, COL_BK, COL_BV, COL_BG, COL_R, COL_GA, COL_GB),
    (int(s) for s in _W_IN_SRC[:-1]),
    (int(w) for w in np.diff(_W_IN_SRC))))
W_IN_COLS = int(_W_IN_SRC[-1])
W_ROWS_PER_COPY = 256


def _load_permuted_w_in(wt_ref, wbf_ref):
    wbf_ref[COL_R:COL_R + LANE, :] = jnp.zeros((LANE, D_MODEL), BF16)
    for dst, src, width in _W_IN_SEGMENTS:
        for r0 in range(0, width, W_ROWS_PER_COPY):
            n = min(W_ROWS_PER_COPY, width - r0)
            wbf_ref[dst + r0:dst + r0 + n, :] = wt_ref[0, src + r0:src + r0 + n, :].astype(BF16)


def _inproj_kernel(x_ref, g_ref, wt_ref, wg_ref, wu_ref, wd_ref, o_ref, wgb_ref, wub_ref, wdb_ref, wbf_ref):
    @pl.when(pl.program_id(0) == 0)
    def _prep():
        _load_permuted_w_in(wt_ref, wbf_ref)

    wgb_ref[...] = wg_ref[...].astype(BF16)
    wub_ref[...] = wu_ref[...].astype(BF16)
    wdb_ref[...] = wd_ref[...].astype(BF16)

    x = x_ref[...]
    ms = jnp.mean(x * x, axis=-1, keepdims=True)
    xn = (x * lax.rsqrt(ms + EPS) * g_ref[0]).astype(BF16)
    for c0 in range(0, PROJ_W, PROJ_CHUNK):
        sl = slice(c0, min(c0 + PROJ_CHUNK, PROJ_W))
        o_ref[:, sl] = lax.dot_general(xn, wbf_ref[sl, :], (((1,), (1,)), ((), ())),
                                       preferred_element_type=F32).astype(BF16)


def _inproj(x2, g_all, wt_all, wg_all, wu_all, wd_all, layer):
    t = x2.shape[0]
    tm = min(ROW_TILE, t)
    nsteps = t // tm
    epi = -(-N_EXPERTS // nsteps)
    while N_EXPERTS % epi:
        epi += 1
    nwb = N_EXPERTS // epi
    wsel = lambda i: (layer * nwb + jnp.minimum(i, nwb - 1), 0, 0)
    wout = lambda i: (jnp.minimum(i, nwb - 1), 0, 0)
    return pl.pallas_call(
        _inproj_kernel,
        grid=(nsteps,),
        in_specs=[
            pl.BlockSpec((tm, D_MODEL), lambda i: (i, 0)),
            pl.BlockSpec((1, 1, D_MODEL), lambda i: (layer, 0, 0)),
            pl.BlockSpec((1, W_IN_COLS, D_MODEL), lambda i: (layer, 0, 0), pipeline_mode=pl.Buffered(1)),
            pl.BlockSpec((epi, D_MODEL, D_EXPERT), wsel),
            pl.BlockSpec((epi, D_MODEL, D_EXPERT), wsel),
            pl.BlockSpec((epi, D_EXPERT, D_MODEL), wsel),
        ],
        out_specs=[
            pl.BlockSpec((tm, PROJ_W), lambda i: (i, 0)),
            pl.BlockSpec((epi, D_MODEL, D_EXPERT), wout),
            pl.BlockSpec((epi, D_MODEL, D_EXPERT), wout),
            pl.BlockSpec((epi, D_EXPERT, D_MODEL), wout),
        ],
        out_shape=[
            jax.ShapeDtypeStruct((t, PROJ_W), BF16),
            jax.ShapeDtypeStruct((N_EXPERTS, D_MODEL, D_EXPERT), BF16),
            jax.ShapeDtypeStruct((N_EXPERTS, D_MODEL, D_EXPERT), BF16),
            jax.ShapeDtypeStruct((N_EXPERTS, D_EXPERT, D_MODEL), BF16),
        ],
        scratch_shapes=[pltpu.VMEM((PROJ_W, D_MODEL), BF16)],
        compiler_params=_cparams(1),
        name="inproj",
    )(x2, g_all, wt_all, wg_all, wu_all, wd_all)


ATT_QB = 2


def _attn_kernel(bucket_ref, relb_ref, sink_ref, q_ref, k_ref, v_ref, o_ref,
                 bias_ref, logit_ref, prob_ref, *, nblk):
    b = pl.program_id(0)
    n = pl.program_id(1)
    kw = 3 * BLOCK

    @pl.when((b == 0) & (n == 0))
    def _build_bias():
        bucket = bucket_ref[...]
        row = lax.broadcasted_iota(jnp.int32, (BLOCK, kw), 0)
        col = lax.broadcasted_iota(jnp.int32, (BLOCK, kw), 1)
        win = jnp.abs(col - BLOCK - row) <= WINDOW
        keep = (win & (col >= BLOCK), win, win & (col < 2 * BLOCK))
        for hq in range(A_HEADS):
            acc = jnp.zeros((BLOCK, kw), F32)
            for bb in range(N_BUCKETS):
                acc = jnp.where(bucket == bb, relb_ref[bb, hq], acc)
            for v in range(3):
                bias_ref[v, hq] = jnp.where(keep[v], acc, NEG_INF)

    q = (q_ref[...].astype(F32) * (A_HEAD_DIM ** -0.5)).astype(BF16)
    r_prev = pl.multiple_of(jnp.maximum(n * ATT_QB - 1, 0) * BLOCK, BLOCK)
    r_own = pl.multiple_of(n * (ATT_QB * BLOCK), ATT_QB * BLOCK)
    r_next = pl.multiple_of(jnp.minimum((n + 1) * ATT_QB, nblk - 1) * BLOCK, BLOCK)

    def rows3(ref):
        return jnp.concatenate([ref[pl.ds(r_prev, BLOCK), :], ref[pl.ds(r_own, ATT_QB * BLOCK), :],
                                ref[pl.ds(r_next, BLOCK), :]], axis=0)
    kall = rows3(k_ref)
    vall = rows3(v_ref)
    jobs = [(j, hq) for j in range(ATT_QB) for hq in range(A_HEADS)]

    def keys(x, j, hq):
        h = hq // A_GROUP
        return x[j * BLOCK:(j + 3) * BLOCK, h * A_HEAD_DIM:(h + 1) * A_HEAD_DIM]

    for i, (j, hq) in enumerate(jobs):
        logit_ref[i] = lax.dot_general(
            q[j * BLOCK:(j + 1) * BLOCK, hq * A_HEAD_DIM:(hq + 1) * A_HEAD_DIM], keys(kall, j, hq),
            (((1,), (1,)), ((), ())), preferred_element_type=F32)
    dens = []
    for i, (j, hq) in enumerate(jobs):
        blk = n * ATT_QB + j
        variant = jnp.where(blk == 0, 0, jnp.where(blk == nblk - 1, 2, 1))
        s = logit_ref[i] + bias_ref[variant, hq]
        sk = sink_ref[hq]
        m = jnp.maximum(jnp.max(s, axis=-1, keepdims=True), sk)
        p = jnp.exp(s - m)
        dens.append(jnp.sum(p, axis=-1, keepdims=True) + jnp.exp(sk - m))
        prob_ref[i] = p.astype(BF16)
    outs = [jnp.dot(prob_ref[i], keys(vall, j, hq), preferred_element_type=F32) / dens[i]
            for i, (j, hq) in enumerate(jobs)]
    for j in range(ATT_QB):
        o_ref[j * BLOCK:(j + 1) * BLOCK, :] = jnp.concatenate(
            outs[j * A_HEADS:(j + 1) * A_HEADS], axis=1).astype(BF16)


def _attention(proj, bucket, rel_bias, sink, bsz, seq):
    nblk = seq // BLOCK
    t = bsz * seq
    cq = COL_AQ // QA_W
    ck = COL_AK // KA_W
    cv = COL_AV // KA_W

    nsteps = nblk // ATT_QB
    rows = ATT_QB * BLOCK

    def kv_spec(c):
        return pl.BlockSpec((seq, KA_W), lambda b, n: (b, c))

    smem = pl.BlockSpec(memory_space=pltpu.SMEM)
    return pl.pallas_call(
        functools.partial(_attn_kernel, nblk=nblk),
        grid=(bsz, nsteps),
        in_specs=[
            pl.BlockSpec((BLOCK, 3 * BLOCK), lambda b, n: (0, 0)),
            smem, smem,
            pl.BlockSpec((rows, QA_W), lambda b, n: (b * nsteps + n, cq)),
            kv_spec(ck), kv_spec(cv),
        ],
        out_specs=pl.BlockSpec((rows, QA_W), lambda b, n: (b * nsteps + n, 0)),
        out_shape=jax.ShapeDtypeStruct((t, QA_W), BF16),
        scratch_shapes=[pltpu.VMEM((3, A_HEADS, BLOCK, 3 * BLOCK), F32),
                        pltpu.VMEM((ATT_QB * A_HEADS, BLOCK, 3 * BLOCK), F32),
                        pltpu.VMEM((ATT_QB * A_HEADS, BLOCK, 3 * BLOCK), BF16)],
        compiler_params=_cparams(2),
        name="attn",
    )(bucket, rel_bias, sink, proj, proj, proj)


def _log_sigmoid(z):
    return jnp.minimum(z, 0.0) - jnp.log(1.0 + jnp.exp(-jnp.abs(z)))


def _log_decay(r, wr, br):
    z = jnp.dot(r, wr, preferred_element_type=F32) + br
    return _log_sigmoid(z) / GATE_TAU


def _chunk_masks():
    row = lax.broadcasted_iota(jnp.int32, (GLA_TILE, GLA_TILE), 0)
    col = lax.broadcasted_iota(jnp.int32, (GLA_TILE, GLA_TILE), 1)
    sh = int(math.log2(CHUNK))
    same = jnp.right_shift(row, sh) == jnp.right_shift(col, sh)
    return same, same & (col <= row), same & (col >= row)


def _split3(x):
    hi = x.astype(BF16)
    r1 = x - hi.astype(F32)
    mid = r1.astype(BF16)
    lo = (r1 - mid.astype(F32)).astype(BF16)
    return hi, mid, lo


def _mask_dot(mask, x):
    m = mask.astype(BF16)
    hi, mid, lo = _split3(x)
    return (jnp.dot(m, hi, preferred_element_type=F32) + jnp.dot(m, mid, preferred_element_type=F32)
            + jnp.dot(m, lo, preferred_element_type=F32))


def _gla_state_dir(k_ref, v_ref, r_ref, wrt_ref, brc_ref, s_out_ref, state_ref, order, cum_mask, same):
    z = lax.dot_general(wrt_ref[...], r_ref[...], (((1,), (1,)), ((), ())),
                        preferred_element_type=F32) + brc_ref[:, 0:1]
    hi, mid, lo = _split3(_log_sigmoid(z) / GATE_TAU)
    m = jnp.concatenate([cum_mask, same], axis=1).astype(BF16)
    both = (jnp.dot(hi, m, preferred_element_type=F32) + jnp.dot(mid, m, preferred_element_type=F32)
            + jnp.dot(lo, m, preferred_element_type=F32))
    cum = both[:, :GLA_TILE]
    tot = both[:, GLA_TILE:]
    k_end = (k_ref[...].astype(F32).T * jnp.exp(tot - cum)).astype(BF16)
    lane_chunk = jnp.right_shift(lax.broadcasted_iota(jnp.int32, (1, GLA_TILE), 1),
                                 int(math.log2(CHUNK)))
    upd = []
    for h in range(B_HEADS):
        kh = k_end[h * B_KEY_DIM:(h + 1) * B_KEY_DIM]
        lhs = jnp.concatenate([jnp.where(lane_chunk == c, kh, 0.0).astype(BF16) for c in range(CPT)],
                              axis=0)
        upd.append(jnp.dot(lhs, v_ref[:, h * B_VAL_DIM:(h + 1) * B_VAL_DIM],
                           preferred_element_type=F32))
    for c in order:
        s_out_ref[0, c] = state_ref[...].astype(BF16)
        upd_c = jnp.concatenate([u[c * B_KEY_DIM:(c + 1) * B_KEY_DIM] for u in upd], axis=0)
        decay = jnp.exp(tot[:, c * CHUNK:c * CHUNK + 1])
        state_ref[...] = decay * state_ref[...] + upd_c


def _gla_state_kernel(kf_ref, vf_ref, rf_ref, kb_ref, vb_ref, rb_ref,
                      wrf_ref, brf_ref, wrb_ref, brb_ref,
                      sf_ref, sb_ref, stf_ref, stb_ref):
    @pl.when(pl.program_id(1) == 0)
    def _reset():
        stf_ref[...] = jnp.zeros_like(stf_ref)
        stb_ref[...] = jnp.zeros_like(stb_ref)

    same, lower, upper = _chunk_masks()
    _gla_state_dir(kf_ref, vf_ref, rf_ref, wrf_ref, brf_ref, sf_ref, stf_ref,
                   range(CPT), upper, same)
    _gla_state_dir(kb_ref, vb_ref, rb_ref, wrb_ref, brb_ref, sb_ref, stb_ref,
                   range(CPT - 1, -1, -1), lower, same)


def _gla_states(proj, wrf_t, brf_c, wrb_t, brb_c, bsz, seq):
    nt = seq // GLA_TILE
    nchunks = seq // CHUNK
    hk = B_HEADS * B_KEY_DIM
    ck, cv, cr = COL_BK // QB_W, COL_BV // VB_W, COL_R // LANE

    def fwd(b, i):
        return b * nt + i

    def bwd(b, i):
        return b * nt + (nt - 1 - i)

    def tile_specs(f):
        return [pl.BlockSpec((GLA_TILE, QB_W), lambda b, i: (f(b, i), ck)),
                pl.BlockSpec((GLA_TILE, VB_W), lambda b, i: (f(b, i), cv)),
                pl.BlockSpec((GLA_TILE, LANE), lambda b, i: (f(b, i), cr))]

    const = lambda b, i: (0, 0)
    out_sds = jax.ShapeDtypeStruct((bsz, nchunks, hk, B_VAL_DIM), BF16)
    return pl.pallas_call(
        _gla_state_kernel,
        grid=(bsz, nt),
        in_specs=tile_specs(fwd) + tile_specs(bwd) + [
            pl.BlockSpec((hk, LANE), const), pl.BlockSpec((hk, 1), const),
            pl.BlockSpec((hk, LANE), const), pl.BlockSpec((hk, 1), const)],
        out_specs=[pl.BlockSpec((1, CPT, hk, B_VAL_DIM), lambda b, i: (b, i, 0, 0)),
                   pl.BlockSpec((1, CPT, hk, B_VAL_DIM), lambda b, i: (b, nt - 1 - i, 0, 0))],
        out_shape=[out_sds, out_sds],
        scratch_shapes=[pltpu.VMEM((hk, B_VAL_DIM), F32), pltpu.VMEM((hk, B_VAL_DIM), F32)],
        compiler_params=_cparams(2),
        name="gla_state",
    )(proj, proj, proj, proj, proj, proj, wrf_t, brf_c, wrb_t, brb_c)


def _gla_out_kernel(q_ref, k_ref, v_ref, r_ref, g_ref, sf_ref, sb_ref,
                    wrf_ref, brf_ref, wrb_ref, brb_ref, ng_ref, o_ref):
    same, lower, upper = _chunk_masks()
    r = r_ref[...]
    cum_f = _mask_dot(lower, _log_decay(r, wrf_ref[...], brf_ref[...]))
    cum_b = _mask_dot(upper, _log_decay(r, wrb_ref[...], brb_ref[...]))
    q = q_ref[...].astype(F32) * (B_KEY_DIM ** -0.5)
    k = k_ref[...].astype(F32)
    qd_f = q * jnp.exp(cum_f)
    qd_b = q * jnp.exp(cum_b)
    ki_f = (k * jnp.exp(-cum_f)).astype(BF16)
    ki_b = (k * jnp.exp(-cum_b)).astype(BF16)
    lane_head = jnp.right_shift(
        lax.broadcasted_iota(jnp.int32, (1, B_HEADS * B_KEY_DIM), 1), int(math.log2(B_KEY_DIM)))

    def stack_heads(x):
        return jnp.concatenate(
            [jnp.where(lane_head == h, x, 0.0) for h in range(B_HEADS)], axis=0).astype(BF16)

    qs_f = stack_heads(qd_f)
    qs_b = stack_heads(qd_b)
    nt_dims = (((1,), (1,)), ((), ()))
    sc_f = lax.dot_general(qs_f, ki_f, nt_dims, preferred_element_type=F32)
    sc_b = lax.dot_general(qs_b, ki_b, nt_dims, preferred_element_type=F32)
    lower_s = jnp.concatenate([lower] * B_HEADS, axis=0)
    strict_upper_s = jnp.concatenate([upper & ~lower] * B_HEADS, axis=0)
    p = jnp.where(lower_s, sc_f, jnp.where(strict_upper_s, sc_b, 0.0)).astype(BF16)
    v = v_ref[...]
    inter = []
    for c in range(CPT):
        lhs_f = jnp.concatenate(
            [qs_f[h * GLA_TILE + c * CHUNK:h * GLA_TILE + (c + 1) * CHUNK] for h in range(B_HEADS)], axis=0)
        lhs_b = jnp.concatenate(
            [qs_b[h * GLA_TILE + c * CHUNK:h * GLA_TILE + (c + 1) * CHUNK] for h in range(B_HEADS)], axis=0)
        inter.append(jnp.dot(lhs_f, sf_ref[0, c], preferred_element_type=F32)
                     + jnp.dot(lhs_b, sb_ref[0, c], preferred_element_type=F32))
    ng = ng_ref[...]
    g = g_ref[...].astype(F32)
    for h in range(B_HEADS):
        vs = slice(h * B_VAL_DIM, (h + 1) * B_VAL_DIM)
        o = jnp.dot(p[h * GLA_TILE:(h + 1) * GLA_TILE], v[:, vs], preferred_element_type=F32)
        o = o + jnp.concatenate([inter[c][h * CHUNK:(h + 1) * CHUNK] for c in range(CPT)], axis=0)
        o = o * lax.rsqrt(jnp.mean(o * o, axis=-1, keepdims=True) + EPS) * ng
        gh = g[:, vs]
        o_ref[:, vs] = (o * (gh / (1.0 + jnp.exp(-gh)))).astype(BF16)


def _gla_out(proj, s_f, s_b, wrf, brf, wrb, brb, ng, bsz, seq):
    nt = seq // GLA_TILE
    t = bsz * seq
    hk = B_HEADS * B_KEY_DIM
    const = lambda b, i: (0, 0)
    row = lambda c: (lambda b, i: (b * nt + i, c))
    return pl.pallas_call(
        _gla_out_kernel,
        grid=(bsz, nt),
        in_specs=[
            pl.BlockSpec((GLA_TILE, QB_W), row(COL_BQ // QB_W)),
            pl.BlockSpec((GLA_TILE, QB_W), row(COL_BK // QB_W)),
            pl.BlockSpec((GLA_TILE, VB_W), row(COL_BV // VB_W)),
            pl.BlockSpec((GLA_TILE, LANE), row(COL_R // LANE)),
            pl.BlockSpec((GLA_TILE, VB_W), row(COL_BG // VB_W)),
            pl.BlockSpec((1, CPT, hk, B_VAL_DIM), lambda b, i: (b, i, 0, 0)),
            pl.BlockSpec((1, CPT, hk, B_VAL_DIM), lambda b, i: (b, i, 0, 0)),
            pl.BlockSpec((LANE, hk), const), pl.BlockSpec((1, hk), const),
            pl.BlockSpec((LANE, hk), const), pl.BlockSpec((1, hk), const),
            pl.BlockSpec((1, B_VAL_DIM), const),
        ],
        out_specs=pl.BlockSpec((GLA_TILE, VB_W), lambda b, i: (b * nt + i, 0)),
        out_shape=jax.ShapeDtypeStruct((t, VB_W), BF16),
        compiler_params=_cparams(2),
        name="gla_out",
    )(proj, proj, proj, proj, proj, s_f, s_b, wrf, brf, wrb, brb, ng)


def _sigmoid(x):
    return 1.0 / (1.0 + jnp.exp(-x))


def _merge_kernel(ya_ref, yb_ref, ga_ref, gb_ref, x_ref, wa_ref, wb_ref, wo_ref, o_ref,
                  wa_bf, wb_bf, wo_bf):
    @pl.when(pl.program_id(0) == 0)
    def _prep():
        wa_bf[...] = wa_ref[0].astype(BF16)
        wb_bf[...] = wb_ref[0].astype(BF16)
        wo_bf[...] = wo_ref[0].astype(BF16)

    a = jnp.dot(ya_ref[...], wa_bf[...], preferred_element_type=F32)
    b = jnp.dot(yb_ref[...], wb_bf[...], preferred_element_type=F32)
    merged = _sigmoid(ga_ref[...].astype(F32)) * a + _sigmoid(gb_ref[...].astype(F32)) * b
    o_ref[...] = x_ref[...] + jnp.dot(merged.astype(BF16), wo_bf[...], preferred_element_type=F32)


def _merge(ya, yb, proj, x2, wa_all, wb_all, wo_all, layer):
    t = x2.shape[0]
    tm = min(ROW_TILE, t)
    wsel = lambda i: (layer, 0, 0)
    return pl.pallas_call(
        _merge_kernel,
        grid=(t // tm,),
        in_specs=[
            pl.BlockSpec((tm, QA_W), lambda i: (i, 0)),
            pl.BlockSpec((tm, VB_W), lambda i: (i, 0)),
            pl.BlockSpec((tm, D_MODEL), lambda i: (i, COL_GA // D_MODEL)),
            pl.BlockSpec((tm, D_MODEL), lambda i: (i, COL_GB // D_MODEL)),
            pl.BlockSpec((tm, D_MODEL), lambda i: (i, 0)),
            pl.BlockSpec((1, QA_W, D_MODEL), wsel, pipeline_mode=pl.Buffered(1)),
            pl.BlockSpec((1, VB_W, D_MODEL), wsel, pipeline_mode=pl.Buffered(1)),
            pl.BlockSpec((1, D_MODEL, D_MODEL), wsel, pipeline_mode=pl.Buffered(1)),
        ],
        out_specs=pl.BlockSpec((tm, D_MODEL), lambda i: (i, 0)),
        out_shape=jax.ShapeDtypeStruct((t, D_MODEL), F32),
        scratch_shapes=[pltpu.VMEM((QA_W, D_MODEL), BF16), pltpu.VMEM((VB_W, D_MODEL), BF16),
                        pltpu.VMEM((D_MODEL, D_MODEL), BF16)],
        compiler_params=_cparams(1),
        name="merge",
    )(ya, yb, proj, proj, x2, wa_all, wb_all, wo_all)


def _moe_tile_tokens(t):
    return min(4096, t)


MOE_BLK = 512
MOE_UNIT = 16
MOE_CAP = 1536
MOE_CAP_COMMON = 1280
XS_W = D_MODEL + LANE
MOE_MT = 384
MOE_CH = 2 * MOE_MT
MOE_CHU = MOE_CH // MOE_UNIT
MOE_NC = 256
MOE_EPS = 2
MOE_ESTEPS = N_EXPERTS // MOE_EPS
assert MOE_CAP >= 2 * MOE_BLK + N_EXPERTS * (MOE_UNIT - 1) and MOE_CAP % MOE_UNIT == 0


NT_DIMS = (((1,), (1,)), ((), ()))


def _route(hn_bf, wrt_ref, brc_ref):
    logits = lax.dot_general(wrt_ref[...].astype(BF16), hn_bf, NT_DIMS,
                             preferred_element_type=F32) + brc_ref[:, 0:1]
    row = lax.broadcasted_iota(jnp.int32, logits.shape, 0).astype(F32)
    big = float(LANE)
    ninf = -jnp.inf
    gl = jnp.where(row < N_GROUPS, logits, ninf)
    gmax = jnp.max(gl, axis=0, keepdims=True)
    g_idx = jnp.min(jnp.where(gl == gmax, row, big), axis=0, keepdims=True)
    g_w = 1.0 / jnp.sum(jnp.exp(gl - gmax), axis=0, keepdims=True)
    lo = float(N_GROUPS) + g_idx * float(EXPERTS_PER_GROUP)
    el = jnp.where((row >= lo) & (row < lo + EXPERTS_PER_GROUP), logits, ninf)
    v1 = jnp.max(el, axis=0, keepdims=True)
    i1 = jnp.min(jnp.where(el == v1, row, big), axis=0, keepdims=True)
    el2 = jnp.where(row == i1, ninf, el)
    v2 = jnp.max(el2, axis=0, keepdims=True)
    i2 = jnp.min(jnp.where(el2 == v2, row, big), axis=0, keepdims=True)
    e21 = jnp.exp(v2 - v1)
    w1 = g_w / (1.0 + e21)
    w2 = g_w * e21 / (1.0 + e21)
    return i1 - float(N_GROUPS), i2 - float(N_GROUPS), w1, w2, row


def _slot_matrix(pos1, pos2, second):
    slot = lax.broadcasted_iota(jnp.int32, (pos1.shape[0], MOE_CAP), 1)
    return jnp.where(slot == pos1.astype(jnp.int32), 1.0,
                     jnp.where(slot == pos2.astype(jnp.int32), second, 0.0)).astype(BF16)


def _slot_matrix_t(pos1, pos2, second):
    slot = lax.broadcasted_iota(jnp.int32, (MOE_CAP, pos1.shape[1]), 0)
    return jnp.where(slot == pos1.astype(jnp.int32), 1.0,
                     jnp.where(slot == pos2.astype(jnp.int32), second, 0.0)).astype(BF16)


def _split3_f32(w):
    hi = w.astype(BF16).astype(F32)
    r1 = w - hi
    mid = r1.astype(BF16).astype(F32)
    return hi, mid, r1 - mid


def _rows_to_sublanes(vals, row):
    out = jnp.zeros(row.shape, F32)
    for k, v in enumerate(vals):
        out = jnp.where(row == float(k), v, out)
    return out


def _moe_kernel(x_ref, g_ref, wrt_ref, brc_ref, wg_ref, wu_ref, wd_ref, gf_ref, o_ref,
                xs_ref, pos_ref, tab_v, tab_s, xt_ref, addr_ref, sem,
                *, nblk, final_norm):
    s = pl.program_id(1)
    dump_row = nblk * MOE_CAP

    @pl.when(s < nblk)
    def _dispatch():
        @pl.when(s == 0)
        def _init():
            xs_ref[dump_row:dump_row + MOE_UNIT, :] = jnp.zeros((MOE_UNIT, XS_W), BF16)
            xt_ref[...] = jnp.zeros_like(xt_ref)

        x = x_ref[...]
        ms = jnp.mean(x * x, axis=-1, keepdims=True)
        hn = (x * lax.rsqrt(ms + EPS) * g_ref[0]).astype(BF16)
        e1, e2, w1, w2, row = _route(hn, wrt_ref, brc_ref)
        onehot = ((row == e1) | (row == e2)).astype(F32)
        earlier = (lax.broadcasted_iota(jnp.int32, (MOE_BLK, MOE_BLK), 0)
                   < lax.broadcasted_iota(jnp.int32, (MOE_BLK, MOE_BLK), 1)).astype(BF16)
        before = jnp.dot(onehot.astype(BF16), earlier, preferred_element_type=F32)
        counts = jnp.sum(onehot, axis=1, keepdims=True)
        units = jnp.floor((counts + (MOE_UNIT - 1.0)) * (1.0 / MOE_UNIT))
        lower = (lax.broadcasted_iota(jnp.int32, (LANE, LANE), 1)
                 < lax.broadcasted_iota(jnp.int32, (LANE, LANE), 0)).astype(BF16)
        padded = jnp.broadcast_to(units * MOE_UNIT, (LANE, LANE))
        run_off = jnp.dot(lower, padded.astype(BF16), preferred_element_type=F32)[:, 0:1]
        start = run_off + before
        pos1 = jnp.sum(jnp.where(row == e1, start, 0.0), axis=0, keepdims=True)
        pos2 = jnp.sum(jnp.where(row == e2, start, 0.0), axis=0, keepdims=True)
        w_rows = _rows_to_sublanes(
            _split3_f32(w1) + _split3_f32(w2) + (jnp.ones_like(w1),), row).astype(BF16)
        used = jnp.sum(units, axis=0, keepdims=True) * MOE_UNIT
        pt_t = _slot_matrix_t(pos1, pos2, 2.0)
        rows = pl.ds(pl.multiple_of(s * MOE_CAP, MOE_CAP), MOE_CAP)
        for c in range(D_MODEL // MOE_NC):
            cs = slice(c * MOE_NC, (c + 1) * MOE_NC)
            xs_ref[rows, cs] = jnp.dot(pt_t, hn[:, cs], preferred_element_type=F32).astype(BF16)
        xs_ref[rows, D_MODEL:] = lax.dot_general(pt_t, w_rows, NT_DIMS,
                                                 preferred_element_type=F32).astype(BF16)
        pos_ref[s] = _rows_to_sublanes((pos1, pos2), row).T
        lane_e = lax.broadcasted_iota(jnp.int32, (LANE, LANE), 1)
        per_expert = jnp.where(lane_e == 0, run_off,
                               jnp.where(lane_e == 1, units, jnp.where(lane_e == 2, used, 0.0)))
        tab_v[s] = per_expert.T[0:SUBLANE, :].astype(jnp.int32)

        @pl.when(s == nblk - 1)
        def _publish():
            cp = pltpu.make_async_copy(tab_v, tab_s, sem)
            cp.start()
            cp.wait()

    def one_expert(e, wi):
        total = tab_s[0, 1, e]
        for j in range(1, nblk):
            total = total + tab_s[j, 1, e]

        def chunk(c, carry):
            u0 = c * MOE_CHU
            for k in range(MOE_CHU):
                addr_ref[k] = dump_row
            cum = 0
            for j in range(nblk):
                nj = tab_s[j, 1, e]
                run0 = j * MOE_CAP + tab_s[j, 0, e]
                k_off = cum - u0

                def copy_unit(u, c2, run0=run0, k_off=k_off):
                    src = pl.multiple_of(run0 + u * MOE_UNIT, MOE_UNIT)
                    dst = pl.multiple_of((k_off + u) * MOE_UNIT, MOE_UNIT)
                    xt_ref[pl.ds(dst, MOE_UNIT), :] = xs_ref[pl.ds(src, MOE_UNIT), :]
                    addr_ref[k_off + u] = src
                    return c2
                lax.fori_loop(jnp.clip(u0 - cum, 0, nj), jnp.clip(u0 + MOE_CHU - cum, 0, nj),
                              copy_unit, 0)
                cum = cum + nj
            rows_here = jnp.minimum(MOE_CHU, total - u0) * MOE_UNIT

            def tile(t, c3):
                xt = xt_ref[pl.ds(pl.multiple_of(t * MOE_MT, MOE_MT), MOE_MT), :]
                wl = xt[:, D_MODEL:].astype(F32)
                second = wl[:, 6:7] > 1.5
                wcol = jnp.where(second, 0.5 * (wl[:, 3:4] + wl[:, 4:5] + wl[:, 5:6]),
                                 wl[:, 0:1] + wl[:, 1:2] + wl[:, 2:3])
                undo = jnp.where(second, 0.5, 1.0)
                xf = xt[:, :D_MODEL]
                gate = jnp.dot(xf, wg_ref[wi], preferred_element_type=F32) * undo
                up = jnp.dot(xf, wu_ref[wi], preferred_element_type=F32) * undo
                h = gate * _sigmoid(gate) * up * wcol
                y = jnp.dot(h.astype(BF16), wd_ref[wi], preferred_element_type=F32).astype(BF16)
                for uu in range(MOE_MT // MOE_UNIT):
                    dst = pl.multiple_of(addr_ref[t * (MOE_MT // MOE_UNIT) + uu], MOE_UNIT)
                    xs_ref[pl.ds(dst, MOE_UNIT), :D_MODEL] = y[uu * MOE_UNIT:(uu + 1) * MOE_UNIT]
                return c3
            lax.fori_loop(0, (rows_here + MOE_MT - 1) // MOE_MT, tile, 0)
            return carry
        lax.fori_loop(0, (total + MOE_CHU - 1) // MOE_CHU, chunk, 0)

    @pl.when((s >= nblk) & (s < nblk + MOE_ESTEPS))
    def _experts():
        for wi in range(MOE_EPS):
            one_expert((s - nblk) * MOE_EPS + wi, wi)

    @pl.when(s >= nblk + MOE_ESTEPS)
    def _combine():
        j = s - nblk - MOE_ESTEPS
        p = pos_ref[j]
        pt = _slot_matrix(p[:, 0:1], p[:, 1:2], 1.0)
        base = pl.multiple_of(j * MOE_CAP, MOE_CAP)
        head = pl.ds(base, MOE_CAP_COMMON)
        for c in range(D_MODEL // MOE_NC):
            cs = slice(c * MOE_NC, (c + 1) * MOE_NC)
            o_ref[:, cs] = x_ref[:, cs] + jnp.dot(pt[:, :MOE_CAP_COMMON], xs_ref[head, cs],
                                                  preferred_element_type=F32)

        @pl.when(tab_s[j, 2, 0] > MOE_CAP_COMMON)
        def _tail():
            tail = pl.ds(base + MOE_CAP_COMMON, MOE_CAP - MOE_CAP_COMMON)
            for c in range(D_MODEL // MOE_NC):
                cs = slice(c * MOE_NC, (c + 1) * MOE_NC)
                o_ref[:, cs] += jnp.dot(pt[:, MOE_CAP_COMMON:], xs_ref[tail, cs],
                                        preferred_element_type=F32)
        if final_norm:
            y = o_ref[...]
            ms = jnp.mean(y * y, axis=-1, keepdims=True)
            o_ref[...] = y * lax.rsqrt(ms + EPS) * gf_ref[...]


def _moe(x2, g_all, w_r, b_r, wg_all, wu_all, wd_all, g_final, layer, tt, final_norm):
    t = x2.shape[0]
    nblk = tt // MOE_BLK
    steps = 2 * nblk + MOE_ESTEPS

    def tok(b, s):
        j = jnp.where(s < nblk, s, jnp.where(s < nblk + MOE_ESTEPS, nblk - 1, s - nblk - MOE_ESTEPS))
        return (b * nblk + j, 0)

    def out_tok(b, s):
        return (b * nblk + jnp.maximum(s - nblk - MOE_ESTEPS, 0), 0)

    wsel = lambda b, s: (jnp.clip(s - nblk, 0, MOE_ESTEPS - 1), 0, 0)
    const = lambda b, s: (0, 0)
    return pl.pallas_call(
        functools.partial(_moe_kernel, nblk=nblk, final_norm=final_norm),
        grid=(t // tt, steps),
        in_specs=[
            pl.BlockSpec((MOE_BLK, D_MODEL), tok),
            pl.BlockSpec((1, 1, D_MODEL), lambda b, s: (layer, 0, 0)),
            pl.BlockSpec((LANE, D_MODEL), const),
            pl.BlockSpec((LANE, 1), const),
            pl.BlockSpec((MOE_EPS, D_MODEL, D_EXPERT), wsel),
            pl.BlockSpec((MOE_EPS, D_MODEL, D_EXPERT), wsel),
            pl.BlockSpec((MOE_EPS, D_EXPERT, D_MODEL), wsel),
            pl.BlockSpec((1, D_MODEL), const),
        ],
        out_specs=pl.BlockSpec((MOE_BLK, D_MODEL), out_tok),
        out_shape=jax.ShapeDtypeStruct((t, D_MODEL), F32),
        scratch_shapes=[
            pltpu.VMEM((nblk * MOE_CAP + MOE_UNIT, XS_W), BF16),
            pltpu.VMEM((nblk, MOE_BLK, LANE), F32),
            pltpu.VMEM((nblk, SUBLANE, LANE), jnp.int32),
            pltpu.SMEM((nblk, SUBLANE, LANE), jnp.int32),
            pltpu.VMEM((MOE_CH, XS_W), BF16),
            pltpu.SMEM((MOE_CHU,), jnp.int32),
            pltpu.SemaphoreType.DMA,
        ],
        compiler_params=_cparams(2),
        name="moe",
    )(x2, g_all, w_r, b_r, wg_all, wu_all, wd_all, g_final)


def _t5_bucket(rel):
    nb = N_BUCKETS // 2
    max_exact = nb // 2
    base = jnp.where(rel > 0, nb, 0)
    n = jnp.abs(rel)
    large = max_exact + (jnp.log(jnp.maximum(n, 1).astype(jnp.float32) / max_exact)
                         / math.log(MAX_DISTANCE / max_exact) * (nb - max_exact)).astype(jnp.int32)
    large = jnp.minimum(large, nb - 1)
    return base + jnp.where(n < max_exact, n, large)


def _pad_rank(wr):
    return jnp.pad(wr, ((0, LANE - B_RANK), (0, 0))).astype(BF16)


def kernel(x, w_in, rel_bias, attn_sink, gla_wr_fwd, gla_br_fwd, gla_wr_bwd, gla_br_bwd, gla_norm, w_branch_a, w_branch_b, w_out, norm_mix, norm_ffn, router_group_w, router_group_b, router_expert_w, router_expert_b, expert_w_gate, expert_w_up, expert_w_down, norm_final):
    bsz, seq, d = x.shape
    depth = w_in.shape[0]
    t = bsz * seq
    tt = _moe_tile_tokens(t)
    q_off = jnp.arange(BLOCK)
    k_off = jnp.arange(3 * BLOCK) - BLOCK
    bucket = _t5_bucket(k_off[None, :] - q_off[:, None]).astype(jnp.int32) & (N_BUCKETS - 1)
    x2 = x.reshape(t, d)
    norm_mix3 = norm_mix[:, None, :]
    w_in_t = jnp.swapaxes(w_in, 1, 2)
    norm_ffn3 = norm_ffn[:, None, :]
    wg_all = expert_w_gate.reshape(depth * N_EXPERTS, d, D_EXPERT)
    wu_all = expert_w_up.reshape(depth * N_EXPERTS, d, D_EXPERT)
    wd_all = expert_w_down.reshape(depth * N_EXPERTS, D_EXPERT, d)
    for l in range(depth):
        proj, wg_bf, wu_bf, wd_bf = _inproj(x2, norm_mix3, w_in_t, wg_all, wu_all, wd_all, l)
        ya = _attention(proj, bucket, rel_bias, attn_sink[l], bsz, seq)
        wrf, wrb = _pad_rank(gla_wr_fwd[l]), _pad_rank(gla_wr_bwd[l])
        brf, brb = gla_br_fwd[l][None, :], gla_br_bwd[l][None, :]
        s_f, s_b = _gla_states(proj, wrf.T, brf.T, wrb.T, brb.T, bsz, seq)
        yb = _gla_out(proj, s_f, s_b, wrf, brf, wrb, brb, gla_norm[l][None, :], bsz, seq)
        x2 = _merge(ya, yb, proj, x2, w_branch_a, w_branch_b, w_out, l)
        w_r = jnp.concatenate(
            [router_group_w[l].T, router_expert_w[l].T,
             jnp.zeros((LANE - N_GROUPS - N_EXPERTS, d), F32)], axis=0)
        b_r = jnp.concatenate(
            [router_group_b[l], router_expert_b[l],
             jnp.zeros((LANE - N_GROUPS - N_EXPERTS,), F32)])[:, None]
        x2 = _moe(x2, norm_ffn3, w_r, b_r, wg_bf, wu_bf, wd_bf, norm_final[None, :], l, tt,
                  final_norm=(l == depth - 1))
    return x2.reshape(bsz, seq, d)
```

```python
import functools
import math

import numpy as np
import jax
import jax.numpy as jnp
from jax import lax
from jax.experimental import pallas as pl
from jax.experimental.pallas import tpu as pltpu

F32 = jnp.float32
BF16 = jnp.bfloat16

D_MODEL = 1024
A_HEADS = 8
A_KV_HEADS = 2
A_HEAD_DIM = 64
A_GROUP = A_HEADS // A_KV_HEADS
WINDOW = 128
BLOCK = 128
N_BUCKETS = 32
MAX_DISTANCE = 128
B_HEADS = 4
B_KEY_DIM = 64
B_VAL_DIM = 128
B_RANK = 16
GATE_TAU = 16.0
CHUNK = 64
N_GROUPS = 4
EXPERTS_PER_GROUP = 8
N_EXPERTS = N_GROUPS * EXPERTS_PER_GROUP
D_EXPERT = 256
EPS = 1e-6
NEG_INF = -1e30

LANE = 128
SUBLANE = 8
VMEM_LIMIT = 56 * 1024 * 1024

QA_W = A_HEADS * A_HEAD_DIM
KA_W = A_KV_HEADS * A_HEAD_DIM
QB_W = B_HEADS * B_KEY_DIM
VB_W = B_HEADS * B_VAL_DIM
COL_GA = 0
COL_GB = COL_GA + D_MODEL
COL_AQ = COL_GB + D_MODEL
COL_BV = COL_AQ + QA_W
COL_BG = COL_BV + VB_W
COL_BQ = COL_BG + VB_W
COL_BK = COL_BQ + QB_W
COL_AK = COL_BK + QB_W
COL_AV = COL_AK + KA_W
COL_R = COL_AV + KA_W
PROJ_W = COL_R + LANE

ROW_TILE = 512
MERGE_TILE = 1024
PROJ_CHUNK = 1280
GLA_TILE = 256
CPT = GLA_TILE // CHUNK


def _cparams(n_axes):
    return pltpu.CompilerParams(
        dimension_semantics=("arbitrary",) * n_axes, vmem_limit_bytes=VMEM_LIMIT)


_W_IN_SRC = np.cumsum([0, QA_W, KA_W, KA_W, QB_W, QB_W, VB_W, VB_W, B_RANK, D_MODEL, D_MODEL])
_W_IN_SEGMENTS = tuple(zip(
    (COL_AQ, COL_AK, COL_AV, COL_BQ, COL_BK, COL_BV, COL_BG, COL_R, COL_GA, COL_GB),
    (int(s) for s in _W_IN_SRC[:-1]),
    (int(w) for w in np.diff(_W_IN_SRC))))
W_IN_COLS = int(_W_IN_SRC[-1])
W_ROWS_PER_COPY = 256


def _load_permuted_w_in(wt_ref, wbf_ref):
    wbf_ref[COL_R:COL_R + LANE, :] = jnp.zeros((LANE, D_MODEL), BF16)
    for dst, src, width in _W_IN_SEGMENTS:
        for r0 in range(0, width, W_ROWS_PER_COPY):
            n = min(W_ROWS_PER_COPY, width - r0)
            wbf_ref[dst + r0:dst + r0 + n, :] = wt_ref[0, src + r0:src + r0 + n, :].astype(BF16)


def _inproj_kernel(x_ref, g_ref, wt_ref, wg_ref, wu_ref, wd_ref, o_ref, wgb_ref, wub_ref, wdb_ref, wbf_ref):
    @pl.when(pl.program_id(0) == 0)
    def _prep():
        _load_permuted_w_in(wt_ref, wbf_ref)

    wgb_ref[...] = wg_ref[...].astype(BF16)
    wub_ref[...] = wu_ref[...].astype(BF16)
    wdb_ref[...] = wd_ref[...].astype(BF16)

    x = x_ref[...]
    ms = jnp.mean(x * x, axis=-1, keepdims=True)
    xn = (x * lax.rsqrt(ms + EPS) * g_ref[0]).astype(BF16)
    for c0 in range(0, PROJ_W, PROJ_CHUNK):
        sl = slice(c0, min(c0 + PROJ_CHUNK, PROJ_W))
        o_ref[:, sl] = lax.dot_general(xn, wbf_ref[sl, :], (((1,), (1,)), ((), ())),
                                       preferred_element_type=F32).astype(BF16)


def _inproj(x2, g_all, wt_all, wg_all, wu_all, wd_all, layer):
    t = x2.shape[0]
    tm = min(ROW_TILE, t)
    nsteps = t // tm
    epi = -(-N_EXPERTS // nsteps)
    while N_EXPERTS % epi:
        epi += 1
    nwb = N_EXPERTS // epi
    wsel = lambda i: (layer * nwb + jnp.minimum(i, nwb - 1), 0, 0)
    wout = lambda i: (jnp.minimum(i, nwb - 1), 0, 0)
    return pl.pallas_call(
        _inproj_kernel,
        grid=(nsteps,),
        in_specs=[
            pl.BlockSpec((tm, D_MODEL), lambda i: (i, 0)),
            pl.BlockSpec((1, 1, D_MODEL), lambda i: (layer, 0, 0)),
            pl.BlockSpec((1, W_IN_COLS, D_MODEL), lambda i: (layer, 0, 0), pipeline_mode=pl.Buffered(1)),
            pl.BlockSpec((epi, D_MODEL, D_EXPERT), wsel),
            pl.BlockSpec((epi, D_MODEL, D_EXPERT), wsel),
            pl.BlockSpec((epi, D_EXPERT, D_MODEL), wsel),
        ],
        out_specs=[
            pl.BlockSpec((tm, PROJ_W), lambda i: (i, 0)),
            pl.BlockSpec((epi, D_MODEL, D_EXPERT), wout),
            pl.BlockSpec((epi, D_MODEL, D_EXPERT), wout),
            pl.BlockSpec((epi, D_EXPERT, D_MODEL), wout),
        ],
        out_shape=[
            jax.ShapeDtypeStruct((t, PROJ_W), BF16),
            jax.ShapeDtypeStruct((N_EXPERTS, D_MODEL, D_EXPERT), BF16),
            jax.ShapeDtypeStruct((N_EXPERTS, D_MODEL, D_EXPERT), BF16),
            jax.ShapeDtypeStruct((N_EXPERTS, D_EXPERT, D_MODEL), BF16),
        ],
        scratch_shapes=[pltpu.VMEM((PROJ_W, D_MODEL), BF16)],
        compiler_params=_cparams(1),
        name="inproj",
    )(x2, g_all, wt_all, wg_all, wu_all, wd_all)


ATT_QB = 2


def _attn_kernel(bucket_ref, relb_ref, sink_ref, q_ref, k_ref, v_ref, o_ref,
                 bias_ref, logit_ref, prob_ref, *, nblk):
    b = pl.program_id(0)
    n = pl.program_id(1)
    kw = 3 * BLOCK

    @pl.when((b == 0) & (n == 0))
    def _build_bias():
        bucket = bucket_ref[...]
        row = lax.broadcasted_iota(jnp.int32, (BLOCK, kw), 0)
        col = lax.broadcasted_iota(jnp.int32, (BLOCK, kw), 1)
        win = jnp.abs(col - BLOCK - row) <= WINDOW
        keep = (win & (col >= BLOCK), win, win & (col < 2 * BLOCK))
        for hq in range(A_HEADS):
            acc = jnp.zeros((BLOCK, kw), F32)
            for bb in range(N_BUCKETS):
                acc = jnp.where(bucket == bb, relb_ref[bb, hq], acc)
            for v in range(3):
                bias_ref[v, hq] = jnp.where(keep[v], acc, NEG_INF)

    q = (q_ref[...].astype(F32) * (A_HEAD_DIM ** -0.5)).astype(BF16)
    r_prev = pl.multiple_of(jnp.maximum(n * ATT_QB - 1, 0) * BLOCK, BLOCK)
    r_own = pl.multiple_of(n * (ATT_QB * BLOCK), ATT_QB * BLOCK)
    r_next = pl.multiple_of(jnp.minimum((n + 1) * ATT_QB, nblk - 1) * BLOCK, BLOCK)

    def rows3(ref):
        return jnp.concatenate([ref[pl.ds(r_prev, BLOCK), :], ref[pl.ds(r_own, ATT_QB * BLOCK), :],
                                ref[pl.ds(r_next, BLOCK), :]], axis=0)
    kall = rows3(k_ref)
    vall = rows3(v_ref)
    jobs = [(j, hq) for j in range(ATT_QB) for hq in range(A_HEADS)]

    def keys(x, j, hq):
        h = hq // A_GROUP
        return x[j * BLOCK:(j + 3) * BLOCK, h * A_HEAD_DIM:(h + 1) * A_HEAD_DIM]

    for i, (j, hq) in enumerate(jobs):
        logit_ref[i] = lax.dot_general(
            q[j * BLOCK:(j + 1) * BLOCK, hq * A_HEAD_DIM:(hq + 1) * A_HEAD_DIM], keys(kall, j, hq),
            (((1,), (1,)), ((), ())), preferred_element_type=F32)
    dens = []
    for i, (j, hq) in enumerate(jobs):
        blk = n * ATT_QB + j
        variant = jnp.where(blk == 0, 0, jnp.where(blk == nblk - 1, 2, 1))
        s = logit_ref[i] + bias_ref[variant, hq]
        sk = sink_ref[hq]
        m = jnp.maximum(jnp.max(s, axis=-1, keepdims=True), sk)
        p = jnp.exp(s - m)
        dens.append(jnp.sum(p, axis=-1, keepdims=True) + jnp.exp(sk - m))
        prob_ref[i] = p.astype(BF16)
    outs = [jnp.dot(prob_ref[i], keys(vall, j, hq), preferred_element_type=F32) / dens[i]
            for i, (j, hq) in enumerate(jobs)]
    for j in range(ATT_QB):
        o_ref[j * BLOCK:(j + 1) * BLOCK, :] = jnp.concatenate(
            outs[j * A_HEADS:(j + 1) * A_HEADS], axis=1).astype(BF16)


def _attention(proj, bucket, rel_bias, sink, bsz, seq):
    nblk = seq // BLOCK
    t = bsz * seq
    cq = COL_AQ // QA_W
    ck = COL_AK // KA_W
    cv = COL_AV // KA_W

    nsteps = nblk // ATT_QB
    rows = ATT_QB * BLOCK

    def kv_spec(c):
        return pl.BlockSpec((seq, KA_W), lambda b, n: (b, c))

    smem = pl.BlockSpec(memory_space=pltpu.SMEM)
    return pl.pallas_call(
        functools.partial(_attn_kernel, nblk=nblk),
        grid=(bsz, nsteps),
        in_specs=[
            pl.BlockSpec((BLOCK, 3 * BLOCK), lambda b, n: (0, 0)),
            smem, smem,
            pl.BlockSpec((rows, QA_W), lambda b, n: (b * nsteps + n, cq)),
            kv_spec(ck), kv_spec(cv),
        ],
        out_specs=pl.BlockSpec((rows, QA_W), lambda b, n: (b * nsteps + n, 0)),
        out_shape=jax.ShapeDtypeStruct((t, QA_W), BF16),
        scratch_shapes=[pltpu.VMEM((3, A_HEADS, BLOCK, 3 * BLOCK), F32),
                        pltpu.VMEM((ATT_QB * A_HEADS, BLOCK, 3 * BLOCK), F32),
                        pltpu.VMEM((ATT_QB * A_HEADS, BLOCK, 3 * BLOCK), BF16)],
        compiler_params=_cparams(2),
        name="attn",
    )(bucket, rel_bias, sink, proj, proj, proj)


def _log_sigmoid(z):
    return jnp.minimum(z, 0.0) - jnp.log(1.0 + jnp.exp(-jnp.abs(z)))


def _log_decay(r, wr, br):
    z = jnp.dot(r, wr, preferred_element_type=F32) + br
    return _log_sigmoid(z) / GATE_TAU


def _chunk_masks():
    row = lax.broadcasted_iota(jnp.int32, (GLA_TILE, GLA_TILE), 0)
    col = lax.broadcasted_iota(jnp.int32, (GLA_TILE, GLA_TILE), 1)
    sh = int(math.log2(CHUNK))
    same = jnp.right_shift(row, sh) == jnp.right_shift(col, sh)
    return same, same & (col <= row), same & (col >= row)


def _split3(x):
    hi = x.astype(BF16)
    r1 = x - hi.astype(F32)
    mid = r1.astype(BF16)
    lo = (r1 - mid.astype(F32)).astype(BF16)
    return hi, mid, lo


def _mask_dot(mask, x):
    m = mask.astype(BF16)
    hi, mid, lo = _split3(x)
    return (jnp.dot(m, hi, preferred_element_type=F32) + jnp.dot(m, mid, preferred_element_type=F32)
            + jnp.dot(m, lo, preferred_element_type=F32))


def _gla_state_dir(k_ref, v_ref, r_ref, wrt_ref, brc_ref, s_out_ref, state_ref, order, cum_mask, same):
    z = lax.dot_general(wrt_ref[...], r_ref[...], (((1,), (1,)), ((), ())),
                        preferred_element_type=F32) + brc_ref[:, 0:1]
    hi, mid, lo = _split3(_log_sigmoid(z) / GATE_TAU)
    m = jnp.concatenate([cum_mask, same], axis=1).astype(BF16)
    both = (jnp.dot(hi, m, preferred_element_type=F32) + jnp.dot(mid, m, preferred_element_type=F32)
            + jnp.dot(lo, m, preferred_element_type=F32))
    cum = both[:, :GLA_TILE]
    tot = both[:, GLA_TILE:]
    k_end = (k_ref[...].astype(F32).T * jnp.exp(tot - cum)).astype(BF16)
    lane_chunk = jnp.right_shift(lax.broadcasted_iota(jnp.int32, (1, GLA_TILE), 1),
                                 int(math.log2(CHUNK)))
    upd = []
    for h in range(B_HEADS):
        kh = k_end[h * B_KEY_DIM:(h + 1) * B_KEY_DIM]
        lhs = jnp.concatenate([jnp.where(lane_chunk == c, kh, 0.0).astype(BF16) for c in range(CPT)],
                              axis=0)
        upd.append(jnp.dot(lhs, v_ref[:, h * B_VAL_DIM:(h + 1) * B_VAL_DIM],
                           preferred_element_type=F32))
    for c in order:
        s_out_ref[0, c] = state_ref[...].astype(BF16)
        upd_c = jnp.concatenate([u[c * B_KEY_DIM:(c + 1) * B_KEY_DIM] for u in upd], axis=0)
        decay = jnp.exp(tot[:, c * CHUNK:c * CHUNK + 1])
        state_ref[...] = decay * state_ref[...] + upd_c


def _gla_state_kernel(kf_ref, vf_ref, rf_ref, kb_ref, vb_ref, rb_ref,
                      wrf_ref, brf_ref, wrb_ref, brb_ref,
                      sf_ref, sb_ref, stf_ref, stb_ref):
    @pl.when(pl.program_id(1) == 0)
    def _reset():
        stf_ref[...] = jnp.zeros_like(stf_ref)
        stb_ref[...] = jnp.zeros_like(stb_ref)

    same, lower, upper = _chunk_masks()
    _gla_state_dir(kf_ref, vf_ref, rf_ref, wrf_ref, brf_ref, sf_ref, stf_ref,
                   range(CPT), upper, same)
    _gla_state_dir(kb_ref, vb_ref, rb_ref, wrb_ref, brb_ref, sb_ref, stb_ref,
                   range(CPT - 1, -1, -1), lower, same)


def _gla_states(proj, wrf_t, brf_c, wrb_t, brb_c, bsz, seq):
    nt = seq // GLA_TILE
    nchunks = seq // CHUNK
    hk = B_HEADS * B_KEY_DIM
    ck, cv, cr = COL_BK // QB_W, COL_BV // VB_W, COL_R // LANE

    def fwd(b, i):
        return b * nt + i

    def bwd(b, i):
        return b * nt + (nt - 1 - i)

    def tile_specs(f):
        return [pl.BlockSpec((GLA_TILE, QB_W), lambda b, i: (f(b, i), ck)),
                pl.BlockSpec((GLA_TILE, VB_W), lambda b, i: (f(b, i), cv)),
                pl.BlockSpec((GLA_TILE, LANE), lambda b, i: (f(b, i), cr))]

    const = lambda b, i: (0, 0)
    out_sds = jax.ShapeDtypeStruct((bsz, nchunks, hk, B_VAL_DIM), BF16)
    return pl.pallas_call(
        _gla_state_kernel,
        grid=(bsz, nt),
        in_specs=tile_specs(fwd) + tile_specs(bwd) + [
            pl.BlockSpec((hk, LANE), const), pl.BlockSpec((hk, 1), const),
            pl.BlockSpec((hk, LANE), const), pl.BlockSpec((hk, 1), const)],
        out_specs=[pl.BlockSpec((1, CPT, hk, B_VAL_DIM), lambda b, i: (b, i, 0, 0)),
                   pl.BlockSpec((1, CPT, hk, B_VAL_DIM), lambda b, i: (b, nt - 1 - i, 0, 0))],
        out_shape=[out_sds, out_sds],
        scratch_shapes=[pltpu.VMEM((hk, B_VAL_DIM), F32), pltpu.VMEM((hk, B_VAL_DIM), F32)],
        compiler_params=_cparams(2),
        name="gla_state",
    )(proj, proj, proj, proj, proj, proj, wrf_t, brf_c, wrb_t, brb_c)


def _gla_out_kernel(q_ref, k_ref, v_ref, r_ref, g_ref, sf_ref, sb_ref,
                    wrf_ref, brf_ref, wrb_ref, brb_ref, ng_ref, o_ref):
    same, lower, upper = _chunk_masks()
    r = r_ref[...]
    cum_f = _mask_dot(lower, _log_decay(r, wrf_ref[...], brf_ref[...]))
    cum_b = _mask_dot(upper, _log_decay(r, wrb_ref[...], brb_ref[...]))
    q = q_ref[...].astype(F32) * (B_KEY_DIM ** -0.5)
    k = k_ref[...].astype(F32)
    qd_f = q * jnp.exp(cum_f)
    qd_b = q * jnp.exp(cum_b)
    ki_f = (k * jnp.exp(-cum_f)).astype(BF16)
    ki_b = (k * jnp.exp(-cum_b)).astype(BF16)
    lane_head = jnp.right_shift(
        lax.broadcasted_iota(jnp.int32, (1, B_HEADS * B_KEY_DIM), 1), int(math.log2(B_KEY_DIM)))

    def stack_heads(x):
        return jnp.concatenate(
            [jnp.where(lane_head == h, x, 0.0) for h in range(B_HEADS)], axis=0).astype(BF16)

    qs_f = stack_heads(qd_f)
    qs_b = stack_heads(qd_b)
    nt_dims = (((1,), (1,)), ((), ()))
    sc_f = lax.dot_general(qs_f, ki_f, nt_dims, preferred_element_type=F32)
    sc_b = lax.dot_general(qs_b, ki_b, nt_dims, preferred_element_type=F32)
    lower_s = jnp.concatenate([lower] * B_HEADS, axis=0)
    strict_upper_s = jnp.concatenate([upper & ~lower] * B_HEADS, axis=0)
    p = jnp.where(lower_s, sc_f, jnp.where(strict_upper_s, sc_b, 0.0)).astype(BF16)
    v = v_ref[...]
    inter = []
    for c in range(CPT):
        lhs_f = jnp.concatenate(
            [qs_f[h * GLA_TILE + c * CHUNK:h * GLA_TILE + (c + 1) * CHUNK] for h in range(B_HEADS)], axis=0)
        lhs_b = jnp.concatenate(
            [qs_b[h * GLA_TILE + c * CHUNK:h * GLA_TILE + (c + 1) * CHUNK] for h in range(B_HEADS)], axis=0)
        inter.append(jnp.dot(lhs_f, sf_ref[0, c], preferred_element_type=F32)
                     + jnp.dot(lhs_b, sb_ref[0, c], preferred_element_type=F32))
    ng = ng_ref[...]
    g = g_ref[...].astype(F32)
    for h in range(B_HEADS):
        vs = slice(h * B_VAL_DIM, (h + 1) * B_VAL_DIM)
        o = jnp.dot(p[h * GLA_TILE:(h + 1) * GLA_TILE], v[:, vs], preferred_element_type=F32)
        o = o + jnp.concatenate([inter[c][h * CHUNK:(h + 1) * CHUNK] for c in range(CPT)], axis=0)
        o = o * lax.rsqrt(jnp.mean(o * o, axis=-1, keepdims=True) + EPS) * ng
        gh = g[:, vs]
        o_ref[:, vs] = (o * (gh / (1.0 + jnp.exp(-gh)))).astype(BF16)


def _gla_out(proj, s_f, s_b, wrf, brf, wrb, brb, ng, bsz, seq):
    nt = seq // GLA_TILE
    t = bsz * seq
    hk = B_HEADS * B_KEY_DIM
    const = lambda b, i: (0, 0)
    row = lambda c: (lambda b, i: (b * nt + i, c))
    return pl.pallas_call(
        _gla_out_kernel,
        grid=(bsz, nt),
        in_specs=[
            pl.BlockSpec((GLA_TILE, QB_W), row(COL_BQ // QB_W)),
            pl.BlockSpec((GLA_TILE, QB_W), row(COL_BK // QB_W)),
            pl.BlockSpec((GLA_TILE, VB_W), row(COL_BV // VB_W)),
            pl.BlockSpec((GLA_TILE, LANE), row(COL_R // LANE)),
            pl.BlockSpec((GLA_TILE, VB_W), row(COL_BG // VB_W)),
            pl.BlockSpec((1, CPT, hk, B_VAL_DIM), lambda b, i: (b, i, 0, 0)),
            pl.BlockSpec((1, CPT, hk, B_VAL_DIM), lambda b, i: (b, i, 0, 0)),
            pl.BlockSpec((LANE, hk), const), pl.BlockSpec((1, hk), const),
            pl.BlockSpec((LANE, hk), const), pl.BlockSpec((1, hk), const),
            pl.BlockSpec((1, B_VAL_DIM), const),
        ],
        out_specs=pl.BlockSpec((GLA_TILE, VB_W), lambda b, i: (b * nt + i, 0)),
        out_shape=jax.ShapeDtypeStruct((t, VB_W), BF16),
        compiler_params=_cparams(2),
        name="gla_out",
    )(proj, proj, proj, proj, proj, s_f, s_b, wrf, brf, wrb, brb, ng)


def _sigmoid(x):
    return 1.0 / (1.0 + jnp.exp(-x))


def _merge_kernel(ya_ref, yb_ref, ga_ref, gb_ref, x_ref, wa_ref, wb_ref, wo_ref, o_ref,
                  wa_bf, wb_bf, wo_bf):
    @pl.when(pl.program_id(0) == 0)
    def _prep():
        wa_bf[...] = wa_ref[0].astype(BF16)
        wb_bf[...] = wb_ref[0].astype(BF16)
        wo_bf[...] = wo_ref[0].astype(BF16)

    a = jnp.dot(ya_ref[...], wa_bf[...], preferred_element_type=F32)
    b = jnp.dot(yb_ref[...], wb_bf[...], preferred_element_type=F32)
    merged = _sigmoid(ga_ref[...].astype(F32)) * a + _sigmoid(gb_ref[...].astype(F32)) * b
    o_ref[...] = x_ref[...] + jnp.dot(merged.astype(BF16), wo_bf[...], preferred_element_type=F32)


def _merge(ya, yb, proj, x2, wa_all, wb_all, wo_all, layer):
    t = x2.shape[0]
    tm = min(MERGE_TILE, t)
    wsel = lambda i: (layer, 0, 0)
    return pl.pallas_call(
        _merge_kernel,
        grid=(t // tm,),
        in_specs=[
            pl.BlockSpec((tm, QA_W), lambda i: (i, 0)),
            pl.BlockSpec((tm, VB_W), lambda i: (i, 0)),
            pl.BlockSpec((tm, D_MODEL), lambda i: (i, COL_GA // D_MODEL)),
            pl.BlockSpec((tm, D_MODEL), lambda i: (i, COL_GB // D_MODEL)),
            pl.BlockSpec((tm, D_MODEL), lambda i: (i, 0)),
            pl.BlockSpec((1, QA_W, D_MODEL), wsel, pipeline_mode=pl.Buffered(1)),
            pl.BlockSpec((1, VB_W, D_MODEL), wsel, pipeline_mode=pl.Buffered(1)),
            pl.BlockSpec((1, D_MODEL, D_MODEL), wsel, pipeline_mode=pl.Buffered(1)),
        ],
        out_specs=pl.BlockSpec((tm, D_MODEL), lambda i: (i, 0)),
        out_shape=jax.ShapeDtypeStruct((t, D_MODEL), F32),
        scratch_shapes=[pltpu.VMEM((QA_W, D_MODEL), BF16), pltpu.VMEM((VB_W, D_MODEL), BF16),
                        pltpu.VMEM((D_MODEL, D_MODEL), BF16)],
        compiler_params=_cparams(1),
        name="merge",
    )(ya, yb, proj, proj, x2, wa_all, wb_all, wo_all)


def _moe_tile_tokens(t):
    return min(4096, t)


MOE_BLK = 512
MOE_UNIT = 16
MOE_CAP = 1536
MOE_CAP_COMMON = 1280
XS_W = D_MODEL + LANE
MOE_MT = 384
MOE_CH = 2 * MOE_MT
MOE_CHU = MOE_CH // MOE_UNIT
MOE_NC = 256
MOE_EPS = 2
MOE_ESTEPS = N_EXPERTS // MOE_EPS
assert MOE_CAP >= 2 * MOE_BLK + N_EXPERTS * (MOE_UNIT - 1) and MOE_CAP % MOE_UNIT == 0


NT_DIMS = (((1,), (1,)), ((), ()))


def _route(hn_bf, wrt_ref, brc_ref):
    logits = lax.dot_general(wrt_ref[...].astype(BF16), hn_bf, NT_DIMS,
                             preferred_element_type=F32) + brc_ref[:, 0:1]
    row = lax.broadcasted_iota(jnp.int32, logits.shape, 0).astype(F32)
    big = float(LANE)
    ninf = -jnp.inf
    gl = jnp.where(row < N_GROUPS, logits, ninf)
    gmax = jnp.max(gl, axis=0, keepdims=True)
    g_idx = jnp.min(jnp.where(gl == gmax, row, big), axis=0, keepdims=True)
    g_w = 1.0 / jnp.sum(jnp.exp(gl - gmax), axis=0, keepdims=True)
    lo = float(N_GROUPS) + g_idx * float(EXPERTS_PER_GROUP)
    el = jnp.where((row >= lo) & (row < lo + EXPERTS_PER_GROUP), logits, ninf)
    v1 = jnp.max(el, axis=0, keepdims=True)
    i1 = jnp.min(jnp.where(el == v1, row, big), axis=0, keepdims=True)
    el2 = jnp.where(row == i1, ninf, el)
    v2 = jnp.max(el2, axis=0, keepdims=True)
    i2 = jnp.min(jnp.where(el2 == v2, row, big), axis=0, keepdims=True)
    e21 = jnp.exp(v2 - v1)
    w1 = g_w / (1.0 + e21)
    w2 = g_w * e21 / (1.0 + e21)
    return i1 - float(N_GROUPS), i2 - float(N_GROUPS), w1, w2, row


def _slot_matrix(pos1, pos2, second):
    slot = lax.broadcasted_iota(jnp.int32, (pos1.shape[0], MOE_CAP), 1)
    return jnp.where(slot == pos1.astype(jnp.int32), 1.0,
                     jnp.where(slot == pos2.astype(jnp.int32), second, 0.0)).astype(BF16)


def _slot_matrix_t(pos1, pos2, second):
    slot = lax.broadcasted_iota(jnp.int32, (MOE_CAP, pos1.shape[1]), 0)
    return jnp.where(slot == pos1.astype(jnp.int32), 1.0,
                     jnp.where(slot == pos2.astype(jnp.int32), second, 0.0)).astype(BF16)


def _split3_f32(w):
    hi = w.astype(BF16).astype(F32)
    r1 = w - hi
    mid = r1.astype(BF16).astype(F32)
    return hi, mid, r1 - mid


def _rows_to_sublanes(vals, row):
    out = jnp.zeros(row.shape, F32)
    for k, v in enumerate(vals):
        out = jnp.where(row == float(k), v, out)
    return out


def _moe_kernel(x_ref, g_ref, wrt_ref, brc_ref, wg_ref, wu_ref, wd_ref, gf_ref, o_ref,
                xs_ref, pos_ref, tab_v, tab_s, xt_ref, addr_ref, sem,
                *, nblk, final_norm):
    s = pl.program_id(1)
    dump_row = nblk * MOE_CAP

    @pl.when(s < nblk)
    def _dispatch():
        @pl.when(s == 0)
        def _init():
            xs_ref[dump_row:dump_row + MOE_UNIT, :] = jnp.zeros((MOE_UNIT, XS_W), BF16)
            xt_ref[...] = jnp.zeros_like(xt_ref)

        x = x_ref[...]
        ms = jnp.mean(x * x, axis=-1, keepdims=True)
        hn = (x * lax.rsqrt(ms + EPS) * g_ref[0]).astype(BF16)
        e1, e2, w1, w2, row = _route(hn, wrt_ref, brc_ref)
        onehot = ((row == e1) | (row == e2)).astype(F32)
        earlier = (lax.broadcasted_iota(jnp.int32, (MOE_BLK, MOE_BLK), 0)
                   < lax.broadcasted_iota(jnp.int32, (MOE_BLK, MOE_BLK), 1)).astype(BF16)
        before = jnp.dot(onehot.astype(BF16), earlier, preferred_element_type=F32)
        counts = jnp.sum(onehot, axis=1, keepdims=True)
        units = jnp.floor((counts + (MOE_UNIT - 1.0)) * (1.0 / MOE_UNIT))
        lower = (lax.broadcasted_iota(jnp.int32, (LANE, LANE), 1)
                 < lax.broadcasted_iota(jnp.int32, (LANE, LANE), 0)).astype(BF16)
        padded = jnp.broadcast_to(units * MOE_UNIT, (LANE, LANE))
        run_off = jnp.dot(lower, padded.astype(BF16), preferred_element_type=F32)[:, 0:1]
        start = run_off + before
        pos1 = jnp.sum(jnp.where(row == e1, start, 0.0), axis=0, keepdims=True)
        pos2 = jnp.sum(jnp.where(row == e2, start, 0.0), axis=0, keepdims=True)
        w_rows = _rows_to_sublanes(
            _split3_f32(w1) + _split3_f32(w2) + (jnp.ones_like(w1),), row).astype(BF16)
        used = jnp.sum(units, axis=0, keepdims=True) * MOE_UNIT
        pt_t = _slot_matrix_t(pos1, pos2, 2.0)
        rows = pl.ds(pl.multiple_of(s * MOE_CAP, MOE_CAP), MOE_CAP)
        for c in range(D_MODEL // MOE_NC):
            cs = slice(c * MOE_NC, (c + 1) * MOE_NC)
            xs_ref[rows, cs] = jnp.dot(pt_t, hn[:, cs], preferred_element_type=F32).astype(BF16)
        xs_ref[rows, D_MODEL:] = lax.dot_general(pt_t, w_rows, NT_DIMS,
                                                 preferred_element_type=F32).astype(BF16)
        pos_ref[s] = _rows_to_sublanes((pos1, pos2), row).T
        lane_e = lax.broadcasted_iota(jnp.int32, (LANE, LANE), 1)
        per_expert = jnp.where(lane_e == 0, run_off,
                               jnp.where(lane_e == 1, units, jnp.where(lane_e == 2, used, 0.0)))
        tab_v[s] = per_expert.T[0:SUBLANE, :].astype(jnp.int32)

        @pl.when(s == nblk - 1)
        def _publish():
            cp = pltpu.make_async_copy(tab_v, tab_s, sem)
            cp.start()
            cp.wait()

    def one_expert(e, wi):
        total = tab_s[0, 1, e]
        for j in range(1, nblk):
            total = total + tab_s[j, 1, e]

        def chunk(c, carry):
            u0 = c * MOE_CHU
            for k in range(MOE_CHU):
                addr_ref[k] = dump_row
            cum = 0
            for j in range(nblk):
                nj = tab_s[j, 1, e]
                run0 = j * MOE_CAP + tab_s[j, 0, e]
                k_off = cum - u0

                def copy_unit(u, c2, run0=run0, k_off=k_off):
                    src = pl.multiple_of(run0 + u * MOE_UNIT, MOE_UNIT)
                    dst = pl.multiple_of((k_off + u) * MOE_UNIT, MOE_UNIT)
                    xt_ref[pl.ds(dst, MOE_UNIT), :] = xs_ref[pl.ds(src, MOE_UNIT), :]
                    addr_ref[k_off + u] = src
                    return c2
                lax.fori_loop(jnp.clip(u0 - cum, 0, nj), jnp.clip(u0 + MOE_CHU - cum, 0, nj),
                              copy_unit, 0)
                cum = cum + nj
            rows_here = jnp.minimum(MOE_CHU, total - u0) * MOE_UNIT

            def tile(t, c3):
                xt = xt_ref[pl.ds(pl.multiple_of(t * MOE_MT, MOE_MT), MOE_MT), :]
                wl = xt[:, D_MODEL:].astype(F32)
                second = wl[:, 6:7] > 1.5
                wcol = jnp.where(second, 0.5 * (wl[:, 3:4] + wl[:, 4:5] + wl[:, 5:6]),
                                 wl[:, 0:1] + wl[:, 1:2] + wl[:, 2:3])
                undo = jnp.where(second, 0.5, 1.0)
                xf = xt[:, :D_MODEL]
                gate = jnp.dot(xf, wg_ref[wi], preferred_element_type=F32) * undo
                up = jnp.dot(xf, wu_ref[wi], preferred_element_type=F32) * undo
                h = gate * _sigmoid(gate) * up * wcol
                y = jnp.dot(h.astype(BF16), wd_ref[wi], preferred_element_type=F32).astype(BF16)
                for uu in range(MOE_MT // MOE_UNIT):
                    dst = pl.multiple_of(addr_ref[t * (MOE_MT // MOE_UNIT) + uu], MOE_UNIT)
                    xs_ref[pl.ds(dst, MOE_UNIT), :D_MODEL] = y[uu * MOE_UNIT:(uu + 1) * MOE_UNIT]
                return c3
            lax.fori_loop(0, (rows_here + MOE_MT - 1) // MOE_MT, tile, 0)
            return carry
        lax.fori_loop(0, (total + MOE_CHU - 1) // MOE_CHU, chunk, 0)

    @pl.when((s >= nblk) & (s < nblk + MOE_ESTEPS))
    def _experts():
        for wi in range(MOE_EPS):
            one_expert((s - nblk) * MOE_EPS + wi, wi)

    @pl.when(s >= nblk + MOE_ESTEPS)
    def _combine():
        j = s - nblk - MOE_ESTEPS
        p = pos_ref[j]
        pt = _slot_matrix(p[:, 0:1], p[:, 1:2], 1.0)
        base = pl.multiple_of(j * MOE_CAP, MOE_CAP)
        head = pl.ds(base, MOE_CAP_COMMON)
        for c in range(D_MODEL // MOE_NC):
            cs = slice(c * MOE_NC, (c + 1) * MOE_NC)
            o_ref[:, cs] = x_ref[:, cs] + jnp.dot(pt[:, :MOE_CAP_COMMON], xs_ref[head, cs],
                                                  preferred_element_type=F32)

        @pl.when(tab_s[j, 2, 0] > MOE_CAP_COMMON)
        def _tail():
            tail = pl.ds(base + MOE_CAP_COMMON, MOE_CAP - MOE_CAP_COMMON)
            for c in range(D_MODEL // MOE_NC):
                cs = slice(c * MOE_NC, (c + 1) * MOE_NC)
                o_ref[:, cs] += jnp.dot(pt[:, MOE_CAP_COMMON:], xs_ref[tail, cs],
                                        preferred_element_type=F32)
        if final_norm:
            y = o_ref[...]
            ms = jnp.mean(y * y, axis=-1, keepdims=True)
            o_ref[...] = y * lax.rsqrt(ms + EPS) * gf_ref[...]


def _moe(x2, g_all, w_r, b_r, wg_all, wu_all, wd_all, g_final, layer, tt, final_norm):
    t = x2.shape[0]
    nblk = tt // MOE_BLK
    steps = 2 * nblk + MOE_ESTEPS

    def tok(b, s):
        j = jnp.where(s < nblk, s, jnp.where(s < nblk + MOE_ESTEPS, nblk - 1, s - nblk - MOE_ESTEPS))
        return (b * nblk + j, 0)

    def out_tok(b, s):
        return (b * nblk + jnp.maximum(s - nblk - MOE_ESTEPS, 0), 0)

    wsel = lambda b, s: (jnp.clip(s - nblk, 0, MOE_ESTEPS - 1), 0, 0)
    const = lambda b, s: (0, 0)
    return pl.pallas_call(
        functools.partial(_moe_kernel, nblk=nblk, final_norm=final_norm),
        grid=(t // tt, steps),
        in_specs=[
            pl.BlockSpec((MOE_BLK, D_MODEL), tok),
            pl.BlockSpec((1, 1, D_MODEL), lambda b, s: (layer, 0, 0)),
            pl.BlockSpec((LANE, D_MODEL), const),
            pl.BlockSpec((LANE, 1), const),
            pl.BlockSpec((MOE_EPS, D_MODEL, D_EXPERT), wsel),
            pl.BlockSpec((MOE_EPS, D_MODEL, D_EXPERT), wsel),
            pl.BlockSpec((MOE_EPS, D_EXPERT, D_MODEL), wsel),
            pl.BlockSpec((1, D_MODEL), const),
        ],
        out_specs=pl.BlockSpec((MOE_BLK, D_MODEL), out_tok),
        out_shape=jax.ShapeDtypeStruct((t, D_MODEL), F32),
        scratch_shapes=[
            pltpu.VMEM((nblk * MOE_CAP + MOE_UNIT, XS_W), BF16),
            pltpu.VMEM((nblk, MOE_BLK, LANE), F32),
            pltpu.VMEM((nblk, SUBLANE, LANE), jnp.int32),
            pltpu.SMEM((nblk, SUBLANE, LANE), jnp.int32),
            pltpu.VMEM((MOE_CH, XS_W), BF16),
            pltpu.SMEM((MOE_CHU,), jnp.int32),
            pltpu.SemaphoreType.DMA,
        ],
        compiler_params=_cparams(2),
        name="moe",
    )(x2, g_all, w_r, b_r, wg_all, wu_all, wd_all, g_final)


def _t5_bucket(rel):
    nb = N_BUCKETS // 2
    max_exact = nb // 2
    base = jnp.where(rel > 0, nb, 0)
    n = jnp.abs(rel)
    large = max_exact + (jnp.log(jnp.maximum(n, 1).astype(jnp.float32) / max_exact)
                         / math.log(MAX_DISTANCE / max_exact) * (nb - max_exact)).astype(jnp.int32)
    large = jnp.minimum(large, nb - 1)
    return base + jnp.where(n < max_exact, n, large)


def _pad_rank(wr):
    return jnp.pad(wr, ((0, LANE - B_RANK), (0, 0))).astype(BF16)


def kernel(x, w_in, rel_bias, attn_sink, gla_wr_fwd, gla_br_fwd, gla_wr_bwd, gla_br_bwd, gla_norm, w_branch_a, w_branch_b, w_out, norm_mix, norm_ffn, router_group_w, router_group_b, router_expert_w, router_expert_b, expert_w_gate, expert_w_up, expert_w_down, norm_final):
    bsz, seq, d = x.shape
    depth = w_in.shape[0]
    t = bsz * seq
    tt = _moe_tile_tokens(t)
    q_off = jnp.arange(BLOCK)
    k_off = jnp.arange(3 * BLOCK) - BLOCK
    bucket = _t5_bucket(k_off[None, :] - q_off[:, None]).astype(jnp.int32) & (N_BUCKETS - 1)
    x2 = x.reshape(t, d)
    norm_mix3 = norm_mix[:, None, :]
    w_in_t = jnp.swapaxes(w_in, 1, 2)
    norm_ffn3 = norm_ffn[:, None, :]
    wg_all = expert_w_gate.reshape(depth * N_EXPERTS, d, D_EXPERT)
    wu_all = expert_w_up.reshape(depth * N_EXPERTS, d, D_EXPERT)
    wd_all = expert_w_down.reshape(depth * N_EXPERTS, D_EXPERT, d)
    for l in range(depth):
        proj, wg_bf, wu_bf, wd_bf = _inproj(x2, norm_mix3, w_in_t, wg_all, wu_all, wd_all, l)
        ya = _attention(proj, bucket, rel_bias, attn_sink[l], bsz, seq)
        wrf, wrb = _pad_rank(gla_wr_fwd[l]), _pad_rank(gla_wr_bwd[l])
        brf, brb = gla_br_fwd[l][None, :], gla_br_bwd[l][None, :]
        s_f, s_b = _gla_states(proj, wrf.T, brf.T, wrb.T, brb.T, bsz, seq)
        yb = _gla_out(proj, s_f, s_b, wrf, brf, wrb, brb, gla_norm[l][None, :], bsz, seq)
        x2 = _merge(ya, yb, proj, x2, w_branch_a, w_branch_b, w_out, l)
        w_r = jnp.concatenate(
            [router_group_w[l].T, router_expert_w[l].T,
             jnp.zeros((LANE - N_GROUPS - N_EXPERTS, d), F32)], axis=0)
        b_r = jnp.concatenate(
            [router_group_b[l], router_expert_b[l],
             jnp.zeros((LANE - N_GROUPS - N_EXPERTS,), F32)])[:, None]
        x2 = _moe(x2, norm_ffn3, w_r, b_r, wg_bf, wu_bf, wd_bf, norm_final[None, :], l, tt,
                  final_norm=(l == depth - 1))
    return x2.reshape(bsz, seq, d)
```

```python
import functools
import math

import numpy as np
import jax
import jax.numpy as jnp
from jax import lax
from jax.experimental import pallas as pl
from jax.experimental.pallas import tpu as pltpu

F32 = jnp.float32
BF16 = jnp.bfloat16

D_MODEL = 1024
A_HEADS = 8
A_KV_HEADS = 2
A_HEAD_DIM = 64
A_GROUP = A_HEADS // A_KV_HEADS
WINDOW = 128
BLOCK = 128
N_BUCKETS = 32
MAX_DISTANCE = 128
B_HEADS = 4
B_KEY_DIM = 64
B_VAL_DIM = 128
B_RANK = 16
GATE_TAU = 16.0
CHUNK = 64
N_GROUPS = 4
EXPERTS_PER_GROUP = 8
N_EXPERTS = N_GROUPS * EXPERTS_PER_GROUP
D_EXPERT = 256
EPS = 1e-6
NEG_INF = -1e30

LANE = 128
SUBLANE = 8
VMEM_LIMIT = 56 * 1024 * 1024

QA_W = A_HEADS * A_HEAD_DIM
KA_W = A_KV_HEADS * A_HEAD_DIM
QB_W = B_HEADS * B_KEY_DIM
VB_W = B_HEADS * B_VAL_DIM
COL_GA = 0
COL_GB = COL_GA + D_MODEL
COL_AQ = COL_GB + D_MODEL
COL_BV = COL_AQ + QA_W
COL_BG = COL_BV + VB_W
COL_BQ = COL_BG + VB_W
COL_BK = COL_BQ + QB_W
COL_AK = COL_BK + QB_W
COL_AV = COL_AK + KA_W
COL_R = COL_AV + KA_W
PROJ_W = COL_R + LANE

ROW_TILE = 512
MERGE_TILE = 1024
MERGE_KC = 512
PROJ_CHUNK = 1280
GLA_TILE = 256
CPT = GLA_TILE // CHUNK


def _cparams(n_axes):
    return pltpu.CompilerParams(
        dimension_semantics=("arbitrary",) * n_axes, vmem_limit_bytes=VMEM_LIMIT)


_W_IN_SRC = np.cumsum([0, QA_W, KA_W, KA_W, QB_W, QB_W, VB_W, VB_W, B_RANK, D_MODEL, D_MODEL])
_W_IN_SEGMENTS = tuple(zip(
    (COL_AQ, COL_AK, COL_AV, COL_BQ, COL_BK, COL_BV, COL_BG, COL_R, COL_GA, COL_GB),
    (int(s) for s in _W_IN_SRC[:-1]),
    (int(w) for w in np.diff(_W_IN_SRC))))
W_IN_COLS = int(_W_IN_SRC[-1])
W_ROWS_PER_COPY = 256


def _load_permuted_w_in(wt_ref, wbf_ref):
    wbf_ref[COL_R:COL_R + LANE, :] = jnp.zeros((LANE, D_MODEL), BF16)
    for dst, src, width in _W_IN_SEGMENTS:
        for r0 in range(0, width, W_ROWS_PER_COPY):
            n = min(W_ROWS_PER_COPY, width - r0)
            wbf_ref[dst + r0:dst + r0 + n, :] = wt_ref[0, src + r0:src + r0 + n, :].astype(BF16)


def _inproj_kernel(x_ref, g_ref, wt_ref, wg_ref, wu_ref, wd_ref, o_ref, wgb_ref, wub_ref, wdb_ref, wbf_ref):
    @pl.when(pl.program_id(0) == 0)
    def _prep():
        _load_permuted_w_in(wt_ref, wbf_ref)

    wgb_ref[...] = wg_ref[...].astype(BF16)
    wub_ref[...] = wu_ref[...].astype(BF16)
    wdb_ref[...] = wd_ref[...].astype(BF16)

    x = x_ref[...]
    ms = jnp.mean(x * x, axis=-1, keepdims=True)
    xn = (x * lax.rsqrt(ms + EPS) * g_ref[0]).astype(BF16)
    for c0 in range(0, PROJ_W, PROJ_CHUNK):
        sl = slice(c0, min(c0 + PROJ_CHUNK, PROJ_W))
        o_ref[:, sl] = lax.dot_general(xn, wbf_ref[sl, :], (((1,), (1,)), ((), ())),
                                       preferred_element_type=F32).astype(BF16)


def _inproj(x2, g_all, wt_all, wg_all, wu_all, wd_all, layer):
    t = x2.shape[0]
    tm = min(ROW_TILE, t)
    nsteps = t // tm
    epi = -(-N_EXPERTS // nsteps)
    while N_EXPERTS % epi:
        epi += 1
    nwb = N_EXPERTS // epi
    wsel = lambda i: (layer * nwb + jnp.minimum(i, nwb - 1), 0, 0)
    wout = lambda i: (jnp.minimum(i, nwb - 1), 0, 0)
    return pl.pallas_call(
        _inproj_kernel,
        grid=(nsteps,),
        in_specs=[
            pl.BlockSpec((tm, D_MODEL), lambda i: (i, 0)),
            pl.BlockSpec((1, 1, D_MODEL), lambda i: (layer, 0, 0)),
            pl.BlockSpec((1, W_IN_COLS, D_MODEL), lambda i: (layer, 0, 0), pipeline_mode=pl.Buffered(1)),
            pl.BlockSpec((epi, D_MODEL, D_EXPERT), wsel),
            pl.BlockSpec((epi, D_MODEL, D_EXPERT), wsel),
            pl.BlockSpec((epi, D_EXPERT, D_MODEL), wsel),
        ],
        out_specs=[
            pl.BlockSpec((tm, PROJ_W), lambda i: (i, 0)),
            pl.BlockSpec((epi, D_MODEL, D_EXPERT), wout),
            pl.BlockSpec((epi, D_MODEL, D_EXPERT), wout),
            pl.BlockSpec((epi, D_EXPERT, D_MODEL), wout),
        ],
        out_shape=[
            jax.ShapeDtypeStruct((t, PROJ_W), BF16),
            jax.ShapeDtypeStruct((N_EXPERTS, D_MODEL, D_EXPERT), BF16),
            jax.ShapeDtypeStruct((N_EXPERTS, D_MODEL, D_EXPERT), BF16),
            jax.ShapeDtypeStruct((N_EXPERTS, D_EXPERT, D_MODEL), BF16),
        ],
        scratch_shapes=[pltpu.VMEM((PROJ_W, D_MODEL), BF16)],
        compiler_params=_cparams(1),
        name="inproj",
    )(x2, g_all, wt_all, wg_all, wu_all, wd_all)


ATT_QB = 2


def _attn_kernel(bucket_ref, relb_ref, sink_ref, q_ref, k_ref, v_ref, o_ref,
                 bias_ref, logit_ref, prob_ref, *, nblk):
    b = pl.program_id(0)
    n = pl.program_id(1)
    kw = 3 * BLOCK

    @pl.when((b == 0) & (n == 0))
    def _build_bias():
        bucket = bucket_ref[...]
        row = lax.broadcasted_iota(jnp.int32, (BLOCK, kw), 0)
        col = lax.broadcasted_iota(jnp.int32, (BLOCK, kw), 1)
        win = jnp.abs(col - BLOCK - row) <= WINDOW
        keep = (win & (col >= BLOCK), win, win & (col < 2 * BLOCK))
        for hq in range(A_HEADS):
            acc = jnp.zeros((BLOCK, kw), F32)
            for bb in range(N_BUCKETS):
                acc = jnp.where(bucket == bb, relb_ref[bb, hq], acc)
            for v in range(3):
                bias_ref[v, hq] = jnp.where(keep[v], acc, NEG_INF)

    q = (q_ref[...].astype(F32) * (A_HEAD_DIM ** -0.5)).astype(BF16)
    r_prev = pl.multiple_of(jnp.maximum(n * ATT_QB - 1, 0) * BLOCK, BLOCK)
    r_own = pl.multiple_of(n * (ATT_QB * BLOCK), ATT_QB * BLOCK)
    r_next = pl.multiple_of(jnp.minimum((n + 1) * ATT_QB, nblk - 1) * BLOCK, BLOCK)

    def rows3(ref):
        return jnp.concatenate([ref[pl.ds(r_prev, BLOCK), :], ref[pl.ds(r_own, ATT_QB * BLOCK), :],
                                ref[pl.ds(r_next, BLOCK), :]], axis=0)
    kall = rows3(k_ref)
    vall = rows3(v_ref)
    jobs = [(j, hq) for j in range(ATT_QB) for hq in range(A_HEADS)]

    def keys(x, j, hq):
        h = hq // A_GROUP
        return x[j * BLOCK:(j + 3) * BLOCK, h * A_HEAD_DIM:(h + 1) * A_HEAD_DIM]

    for i, (j, hq) in enumerate(jobs):
        logit_ref[i] = lax.dot_general(
            q[j * BLOCK:(j + 1) * BLOCK, hq * A_HEAD_DIM:(hq + 1) * A_HEAD_DIM], keys(kall, j, hq),
            (((1,), (1,)), ((), ())), preferred_element_type=F32)
    dens = []
    for i, (j, hq) in enumerate(jobs):
        blk = n * ATT_QB + j
        variant = jnp.where(blk == 0, 0, jnp.where(blk == nblk - 1, 2, 1))
        s = logit_ref[i] + bias_ref[variant, hq]
        sk = sink_ref[hq]
        m = jnp.maximum(jnp.max(s, axis=-1, keepdims=True), sk)
        p = jnp.exp(s - m)
        dens.append(jnp.sum(p, axis=-1, keepdims=True) + jnp.exp(sk - m))
        prob_ref[i] = p.astype(BF16)
    outs = [jnp.dot(prob_ref[i], keys(vall, j, hq), preferred_element_type=F32) / dens[i]
            for i, (j, hq) in enumerate(jobs)]
    for j in range(ATT_QB):
        o_ref[j * BLOCK:(j + 1) * BLOCK, :] = jnp.concatenate(
            outs[j * A_HEADS:(j + 1) * A_HEADS], axis=1).astype(BF16)


def _attention(proj, bucket, rel_bias, sink, bsz, seq):
    nblk = seq // BLOCK
    t = bsz * seq
    cq = COL_AQ // QA_W
    ck = COL_AK // KA_W
    cv = COL_AV // KA_W

    nsteps = nblk // ATT_QB
    rows = ATT_QB * BLOCK

    def kv_spec(c):
        return pl.BlockSpec((seq, KA_W), lambda b, n: (b, c))

    smem = pl.BlockSpec(memory_space=pltpu.SMEM)
    return pl.pallas_call(
        functools.partial(_attn_kernel, nblk=nblk),
        grid=(bsz, nsteps),
        in_specs=[
            pl.BlockSpec((BLOCK, 3 * BLOCK), lambda b, n: (0, 0)),
            smem, smem,
            pl.BlockSpec((rows, QA_W), lambda b, n: (b * nsteps + n, cq)),
            kv_spec(ck), kv_spec(cv),
        ],
        out_specs=pl.BlockSpec((rows, QA_W), lambda b, n: (b * nsteps + n, 0)),
        out_shape=jax.ShapeDtypeStruct((t, QA_W), BF16),
        scratch_shapes=[pltpu.VMEM((3, A_HEADS, BLOCK, 3 * BLOCK), F32),
                        pltpu.VMEM((ATT_QB * A_HEADS, BLOCK, 3 * BLOCK), F32),
                        pltpu.VMEM((ATT_QB * A_HEADS, BLOCK, 3 * BLOCK), BF16)],
        compiler_params=_cparams(2),
        name="attn",
    )(bucket, rel_bias, sink, proj, proj, proj)


def _log_sigmoid(z):
    return jnp.minimum(z, 0.0) - jnp.log(1.0 + jnp.exp(-jnp.abs(z)))


def _log_decay(r, wr, br):
    z = jnp.dot(r, wr, preferred_element_type=F32) + br
    return _log_sigmoid(z) / GATE_TAU


def _chunk_masks():
    row = lax.broadcasted_iota(jnp.int32, (GLA_TILE, GLA_TILE), 0)
    col = lax.broadcasted_iota(jnp.int32, (GLA_TILE, GLA_TILE), 1)
    sh = int(math.log2(CHUNK))
    same = jnp.right_shift(row, sh) == jnp.right_shift(col, sh)
    return same, same & (col <= row), same & (col >= row)


def _split3(x):
    hi = x.astype(BF16)
    r1 = x - hi.astype(F32)
    mid = r1.astype(BF16)
    lo = (r1 - mid.astype(F32)).astype(BF16)
    return hi, mid, lo


def _mask_dot(mask, x):
    m = mask.astype(BF16)
    hi, mid, lo = _split3(x)
    return (jnp.dot(m, hi, preferred_element_type=F32) + jnp.dot(m, mid, preferred_element_type=F32)
            + jnp.dot(m, lo, preferred_element_type=F32))


def _gla_state_dir(k_ref, v_ref, r_ref, wrt_ref, brc_ref, s_out_ref, state_ref, order, cum_mask, same):
    z = lax.dot_general(wrt_ref[...], r_ref[...], (((1,), (1,)), ((), ())),
                        preferred_element_type=F32) + brc_ref[:, 0:1]
    hi, mid, lo = _split3(_log_sigmoid(z) / GATE_TAU)
    m = jnp.concatenate([cum_mask, same], axis=1).astype(BF16)
    both = (jnp.dot(hi, m, preferred_element_type=F32) + jnp.dot(mid, m, preferred_element_type=F32)
            + jnp.dot(lo, m, preferred_element_type=F32))
    cum = both[:, :GLA_TILE]
    tot = both[:, GLA_TILE:]
    k_end = (k_ref[...].astype(F32).T * jnp.exp(tot - cum)).astype(BF16)
    lane_chunk = jnp.right_shift(lax.broadcasted_iota(jnp.int32, (1, GLA_TILE), 1),
                                 int(math.log2(CHUNK)))
    upd = []
    for h in range(B_HEADS):
        kh = k_end[h * B_KEY_DIM:(h + 1) * B_KEY_DIM]
        lhs = jnp.concatenate([jnp.where(lane_chunk == c, kh, 0.0).astype(BF16) for c in range(CPT)],
                              axis=0)
        upd.append(jnp.dot(lhs, v_ref[:, h * B_VAL_DIM:(h + 1) * B_VAL_DIM],
                           preferred_element_type=F32))
    for c in order:
        s_out_ref[0, c] = state_ref[...].astype(BF16)
        upd_c = jnp.concatenate([u[c * B_KEY_DIM:(c + 1) * B_KEY_DIM] for u in upd], axis=0)
        decay = jnp.exp(tot[:, c * CHUNK:c * CHUNK + 1])
        state_ref[...] = decay * state_ref[...] + upd_c


def _gla_state_kernel(kf_ref, vf_ref, rf_ref, kb_ref, vb_ref, rb_ref,
                      wrf_ref, brf_ref, wrb_ref, brb_ref,
                      sf_ref, sb_ref, stf_ref, stb_ref):
    @pl.when(pl.program_id(1) == 0)
    def _reset():
        stf_ref[...] = jnp.zeros_like(stf_ref)
        stb_ref[...] = jnp.zeros_like(stb_ref)

    same, lower, upper = _chunk_masks()
    _gla_state_dir(kf_ref, vf_ref, rf_ref, wrf_ref, brf_ref, sf_ref, stf_ref,
                   range(CPT), upper, same)
    _gla_state_dir(kb_ref, vb_ref, rb_ref, wrb_ref, brb_ref, sb_ref, stb_ref,
                   range(CPT - 1, -1, -1), lower, same)


def _gla_states(proj, wrf_t, brf_c, wrb_t, brb_c, bsz, seq):
    nt = seq // GLA_TILE
    nchunks = seq // CHUNK
    hk = B_HEADS * B_KEY_DIM
    ck, cv, cr = COL_BK // QB_W, COL_BV // VB_W, COL_R // LANE

    def fwd(b, i):
        return b * nt + i

    def bwd(b, i):
        return b * nt + (nt - 1 - i)

    def tile_specs(f):
        return [pl.BlockSpec((GLA_TILE, QB_W), lambda b, i: (f(b, i), ck)),
                pl.BlockSpec((GLA_TILE, VB_W), lambda b, i: (f(b, i), cv)),
                pl.BlockSpec((GLA_TILE, LANE), lambda b, i: (f(b, i), cr))]

    const = lambda b, i: (0, 0)
    out_sds = jax.ShapeDtypeStruct((bsz, nchunks, hk, B_VAL_DIM), BF16)
    return pl.pallas_call(
        _gla_state_kernel,
        grid=(bsz, nt),
        in_specs=tile_specs(fwd) + tile_specs(bwd) + [
            pl.BlockSpec((hk, LANE), const), pl.BlockSpec((hk, 1), const),
            pl.BlockSpec((hk, LANE), const), pl.BlockSpec((hk, 1), const)],
        out_specs=[pl.BlockSpec((1, CPT, hk, B_VAL_DIM), lambda b, i: (b, i, 0, 0)),
                   pl.BlockSpec((1, CPT, hk, B_VAL_DIM), lambda b, i: (b, nt - 1 - i, 0, 0))],
        out_shape=[out_sds, out_sds],
        scratch_shapes=[pltpu.VMEM((hk, B_VAL_DIM), F32), pltpu.VMEM((hk, B_VAL_DIM), F32)],
        compiler_params=_cparams(2),
        name="gla_state",
    )(proj, proj, proj, proj, proj, proj, wrf_t, brf_c, wrb_t, brb_c)


def _gla_out_kernel(q_ref, k_ref, v_ref, r_ref, g_ref, sf_ref, sb_ref,
                    wrf_ref, brf_ref, wrb_ref, brb_ref, ng_ref, o_ref):
    same, lower, upper = _chunk_masks()
    r = r_ref[...]
    cum_f = _mask_dot(lower, _log_decay(r, wrf_ref[...], brf_ref[...]))
    cum_b = _mask_dot(upper, _log_decay(r, wrb_ref[...], brb_ref[...]))
    q = q_ref[...].astype(F32) * (B_KEY_DIM ** -0.5)
    k = k_ref[...].astype(F32)
    qd_f = q * jnp.exp(cum_f)
    qd_b = q * jnp.exp(cum_b)
    ki_f = (k * jnp.exp(-cum_f)).astype(BF16)
    ki_b = (k * jnp.exp(-cum_b)).astype(BF16)
    lane_head = jnp.right_shift(
        lax.broadcasted_iota(jnp.int32, (1, B_HEADS * B_KEY_DIM), 1), int(math.log2(B_KEY_DIM)))

    def stack_heads(x):
        return jnp.concatenate(
            [jnp.where(lane_head == h, x, 0.0) for h in range(B_HEADS)], axis=0).astype(BF16)

    qs_f = stack_heads(qd_f)
    qs_b = stack_heads(qd_b)
    nt_dims = (((1,), (1,)), ((), ()))
    sc_f = lax.dot_general(qs_f, ki_f, nt_dims, preferred_element_type=F32)
    sc_b = lax.dot_general(qs_b, ki_b, nt_dims, preferred_element_type=F32)
    lower_s = jnp.concatenate([lower] * B_HEADS, axis=0)
    strict_upper_s = jnp.concatenate([upper & ~lower] * B_HEADS, axis=0)
    p = jnp.where(lower_s, sc_f, jnp.where(strict_upper_s, sc_b, 0.0)).astype(BF16)
    v = v_ref[...]
    inter = []
    for c in range(CPT):
        lhs_f = jnp.concatenate(
            [qs_f[h * GLA_TILE + c * CHUNK:h * GLA_TILE + (c + 1) * CHUNK] for h in range(B_HEADS)], axis=0)
        lhs_b = jnp.concatenate(
            [qs_b[h * GLA_TILE + c * CHUNK:h * GLA_TILE + (c + 1) * CHUNK] for h in range(B_HEADS)], axis=0)
        inter.append(jnp.dot(lhs_f, sf_ref[0, c], preferred_element_type=F32)
                     + jnp.dot(lhs_b, sb_ref[0, c], preferred_element_type=F32))
    ng = ng_ref[...]
    g = g_ref[...].astype(F32)
    for h in range(B_HEADS):
        vs = slice(h * B_VAL_DIM, (h + 1) * B_VAL_DIM)
        o = jnp.dot(p[h * GLA_TILE:(h + 1) * GLA_TILE], v[:, vs], preferred_element_type=F32)
        o = o + jnp.concatenate([inter[c][h * CHUNK:(h + 1) * CHUNK] for c in range(CPT)], axis=0)
        o = o * lax.rsqrt(jnp.mean(o * o, axis=-1, keepdims=True) + EPS) * ng
        gh = g[:, vs]
        o_ref[:, vs] = (o * (gh / (1.0 + jnp.exp(-gh)))).astype(BF16)


def _gla_out(proj, s_f, s_b, wrf, brf, wrb, brb, ng, bsz, seq):
    nt = seq // GLA_TILE
    t = bsz * seq
    hk = B_HEADS * B_KEY_DIM
    const = lambda b, i: (0, 0)
    row = lambda c: (lambda b, i: (b * nt + i, c))
    return pl.pallas_call(
        _gla_out_kernel,
        grid=(bsz, nt),
        in_specs=[
            pl.BlockSpec((GLA_TILE, QB_W), row(COL_BQ // QB_W)),
            pl.BlockSpec((GLA_TILE, QB_W), row(COL_BK // QB_W)),
            pl.BlockSpec((GLA_TILE, VB_W), row(COL_BV // VB_W)),
            pl.BlockSpec((GLA_TILE, LANE), row(COL_R // LANE)),
            pl.BlockSpec((GLA_TILE, VB_W), row(COL_BG // VB_W)),
            pl.BlockSpec((1, CPT, hk, B_VAL_DIM), lambda b, i: (b, i, 0, 0)),
            pl.BlockSpec((1, CPT, hk, B_VAL_DIM), lambda b, i: (b, i, 0, 0)),
            pl.BlockSpec((LANE, hk), const), pl.BlockSpec((1, hk), const),
            pl.BlockSpec((LANE, hk), const), pl.BlockSpec((1, hk), const),
            pl.BlockSpec((1, B_VAL_DIM), const),
        ],
        out_specs=pl.BlockSpec((GLA_TILE, VB_W), lambda b, i: (b * nt + i, 0)),
        out_shape=jax.ShapeDtypeStruct((t, VB_W), BF16),
        compiler_params=_cparams(2),
        name="gla_out",
    )(proj, proj, proj, proj, proj, s_f, s_b, wrf, brf, wrb, brb, ng)


def _sigmoid(x):
    return 1.0 / (1.0 + jnp.exp(-x))


def _merge_kernel(ya_ref, yb_ref, ga_ref, gb_ref, x_ref, wa_ref, wb_ref, wo_ref, o_ref,
                  wa_bf, wb_bf, wo_bf):
    @pl.when(pl.program_id(0) == 0)
    def _prep():
        wa_bf[...] = wa_ref[0].astype(BF16)
        wb_bf[...] = wb_ref[0].astype(BF16)
        wo_bf[...] = wo_ref[0].astype(BF16)

    ya = ya_ref[...]
    yb = yb_ref[...]
    for c0 in range(0, D_MODEL, MERGE_KC):
        sl = slice(c0, c0 + MERGE_KC)
        a = jnp.dot(ya, wa_bf[:, sl], preferred_element_type=F32)
        b = jnp.dot(yb, wb_bf[:, sl], preferred_element_type=F32)
        merged = _sigmoid(ga_ref[:, sl].astype(F32)) * a + _sigmoid(gb_ref[:, sl].astype(F32)) * b
        part = jnp.dot(merged.astype(BF16), wo_bf[sl, :], preferred_element_type=F32)
        acc = part if c0 == 0 else acc + part
    o_ref[...] = x_ref[...] + acc


def _merge(ya, yb, proj, x2, wa_all, wb_all, wo_all, layer):
    t = x2.shape[0]
    tm = min(MERGE_TILE, t)
    wsel = lambda i: (layer, 0, 0)
    return pl.pallas_call(
        _merge_kernel,
        grid=(t // tm,),
        in_specs=[
            pl.BlockSpec((tm, QA_W), lambda i: (i, 0)),
            pl.BlockSpec((tm, VB_W), lambda i: (i, 0)),
            pl.BlockSpec((tm, D_MODEL), lambda i: (i, COL_GA // D_MODEL)),
            pl.BlockSpec((tm, D_MODEL), lambda i: (i, COL_GB // D_MODEL)),
            pl.BlockSpec((tm, D_MODEL), lambda i: (i, 0)),
            pl.BlockSpec((1, QA_W, D_MODEL), wsel, pipeline_mode=pl.Buffered(1)),
            pl.BlockSpec((1, VB_W, D_MODEL), wsel, pipeline_mode=pl.Buffered(1)),
            pl.BlockSpec((1, D_MODEL, D_MODEL), wsel, pipeline_mode=pl.Buffered(1)),
        ],
        out_specs=pl.BlockSpec((tm, D_MODEL), lambda i: (i, 0)),
        out_shape=jax.ShapeDtypeStruct((t, D_MODEL), F32),
        scratch_shapes=[pltpu.VMEM((QA_W, D_MODEL), BF16), pltpu.VMEM((VB_W, D_MODEL), BF16),
                        pltpu.VMEM((D_MODEL, D_MODEL), BF16)],
        compiler_params=_cparams(1),
        name="merge",
    )(ya, yb, proj, proj, x2, wa_all, wb_all, wo_all)


def _moe_tile_tokens(t):
    return min(4096, t)


MOE_BLK = 512
MOE_UNIT = 16
MOE_CAP = 1536
MOE_CAP_COMMON = 1280
XS_W = D_MODEL + LANE
MOE_MT = 384
MOE_CH = 2 * MOE_MT
MOE_CHU = MOE_CH // MOE_UNIT
MOE_NC = 256
MOE_EPS = 2
MOE_ESTEPS = N_EXPERTS // MOE_EPS
assert MOE_CAP >= 2 * MOE_BLK + N_EXPERTS * (MOE_UNIT - 1) and MOE_CAP % MOE_UNIT == 0


NT_DIMS = (((1,), (1,)), ((), ()))


def _route(hn_bf, wrt_ref, brc_ref):
    logits = lax.dot_general(wrt_ref[...].astype(BF16), hn_bf, NT_DIMS,
                             preferred_element_type=F32) + brc_ref[:, 0:1]
    row = lax.broadcasted_iota(jnp.int32, logits.shape, 0).astype(F32)
    big = float(LANE)
    ninf = -jnp.inf
    gl = jnp.where(row < N_GROUPS, logits, ninf)
    gmax = jnp.max(gl, axis=0, keepdims=True)
    g_idx = jnp.min(jnp.where(gl == gmax, row, big), axis=0, keepdims=True)
    g_w = 1.0 / jnp.sum(jnp.exp(gl - gmax), axis=0, keepdims=True)
    lo = float(N_GROUPS) + g_idx * float(EXPERTS_PER_GROUP)
    el = jnp.where((row >= lo) & (row < lo + EXPERTS_PER_GROUP), logits, ninf)
    v1 = jnp.max(el, axis=0, keepdims=True)
    i1 = jnp.min(jnp.where(el == v1, row, big), axis=0, keepdims=True)
    el2 = jnp.where(row == i1, ninf, el)
    v2 = jnp.max(el2, axis=0, keepdims=True)
    i2 = jnp.min(jnp.where(el2 == v2, row, big), axis=0, keepdims=True)
    e21 = jnp.exp(v2 - v1)
    w1 = g_w / (1.0 + e21)
    w2 = g_w * e21 / (1.0 + e21)
    return i1 - float(N_GROUPS), i2 - float(N_GROUPS), w1, w2, row


def _slot_matrix(pos1, pos2, second):
    slot = lax.broadcasted_iota(jnp.int32, (pos1.shape[0], MOE_CAP), 1)
    return jnp.where(slot == pos1.astype(jnp.int32), 1.0,
                     jnp.where(slot == pos2.astype(jnp.int32), second, 0.0)).astype(BF16)


def _slot_matrix_t(pos1, pos2, second):
    slot = lax.broadcasted_iota(jnp.int32, (MOE_CAP, pos1.shape[1]), 0)
    return jnp.where(slot == pos1.astype(jnp.int32), 1.0,
                     jnp.where(slot == pos2.astype(jnp.int32), second, 0.0)).astype(BF16)


def _split3_f32(w):
    hi = w.astype(BF16).astype(F32)
    r1 = w - hi
    mid = r1.astype(BF16).astype(F32)
    return hi, mid, r1 - mid


def _rows_to_sublanes(vals, row):
    out = jnp.zeros(row.shape, F32)
    for k, v in enumerate(vals):
        out = jnp.where(row == float(k), v, out)
    return out


def _moe_kernel(x_ref, g_ref, wrt_ref, brc_ref, wg_ref, wu_ref, wd_ref, gf_ref, o_ref,
                xs_ref, pos_ref, tab_v, tab_s, xt_ref, addr_ref, sem,
                *, nblk, final_norm):
    s = pl.program_id(1)
    dump_row = nblk * MOE_CAP

    @pl.when(s < nblk)
    def _dispatch():
        @pl.when(s == 0)
        def _init():
            xs_ref[dump_row:dump_row + MOE_UNIT, :] = jnp.zeros((MOE_UNIT, XS_W), BF16)
            xt_ref[...] = jnp.zeros_like(xt_ref)

        x = x_ref[...]
        ms = jnp.mean(x * x, axis=-1, keepdims=True)
        hn = (x * lax.rsqrt(ms + EPS) * g_ref[0]).astype(BF16)
        e1, e2, w1, w2, row = _route(hn, wrt_ref, brc_ref)
        onehot = ((row == e1) | (row == e2)).astype(F32)
        earlier = (lax.broadcasted_iota(jnp.int32, (MOE_BLK, MOE_BLK), 0)
                   < lax.broadcasted_iota(jnp.int32, (MOE_BLK, MOE_BLK), 1)).astype(BF16)
        before = jnp.dot(onehot.astype(BF16), earlier, preferred_element_type=F32)
        counts = jnp.sum(onehot, axis=1, keepdims=True)
        units = jnp.floor((counts + (MOE_UNIT - 1.0)) * (1.0 / MOE_UNIT))
        lower = (lax.broadcasted_iota(jnp.int32, (LANE, LANE), 1)
                 < lax.broadcasted_iota(jnp.int32, (LANE, LANE), 0)).astype(BF16)
        padded = jnp.broadcast_to(units * MOE_UNIT, (LANE, LANE))
        run_off = jnp.dot(lower, padded.astype(BF16), preferred_element_type=F32)[:, 0:1]
        start = run_off + before
        pos1 = jnp.sum(jnp.where(row == e1, start, 0.0), axis=0, keepdims=True)
        pos2 = jnp.sum(jnp.where(row == e2, start, 0.0), axis=0, keepdims=True)
        w_rows = _rows_to_sublanes(
            _split3_f32(w1) + _split3_f32(w2) + (jnp.ones_like(w1),), row).astype(BF16)
        used = jnp.sum(units, axis=0, keepdims=True) * MOE_UNIT
        pt_t = _slot_matrix_t(pos1, pos2, 2.0)
        rows = pl.ds(pl.multiple_of(s * MOE_CAP, MOE_CAP), MOE_CAP)
        for c in range(D_MODEL // MOE_NC):
            cs = slice(c * MOE_NC, (c + 1) * MOE_NC)
            xs_ref[rows, cs] = jnp.dot(pt_t, hn[:, cs], preferred_element_type=F32).astype(BF16)
        xs_ref[rows, D_MODEL:] = lax.dot_general(pt_t, w_rows, NT_DIMS,
                                                 preferred_element_type=F32).astype(BF16)
        pos_ref[s] = _rows_to_sublanes((pos1, pos2), row).T
        lane_e = lax.broadcasted_iota(jnp.int32, (LANE, LANE), 1)
        per_expert = jnp.where(lane_e == 0, run_off,
                               jnp.where(lane_e == 1, units, jnp.where(lane_e == 2, used, 0.0)))
        tab_v[s] = per_expert.T[0:SUBLANE, :].astype(jnp.int32)

        @pl.when(s == nblk - 1)
        def _publish():
            cp = pltpu.make_async_copy(tab_v, tab_s, sem)
            cp.start()
            cp.wait()

    def one_expert(e, wi):
        total = tab_s[0, 1, e]
        for j in range(1, nblk):
            total = total + tab_s[j, 1, e]

        def chunk(c, carry):
            u0 = c * MOE_CHU
            for k in range(MOE_CHU):
                addr_ref[k] = dump_row
            cum = 0
            for j in range(nblk):
                nj = tab_s[j, 1, e]
                run0 = j * MOE_CAP + tab_s[j, 0, e]
                k_off = cum - u0

                def copy_unit(u, c2, run0=run0, k_off=k_off):
                    src = pl.multiple_of(run0 + u * MOE_UNIT, MOE_UNIT)
                    dst = pl.multiple_of((k_off + u) * MOE_UNIT, MOE_UNIT)
                    xt_ref[pl.ds(dst, MOE_UNIT), :] = xs_ref[pl.ds(src, MOE_UNIT), :]
                    addr_ref[k_off + u] = src
                    return c2
                lax.fori_loop(jnp.clip(u0 - cum, 0, nj), jnp.clip(u0 + MOE_CHU - cum, 0, nj),
                              copy_unit, 0)
                cum = cum + nj
            rows_here = jnp.minimum(MOE_CHU, total - u0) * MOE_UNIT

            def tile(t, c3):
                xt = xt_ref[pl.ds(pl.multiple_of(t * MOE_MT, MOE_MT), MOE_MT), :]
                wl = xt[:, D_MODEL:].astype(F32)
                second = wl[:, 6:7] > 1.5
                wcol = jnp.where(second, 0.5 * (wl[:, 3:4] + wl[:, 4:5] + wl[:, 5:6]),
                                 wl[:, 0:1] + wl[:, 1:2] + wl[:, 2:3])
                undo = jnp.where(second, 0.5, 1.0)
                xf = xt[:, :D_MODEL]
                gate = jnp.dot(xf, wg_ref[wi], preferred_element_type=F32) * undo
                up = jnp.dot(xf, wu_ref[wi], preferred_element_type=F32) * undo
                h = gate * _sigmoid(gate) * up * wcol
                y = jnp.dot(h.astype(BF16), wd_ref[wi], preferred_element_type=F32).astype(BF16)
                for uu in range(MOE_MT // MOE_UNIT):
                    dst = pl.multiple_of(addr_ref[t * (MOE_MT // MOE_UNIT) + uu], MOE_UNIT)
                    xs_ref[pl.ds(dst, MOE_UNIT), :D_MODEL] = y[uu * MOE_UNIT:(uu + 1) * MOE_UNIT]
                return c3
            lax.fori_loop(0, (rows_here + MOE_MT - 1) // MOE_MT, tile, 0)
            return carry
        lax.fori_loop(0, (total + MOE_CHU - 1) // MOE_CHU, chunk, 0)

    @pl.when((s >= nblk) & (s < nblk + MOE_ESTEPS))
    def _experts():
        for wi in range(MOE_EPS):
            one_expert((s - nblk) * MOE_EPS + wi, wi)

    @pl.when(s >= nblk + MOE_ESTEPS)
    def _combine():
        j = s - nblk - MOE_ESTEPS
        p = pos_ref[j]
        pt = _slot_matrix(p[:, 0:1], p[:, 1:2], 1.0)
        base = pl.multiple_of(j * MOE_CAP, MOE_CAP)
        head = pl.ds(base, MOE_CAP_COMMON)
        for c in range(D_MODEL // MOE_NC):
            cs = slice(c * MOE_NC, (c + 1) * MOE_NC)
            o_ref[:, cs] = x_ref[:, cs] + jnp.dot(pt[:, :MOE_CAP_COMMON], xs_ref[head, cs],
                                                  preferred_element_type=F32)

        @pl.when(tab_s[j, 2, 0] > MOE_CAP_COMMON)
        def _tail():
            tail = pl.ds(base + MOE_CAP_COMMON, MOE_CAP - MOE_CAP_COMMON)
            for c in range(D_MODEL // MOE_NC):
                cs = slice(c * MOE_NC, (c + 1) * MOE_NC)
                o_ref[:, cs] += jnp.dot(pt[:, MOE_CAP_COMMON:], xs_ref[tail, cs],
                                        preferred_element_type=F32)
        if final_norm:
            y = o_ref[...]
            ms = jnp.mean(y * y, axis=-1, keepdims=True)
            o_ref[...] = y * lax.rsqrt(ms + EPS) * gf_ref[...]


def _moe(x2, g_all, w_r, b_r, wg_all, wu_all, wd_all, g_final, layer, tt, final_norm):
    t = x2.shape[0]
    nblk = tt // MOE_BLK
    steps = 2 * nblk + MOE_ESTEPS

    def tok(b, s):
        j = jnp.where(s < nblk, s, jnp.where(s < nblk + MOE_ESTEPS, nblk - 1, s - nblk - MOE_ESTEPS))
        return (b * nblk + j, 0)

    def out_tok(b, s):
        return (b * nblk + jnp.maximum(s - nblk - MOE_ESTEPS, 0), 0)

    wsel = lambda b, s: (jnp.clip(s - nblk, 0, MOE_ESTEPS - 1), 0, 0)
    const = lambda b, s: (0, 0)
    return pl.pallas_call(
        functools.partial(_moe_kernel, nblk=nblk, final_norm=final_norm),
        grid=(t // tt, steps),
        in_specs=[
            pl.BlockSpec((MOE_BLK, D_MODEL), tok),
            pl.BlockSpec((1, 1, D_MODEL), lambda b, s: (layer, 0, 0)),
            pl.BlockSpec((LANE, D_MODEL), const),
            pl.BlockSpec((LANE, 1), const),
            pl.BlockSpec((MOE_EPS, D_MODEL, D_EXPERT), wsel),
            pl.BlockSpec((MOE_EPS, D_MODEL, D_EXPERT), wsel),
            pl.BlockSpec((MOE_EPS, D_EXPERT, D_MODEL), wsel),
            pl.BlockSpec((1, D_MODEL), const),
        ],
        out_specs=pl.BlockSpec((MOE_BLK, D_MODEL), out_tok),
        out_shape=jax.ShapeDtypeStruct((t, D_MODEL), F32),
        scratch_shapes=[
            pltpu.VMEM((nblk * MOE_CAP + MOE_UNIT, XS_W), BF16),
            pltpu.VMEM((nblk, MOE_BLK, LANE), F32),
            pltpu.VMEM((nblk, SUBLANE, LANE), jnp.int32),
            pltpu.SMEM((nblk, SUBLANE, LANE), jnp.int32),
            pltpu.VMEM((MOE_CH, XS_W), BF16),
            pltpu.SMEM((MOE_CHU,), jnp.int32),
            pltpu.SemaphoreType.DMA,
        ],
        compiler_params=_cparams(2),
        name="moe",
    )(x2, g_all, w_r, b_r, wg_all, wu_all, wd_all, g_final)


def _t5_bucket(rel):
    nb = N_BUCKETS // 2
    max_exact = nb // 2
    base = jnp.where(rel > 0, nb, 0)
    n = jnp.abs(rel)
    large = max_exact + (jnp.log(jnp.maximum(n, 1).astype(jnp.float32) / max_exact)
                         / math.log(MAX_DISTANCE / max_exact) * (nb - max_exact)).astype(jnp.int32)
    large = jnp.minimum(large, nb - 1)
    return base + jnp.where(n < max_exact, n, large)


def _pad_rank(wr):
    return jnp.pad(wr, ((0, LANE - B_RANK), (0, 0))).astype(BF16)


def kernel(x, w_in, rel_bias, attn_sink, gla_wr_fwd, gla_br_fwd, gla_wr_bwd, gla_br_bwd, gla_norm, w_branch_a, w_branch_b, w_out, norm_mix, norm_ffn, router_group_w, router_group_b, router_expert_w, router_expert_b, expert_w_gate, expert_w_up, expert_w_down, norm_final):
    bsz, seq, d = x.shape
    depth = w_in.shape[0]
    t = bsz * seq
    tt = _moe_tile_tokens(t)
    q_off = jnp.arange(BLOCK)
    k_off = jnp.arange(3 * BLOCK) - BLOCK
    bucket = _t5_bucket(k_off[None, :] - q_off[:, None]).astype(jnp.int32) & (N_BUCKETS - 1)
    x2 = x.reshape(t, d)
    norm_mix3 = norm_mix[:, None, :]
    w_in_t = jnp.swapaxes(w_in, 1, 2)
    norm_ffn3 = norm_ffn[:, None, :]
    wg_all = expert_w_gate.reshape(depth * N_EXPERTS, d, D_EXPERT)
    wu_all = expert_w_up.reshape(depth * N_EXPERTS, d, D_EXPERT)
    wd_all = expert_w_down.reshape(depth * N_EXPERTS, D_EXPERT, d)
    for l in range(depth):
        proj, wg_bf, wu_bf, wd_bf = _inproj(x2, norm_mix3, w_in_t, wg_all, wu_all, wd_all, l)
        ya = _attention(proj, bucket, rel_bias, attn_sink[l], bsz, seq)
        wrf, wrb = _pad_rank(gla_wr_fwd[l]), _pad_rank(gla_wr_bwd[l])
        brf, brb = gla_br_fwd[l][None, :], gla_br_bwd[l][None, :]
        s_f, s_b = _gla_states(proj, wrf.T, brf.T, wrb.T, brb.T, bsz, seq)
        yb = _gla_out(proj, s_f, s_b, wrf, brf, wrb, brb, gla_norm[l][None, :], bsz, seq)
        x2 = _merge(ya, yb, proj, x2, w_branch_a, w_branch_b, w_out, l)
        w_r = jnp.concatenate(
            [router_group_w[l].T, router_expert_w[l].T,
             jnp.zeros((LANE - N_GROUPS - N_EXPERTS, d), F32)], axis=0)
        b_r = jnp.concatenate(
            [router_group_b[l], router_expert_b[l],
             jnp.zeros((LANE - N_GROUPS - N_EXPERTS,), F32)])[:, None]
        x2 = _moe(x2, norm_ffn3, w_r, b_r, wg_bf, wu_bf, wd_bf, norm_final[None, :], l, tt,
                  final_norm=(l == depth - 1))
    return x2.reshape(bsz, seq, d)
```
